```python
import math
import jax, jax.numpy as jnp
from jax import lax
import numpy as np

D_MODEL = 2048
BATCH = 2
SEQ = 4096
DEPTH = 2

HEAD_DIM = 64
N_Q_HEADS = 16
N_KV_HEADS = 4
GQA_GROUP = N_Q_HEADS // N_KV_HEADS
ATTN_WIDTH = N_Q_HEADS * HEAD_DIM
KV_WIDTH = N_KV_HEADS * HEAD_DIM
WINDOW = 128
BLOCK = 128
SSM_WIDTH = D_MODEL // 2
SSM_GROUP_CH = 16
SSM_GROUPS = SSM_WIDTH // SSM_GROUP_CH
SSM_STATE = 64
DT_MIN = 1e-3
DT_MAX = 1e-1
D_FF = -(-8 * D_MODEL // (3 * 256)) * 256
OFF_Q = 0
OFF_K = OFF_Q + ATTN_WIDTH
OFF_V = OFF_K + KV_WIDTH
OFF_U = OFF_V + KV_WIDTH
OFF_G = OFF_U + SSM_WIDTH
IN_WIDTH = OFF_G + 2 * D_MODEL
RMS_EPS = 1e-6

kernel_name = "hybrid_swa_sink_s5_gated_block"


def rmsnorm(x, g):
    xf = x.astype(jnp.float32)
    y = xf * lax.rsqrt(jnp.mean(xf * xf, axis=-1, keepdims=True) + RMS_EPS)
    return (y * g.astype(jnp.float32)).astype(x.dtype)


def alibi_slopes():
    return jnp.exp2(-8.0 * jnp.arange(1, N_Q_HEADS + 1, dtype=jnp.float32) / N_Q_HEADS)


def sliding_window_attention(q, k, v, q_gain, k_gain, sinks):
    B, L = q.shape[0], q.shape[1]
    nb = L // BLOCK
    q = rmsnorm(q, q_gain).astype(jnp.float32)
    k = rmsnorm(k, k_gain).astype(jnp.float32)
    v = v.astype(jnp.float32)
    qb = q.reshape(B, nb, BLOCK, N_KV_HEADS, GQA_GROUP, HEAD_DIM)
    pad = ((0, 0), (BLOCK, 0), (0, 0), (0, 0))
    kp = jnp.pad(k, pad)[:, :L].reshape(B, nb, BLOCK, N_KV_HEADS, HEAD_DIM)
    vp = jnp.pad(v, pad)[:, :L].reshape(B, nb, BLOCK, N_KV_HEADS, HEAD_DIM)
    kb = jnp.concatenate([kp, k.reshape(B, nb, BLOCK, N_KV_HEADS, HEAD_DIM)], axis=2)
    vb = jnp.concatenate([vp, v.reshape(B, nb, BLOCK, N_KV_HEADS, HEAD_DIM)], axis=2)
    scores = jnp.einsum('bnqkgd,bnskd->bnkgqs', qb, kb) * (HEAD_DIM ** -0.5)
    t_loc = jnp.arange(BLOCK)
    s_loc = jnp.arange(2 * BLOCK) - BLOCK
    dist = (t_loc[:, None] - s_loc[None, :]).astype(jnp.float32)
    s_abs = jnp.arange(nb)[:, None] * BLOCK + s_loc[None, :]
    valid = (dist >= 0)[None] & (dist < WINDOW)[None] & (s_abs >= 0)[:, None, :]
    bias = (-alibi_slopes()[:, None, None] * dist[None]).reshape(N_KV_HEADS, GQA_GROUP, BLOCK, 2 * BLOCK)
    scores = jnp.where(valid[None, :, None, None], scores + bias[None, None], -jnp.inf)
    sink = sinks.astype(jnp.float32).reshape(1, 1, N_KV_HEADS, GQA_GROUP, 1, 1)
    m = jnp.maximum(jnp.max(scores, axis=-1, keepdims=True), sink)
    p = jnp.exp(scores - m)
    denom = jnp.sum(p, axis=-1, keepdims=True) + jnp.exp(sink - m)
    out = jnp.einsum('bnkgqs,bnskd->bnqkgd', p / denom, vb)
    return out.reshape(B, L, ATTN_WIDTH)


def s5_ssm(u, lam_re, lam_im, log_dt, b_re, b_im, c_re, c_im, d_skip):
    B, L = u.shape[0], u.shape[1]
    uf = u.astype(jnp.float32).reshape(B, L, SSM_GROUPS, SSM_GROUP_CH)
    lr = lam_re.astype(jnp.float32)
    li = lam_im.astype(jnp.float32)
    dt = jnp.exp(log_dt.astype(jnp.float32))[:, None]
    mag = jnp.exp(lr * dt)
    ar = mag * jnp.cos(li * dt)
    ai = mag * jnp.sin(li * dt)
    den = lr * lr + li * li
    fr = ((ar - 1.0) * lr + ai * li) / den
    fi = (ai * lr - (ar - 1.0) * li) / den
    br = b_re.astype(jnp.float32)
    bi = b_im.astype(jnp.float32)
    bbar_r = fr[:, :, None] * br - fi[:, :, None] * bi
    bbar_i = fr[:, :, None] * bi + fi[:, :, None] * br
    bu_r = jnp.einsum('blgh,gph->blgp', uf, bbar_r)
    bu_i = jnp.einsum('blgh,gph->blgp', uf, bbar_i)
    a_r = jnp.broadcast_to(ar[None, None], (1, L, SSM_GROUPS, SSM_STATE))
    a_i = jnp.broadcast_to(ai[None, None], (1, L, SSM_GROUPS, SSM_STATE))

    def combine(e1, e2):
        ar1, ai1, br1, bi1 = e1
        ar2, ai2, br2, bi2 = e2
        return (ar2 * ar1 - ai2 * ai1,
                ar2 * ai1 + ai2 * ar1,
                ar2 * br1 - ai2 * bi1 + br2,
                ar2 * bi1 + ai2 * br1 + bi2)

    _, _, s_r, s_i = lax.associative_scan(combine, (a_r, a_i, bu_r, bu_i), axis=1)
    y = (jnp.einsum('blgp,ghp->blgh', s_r, c_re.astype(jnp.float32))
         - jnp.einsum('blgp,ghp->blgh', s_i, c_im.astype(jnp.float32))
         + d_skip.astype(jnp.float32).reshape(SSM_GROUPS, SSM_GROUP_CH) * uf)
    return y.reshape(B, L, SSM_WIDTH)


def setup_inputs(seed: int = 0) -> dict:
    key = jax.random.key(seed)
    ks = jax.random.split(key, 24)
    f32 = jnp.float32
    nrm = lambda k, shape, scale: jax.random.normal(k, shape, f32) * scale
    x = jax.random.normal(ks[0], (BATCH, SEQ, D_MODEL), f32)
    norm_mix_g = 1.0 + nrm(ks[1], (DEPTH, D_MODEL), 0.02)
    w_in = nrm(ks[2], (DEPTH, D_MODEL, IN_WIDTH), D_MODEL ** -0.5)
    gate_bias = nrm(ks[3], (DEPTH, 2 * D_MODEL), 0.02)
    q_norm_g = 1.0 + nrm(ks[4], (DEPTH, HEAD_DIM), 0.02)
    k_norm_g = 1.0 + nrm(ks[5], (DEPTH, HEAD_DIM), 0.02)
    attn_sinks = nrm(ks[6], (DEPTH, N_Q_HEADS), 0.5)
    ssm_lambda_re = -0.5 + nrm(ks[7], (DEPTH, SSM_GROUPS, SSM_STATE), 0.01)
    ssm_lambda_im = (math.pi * jnp.arange(SSM_STATE, dtype=f32))[None, None, :] + nrm(ks[8], (DEPTH, SSM_GROUPS, SSM_STATE), 0.01)
    ssm_log_dt = jax.random.uniform(ks[9], (DEPTH, SSM_GROUPS), f32, math.log(DT_MIN), math.log(DT_MAX))
    ssm_b_re = nrm(ks[10], (DEPTH, SSM_GROUPS, SSM_STATE, SSM_GROUP_CH), (2 * SSM_GROUP_CH) ** -0.5)
    ssm_b_im = nrm(ks[11], (DEPTH, SSM_GROUPS, SSM_STATE, SSM_GROUP_CH), (2 * SSM_GROUP_CH) ** -0.5)
    ssm_c_re = nrm(ks[12], (DEPTH, SSM_GROUPS, SSM_GROUP_CH, SSM_STATE), (2 * SSM_STATE) ** -0.5)
    ssm_c_im = nrm(ks[13], (DEPTH, SSM_GROUPS, SSM_GROUP_CH, SSM_STATE), (2 * SSM_STATE) ** -0.5)
    ssm_d = nrm(ks[14], (DEPTH, SSM_WIDTH), 1.0)
    ssm_glu_w = nrm(ks[15], (DEPTH, SSM_WIDTH, SSM_WIDTH), SSM_WIDTH ** -0.5)
    ssm_glu_b = nrm(ks[16], (DEPTH, SSM_WIDTH), 0.02)
    w_attn_branch = nrm(ks[17], (DEPTH, ATTN_WIDTH, D_MODEL), ATTN_WIDTH ** -0.5)
    w_ssm_branch = nrm(ks[18], (DEPTH, SSM_WIDTH, D_MODEL), SSM_WIDTH ** -0.5)
    w_out = nrm(ks[19], (DEPTH, D_MODEL, D_MODEL), D_MODEL ** -0.5)
    norm_ffn_g = 1.0 + nrm(ks[20], (DEPTH, D_MODEL), 0.02)
    w_ffn_in = nrm(ks[21], (DEPTH, D_MODEL, 2 * D_FF), D_MODEL ** -0.5)
    w_ffn_out = nrm(ks[22], (DEPTH, D_FF, D_MODEL), D_FF ** -0.5)
    return {"x": x, "norm_mix_g": norm_mix_g, "w_in": w_in, "gate_bias": gate_bias,
            "q_norm_g": q_norm_g, "k_norm_g": k_norm_g, "attn_sinks": attn_sinks,
            "ssm_lambda_re": ssm_lambda_re, "ssm_lambda_im": ssm_lambda_im, "ssm_log_dt": ssm_log_dt,
            "ssm_b_re": ssm_b_re, "ssm_b_im": ssm_b_im, "ssm_c_re": ssm_c_re, "ssm_c_im": ssm_c_im,
            "ssm_d": ssm_d, "ssm_glu_w": ssm_glu_w, "ssm_glu_b": ssm_glu_b,
            "w_attn_branch": w_attn_branch, "w_ssm_branch": w_ssm_branch, "w_out": w_out,
            "norm_ffn_g": norm_ffn_g, "w_ffn_in": w_ffn_in, "w_ffn_out": w_ffn_out}


def reference(x, norm_mix_g, w_in, gate_bias, q_norm_g, k_norm_g, attn_sinks,
              ssm_lambda_re, ssm_lambda_im, ssm_log_dt, ssm_b_re, ssm_b_im, ssm_c_re, ssm_c_im,
              ssm_d, ssm_glu_w, ssm_glu_b, w_attn_branch, w_ssm_branch, w_out,
              norm_ffn_g, w_ffn_in, w_ffn_out):
    B, L = x.shape[0], x.shape[1]
    for l in range(DEPTH):
        h = rmsnorm(x, norm_mix_g[l])
        z = h @ w_in[l]
        q = z[..., OFF_Q:OFF_K].reshape(B, L, N_Q_HEADS, HEAD_DIM)
        k = z[..., OFF_K:OFF_V].reshape(B, L, N_KV_HEADS, HEAD_DIM)
        v = z[..., OFF_V:OFF_U].reshape(B, L, N_KV_HEADS, HEAD_DIM)
        u = z[..., OFF_U:OFF_G]
        gates = jax.nn.sigmoid(z[..., OFF_G:] + gate_bias[l])
        g_attn = gates[..., :D_MODEL]
        g_ssm = gates[..., D_MODEL:]
        y_attn = sliding_window_attention(q, k, v, q_norm_g[l], k_norm_g[l], attn_sinks[l]).astype(x.dtype)
        y_ssm = s5_ssm(u, ssm_lambda_re[l], ssm_lambda_im[l], ssm_log_dt[l], ssm_b_re[l], ssm_b_im[l],
                       ssm_c_re[l], ssm_c_im[l], ssm_d[l])
        y_ssm = jax.nn.gelu(y_ssm)
        y_ssm = (y_ssm * jax.nn.sigmoid(y_ssm @ ssm_glu_w[l].astype(jnp.float32)
                                        + ssm_glu_b[l].astype(jnp.float32))).astype(x.dtype)
        merged = g_attn * (y_attn @ w_attn_branch[l]) + g_ssm * (y_ssm @ w_ssm_branch[l])
        x = x + merged @ w_out[l]
        h2 = rmsnorm(x, norm_ffn_g[l])
        gu = h2 @ w_ffn_in[l]
        x = x + (jax.nn.silu(gu[..., :D_FF]) * gu[..., D_FF:]) @ w_ffn_out[l]
    return x
```

```python
import math
from functools import partial

import jax
import jax.numpy as jnp
from jax import lax
from jax.experimental import pallas as pl
from jax.experimental.pallas import tpu as pltpu

D_MODEL = 2048
DEPTH = 2
HEAD_DIM = 64
N_Q_HEADS = 16
N_KV_HEADS = 4
GQA_GROUP = N_Q_HEADS // N_KV_HEADS
ATTN_WIDTH = N_Q_HEADS * HEAD_DIM
KV_WIDTH = N_KV_HEADS * HEAD_DIM
WINDOW = 128
BLOCK = 128
SSM_WIDTH = D_MODEL // 2
SSM_GROUP_CH = 16
SSM_GROUPS = SSM_WIDTH // SSM_GROUP_CH
SSM_STATE = 64
D_FF = -(-8 * D_MODEL // (3 * 256)) * 256
OFF_K = ATTN_WIDTH
OFF_V = OFF_K + KV_WIDTH
OFF_U = OFF_V + KV_WIDTH
OFF_G = OFF_U + SSM_WIDTH
IN_WIDTH = OFF_G + 2 * D_MODEL
RMS_EPS = 1e-6

F32 = jnp.float32
BF16 = jnp.bfloat16

SUB = 16
SUB_W = SUB * SSM_GROUP_CH
STATE_W = 2 * SSM_STATE

VMEM_LIMIT = 56 * 1024 * 1024

INPROJ_TM = 1024
INPROJ_TN = 512
MERGE_TM = 256
FFN_TM = 1024
FFN_TF = 512


def _rms(x, g):
    return x * lax.rsqrt(jnp.mean(x * x, axis=-1, keepdims=True) + RMS_EPS) * g


N_MAIN_BLOCKS = OFF_G // INPROJ_TN
N_IN_BLOCKS = IN_WIDTH // INPROJ_TN


def _inproj_kernel(x_ref, g_ref, w_ref, b_ref, zm_ref, gt_ref, h_ref):
    j = pl.program_id(1)

    @pl.when(j == 0)
    def _():
        h_ref[...] = _rms(x_ref[...], g_ref[...]).astype(BF16)

    z = jnp.dot(h_ref[...], w_ref[...], preferred_element_type=F32)

    @pl.when(j < N_MAIN_BLOCKS)
    def _():
        zm_ref[...] = z.astype(BF16)

    @pl.when(j >= N_MAIN_BLOCKS)
    def _():
        gt_ref[...] = jax.nn.sigmoid(z + b_ref[...]).astype(BF16)


def _inproj(x2, gain, w_bf, bias):
    t = x2.shape[0]
    return pl.pallas_call(
        _inproj_kernel,
        grid=(t // INPROJ_TM, N_IN_BLOCKS),
        in_specs=[
            pl.BlockSpec((INPROJ_TM, D_MODEL), lambda i, j: (i, 0)),
            pl.BlockSpec((1, D_MODEL), lambda i, j: (0, 0)),
            pl.BlockSpec((D_MODEL, INPROJ_TN), lambda i, j: (0, j)),
            pl.BlockSpec((1, INPROJ_TN), lambda i, j: (0, jnp.maximum(j - N_MAIN_BLOCKS, 0))),
        ],
        out_specs=[
            pl.BlockSpec((INPROJ_TM, INPROJ_TN), lambda i, j: (i, jnp.minimum(j, N_MAIN_BLOCKS - 1))),
            pl.BlockSpec((INPROJ_TM, INPROJ_TN), lambda i, j: (i, jnp.maximum(j - N_MAIN_BLOCKS, 0))),
        ],
        out_shape=[
            jax.ShapeDtypeStruct((t, OFF_G), BF16),
            jax.ShapeDtypeStruct((t, 2 * D_MODEL), BF16),
        ],
        scratch_shapes=[pltpu.VMEM((INPROJ_TM, D_MODEL), BF16)],
        compiler_params=pltpu.CompilerParams(
            dimension_semantics=("arbitrary", "arbitrary"), vmem_limit_bytes=VMEM_LIMIT),
        name="inproj",
    )(x2, gain, w_bf, bias)


def _attn_kernel(sink_ref, q_ref, kvc_ref, kvp_ref, qg_ref, kg_ref, o_ref):
    n = pl.program_id(1)
    qg = qg_ref[...]
    kg = kg_ref[...]
    t_loc = lax.broadcasted_iota(jnp.int32, (BLOCK, 2 * BLOCK), 0)
    s_loc = lax.broadcasted_iota(jnp.int32, (BLOCK, 2 * BLOCK), 1)
    dist = t_loc - s_loc + BLOCK
    valid = (dist >= 0) & (dist < WINDOW) & ((s_loc >= BLOCK) | (n > 0))
    distf = dist.astype(F32)
    for kh in range(N_KV_HEADS):
        ks = slice(kh * HEAD_DIM, (kh + 1) * HEAD_DIM)
        vs = slice(KV_WIDTH + kh * HEAD_DIM, KV_WIDTH + (kh + 1) * HEAD_DIM)
        k = jnp.concatenate([kvp_ref[:, ks], kvc_ref[:, ks]], axis=0).astype(F32)
        k = _rms(k, kg).astype(BF16)
        v = jnp.concatenate([kvp_ref[:, vs], kvc_ref[:, vs]], axis=0)
        for g in range(GQA_GROUP):
            h = kh * GQA_GROUP + g
            slope = 2.0 ** (-8.0 * (h + 1) / N_Q_HEADS)
            hs = slice(h * HEAD_DIM, (h + 1) * HEAD_DIM)
            q = _rms(q_ref[:, hs].astype(F32), qg) * (HEAD_DIM ** -0.5)
            s = lax.dot_general(q.astype(BF16), k, (((1,), (1,)), ((), ())),
                                preferred_element_type=F32)
            s = jnp.where(valid, s - slope * distf, -jnp.inf)
            sink = sink_ref[h]
            m = jnp.maximum(jnp.max(s, axis=-1, keepdims=True), sink)
            p = jnp.exp(s - m)
            denom = jnp.sum(p, axis=-1, keepdims=True) + jnp.exp(sink - m)
            o = jnp.dot(p.astype(BF16), v, preferred_element_type=F32) / denom
            o_ref[:, hs] = o.astype(BF16)


def _attention(zm, sinks, q_gain, k_gain, batch, seq):
    t = zm.shape[0]
    nb = seq // BLOCK
    kv_col = OFF_K // (2 * KV_WIDTH)
    return pl.pallas_call(
        _attn_kernel,
        grid=(batch, nb),
        in_specs=[
            pl.BlockSpec(memory_space=pltpu.SMEM),
            pl.BlockSpec((BLOCK, ATTN_WIDTH), lambda b, n: (b * nb + n, 0)),
            pl.BlockSpec((BLOCK, 2 * KV_WIDTH), lambda b, n: (b * nb + n, kv_col)),
            pl.BlockSpec((BLOCK, 2 * KV_WIDTH), lambda b, n: (b * nb + jnp.maximum(n - 1, 0), kv_col)),
            pl.BlockSpec((1, HEAD_DIM), lambda b, n: (0, 0)),
            pl.BlockSpec((1, HEAD_DIM), lambda b, n: (0, 0)),
        ],
        out_specs=pl.BlockSpec((BLOCK, ATTN_WIDTH), lambda b, n: (b * nb + n, 0)),
        out_shape=jax.ShapeDtypeStruct((t, ATTN_WIDTH), BF16),
        compiler_params=pltpu.CompilerParams(
            dimension_semantics=("arbitrary", "arbitrary"), vmem_limit_bytes=VMEM_LIMIT),
        name="swa_attention",
    )(sinks, zm, zm, zm, q_gain, k_gain)


def _ssm_prep(lam_re, lam_im, log_dt, b_re, b_im, c_re, c_im, d_skip):
    hp = lax.Precision.HIGHEST
    g_, p_, h_ = SSM_GROUPS, SSM_STATE, SSM_GROUP_CH
    lr = lam_re.astype(F32)
    li = lam_im.astype(F32)
    dt = jnp.exp(log_dt.astype(F32))[:, None]
    lrd = lr * dt
    ang = li * dt

    def apow(m):
        m = jnp.asarray(m, F32)[:, None, None]
        mag = jnp.exp(m * lrd[None])
        return mag * jnp.cos(m * ang[None]), mag * jnp.sin(m * ang[None])

    ar, ai = apow([1.0])
    ar, ai = ar[0], ai[0]
    den = lr * lr + li * li
    fr = ((ar - 1.0) * lr + ai * li) / den
    fi = (ai * lr - (ar - 1.0) * li) / den
    br = b_re.astype(F32)
    bi = b_im.astype(F32)
    bbar_r = fr[:, :, None] * br - fi[:, :, None] * bi
    bbar_i = fr[:, :, None] * bi + fi[:, :, None] * br
    pr, pi = apow(list(range(SUB + 1)))
    er = pr[..., None] * bbar_r[None] - pi[..., None] * bbar_i[None]
    ei = pr[..., None] * bbar_i[None] + pi[..., None] * bbar_r[None]
    cr = c_re.astype(F32)
    ci = c_im.astype(F32)
    kern = (jnp.einsum('ghp,tgpk->tghk', cr, er[:SUB], precision=hp)
            - jnp.einsum('ghp,tgpk->tghk', ci, ei[:SUB], precision=hp))
    dmat = d_skip.astype(F32).reshape(g_, h_)[:, :, None] * jnp.eye(h_, dtype=F32)
    kern = kern.at[0].add(dmat)
    tau = jnp.arange(SUB)[None, :] - jnp.arange(SUB)[:, None]
    tm = kern[jnp.clip(tau, 0, SUB - 1)]
    tm = jnp.where((tau >= 0)[:, :, None, None, None], tm, 0.0)
    tmat = tm.transpose(2, 0, 4, 1, 3).reshape(g_, SUB_W, SUB_W)
    bm_r = er[:SUB][::-1].transpose(1, 0, 3, 2).reshape(g_, SUB_W, p_)
    bm_i = ei[:SUB][::-1].transpose(1, 0, 3, 2).reshape(g_, SUB_W, p_)
    tb = jnp.concatenate([tmat, bm_r, bm_i], axis=-1)
    pr1 = pr[1:].transpose(1, 2, 0)[..., None]
    pi1 = pi[1:].transpose(1, 2, 0)[..., None]
    crt = cr.transpose(0, 2, 1)[:, :, None, :]
    cit = ci.transpose(0, 2, 1)[:, :, None, :]
    c_top = (crt * pr1 - cit * pi1).reshape(g_, p_, SUB_W)
    c_bot = (-crt * pi1 - cit * pr1).reshape(g_, p_, SUB_W)
    cmat = jnp.concatenate([c_top, c_bot], axis=1)
    return tb.astype(BF16), cmat.astype(BF16), (lrd, ang)


def _ssm_levels(lrd, ang, n_levels):
    m = jnp.asarray([float(SUB * (1 << k)) for k in range(n_levels)], F32)[:, None, None]
    mag = jnp.exp(m * lrd[None])
    lr_ = (mag * jnp.cos(m * ang[None])).transpose(1, 0, 2)
    li_ = (mag * jnp.sin(m * ang[None])).transpose(1, 0, 2)
    return jnp.concatenate([lr_, lr_], axis=-1), jnp.concatenate([-li_, li_], axis=-1)


def _ssm_kernel(x_ref, tb_ref, c_ref, ar_ref, ai_ref, y_ref, *, rows_per_seq):
    x = x_ref[0]
    rows = x.shape[0]
    r = jnp.dot(x, tb_ref[0], preferred_element_type=F32)
    yt = r[:, :SUB_W]
    s = r[:, SUB_W:]
    row = lax.broadcasted_iota(jnp.int32, (rows, STATE_W), 0) % rows_per_seq
    n_levels = rows_per_seq.bit_length() - 1
    for k in range(n_levels):
        sh = 1 << k
        prev = jnp.where(row >= sh, pltpu.roll(s, sh, axis=0), 0.0)
        swp = pltpu.roll(prev, SSM_STATE, axis=1)
        s = s + ar_ref[0, k:k + 1, :] * prev + ai_ref[0, k:k + 1, :] * swp
    sprev = jnp.where(row >= 1, pltpu.roll(s, 1, axis=0), 0.0)
    y = yt + jnp.dot(sprev.astype(BF16), c_ref[0], preferred_element_type=F32)
    y_ref[0] = jax.nn.gelu(y).astype(BF16)


def _ssm(xg, tb, cmat, lev_r, lev_i, rows_per_seq):
    g_, rows, _ = xg.shape
    n_levels = lev_r.shape[1]
    return pl.pallas_call(
        partial(_ssm_kernel, rows_per_seq=rows_per_seq),
        grid=(g_,),
        in_specs=[
            pl.BlockSpec((1, rows, SUB_W), lambda g: (g, 0, 0)),
            pl.BlockSpec((1, SUB_W, SUB_W + STATE_W), lambda g: (g, 0, 0)),
            pl.BlockSpec((1, STATE_W, SUB_W), lambda g: (g, 0, 0)),
            pl.BlockSpec((1, n_levels, STATE_W), lambda g: (g, 0, 0)),
            pl.BlockSpec((1, n_levels, STATE_W), lambda g: (g, 0, 0)),
        ],
        out_specs=pl.BlockSpec((1, rows, SUB_W), lambda g: (g, 0, 0)),
        out_shape=jax.ShapeDtypeStruct((g_, rows, SUB_W), BF16),
        compiler_params=pltpu.CompilerParams(
            dimension_semantics=("arbitrary",), vmem_limit_bytes=VMEM_LIMIT),
        name="s5_scan",
    )(xg, tb, cmat, lev_r, lev_i)


def _merge_kernel(ya_ref, yg_ref, gt_ref, x_ref, wglu_ref, bglu_ref, wa_ref, ws_ref, wo_ref,
                  gffn_ref, x1_ref, h2_ref):
    yg = yg_ref[...]
    t = jnp.dot(yg, wglu_ref[...], preferred_element_type=F32) + bglu_ref[...]
    ys = (yg.astype(F32) * jax.nn.sigmoid(t)).astype(BF16)
    ma = jnp.dot(ya_ref[...], wa_ref[...], preferred_element_type=F32)
    ms = jnp.dot(ys, ws_ref[...], preferred_element_type=F32)
    merged = gt_ref[:, :D_MODEL].astype(F32) * ma + gt_ref[:, D_MODEL:].astype(F32) * ms
    x1 = x_ref[...] + jnp.dot(merged.astype(BF16), wo_ref[...], preferred_element_type=F32)
    x1_ref[...] = x1
    h2_ref[...] = _rms(x1, gffn_ref[...]).astype(BF16)


def _const_spec(shape):
    return pl.BlockSpec(shape, lambda i: (0,) * len(shape), pipeline_mode=pl.Buffered(1))


def _merge(ya, yg, gates, x2, wglu, bglu, wa, ws, wo, gffn):
    t = x2.shape[0]
    tm = MERGE_TM
    return pl.pallas_call(
        _merge_kernel,
        grid=(t // tm,),
        in_specs=[
            pl.BlockSpec((tm, ATTN_WIDTH), lambda i: (i, 0)),
            pl.BlockSpec((tm, SSM_WIDTH), lambda i: (i, 0)),
            pl.BlockSpec((tm, 2 * D_MODEL), lambda i: (i, 0)),
            pl.BlockSpec((tm, D_MODEL), lambda i: (i, 0)),
            _const_spec((SSM_WIDTH, SSM_WIDTH)),
            _const_spec((1, SSM_WIDTH)),
            _const_spec((ATTN_WIDTH, D_MODEL)),
            _const_spec((SSM_WIDTH, D_MODEL)),
            _const_spec((D_MODEL, D_MODEL)),
            _const_spec((1, D_MODEL)),
        ],
        out_specs=[
            pl.BlockSpec((tm, D_MODEL), lambda i: (i, 0)),
            pl.BlockSpec((tm, D_MODEL), lambda i: (i, 0)),
        ],
        out_shape=[
            jax.ShapeDtypeStruct((t, D_MODEL), F32),
            jax.ShapeDtypeStruct((t, D_MODEL), BF16),
        ],
        compiler_params=pltpu.CompilerParams(
            dimension_semantics=("arbitrary",), vmem_limit_bytes=VMEM_LIMIT),
        name="merge_out",
    )(ya, yg, gates, x2, wglu, bglu, wa, ws, wo, gffn)


def _ffn_kernel(h_ref, x_ref, wg_ref, wu_ref, wo_ref, o_ref):
    k = pl.program_id(1)

    @pl.when(k == 0)
    def _():
        o_ref[...] = x_ref[...]

    h = h_ref[...]
    g = jnp.dot(h, wg_ref[...], preferred_element_type=F32)
    u = jnp.dot(h, wu_ref[...], preferred_element_type=F32)
    act = (jax.nn.silu(g) * u).astype(BF16)
    o_ref[...] += jnp.dot(act, wo_ref[...], preferred_element_type=F32)


def _ffn(h2, x1, w_in, w_out):
    t = x1.shape[0]
    nk = D_FF // FFN_TF
    return pl.pallas_call(
        _ffn_kernel,
        grid=(t // FFN_TM, nk),
        in_specs=[
            pl.BlockSpec((FFN_TM, D_MODEL), lambda i, k: (i, 0), pipeline_mode=pl.Buffered(1)),
            pl.BlockSpec((FFN_TM, D_MODEL), lambda i, k: (i, 0), pipeline_mode=pl.Buffered(1)),
            pl.BlockSpec((D_MODEL, FFN_TF), lambda i, k: (0, k)),
            pl.BlockSpec((D_MODEL, FFN_TF), lambda i, k: (0, nk + k)),
            pl.BlockSpec((FFN_TF, D_MODEL), lambda i, k: (k, 0)),
        ],
        out_specs=pl.BlockSpec((FFN_TM, D_MODEL), lambda i, k: (i, 0)),
        out_shape=jax.ShapeDtypeStruct((t, D_MODEL), F32),
        compiler_params=pltpu.CompilerParams(
            dimension_semantics=("arbitrary", "arbitrary"), vmem_limit_bytes=VMEM_LIMIT),
        name="swiglu_ffn",
    )(h2, x1, w_in, w_in, w_out)


def kernel(x, norm_mix_g, w_in, gate_bias, q_norm_g, k_norm_g, attn_sinks, ssm_lambda_re, ssm_lambda_im, ssm_log_dt, ssm_b_re, ssm_b_im, ssm_c_re, ssm_c_im, ssm_d, ssm_glu_w, ssm_glu_b, w_attn_branch, w_ssm_branch, w_out, norm_ffn_g, w_ffn_in, w_ffn_out):
    batch, seq, _ = x.shape
    t = batch * seq
    n_sub = seq // SUB
    n_levels = n_sub.bit_length() - 1
    x2 = x.reshape(t, D_MODEL).astype(F32)
    for l in range(DEPTH):
        zm, gates = _inproj(x2, norm_mix_g[l].reshape(1, D_MODEL).astype(F32),
                            w_in[l].astype(BF16), gate_bias[l].reshape(1, 2 * D_MODEL).astype(F32))
        ya = _attention(zm, attn_sinks[l].astype(F32),
                        q_norm_g[l].reshape(1, HEAD_DIM).astype(F32),
                        k_norm_g[l].reshape(1, HEAD_DIM).astype(F32), batch, seq)
        tb, cmat, (lrd, ang) = _ssm_prep(ssm_lambda_re[l], ssm_lambda_im[l], ssm_log_dt[l],
                                         ssm_b_re[l], ssm_b_im[l], ssm_c_re[l], ssm_c_im[l], ssm_d[l])
        lev_r, lev_i = _ssm_levels(lrd, ang, n_levels)
        u = zm[:, OFF_U:OFF_G]
        xg = (u.reshape(batch, n_sub, SUB, SSM_GROUPS, SSM_GROUP_CH)
              .transpose(3, 0, 1, 2, 4).reshape(SSM_GROUPS, batch * n_sub, SUB_W))
        yg = _ssm(xg, tb, cmat, lev_r, lev_i, n_sub)
        yg = (yg.reshape(SSM_GROUPS, batch, n_sub, SUB, SSM_GROUP_CH)
              .transpose(1, 2, 3, 0, 4).reshape(t, SSM_WIDTH))
        x1, h2 = _merge(ya, yg, gates, x2,
                        ssm_glu_w[l].astype(BF16), ssm_glu_b[l].reshape(1, SSM_WIDTH).astype(F32),
                        w_attn_branch[l].astype(BF16), w_ssm_branch[l].astype(BF16),
                        w_out[l].astype(BF16), norm_ffn_g[l].reshape(1, D_MODEL).astype(F32))
        x2 = _ffn(h2, x1, w_ffn_in[l].astype(BF16), w_ffn_out[l].astype(BF16))
    return x2.reshape(batch, seq, D_MODEL).astype(x.dtype)
```

```python
from functools import partial

import jax
import jax.numpy as jnp
from jax import lax
from jax.experimental import pallas as pl
from jax.experimental.pallas import tpu as pltpu

D_MODEL = 2048
DEPTH = 2
HEAD_DIM = 64
N_Q_HEADS = 16
N_KV_HEADS = 4
GQA_GROUP = N_Q_HEADS // N_KV_HEADS
ATTN_WIDTH = N_Q_HEADS * HEAD_DIM
KV_WIDTH = N_KV_HEADS * HEAD_DIM
WINDOW = 128
BLOCK = 128
SSM_WIDTH = D_MODEL // 2
SSM_GROUP_CH = 16
SSM_GROUPS = SSM_WIDTH // SSM_GROUP_CH
SSM_STATE = 64
D_FF = -(-8 * D_MODEL // (3 * 256)) * 256
OFF_K = ATTN_WIDTH
OFF_V = OFF_K + KV_WIDTH
OFF_U = OFF_V + KV_WIDTH
OFF_G = OFF_U + SSM_WIDTH
IN_WIDTH = OFF_G + 2 * D_MODEL
RMS_EPS = 1e-6

F32 = jnp.float32
BF16 = jnp.bfloat16

LANES = 128
SUB = 16
SUB_W = SUB * SSM_GROUP_CH
STATE_W = 2 * SSM_STATE
SLAB_GROUPS = LANES // SSM_GROUP_CH
N_SLABS = SSM_GROUPS // SLAB_GROUPS
SLAB_W = SLAB_GROUPS * SUB_W

VMEM_LIMIT = 56 * 1024 * 1024

INPROJ_TM = 1024
INPROJ_TN = 512
MERGE_TM = 256
FFN_TM = 1024
FFN_TF = 512


def _rms(x, g):
    return x * lax.rsqrt(jnp.mean(x * x, axis=-1, keepdims=True) + RMS_EPS) * g


N_QKV_BLOCKS = OFF_U // INPROJ_TN
N_MAIN_BLOCKS = OFF_G // INPROJ_TN
N_IN_BLOCKS = IN_WIDTH // INPROJ_TN


def _inproj_kernel(x_ref, g_ref, w_ref, b_ref, seg_ref, nt_ref, zq_ref, u_ref, gt_ref, h_ref):
    j = pl.program_id(1)

    @pl.when(j == 0)
    def _():
        h_ref[...] = _rms(x_ref[...], g_ref[...]).astype(BF16)

    z = jnp.dot(h_ref[...], w_ref[...], preferred_element_type=F32)

    @pl.when(j < N_QKV_BLOCKS)
    def _():
        zz = z * z
        hi = zz.astype(BF16)
        lo = (zz - hi.astype(F32)).astype(BF16)
        ssq = (jnp.dot(hi, seg_ref[...], preferred_element_type=F32)
               + jnp.dot(lo, seg_ref[...], preferred_element_type=F32))
        inv = lax.rsqrt(ssq * (1.0 / HEAD_DIM) + RMS_EPS)
        fac = jnp.where(nt_ref[1:2, :] > 0.0, inv, 1.0) * nt_ref[0:1, :]
        zq_ref[...] = (z * fac).astype(BF16)

    @pl.when((j >= N_QKV_BLOCKS) & (j < N_MAIN_BLOCKS))
    def _():
        u_ref[...] = z

    @pl.when(j >= N_MAIN_BLOCKS)
    def _():
        gt_ref[...] = jax.nn.sigmoid(z + b_ref[...]).astype(BF16)


def _inproj(x2, gain, w_bf, layer, bias, seg, ntab):
    t = x2.shape[0]
    n_u = N_MAIN_BLOCKS - N_QKV_BLOCKS
    return pl.pallas_call(
        _inproj_kernel,
        grid=(t // INPROJ_TM, N_IN_BLOCKS),
        in_specs=[
            pl.BlockSpec((INPROJ_TM, D_MODEL), lambda i, j: (i, 0)),
            pl.BlockSpec((1, D_MODEL), lambda i, j: (0, 0)),
            pl.BlockSpec((None, D_MODEL, INPROJ_TN), lambda i, j: (layer, 0, j)),
            pl.BlockSpec((1, INPROJ_TN), lambda i, j: (0, jnp.maximum(j - N_MAIN_BLOCKS, 0))),
            pl.BlockSpec((INPROJ_TN, INPROJ_TN), lambda i, j: (0, 0)),
            pl.BlockSpec((None, 2, INPROJ_TN), lambda i, j: (jnp.minimum(j, N_QKV_BLOCKS - 1), 0, 0)),
        ],
        out_specs=[
            pl.BlockSpec((INPROJ_TM, INPROJ_TN), lambda i, j: (i, jnp.minimum(j, N_QKV_BLOCKS - 1))),
            pl.BlockSpec((INPROJ_TM, INPROJ_TN), lambda i, j: (i, jnp.clip(j - N_QKV_BLOCKS, 0, n_u - 1))),
            pl.BlockSpec((INPROJ_TM, INPROJ_TN), lambda i, j: (i, jnp.maximum(j - N_MAIN_BLOCKS, 0))),
        ],
        out_shape=[
            jax.ShapeDtypeStruct((t, OFF_U), BF16),
            jax.ShapeDtypeStruct((t, SSM_WIDTH), F32),
            jax.ShapeDtypeStruct((t, 2 * D_MODEL), BF16),
        ],
        scratch_shapes=[pltpu.VMEM((INPROJ_TM, D_MODEL), BF16)],
        compiler_params=pltpu.CompilerParams(
            dimension_semantics=("arbitrary", "arbitrary"), vmem_limit_bytes=VMEM_LIMIT),
        name="inproj",
    )(x2, gain, w_bf, bias, seg, ntab)


def _qk_norm_tables(q_gain, k_gain):
    qrow = jnp.tile(q_gain.astype(F32), INPROJ_TN // HEAD_DIM) * (HEAD_DIM ** -0.5)
    ones = jnp.ones((INPROJ_TN,), F32)
    kvrow = jnp.concatenate([jnp.tile(k_gain.astype(F32), N_KV_HEADS), jnp.ones((KV_WIDTH,), F32)])
    kvmask = jnp.concatenate([jnp.ones((KV_WIDTH,), F32), jnp.zeros((KV_WIDTH,), F32)])
    return jnp.stack([jnp.stack([qrow, ones]), jnp.stack([qrow, ones]), jnp.stack([kvrow, kvmask])])


def _segment_ones():
    r = jnp.arange(INPROJ_TN) // HEAD_DIM
    return (r[:, None] == r[None, :]).astype(BF16)


def _attn_bias_tables():
    t_loc = jnp.arange(BLOCK)[:, None]
    s_loc = jnp.arange(2 * BLOCK)[None, :] - BLOCK
    dist = (t_loc - s_loc).astype(F32)
    valid = (dist >= 0) & (dist < WINDOW)
    slopes = jnp.exp2(-8.0 * jnp.arange(1, N_Q_HEADS + 1, dtype=F32) / N_Q_HEADS)
    bias = -slopes[:, None, None] * dist[None]
    full = jnp.where(valid[None], bias, -jnp.inf)
    first = jnp.where((valid & (s_loc >= 0))[None], bias, -jnp.inf)
    return jnp.stack([first, full])


def _attn_kernel(sink_ref, bias_ref, q_ref, kvc_ref, kvp_ref, o_ref):
    rows = lax.broadcasted_iota(jnp.int32, (GQA_GROUP * BLOCK, 1), 0)
    for kh in range(N_KV_HEADS):
        ks = slice(kh * HEAD_DIM, (kh + 1) * HEAD_DIM)
        vs = slice(KV_WIDTH + kh * HEAD_DIM, KV_WIDTH + (kh + 1) * HEAD_DIM)
        k = jnp.concatenate([kvp_ref[:, ks], kvc_ref[:, ks]], axis=0)
        v = jnp.concatenate([kvp_ref[:, vs], kvc_ref[:, vs]], axis=0)
        h0 = kh * GQA_GROUP
        q = jnp.concatenate([q_ref[:, (h0 + g) * HEAD_DIM:(h0 + g + 1) * HEAD_DIM]
                             for g in range(GQA_GROUP)], axis=0)
        s = lax.dot_general(q, k, (((1,), (1,)), ((), ())), preferred_element_type=F32)
        s = s + bias_ref[h0:h0 + GQA_GROUP].reshape(GQA_GROUP * BLOCK, 2 * BLOCK)
        sink = jnp.full((GQA_GROUP * BLOCK, 1), sink_ref[h0 + GQA_GROUP - 1], F32)
        for g in range(GQA_GROUP - 2, -1, -1):
            sink = jnp.where(rows < (g + 1) * BLOCK, sink_ref[h0 + g], sink)
        m = jnp.maximum(jnp.max(s, axis=-1, keepdims=True), sink)
        p = jnp.exp(s - m)
        denom = jnp.sum(p, axis=-1, keepdims=True) + jnp.exp(sink - m)
        o = jnp.dot(p.astype(BF16), v, preferred_element_type=F32) / denom
        for g in range(GQA_GROUP):
            o_ref[:, (h0 + g) * HEAD_DIM:(h0 + g + 1) * HEAD_DIM] = o[g * BLOCK:(g + 1) * BLOCK].astype(BF16)


def _attention(zq, sinks, bias_tab, batch, seq):
    t = zq.shape[0]
    nb = seq // BLOCK
    kv_col = OFF_K // (2 * KV_WIDTH)
    return pl.pallas_call(
        _attn_kernel,
        grid=(batch, nb),
        in_specs=[
            pl.BlockSpec(memory_space=pltpu.SMEM),
            pl.BlockSpec((None, N_Q_HEADS, BLOCK, 2 * BLOCK), lambda b, n: (jnp.minimum(n, 1), 0, 0, 0)),
            pl.BlockSpec((BLOCK, ATTN_WIDTH), lambda b, n: (b * nb + n, 0)),
            pl.BlockSpec((BLOCK, 2 * KV_WIDTH), lambda b, n: (b * nb + n, kv_col)),
            pl.BlockSpec((BLOCK, 2 * KV_WIDTH), lambda b, n: (b * nb + jnp.maximum(n - 1, 0), kv_col)),
        ],
        out_specs=pl.BlockSpec((BLOCK, ATTN_WIDTH), lambda b, n: (b * nb + n, 0)),
        out_shape=jax.ShapeDtypeStruct((t, ATTN_WIDTH), BF16),
        compiler_params=pltpu.CompilerParams(
            dimension_semantics=("arbitrary", "arbitrary"), vmem_limit_bytes=VMEM_LIMIT),
        name="swa_attention",
    )(sinks, bias_tab, zq, zq, zq)


def _ssm_prep(lam_re, lam_im, log_dt, b_re, b_im, c_re, c_im, d_skip):
    hp = lax.Precision.HIGHEST
    g_, p_, h_ = SSM_GROUPS, SSM_STATE, SSM_GROUP_CH
    lr = lam_re.astype(F32)
    li = lam_im.astype(F32)
    dt = jnp.exp(log_dt.astype(F32))[:, None]
    lrd = lr * dt
    ang = li * dt

    def apow(m):
        m = jnp.asarray(m, F32)[:, None, None]
        mag = jnp.exp(m * lrd[None])
        return mag * jnp.cos(m * ang[None]), mag * jnp.sin(m * ang[None])

    ar, ai = apow([1.0])
    ar, ai = ar[0], ai[0]
    den = lr * lr + li * li
    fr = ((ar - 1.0) * lr + ai * li) / den
    fi = (ai * lr - (ar - 1.0) * li) / den
    br = b_re.astype(F32)
    bi = b_im.astype(F32)
    bbar_r = fr[:, :, None] * br - fi[:, :, None] * bi
    bbar_i = fr[:, :, None] * bi + fi[:, :, None] * br
    pr, pi = apow(list(range(SUB + 1)))
    er = pr[..., None] * bbar_r[None] - pi[..., None] * bbar_i[None]
    ei = pr[..., None] * bbar_i[None] + pi[..., None] * bbar_r[None]
    cr = c_re.astype(F32)
    ci = c_im.astype(F32)
    kern = (jnp.einsum('ghp,tgpk->tghk', cr, er[:SUB], precision=hp)
            - jnp.einsum('ghp,tgpk->tghk', ci, ei[:SUB], precision=hp))
    dmat = d_skip.astype(F32).reshape(g_, h_)[:, :, None] * jnp.eye(h_, dtype=F32)
    kern = kern.at[0].add(dmat)
    tau = jnp.arange(SUB)[None, :] - jnp.arange(SUB)[:, None]
    tm = kern[jnp.clip(tau, 0, SUB - 1)]
    tm = jnp.where((tau >= 0)[:, :, None, None, None], tm, 0.0)
    tmat = tm.transpose(2, 0, 4, 1, 3).reshape(g_, SUB_W, SUB_W)
    bm_r = er[:SUB][::-1].transpose(1, 0, 3, 2).reshape(g_, SUB_W, p_)
    bm_i = ei[:SUB][::-1].transpose(1, 0, 3, 2).reshape(g_, SUB_W, p_)
    tb = jnp.concatenate([tmat, bm_r, bm_i], axis=-1)
    pr1 = pr[1:].transpose(1, 2, 0)[..., None]
    pi1 = pi[1:].transpose(1, 2, 0)[..., None]
    crt = cr.transpose(0, 2, 1)[:, :, None, :]
    cit = ci.transpose(0, 2, 1)[:, :, None, :]
    c_top = (crt * pr1 - cit * pi1).reshape(g_, p_, SUB_W)
    c_bot = (-crt * pi1 - cit * pr1).reshape(g_, p_, SUB_W)
    cmat = jnp.concatenate([c_top, c_bot], axis=1)
    return tb.astype(BF16), cmat.astype(BF16), (lrd, ang)


def _ssm_levels(lrd, ang, n_levels):
    m = jnp.asarray([float(SUB * (1 << k)) for k in range(n_levels)], F32)[:, None, None]
    mag = jnp.exp(m * lrd[None])
    lr_ = (mag * jnp.cos(m * ang[None])).transpose(1, 0, 2)
    li_ = (mag * jnp.sin(m * ang[None])).transpose(1, 0, 2)
    return jnp.concatenate([lr_, lr_], axis=-1), jnp.concatenate([-li_, li_], axis=-1)


def _slab_permutations():
    r = jnp.arange(2 * LANES)
    jj, lane = r // LANES, r % LANES
    grp, ch = lane // SSM_GROUP_CH, lane % SSM_GROUP_CH
    jp = jnp.arange(SUB // 2)[:, None]
    col = grp[None] * SUB_W + (2 * jp + jj[None]) * SSM_GROUP_CH + ch[None]
    perm_in = (col[:, :, None] == jnp.arange(SLAB_W)[None, None, :]).astype(BF16)
    return perm_in, perm_in.transpose(0, 2, 1)


def _ssm_kernel(u_ref, pin_ref, pout_ref, tb_ref, c_ref, ar_ref, ai_ref, y_ref, xs_ref, ys_ref):
    rows = xs_ref.shape[0]
    acc = jnp.zeros((rows, SLAB_W), F32)
    for jp in range(SUB // 2):
        pair = jnp.concatenate([u_ref[pl.ds(2 * jp, rows, stride=SUB), :],
                                u_ref[pl.ds(2 * jp + 1, rows, stride=SUB), :]], axis=1)
        acc = acc + jnp.dot(pair.astype(BF16), pin_ref[jp], preferred_element_type=F32)
    xs_ref[...] = acc.astype(BF16)

    row = lax.broadcasted_iota(jnp.int32, (rows, STATE_W), 0)
    n_levels = rows.bit_length() - 1
    for g in range(SLAB_GROUPS):
        x = xs_ref[:, g * SUB_W:(g + 1) * SUB_W]
        r = jnp.dot(x, tb_ref[g], preferred_element_type=F32)
        yt = r[:, :SUB_W]
        s = r[:, SUB_W:]
        for k in range(n_levels):
            sh = 1 << k
            prev = jnp.where(row >= sh, pltpu.roll(s, sh, axis=0), 0.0)
            swp = pltpu.roll(prev, SSM_STATE, axis=1)
            s = s + ar_ref[g, k:k + 1, :] * prev + ai_ref[g, k:k + 1, :] * swp
        sprev = jnp.where(row >= 1, pltpu.roll(s, 1, axis=0), 0.0)
        y = yt + jnp.dot(sprev.astype(BF16), c_ref[g], preferred_element_type=F32)
        ys_ref[:, g * SUB_W:(g + 1) * SUB_W] = jax.nn.gelu(y).astype(BF16)

    ys = ys_ref[...]
    for ip in range(SUB // 2):
        o = jnp.dot(ys, pout_ref[ip], preferred_element_type=F32)
        y_ref[pl.ds(2 * ip, rows, stride=SUB), :] = o[:, :LANES]
        y_ref[pl.ds(2 * ip + 1, rows, stride=SUB), :] = o[:, LANES:]


def _ssm(uf, perm_in, perm_out, tb, cmat, lev_r, lev_i, batch, seq):
    t = uf.shape[0]
    n_sub = seq // SUB
    n_levels = lev_r.shape[1]
    const3 = lambda b, s: (0, 0, 0)
    slab3 = lambda b, s: (s, 0, 0)
    return pl.pallas_call(
        _ssm_kernel,
        grid=(batch, N_SLABS),
        in_specs=[
            pl.BlockSpec((seq, LANES), lambda b, s: (b, s)),
            pl.BlockSpec((SUB // 2, 2 * LANES, SLAB_W), const3, pipeline_mode=pl.Buffered(1)),
            pl.BlockSpec((SUB // 2, SLAB_W, 2 * LANES), const3, pipeline_mode=pl.Buffered(1)),
            pl.BlockSpec((SLAB_GROUPS, SUB_W, SUB_W + STATE_W), slab3),
            pl.BlockSpec((SLAB_GROUPS, STATE_W, SUB_W), slab3),
            pl.BlockSpec((SLAB_GROUPS, n_levels, STATE_W), slab3),
            pl.BlockSpec((SLAB_GROUPS, n_levels, STATE_W), slab3),
        ],
        out_specs=pl.BlockSpec((seq, LANES), lambda b, s: (b, s)),
        out_shape=jax.ShapeDtypeStruct((t, SSM_WIDTH), F32),
        scratch_shapes=[pltpu.VMEM((n_sub, SLAB_W), BF16), pltpu.VMEM((n_sub, SLAB_W), BF16)],
        compiler_params=pltpu.CompilerParams(
            dimension_semantics=("arbitrary", "arbitrary"), vmem_limit_bytes=VMEM_LIMIT),
        name="s5_scan",
    )(uf, perm_in, perm_out, tb, cmat, lev_r, lev_i)


def _merge_kernel(ya_ref, yg_ref, gt_ref, x_ref, wglu_ref, bglu_ref, wa_ref, ws_ref, wo_ref,
                  gffn_ref, x1_ref, h2_ref):
    yg = yg_ref[...]
    t = jnp.dot(yg.astype(BF16), wglu_ref[...], preferred_element_type=F32) + bglu_ref[...]
    ys = (yg * jax.nn.sigmoid(t)).astype(BF16)
    ma = jnp.dot(ya_ref[...], wa_ref[...], preferred_element_type=F32)
    ms = jnp.dot(ys, ws_ref[...], preferred_element_type=F32)
    merged = gt_ref[:, :D_MODEL].astype(F32) * ma + gt_ref[:, D_MODEL:].astype(F32) * ms
    x1 = x_ref[...] + jnp.dot(merged.astype(BF16), wo_ref[...], preferred_element_type=F32)
    x1_ref[...] = x1
    h2_ref[...] = _rms(x1, gffn_ref[...]).astype(BF16)


def _merge(ya, yg, gates, x2, wglu, bglu, wa, ws, wo, gffn, layer):
    t = x2.shape[0]
    tm = MERGE_TM

    def wspec(rows, cols):
        return pl.BlockSpec((None, rows, cols), lambda i: (layer, 0, 0), pipeline_mode=pl.Buffered(1))

    return pl.pallas_call(
        _merge_kernel,
        grid=(t // tm,),
        in_specs=[
            pl.BlockSpec((tm, ATTN_WIDTH), lambda i: (i, 0)),
            pl.BlockSpec((tm, SSM_WIDTH), lambda i: (i, 0)),
            pl.BlockSpec((tm, 2 * D_MODEL), lambda i: (i, 0)),
            pl.BlockSpec((tm, D_MODEL), lambda i: (i, 0)),
            wspec(SSM_WIDTH, SSM_WIDTH),
            pl.BlockSpec((1, SSM_WIDTH), lambda i: (0, 0)),
            wspec(ATTN_WIDTH, D_MODEL),
            wspec(SSM_WIDTH, D_MODEL),
            wspec(D_MODEL, D_MODEL),
            pl.BlockSpec((1, D_MODEL), lambda i: (0, 0)),
        ],
        out_specs=[
            pl.BlockSpec((tm, D_MODEL), lambda i: (i, 0)),
            pl.BlockSpec((tm, D_MODEL), lambda i: (i, 0)),
        ],
        out_shape=[
            jax.ShapeDtypeStruct((t, D_MODEL), F32),
            jax.ShapeDtypeStruct((t, D_MODEL), BF16),
        ],
        compiler_params=pltpu.CompilerParams(
            dimension_semantics=("arbitrary",), vmem_limit_bytes=VMEM_LIMIT),
        name="merge_out",
    )(ya, yg, gates, x2, wglu, bglu, wa, ws, wo, gffn)


def _ffn_kernel(h_ref, x_ref, wg_ref, wu_ref, wo_ref, o_ref):
    k = pl.program_id(1)

    @pl.when(k == 0)
    def _():
        o_ref[...] = x_ref[...]

    h = h_ref[...]
    g = jnp.dot(h, wg_ref[...], preferred_element_type=F32)
    u = jnp.dot(h, wu_ref[...], preferred_element_type=F32)
    act = (jax.nn.silu(g) * u).astype(BF16)
    o_ref[...] += jnp.dot(act, wo_ref[...], preferred_element_type=F32)


def _ffn(h2, x1, w_in, w_out, layer):
    t = x1.shape[0]
    nk = D_FF // FFN_TF
    return pl.pallas_call(
        _ffn_kernel,
        grid=(t // FFN_TM, nk),
        in_specs=[
            pl.BlockSpec((FFN_TM, D_MODEL), lambda i, k: (i, 0), pipeline_mode=pl.Buffered(1)),
            pl.BlockSpec((FFN_TM, D_MODEL), lambda i, k: (i, 0), pipeline_mode=pl.Buffered(1)),
            pl.BlockSpec((None, D_MODEL, FFN_TF), lambda i, k: (layer, 0, k)),
            pl.BlockSpec((None, D_MODEL, FFN_TF), lambda i, k: (layer, 0, nk + k)),
            pl.BlockSpec((None, FFN_TF, D_MODEL), lambda i, k: (layer, k, 0)),
        ],
        out_specs=pl.BlockSpec((FFN_TM, D_MODEL), lambda i, k: (i, 0)),
        out_shape=jax.ShapeDtypeStruct((t, D_MODEL), F32),
        compiler_params=pltpu.CompilerParams(
            dimension_semantics=("arbitrary", "arbitrary"), vmem_limit_bytes=VMEM_LIMIT),
        name="swiglu_ffn",
    )(h2, x1, w_in, w_in, w_out)


def kernel(x, norm_mix_g, w_in, gate_bias, q_norm_g, k_norm_g, attn_sinks, ssm_lambda_re, ssm_lambda_im, ssm_log_dt, ssm_b_re, ssm_b_im, ssm_c_re, ssm_c_im, ssm_d, ssm_glu_w, ssm_glu_b, w_attn_branch, w_ssm_branch, w_out, norm_ffn_g, w_ffn_in, w_ffn_out):
    batch, seq, _ = x.shape
    t = batch * seq
    n_levels = (seq // SUB).bit_length() - 1
    x2 = x.reshape(t, D_MODEL).astype(F32)
    w_in_bf = w_in.astype(BF16)
    wglu_bf = ssm_glu_w.astype(BF16)
    wa_bf = w_attn_branch.astype(BF16)
    ws_bf = w_ssm_branch.astype(BF16)
    wo_bf = w_out.astype(BF16)
    wfi_bf = w_ffn_in.astype(BF16)
    wfo_bf = w_ffn_out.astype(BF16)
    seg = _segment_ones()
    bias_tab = _attn_bias_tables()
    perm_in, perm_out = _slab_permutations()
    for l in range(DEPTH):
        zq, uf, gates = _inproj(x2, norm_mix_g[l].reshape(1, D_MODEL).astype(F32), w_in_bf, l,
                                gate_bias[l].reshape(1, 2 * D_MODEL).astype(F32), seg,
                                _qk_norm_tables(q_norm_g[l], k_norm_g[l]))
        ya = _attention(zq, attn_sinks[l].astype(F32), bias_tab, batch, seq)
        tb, cmat, (lrd, ang) = _ssm_prep(ssm_lambda_re[l], ssm_lambda_im[l], ssm_log_dt[l],
                                         ssm_b_re[l], ssm_b_im[l], ssm_c_re[l], ssm_c_im[l], ssm_d[l])
        lev_r, lev_i = _ssm_levels(lrd, ang, n_levels)
        yg = _ssm(uf, perm_in, perm_out, tb, cmat, lev_r, lev_i, batch, seq)
        x1, h2 = _merge(ya, yg, gates, x2, wglu_bf, ssm_glu_b[l].reshape(1, SSM_WIDTH).astype(F32),
                        wa_bf, ws_bf, wo_bf, norm_ffn_g[l].reshape(1, D_MODEL).astype(F32), l)
        x2 = _ffn(h2, x1, wfi_bf, wfo_bf, l)
    return x2.reshape(batch, seq, D_MODEL).astype(x.dtype)
```

```python
from functools import partial

import jax
import jax.numpy as jnp
from jax import lax
from jax.experimental import pallas as pl
from jax.experimental.pallas import tpu as pltpu

D_MODEL = 2048
DEPTH = 2
HEAD_DIM = 64
N_Q_HEADS = 16
N_KV_HEADS = 4
GQA_GROUP = N_Q_HEADS // N_KV_HEADS
ATTN_WIDTH = N_Q_HEADS * HEAD_DIM
KV_WIDTH = N_KV_HEADS * HEAD_DIM
WINDOW = 128
BLOCK = 128
SSM_WIDTH = D_MODEL // 2
SSM_GROUP_CH = 16
SSM_GROUPS = SSM_WIDTH // SSM_GROUP_CH
SSM_STATE = 64
D_FF = -(-8 * D_MODEL // (3 * 256)) * 256
OFF_K = ATTN_WIDTH
OFF_V = OFF_K + KV_WIDTH
OFF_U = OFF_V + KV_WIDTH
OFF_G = OFF_U + SSM_WIDTH
IN_WIDTH = OFF_G + 2 * D_MODEL
RMS_EPS = 1e-6

F32 = jnp.float32
BF16 = jnp.bfloat16

LANES = 128
SUB = 16
SUB_W = SUB * SSM_GROUP_CH
STATE_W = 2 * SSM_STATE
SLAB_GROUPS = LANES // SSM_GROUP_CH
N_SLABS = SSM_GROUPS // SLAB_GROUPS
SLAB_W = SLAB_GROUPS * SUB_W

VMEM_LIMIT = 56 * 1024 * 1024

INPROJ_TM = 1024
INPROJ_TN = 512
INPROJ_SUB = 256
INPROJ_KC = 512
MERGE_TM = 256
FFN_TM = 1024
FFN_TF = 512
FFN_SUB = 512


def _rms(x, g):
    return x * lax.rsqrt(jnp.mean(x * x, axis=-1, keepdims=True) + RMS_EPS) * g


N_QKV_BLOCKS = OFF_U // INPROJ_TN
N_MAIN_BLOCKS = OFF_G // INPROJ_TN
N_IN_BLOCKS = IN_WIDTH // INPROJ_TN


def _sigmoid(x):
    return 0.5 * jnp.tanh(0.5 * x) + 0.5


def _inproj_kernel(x_ref, g_ref, w_ref, b_ref, seg_ref, nt_ref, zq_ref, u_ref, gt_ref, h_ref, wb_ref):
    j = pl.program_id(1)

    @pl.when(j == 0)
    def _():
        h_ref[...] = _rms(x_ref[...], g_ref[...]).astype(BF16)

    def row_tiles(epilogue):
        rs = pl.ds(0, INPROJ_SUB)
        z = None
        for c in range(D_MODEL // INPROJ_KC):
            ks = pl.ds(c * INPROJ_KC, INPROJ_KC)
            wc = w_ref[ks, :].astype(BF16)
            wb_ref[ks, :] = wc
            part = jnp.dot(h_ref[rs, ks], wc, preferred_element_type=F32)
            z = part if z is None else z + part
        epilogue(rs, z)
        for r in range(1, INPROJ_TM // INPROJ_SUB):
            rs = pl.ds(r * INPROJ_SUB, INPROJ_SUB)
            epilogue(rs, jnp.dot(h_ref[rs, :], wb_ref[...], preferred_element_type=F32))

    @pl.when(j < N_QKV_BLOCKS)
    def _():
        def qk_norm(rs, z):
            zz = z * z
            hi = zz.astype(BF16)
            lo = (zz - hi.astype(F32)).astype(BF16)
            ssq = (jnp.dot(hi, seg_ref[...], preferred_element_type=F32)
                   + jnp.dot(lo, seg_ref[...], preferred_element_type=F32))
            inv = lax.rsqrt(ssq * (1.0 / HEAD_DIM) + RMS_EPS)
            fac = jnp.where(nt_ref[1:2, :] > 0.0, inv, 1.0) * nt_ref[0:1, :]
            zq_ref[rs, :] = (z * fac).astype(BF16)
        row_tiles(qk_norm)

    @pl.when((j >= N_QKV_BLOCKS) & (j < N_MAIN_BLOCKS))
    def _():
        def store_u(rs, z):
            u_ref[rs, :] = z
        row_tiles(store_u)

    @pl.when(j >= N_MAIN_BLOCKS)
    def _():
        def gate(rs, z):
            gt_ref[rs, :] = _sigmoid(z + b_ref[...]).astype(BF16)
        row_tiles(gate)


def _inproj(x2, gain, w, layer, bias, seg, ntab):
    t = x2.shape[0]
    n_u = N_MAIN_BLOCKS - N_QKV_BLOCKS
    return pl.pallas_call(
        _inproj_kernel,
        grid=(t // INPROJ_TM, N_IN_BLOCKS),
        in_specs=[
            pl.BlockSpec((INPROJ_TM, D_MODEL), lambda i, j: (i, 0)),
            pl.BlockSpec((1, D_MODEL), lambda i, j: (0, 0)),
            pl.BlockSpec((None, D_MODEL, INPROJ_TN), lambda i, j: (layer, 0, j)),
            pl.BlockSpec((1, INPROJ_TN), lambda i, j: (0, jnp.maximum(j - N_MAIN_BLOCKS, 0))),
            pl.BlockSpec((INPROJ_TN, INPROJ_TN), lambda i, j: (0, 0)),
            pl.BlockSpec((None, 2, INPROJ_TN), lambda i, j: (jnp.minimum(j, N_QKV_BLOCKS - 1), 0, 0)),
        ],
        out_specs=[
            pl.BlockSpec((INPROJ_TM, INPROJ_TN), lambda i, j: (i, jnp.minimum(j, N_QKV_BLOCKS - 1))),
            pl.BlockSpec((INPROJ_TM, INPROJ_TN), lambda i, j: (i, jnp.clip(j - N_QKV_BLOCKS, 0, n_u - 1))),
            pl.BlockSpec((INPROJ_TM, INPROJ_TN), lambda i, j: (i, jnp.maximum(j - N_MAIN_BLOCKS, 0))),
        ],
        out_shape=[
            jax.ShapeDtypeStruct((t, OFF_U), BF16),
            jax.ShapeDtypeStruct((t, SSM_WIDTH), F32),
            jax.ShapeDtypeStruct((t, 2 * D_MODEL), BF16),
        ],
        scratch_shapes=[pltpu.VMEM((INPROJ_TM, D_MODEL), BF16), pltpu.VMEM((D_MODEL, INPROJ_TN), BF16)],
        compiler_params=pltpu.CompilerParams(
            dimension_semantics=("arbitrary", "arbitrary"), vmem_limit_bytes=VMEM_LIMIT),
        name="inproj",
    )(x2, gain, w, bias, seg, ntab)


def _qk_norm_tables(q_gain, k_gain):
    qrow = jnp.tile(q_gain.astype(F32), INPROJ_TN // HEAD_DIM) * (HEAD_DIM ** -0.5)
    ones = jnp.ones((INPROJ_TN,), F32)
    kvrow = jnp.concatenate([jnp.tile(k_gain.astype(F32), N_KV_HEADS), jnp.ones((KV_WIDTH,), F32)])
    kvmask = jnp.concatenate([jnp.ones((KV_WIDTH,), F32), jnp.zeros((KV_WIDTH,), F32)])
    return jnp.stack([jnp.stack([qrow, ones]), jnp.stack([qrow, ones]), jnp.stack([kvrow, kvmask])])


def _segment_ones():
    r = jnp.arange(INPROJ_TN) // HEAD_DIM
    return (r[:, None] == r[None, :]).astype(BF16)


def _attn_bias_tables():
    t_loc = jnp.arange(BLOCK)[:, None]
    s_loc = jnp.arange(2 * BLOCK)[None, :] - BLOCK
    dist = (t_loc - s_loc).astype(F32)
    valid = (dist >= 0) & (dist < WINDOW)
    slopes = jnp.exp2(-8.0 * jnp.arange(1, N_Q_HEADS + 1, dtype=F32) / N_Q_HEADS)
    bias = -slopes[:, None, None] * dist[None]
    full = jnp.where(valid[None], bias, -jnp.inf)
    first = jnp.where((valid & (s_loc >= 0))[None], bias, -jnp.inf)
    return jnp.stack([first, full])


def _attn_kernel(sink_ref, bias_ref, q_ref, kvc_ref, kvp_ref, o_ref):
    kv_rows = 2 * BLOCK
    left_kv = lax.broadcasted_iota(jnp.int32, (kv_rows, LANES), 1) < HEAD_DIM
    left_q = lax.broadcasted_iota(jnp.int32, (BLOCK, LANES), 1) < HEAD_DIM
    zeros = jnp.zeros((kv_rows, LANES), BF16)
    ones_l = jnp.where(left_kv, 1.0, 0.0).astype(BF16)
    ones_r = jnp.where(left_kv, 0.0, 1.0).astype(BF16)
    contract_lanes = (((1,), (1,)), ((), ()))

    def slab(off):
        a = jnp.concatenate([kvp_ref[:, off:off + LANES], kvc_ref[:, off:off + LANES]], axis=0)
        return a, pltpu.roll(a.astype(F32), HEAD_DIM, axis=1).astype(BF16)

    for c in range(N_KV_HEADS // 2):
        k_slabs = slab(c * LANES)
        v_slabs = slab(KV_WIDTH + c * LANES)
        for side in range(2):
            kh = 2 * c + side
            k_l = jnp.where(left_kv, k_slabs[side], zeros)
            k_r = jnp.where(left_kv, zeros, k_slabs[1 - side])
            v_l = jnp.where(left_kv, v_slabs[side], zeros)
            v_r = jnp.where(left_kv, zeros, v_slabs[1 - side])
            kk = jnp.concatenate([k_l, k_r], axis=0)
            vv = jnp.concatenate([jnp.concatenate([v_l, ones_l], axis=1),
                                  jnp.concatenate([v_r, ones_r], axis=1)], axis=0)
            for pair in range(GQA_GROUP // 2):
                e = kh * GQA_GROUP + 2 * pair
                cols = slice(e * HEAD_DIM, (e + 2) * HEAD_DIM)
                s2 = lax.dot_general(q_ref[:, cols], kk, contract_lanes, preferred_element_type=F32)
                ps, ds = [], []
                for t in range(2):
                    s = s2[:, t * kv_rows:(t + 1) * kv_rows] + bias_ref[e + t]
                    sink = sink_ref[e + t]
                    m = jnp.maximum(jnp.max(s, axis=-1, keepdims=True), sink)
                    ps.append(jnp.exp(s - m).astype(BF16))
                    ds.append(jnp.exp(sink - m))
                r = jnp.dot(jnp.concatenate(ps, axis=1), vv, preferred_element_type=F32)
                denom = r[:, LANES:] + jnp.where(left_q, ds[0], ds[1])
                o_ref[:, cols] = (r[:, :LANES] / denom).astype(BF16)


def _attention(zq, sinks, bias_tab, batch, seq):
    t = zq.shape[0]
    nb = seq // BLOCK
    kv_col = OFF_K // (2 * KV_WIDTH)
    return pl.pallas_call(
        _attn_kernel,
        grid=(batch, nb),
        in_specs=[
            pl.BlockSpec(memory_space=pltpu.SMEM),
            pl.BlockSpec((None, N_Q_HEADS, BLOCK, 2 * BLOCK), lambda b, n: (jnp.minimum(n, 1), 0, 0, 0)),
            pl.BlockSpec((BLOCK, ATTN_WIDTH), lambda b, n: (b * nb + n, 0)),
            pl.BlockSpec((BLOCK, 2 * KV_WIDTH), lambda b, n: (b * nb + n, kv_col)),
            pl.BlockSpec((BLOCK, 2 * KV_WIDTH), lambda b, n: (b * nb + jnp.maximum(n - 1, 0), kv_col)),
        ],
        out_specs=pl.BlockSpec((BLOCK, ATTN_WIDTH), lambda b, n: (b * nb + n, 0)),
        out_shape=jax.ShapeDtypeStruct((t, ATTN_WIDTH), BF16),
        compiler_params=pltpu.CompilerParams(
            dimension_semantics=("arbitrary", "arbitrary"), vmem_limit_bytes=VMEM_LIMIT),
        name="swa_attention",
    )(sinks, bias_tab, zq, zq, zq)


def _ssm_prep(lam_re, lam_im, log_dt, b_re, b_im, c_re, c_im, d_skip):
    hp = lax.Precision.HIGHEST
    g_, p_, h_ = SSM_GROUPS, SSM_STATE, SSM_GROUP_CH
    lr = lam_re.astype(F32)
    li = lam_im.astype(F32)
    dt = jnp.exp(log_dt.astype(F32))[:, None]
    lrd = lr * dt
    ang = li * dt

    def apow(m):
        m = jnp.asarray(m, F32)[:, None, None]
        mag = jnp.exp(m * lrd[None])
        return mag * jnp.cos(m * ang[None]), mag * jnp.sin(m * ang[None])

    ar, ai = apow([1.0])
    ar, ai = ar[0], ai[0]
    den = lr * lr + li * li
    fr = ((ar - 1.0) * lr + ai * li) / den
    fi = (ai * lr - (ar - 1.0) * li) / den
    br = b_re.astype(F32)
    bi = b_im.astype(F32)
    bbar_r = fr[:, :, None] * br - fi[:, :, None] * bi
    bbar_i = fr[:, :, None] * bi + fi[:, :, None] * br
    pr, pi = apow(list(range(SUB + 1)))
    er = pr[..., None] * bbar_r[None] - pi[..., None] * bbar_i[None]
    ei = pr[..., None] * bbar_i[None] + pi[..., None] * bbar_r[None]
    cr = c_re.astype(F32)
    ci = c_im.astype(F32)
    kern = (jnp.einsum('ghp,tgpk->tghk', cr, er[:SUB], precision=hp)
            - jnp.einsum('ghp,tgpk->tghk', ci, ei[:SUB], precision=hp))
    dmat = d_skip.astype(F32).reshape(g_, h_)[:, :, None] * jnp.eye(h_, dtype=F32)
    kern = kern.at[0].add(dmat)
    tau = jnp.arange(SUB)[None, :] - jnp.arange(SUB)[:, None]
    tm = kern[jnp.clip(tau, 0, SUB - 1)]
    tm = jnp.where((tau >= 0)[:, :, None, None, None], tm, 0.0)
    tmat = tm.transpose(2, 0, 4, 1, 3).reshape(g_, SUB_W, SUB_W)
    bm_r = er[:SUB][::-1].transpose(1, 0, 3, 2).reshape(g_, SUB_W, p_)
    bm_i = ei[:SUB][::-1].transpose(1, 0, 3, 2).reshape(g_, SUB_W, p_)
    tb = jnp.concatenate([tmat, bm_r, bm_i, bm_i, bm_r], axis=-1)
    pr1 = pr[1:].transpose(1, 2, 0)[..., None]
    pi1 = pi[1:].transpose(1, 2, 0)[..., None]
    crt = cr.transpose(0, 2, 1)[:, :, None, :]
    cit = ci.transpose(0, 2, 1)[:, :, None, :]
    c_top = (crt * pr1 - cit * pi1).reshape(g_, p_, SUB_W)
    c_bot = (-crt * pi1 - cit * pr1).reshape(g_, p_, SUB_W)
    cmat = jnp.concatenate([c_top, c_bot], axis=1)
    return tb.astype(BF16), cmat.astype(BF16), (lrd, ang)


def _ssm_levels(lrd, ang, n_levels):
    m = jnp.asarray([float(SUB * (1 << k)) for k in range(n_levels)], F32)[:, None, None]
    mag = jnp.exp(m * lrd[None])
    lr_ = (mag * jnp.cos(m * ang[None])).transpose(1, 0, 2)
    li_ = (mag * jnp.sin(m * ang[None])).transpose(1, 0, 2)
    return jnp.concatenate([lr_, lr_], axis=-1), jnp.concatenate([-li_, li_], axis=-1)


def _slab_permutations():
    r = jnp.arange(2 * LANES)
    jj, lane = r // LANES, r % LANES
    grp, ch = lane // SSM_GROUP_CH, lane % SSM_GROUP_CH
    jp = jnp.arange(SUB // 2)[:, None]
    col = grp[None] * SUB_W + (2 * jp + jj[None]) * SSM_GROUP_CH + ch[None]
    perm_in = (col[:, :, None] == jnp.arange(SLAB_W)[None, None, :]).astype(BF16)
    return perm_in, perm_in.transpose(0, 2, 1)


def _ssm_kernel(u_ref, pin_ref, pout_ref, tb_ref, c_ref, ar_ref, ai_ref, y_ref, xs_ref, ys_ref):
    rows = xs_ref.shape[0]
    acc = jnp.zeros((rows, SLAB_W), F32)
    for jp in range(SUB // 2):
        pair = jnp.concatenate([u_ref[pl.ds(2 * jp, rows, stride=SUB), :],
                                u_ref[pl.ds(2 * jp + 1, rows, stride=SUB), :]], axis=1)
        acc = acc + jnp.dot(pair.astype(BF16), pin_ref[jp], preferred_element_type=F32)
    xs_ref[...] = acc.astype(BF16)

    n_levels = rows.bit_length() - 1

    def shifted(a, sh):
        return jnp.concatenate([jnp.zeros((sh, a.shape[1]), a.dtype), a[:rows - sh]], axis=0)

    for g in range(SLAB_GROUPS):
        x = xs_ref[:, g * SUB_W:(g + 1) * SUB_W]
        r = jnp.dot(x, tb_ref[g], preferred_element_type=F32)
        yt = r[:, :SUB_W]
        s = r[:, SUB_W:SUB_W + STATE_W]
        w = r[:, SUB_W + STATE_W:]
        for k in range(n_levels):
            sh = 1 << k
            ar = ar_ref[g, k:k + 1, :]
            ai = ai_ref[g, k:k + 1, :]
            ps, pw = shifted(s, sh), shifted(w, sh)
            s, w = s + ar * ps + ai * pw, (w + ar * pw - ai * ps if k + 1 < n_levels else None)
        y = yt + jnp.dot(shifted(s, 1).astype(BF16), c_ref[g], preferred_element_type=F32)
        ys_ref[:, g * SUB_W:(g + 1) * SUB_W] = jax.nn.gelu(y).astype(BF16)

    ys = ys_ref[...]
    for ip in range(SUB // 2):
        o = jnp.dot(ys, pout_ref[ip], preferred_element_type=F32)
        y_ref[pl.ds(2 * ip, rows, stride=SUB), :] = o[:, :LANES]
        y_ref[pl.ds(2 * ip + 1, rows, stride=SUB), :] = o[:, LANES:]


def _ssm(uf, perm_in, perm_out, tb, cmat, lev_r, lev_i, batch, seq):
    t = uf.shape[0]
    n_sub = seq // SUB
    n_levels = lev_r.shape[1]
    const3 = lambda b, s: (0, 0, 0)
    slab3 = lambda b, s: (s, 0, 0)
    return pl.pallas_call(
        _ssm_kernel,
        grid=(batch, N_SLABS),
        in_specs=[
            pl.BlockSpec((seq, LANES), lambda b, s: (b, s)),
            pl.BlockSpec((SUB // 2, 2 * LANES, SLAB_W), const3, pipeline_mode=pl.Buffered(1)),
            pl.BlockSpec((SUB // 2, SLAB_W, 2 * LANES), const3, pipeline_mode=pl.Buffered(1)),
            pl.BlockSpec((SLAB_GROUPS, SUB_W, SUB_W + 2 * STATE_W), slab3),
            pl.BlockSpec((SLAB_GROUPS, STATE_W, SUB_W), slab3),
            pl.BlockSpec((SLAB_GROUPS, n_levels, STATE_W), slab3),
            pl.BlockSpec((SLAB_GROUPS, n_levels, STATE_W), slab3),
        ],
        out_specs=pl.BlockSpec((seq, LANES), lambda b, s: (b, s)),
        out_shape=jax.ShapeDtypeStruct((t, SSM_WIDTH), F32),
        scratch_shapes=[pltpu.VMEM((n_sub, SLAB_W), BF16), pltpu.VMEM((n_sub, SLAB_W), BF16)],
        compiler_params=pltpu.CompilerParams(
            dimension_semantics=("arbitrary", "arbitrary"), vmem_limit_bytes=VMEM_LIMIT),
        name="s5_scan",
    )(uf, perm_in, perm_out, tb, cmat, lev_r, lev_i)


def _merge_kernel(ya_ref, yg_ref, gt_ref, x_ref, wglu_ref, bglu_ref, wa_ref, ws_ref, wo_ref,
                  gffn_ref, x1_ref, h2_ref):
    yg = yg_ref[...]
    t = jnp.dot(yg.astype(BF16), wglu_ref[...], preferred_element_type=F32) + bglu_ref[...]
    ys = (yg * _sigmoid(t)).astype(BF16)
    ma = jnp.dot(ya_ref[...], wa_ref[...], preferred_element_type=F32)
    ms = jnp.dot(ys, ws_ref[...], preferred_element_type=F32)
    merged = gt_ref[:, :D_MODEL].astype(F32) * ma + gt_ref[:, D_MODEL:].astype(F32) * ms
    x1 = x_ref[...] + jnp.dot(merged.astype(BF16), wo_ref[...], preferred_element_type=F32)
    x1_ref[...] = x1
    h2_ref[...] = _rms(x1, gffn_ref[...]).astype(BF16)


def _merge(ya, yg, gates, x2, wglu, bglu, wa, ws, wo, gffn, layer):
    t = x2.shape[0]
    tm = MERGE_TM

    def wspec(rows, cols):
        return pl.BlockSpec((None, rows, cols), lambda i: (layer, 0, 0), pipeline_mode=pl.Buffered(1))

    return pl.pallas_call(
        _merge_kernel,
        grid=(t // tm,),
        in_specs=[
            pl.BlockSpec((tm, ATTN_WIDTH), lambda i: (i, 0)),
            pl.BlockSpec((tm, SSM_WIDTH), lambda i: (i, 0)),
            pl.BlockSpec((tm, 2 * D_MODEL), lambda i: (i, 0)),
            pl.BlockSpec((tm, D_MODEL), lambda i: (i, 0)),
            wspec(SSM_WIDTH, SSM_WIDTH),
            pl.BlockSpec((1, SSM_WIDTH), lambda i: (0, 0)),
            wspec(ATTN_WIDTH, D_MODEL),
            wspec(SSM_WIDTH, D_MODEL),
            wspec(D_MODEL, D_MODEL),
            pl.BlockSpec((1, D_MODEL), lambda i: (0, 0)),
        ],
        out_specs=[
            pl.BlockSpec((tm, D_MODEL), lambda i: (i, 0)),
            pl.BlockSpec((tm, D_MODEL), lambda i: (i, 0)),
        ],
        out_shape=[
            jax.ShapeDtypeStruct((t, D_MODEL), F32),
            jax.ShapeDtypeStruct((t, D_MODEL), BF16),
        ],
        compiler_params=pltpu.CompilerParams(
            dimension_semantics=("arbitrary",), vmem_limit_bytes=VMEM_LIMIT),
        name="merge_out",
    )(ya, yg, gates, x2, wglu, bglu, wa, ws, wo, gffn)


def _ffn_kernel(h_ref, x_ref, wg_ref, wu_ref, wo_ref, o_ref):
    k = pl.program_id(1)

    @pl.when(k == 0)
    def _():
        o_ref[...] = x_ref[...]

    for r in range(FFN_TM // FFN_SUB):
        rs = pl.ds(r * FFN_SUB, FFN_SUB)
        h = h_ref[rs, :]
        g = jnp.dot(h, wg_ref[...], preferred_element_type=F32)
        u = jnp.dot(h, wu_ref[...], preferred_element_type=F32)
        act = (g * _sigmoid(g) * u).astype(BF16)
        o_ref[rs, :] += jnp.dot(act, wo_ref[...], preferred_element_type=F32)


def _ffn(h2, x1, w_in, w_out, layer):
    t = x1.shape[0]
    nk = D_FF // FFN_TF
    return pl.pallas_call(
        _ffn_kernel,
        grid=(t // FFN_TM, nk),
        in_specs=[
            pl.BlockSpec((FFN_TM, D_MODEL), lambda i, k: (i, 0), pipeline_mode=pl.Buffered(1)),
            pl.BlockSpec((FFN_TM, D_MODEL), lambda i, k: (i, 0), pipeline_mode=pl.Buffered(1)),
            pl.BlockSpec((None, D_MODEL, FFN_TF), lambda i, k: (layer, 0, k)),
            pl.BlockSpec((None, D_MODEL, FFN_TF), lambda i, k: (layer, 0, nk + k)),
            pl.BlockSpec((None, FFN_TF, D_MODEL), lambda i, k: (layer, k, 0)),
        ],
        out_specs=pl.BlockSpec((FFN_TM, D_MODEL), lambda i, k: (i, 0)),
        out_shape=jax.ShapeDtypeStruct((t, D_MODEL), F32),
        compiler_params=pltpu.CompilerParams(
            dimension_semantics=("arbitrary", "arbitrary"), vmem_limit_bytes=VMEM_LIMIT),
        name="swiglu_ffn",
    )(h2, x1, w_in, w_in, w_out)


def kernel(x, norm_mix_g, w_in, gate_bias, q_norm_g, k_norm_g, attn_sinks, ssm_lambda_re, ssm_lambda_im, ssm_log_dt, ssm_b_re, ssm_b_im, ssm_c_re, ssm_c_im, ssm_d, ssm_glu_w, ssm_glu_b, w_attn_branch, w_ssm_branch, w_out, norm_ffn_g, w_ffn_in, w_ffn_out):
    batch, seq, _ = x.shape
    t = batch * seq
    n_levels = (seq // SUB).bit_length() - 1
    x2 = x.reshape(t, D_MODEL).astype(F32)
    w_in_f = w_in.astype(F32)
    wglu_bf = ssm_glu_w.astype(BF16)
    wa_bf = w_attn_branch.astype(BF16)
    ws_bf = w_ssm_branch.astype(BF16)
    wo_bf = w_out.astype(BF16)
    wfi_bf = w_ffn_in.astype(BF16)
    wfo_bf = w_ffn_out.astype(BF16)
    seg = _segment_ones()
    bias_tab = _attn_bias_tables()
    perm_in, perm_out = _slab_permutations()
    for l in range(DEPTH):
        zq, uf, gates = _inproj(x2, norm_mix_g[l].reshape(1, D_MODEL).astype(F32), w_in_f, l,
                                gate_bias[l].reshape(1, 2 * D_MODEL).astype(F32), seg,
                                _qk_norm_tables(q_norm_g[l], k_norm_g[l]))
        ya = _attention(zq, attn_sinks[l].astype(F32), bias_tab, batch, seq)
        tb, cmat, (lrd, ang) = _ssm_prep(ssm_lambda_re[l], ssm_lambda_im[l], ssm_log_dt[l],
                                         ssm_b_re[l], ssm_b_im[l], ssm_c_re[l], ssm_c_im[l], ssm_d[l])
        lev_r, lev_i = _ssm_levels(lrd, ang, n_levels)
        yg = _ssm(uf, perm_in, perm_out, tb, cmat, lev_r, lev_i, batch, seq)
        x1, h2 = _merge(ya, yg, gates, x2, wglu_bf, ssm_glu_b[l].reshape(1, SSM_WIDTH).astype(F32),
                        wa_bf, ws_bf, wo_bf, norm_ffn_g[l].reshape(1, D_MODEL).astype(F32), l)
        x2 = _ffn(h2, x1, wfi_bf, wfo_bf, l)
    return x2.reshape(batch, seq, D_MODEL).astype(x.dtype)
```

```python
import jax
import jax.numpy as jnp
from jax import lax
from jax.experimental import pallas as pl
from jax.experimental.pallas import tpu as pltpu

D_MODEL = 2048
DEPTH = 2
HEAD_DIM = 64
N_Q_HEADS = 16
N_KV_HEADS = 4
GQA_GROUP = N_Q_HEADS // N_KV_HEADS
ATTN_WIDTH = N_Q_HEADS * HEAD_DIM
KV_WIDTH = N_KV_HEADS * HEAD_DIM
WINDOW = 128
BLOCK = 128
SSM_WIDTH = D_MODEL // 2
SSM_GROUP_CH = 16
SSM_GROUPS = SSM_WIDTH // SSM_GROUP_CH
SSM_STATE = 64
D_FF = -(-8 * D_MODEL // (3 * 256)) * 256
OFF_K = ATTN_WIDTH
OFF_V = OFF_K + KV_WIDTH
OFF_U = OFF_V + KV_WIDTH
OFF_G = OFF_U + SSM_WIDTH
IN_WIDTH = OFF_G + 2 * D_MODEL
RMS_EPS = 1e-6

F32 = jnp.float32
BF16 = jnp.bfloat16
HIGHEST = lax.Precision.HIGHEST

LANES = 128
SUB = 16
SUB_W = SUB * SSM_GROUP_CH
STATE_W = 2 * SSM_STATE
TB_W = SUB_W + 2 * STATE_W
SLAB_GROUPS = LANES // SSM_GROUP_CH
N_SLABS = SSM_GROUPS // SLAB_GROUPS
SLAB_W = SLAB_GROUPS * SUB_W

VMEM_LIMIT = 56 * 1024 * 1024

INPROJ_TM = 1024
INPROJ_TN = 512
INPROJ_SUB = 256
MERGE_TM = 256
FFN_TM = 1024
FFN_TF = 512
FFN_SUB = 512


def _rms(x, g):
    return x * lax.rsqrt(jnp.mean(x * x, axis=-1, keepdims=True) + RMS_EPS) * g


def _sigmoid(x):
    return 0.5 * jnp.tanh(0.5 * x) + 0.5


def _column_blocks(w, width):
    l, k, n = w.shape
    return w.astype(BF16).reshape(l, k, n // width, width).transpose(0, 2, 1, 3)


N_QKV_BLOCKS = OFF_U // INPROJ_TN
N_MAIN_BLOCKS = OFF_G // INPROJ_TN
N_IN_BLOCKS = IN_WIDTH // INPROJ_TN


def _inproj_kernel(x_ref, g_ref, w_ref, b_ref, seg_ref, nt_ref, zq_ref, u_ref, gt_ref, h_ref):
    j = pl.program_id(1)

    @pl.when(j == 0)
    def _():
        h_ref[...] = _rms(x_ref[...], g_ref[...]).astype(BF16)

    def row_tiles(epilogue):
        for r in range(INPROJ_TM // INPROJ_SUB):
            rs = pl.ds(r * INPROJ_SUB, INPROJ_SUB)
            epilogue(rs, jnp.dot(h_ref[rs, :], w_ref[...], preferred_element_type=F32))

    @pl.when(j < N_QKV_BLOCKS)
    def _():
        def qk_norm(rs, z):
            zz = z * z
            hi = zz.astype(BF16)
            lo = (zz - hi.astype(F32)).astype(BF16)
            ssq = (jnp.dot(hi, seg_ref[...], preferred_element_type=F32)
                   + jnp.dot(lo, seg_ref[...], preferred_element_type=F32))
            inv = lax.rsqrt(ssq * (1.0 / HEAD_DIM) + RMS_EPS)
            fac = jnp.where(nt_ref[1:2, :] > 0.0, inv, 1.0) * nt_ref[0:1, :]
            zq_ref[rs, :] = (z * fac).astype(BF16)
        row_tiles(qk_norm)

    @pl.when((j >= N_QKV_BLOCKS) & (j < N_MAIN_BLOCKS))
    def _():
        def store_u(rs, z):
            u_ref[rs, :] = z
        row_tiles(store_u)

    @pl.when(j >= N_MAIN_BLOCKS)
    def _():
        def gate(rs, z):
            gt_ref[rs, :] = _sigmoid(z + b_ref[...]).astype(BF16)
        row_tiles(gate)


def _inproj(x2, gain, w_blk, layer, bias, seg, ntab):
    t = x2.shape[0]
    n_u = N_MAIN_BLOCKS - N_QKV_BLOCKS
    return pl.pallas_call(
        _inproj_kernel,
        grid=(t // INPROJ_TM, N_IN_BLOCKS),
        in_specs=[
            pl.BlockSpec((INPROJ_TM, D_MODEL), lambda i, j: (i, 0)),
            pl.BlockSpec((1, D_MODEL), lambda i, j: (0, 0)),
            pl.BlockSpec((None, None, D_MODEL, INPROJ_TN), lambda i, j: (layer, j, 0, 0)),
            pl.BlockSpec((1, INPROJ_TN), lambda i, j: (0, jnp.maximum(j - N_MAIN_BLOCKS, 0))),
            pl.BlockSpec((INPROJ_TN, INPROJ_TN), lambda i, j: (0, 0)),
            pl.BlockSpec((None, 2, INPROJ_TN), lambda i, j: (jnp.minimum(j, N_QKV_BLOCKS - 1), 0, 0)),
        ],
        out_specs=[
            pl.BlockSpec((INPROJ_TM, INPROJ_TN), lambda i, j: (i, jnp.minimum(j, N_QKV_BLOCKS - 1))),
            pl.BlockSpec((INPROJ_TM, INPROJ_TN), lambda i, j: (i, jnp.clip(j - N_QKV_BLOCKS, 0, n_u - 1))),
            pl.BlockSpec((INPROJ_TM, INPROJ_TN), lambda i, j: (i, jnp.maximum(j - N_MAIN_BLOCKS, 0))),
        ],
        out_shape=[
            jax.ShapeDtypeStruct((t, OFF_U), BF16),
            jax.ShapeDtypeStruct((t, SSM_WIDTH), F32),
            jax.ShapeDtypeStruct((t, 2 * D_MODEL), BF16),
        ],
        scratch_shapes=[pltpu.VMEM((INPROJ_TM, D_MODEL), BF16)],
        compiler_params=pltpu.CompilerParams(
            dimension_semantics=("arbitrary", "arbitrary"), vmem_limit_bytes=VMEM_LIMIT),
        name="inproj",
    )(x2, gain, w_blk, bias, seg, ntab)


def _qk_norm_tables(q_gain, k_gain):
    qrow = jnp.tile(q_gain.astype(F32), INPROJ_TN // HEAD_DIM) * (HEAD_DIM ** -0.5)
    ones = jnp.ones((INPROJ_TN,), F32)
    kvrow = jnp.concatenate([jnp.tile(k_gain.astype(F32), N_KV_HEADS), jnp.ones((KV_WIDTH,), F32)])
    kvmask = jnp.concatenate([jnp.ones((KV_WIDTH,), F32), jnp.zeros((KV_WIDTH,), F32)])
    return jnp.stack([jnp.stack([qrow, ones]), jnp.stack([qrow, ones]), jnp.stack([kvrow, kvmask])])


def _segment_ones():
    r = jnp.arange(INPROJ_TN) // HEAD_DIM
    return (r[:, None] == r[None, :]).astype(BF16)


def _attn_bias_tables():
    t_loc = jnp.arange(BLOCK)[:, None]
    s_loc = jnp.arange(2 * BLOCK)[None, :] - BLOCK
    dist = (t_loc - s_loc).astype(F32)
    valid = (dist >= 0) & (dist < WINDOW)
    slopes = jnp.exp2(-8.0 * jnp.arange(1, N_Q_HEADS + 1, dtype=F32) / N_Q_HEADS)
    bias = -slopes[:, None, None] * dist[None]
    full = jnp.where(valid[None], bias, -jnp.inf)
    first = jnp.where((valid & (s_loc >= 0))[None], bias, -jnp.inf)
    return jnp.stack([first, full])


def _attn_kernel(sink_ref, bias_ref, q_ref, kvc_ref, kvp_ref, o_ref):
    kv_rows = 2 * BLOCK
    left_kv = lax.broadcasted_iota(jnp.int32, (kv_rows, LANES), 1) < HEAD_DIM
    left_q = lax.broadcasted_iota(jnp.int32, (BLOCK, LANES), 1) < HEAD_DIM
    zeros = jnp.zeros((kv_rows, LANES), BF16)
    ones_l = jnp.where(left_kv, 1.0, 0.0).astype(BF16)
    ones_r = jnp.where(left_kv, 0.0, 1.0).astype(BF16)
    contract_lanes = (((1,), (1,)), ((), ()))

    def slab(off):
        a = jnp.concatenate([kvp_ref[:, off:off + LANES], kvc_ref[:, off:off + LANES]], axis=0)
        return a, pltpu.roll(a.astype(F32), HEAD_DIM, axis=1).astype(BF16)

    for c in range(N_KV_HEADS // 2):
        k_slabs = slab(c * LANES)
        v_slabs = slab(KV_WIDTH + c * LANES)
        for side in range(2):
            kh = 2 * c + side
            k_l = jnp.where(left_kv, k_slabs[side], zeros)
            k_r = jnp.where(left_kv, zeros, k_slabs[1 - side])
            v_l = jnp.where(left_kv, v_slabs[side], zeros)
            v_r = jnp.where(left_kv, zeros, v_slabs[1 - side])
            kk = jnp.concatenate([k_l, k_r], axis=0)
            vv = jnp.concatenate([jnp.concatenate([v_l, ones_l], axis=1),
                                  jnp.concatenate([v_r, ones_r], axis=1)], axis=0)
            for pair in range(GQA_GROUP // 2):
                e = kh * GQA_GROUP + 2 * pair
                cols = slice(e * HEAD_DIM, (e + 2) * HEAD_DIM)
                s2 = lax.dot_general(q_ref[:, cols], kk, contract_lanes, preferred_element_type=F32)
                ps, ds = [], []
                for t in range(2):
                    s = s2[:, t * kv_rows:(t + 1) * kv_rows] + bias_ref[e + t]
                    sink = sink_ref[e + t]
                    m = jnp.maximum(jnp.max(s, axis=-1, keepdims=True), sink)
                    ps.append(jnp.exp(s - m).astype(BF16))
                    ds.append(jnp.exp(sink - m))
                r = jnp.dot(jnp.concatenate(ps, axis=1), vv, preferred_element_type=F32)
                denom = r[:, LANES:] + jnp.where(left_q, ds[0], ds[1])
                o_ref[:, cols] = (r[:, :LANES] / denom).astype(BF16)


def _attention(zq, sinks, bias_tab, batch, seq):
    t = zq.shape[0]
    nb = seq // BLOCK
    kv_col = OFF_K // (2 * KV_WIDTH)
    return pl.pallas_call(
        _attn_kernel,
        grid=(batch, nb),
        in_specs=[
            pl.BlockSpec(memory_space=pltpu.SMEM),
            pl.BlockSpec((None, N_Q_HEADS, BLOCK, 2 * BLOCK), lambda b, n: (jnp.minimum(n, 1), 0, 0, 0)),
            pl.BlockSpec((BLOCK, ATTN_WIDTH), lambda b, n: (b * nb + n, 0)),
            pl.BlockSpec((BLOCK, 2 * KV_WIDTH), lambda b, n: (b * nb + n, kv_col)),
            pl.BlockSpec((BLOCK, 2 * KV_WIDTH), lambda b, n: (b * nb + jnp.maximum(n - 1, 0), kv_col)),
        ],
        out_specs=pl.BlockSpec((BLOCK, ATTN_WIDTH), lambda b, n: (b * nb + n, 0)),
        out_shape=jax.ShapeDtypeStruct((t, ATTN_WIDTH), BF16),
        compiler_params=pltpu.CompilerParams(
            dimension_semantics=("arbitrary", "arbitrary"), vmem_limit_bytes=VMEM_LIMIT),
        name="swa_attention",
    )(sinks, bias_tab, zq, zq, zq)


def _cmul(ar, ai, br, bi):
    return ar * br - ai * bi, ar * bi + ai * br


def _pow_by_bits(exps, squares):
    pr = jnp.ones(exps.shape, F32)
    pi = jnp.zeros(exps.shape, F32)
    for b, (sr, si) in enumerate(squares):
        on = ((exps >> b) & 1) == 1
        pr, pi = _cmul(pr, pi, jnp.where(on, sr, 1.0), jnp.where(on, si, 0.0))
    return pr, pi


def _ssm_prep_kernel(lam_ref, lcr_ref, lci_ref, ldt_ref, btr_ref, bti_ref, ctr_ref, cti_ref, d_ref,
                     tb_ref, c_ref, levr_ref, levi_ref):
    n_levels = levr_ref.shape[1]
    tau_lane = lax.broadcasted_iota(jnp.int32, (SSM_STATE, SUB_W), 1) // SSM_GROUP_CH
    row_h = lax.broadcasted_iota(jnp.int32, (SSM_GROUP_CH, SUB_W), 0)
    lane_h = lax.broadcasted_iota(jnp.int32, (SSM_GROUP_CH, SUB_W), 1)
    tile_ch = (lane_h % SSM_GROUP_CH == row_h).astype(F32)
    row_m = lax.broadcasted_iota(jnp.int32, (SUB, SSM_STATE), 0)

    def squares(a, n):
        out = [a]
        for _ in range(n - 1):
            out.append(_cmul(*out[-1], *out[-1]))
        return out

    for g in range(SLAB_GROUPS):
        dt = jnp.exp(ldt_ref[g])

        def discretise(lr, li):
            mag = jnp.exp(lr * dt)
            return mag * jnp.cos(li * dt), mag * jnp.sin(li * dt)

        lr_row, li_row = lam_ref[g, 0:1, :], lam_ref[g, 1:2, :]
        sq_row = squares(discretise(lr_row, li_row), 5)
        sq_col = squares(discretise(lcr_ref[g], lci_ref[g]), 4)
        ar, ai = sq_row[0]
        den = lr_row * lr_row + li_row * li_row
        fr = ((ar - 1.0) * lr_row + ai * li_row) / den
        fi = (ai * lr_row - (ar - 1.0) * li_row) / den
        bbr, bbi = _cmul(fr, fi, btr_ref[g], bti_ref[g])

        e0 = _pow_by_bits(tau_lane, sq_col)
        e1 = _cmul(*e0, *sq_col[0])
        ctr = jnp.dot(ctr_ref[g], tile_ch, precision=HIGHEST, preferred_element_type=F32)
        cti = jnp.dot(cti_ref[g], tile_ch, precision=HIGHEST, preferred_element_type=F32)
        mr, mi = _cmul(*e0, ctr, cti)
        kt = (jnp.dot(bbr, mr, precision=HIGHEST, preferred_element_type=F32)
              - jnp.dot(bbi, mi, precision=HIGHEST, preferred_element_type=F32))
        kt = kt + jnp.where(lane_h == row_h, d_ref[g], 0.0)
        pw = _pow_by_bits(row_m, sq_row[:4])
        for j in range(SUB):
            rows = pl.ds(j * SSM_GROUP_CH, SSM_GROUP_CH)
            tj = kt if j == 0 else jnp.where(lane_h >= j * SSM_GROUP_CH,
                                             pltpu.roll(kt, j * SSM_GROUP_CH, axis=1), 0.0)
            tb_ref[g, rows, 0:SUB_W] = tj.astype(BF16)
            m = SUB - 1 - j
            br_, bi_ = _cmul(pw[0][m:m + 1, :], pw[1][m:m + 1, :], bbr, bbi)
            tb_ref[g, rows, SUB_W:SUB_W + STATE_W] = jnp.concatenate([br_, bi_], axis=1).astype(BF16)
            tb_ref[g, rows, SUB_W + STATE_W:TB_W] = jnp.concatenate([bi_, br_], axis=1).astype(BF16)
        c_ref[g, 0:SSM_STATE, :] = (ctr * e1[0] - cti * e1[1]).astype(BF16)
        c_ref[g, SSM_STATE:STATE_W, :] = (-ctr * e1[1] - cti * e1[0]).astype(BF16)
        lv = sq_row[4]
        lev_r, lev_i = [], []
        for _ in range(n_levels):
            lev_r.append(jnp.concatenate([lv[0], lv[0]], axis=1))
            lev_i.append(jnp.concatenate([-lv[1], lv[1]], axis=1))
            lv = _cmul(*lv, *lv)
        levr_ref[g] = jnp.concatenate(lev_r, axis=0)
        levi_ref[g] = jnp.concatenate(lev_i, axis=0)


def _ssm_prep(lam_re, lam_im, log_dt, b_re, b_im, c_re, c_im, d_skip, n_levels):
    dg = lam_re.shape[0] * SSM_GROUPS
    h_, p_ = SSM_GROUP_CH, SSM_STATE
    f = lambda a: a.astype(F32)
    lam_rows = jnp.stack([f(lam_re), f(lam_im)], axis=2).reshape(dg, 2, p_)
    btr = f(b_re).transpose(0, 1, 3, 2).reshape(dg, h_, p_)
    bti = f(b_im).transpose(0, 1, 3, 2).reshape(dg, h_, p_)
    ctr = f(c_re).transpose(0, 1, 3, 2).reshape(dg, p_, h_)
    cti = f(c_im).transpose(0, 1, 3, 2).reshape(dg, p_, h_)
    d_rows = jnp.pad(f(d_skip).reshape(dg, 1, h_), ((0, 0), (0, 0), (0, SUB_W - h_)))
    grp = lambda *shape: pl.BlockSpec((SLAB_GROUPS,) + shape, lambda s: (s,) + (0,) * len(shape))
    return pl.pallas_call(
        _ssm_prep_kernel,
        grid=(dg // SLAB_GROUPS,),
        in_specs=[grp(2, p_), grp(p_, 1), grp(p_, 1), grp(1, 1), grp(h_, p_), grp(h_, p_),
                  grp(p_, h_), grp(p_, h_), grp(1, SUB_W)],
        out_specs=[grp(SUB_W, TB_W), grp(STATE_W, SUB_W), grp(n_levels, STATE_W), grp(n_levels, STATE_W)],
        out_shape=[
            jax.ShapeDtypeStruct((dg, SUB_W, TB_W), BF16),
            jax.ShapeDtypeStruct((dg, STATE_W, SUB_W), BF16),
            jax.ShapeDtypeStruct((dg, n_levels, STATE_W), F32),
            jax.ShapeDtypeStruct((dg, n_levels, STATE_W), F32),
        ],
        compiler_params=pltpu.CompilerParams(
            dimension_semantics=("arbitrary",), vmem_limit_bytes=VMEM_LIMIT),
        name="s5_prep",
    )(lam_rows, f(lam_re).reshape(dg, p_, 1), f(lam_im).reshape(dg, p_, 1), f(log_dt).reshape(dg, 1, 1),
      btr, bti, ctr, cti, d_rows)


def _slab_permutations():
    r = jnp.arange(2 * LANES)
    jj, lane = r // LANES, r % LANES
    grp, ch = lane // SSM_GROUP_CH, lane % SSM_GROUP_CH
    jp = jnp.arange(SUB // 2)[:, None]
    col = grp[None] * SUB_W + (2 * jp + jj[None]) * SSM_GROUP_CH + ch[None]
    perm_in = (col[:, :, None] == jnp.arange(SLAB_W)[None, None, :]).astype(BF16)
    return perm_in, perm_in.transpose(0, 2, 1)


def _ssm_kernel(u_ref, pin_ref, pout_ref, tb_ref, c_ref, ar_ref, ai_ref, y_ref, xs_ref, ys_ref):
    rows = xs_ref.shape[0]
    acc = jnp.zeros((rows, SLAB_W), F32)
    for jp in range(SUB // 2):
        pair = jnp.concatenate([u_ref[pl.ds(2 * jp, rows, stride=SUB), :],
                                u_ref[pl.ds(2 * jp + 1, rows, stride=SUB), :]], axis=1)
        acc = acc + jnp.dot(pair.astype(BF16), pin_ref[jp], preferred_element_type=F32)
    xs_ref[...] = acc.astype(BF16)

    n_levels = rows.bit_length() - 1

    def shifted(a, sh):
        return jnp.concatenate([jnp.zeros((sh, a.shape[1]), a.dtype), a[:rows - sh]], axis=0)

    for g in range(SLAB_GROUPS):
        x = xs_ref[:, g * SUB_W:(g + 1) * SUB_W]
        r = jnp.dot(x, tb_ref[g], preferred_element_type=F32)
        yt = r[:, :SUB_W]
        s = r[:, SUB_W:SUB_W + STATE_W]
        w = r[:, SUB_W + STATE_W:]
        for k in range(n_levels):
            sh = 1 << k
            ar = ar_ref[g, k:k + 1, :]
            ai = ai_ref[g, k:k + 1, :]
            ps, pw = shifted(s, sh), shifted(w, sh)
            s, w = s + ar * ps + ai * pw, (w + ar * pw - ai * ps if k + 1 < n_levels else None)
        y = yt + jnp.dot(shifted(s, 1).astype(BF16), c_ref[g], preferred_element_type=F32)
        ys_ref[:, g * SUB_W:(g + 1) * SUB_W] = jax.nn.gelu(y).astype(BF16)

    ys = ys_ref[...]
    for ip in range(SUB // 2):
        o = jnp.dot(ys, pout_ref[ip], preferred_element_type=F32)
        y_ref[pl.ds(2 * ip, rows, stride=SUB), :] = o[:, :LANES]
        y_ref[pl.ds(2 * ip + 1, rows, stride=SUB), :] = o[:, LANES:]


def _ssm(uf, perm_in, perm_out, tb, cmat, lev_r, lev_i, layer, batch, seq):
    t = uf.shape[0]
    n_sub = seq // SUB
    n_levels = lev_r.shape[1]
    const3 = lambda b, s: (0, 0, 0)
    slab3 = lambda b, s: (layer * N_SLABS + s, 0, 0)
    return pl.pallas_call(
        _ssm_kernel,
        grid=(batch, N_SLABS),
        in_specs=[
            pl.BlockSpec((seq, LANES), lambda b, s: (b, s)),
            pl.BlockSpec((SUB // 2, 2 * LANES, SLAB_W), const3, pipeline_mode=pl.Buffered(1)),
            pl.BlockSpec((SUB // 2, SLAB_W, 2 * LANES), const3, pipeline_mode=pl.Buffered(1)),
            pl.BlockSpec((SLAB_GROUPS, SUB_W, TB_W), slab3),
            pl.BlockSpec((SLAB_GROUPS, STATE_W, SUB_W), slab3),
            pl.BlockSpec((SLAB_GROUPS, n_levels, STATE_W), slab3),
            pl.BlockSpec((SLAB_GROUPS, n_levels, STATE_W), slab3),
        ],
        out_specs=pl.BlockSpec((seq, LANES), lambda b, s: (b, s)),
        out_shape=jax.ShapeDtypeStruct((t, SSM_WIDTH), F32),
        scratch_shapes=[pltpu.VMEM((n_sub, SLAB_W), BF16), pltpu.VMEM((n_sub, SLAB_W), BF16)],
        compiler_params=pltpu.CompilerParams(
            dimension_semantics=("arbitrary", "arbitrary"), vmem_limit_bytes=VMEM_LIMIT),
        name="s5_scan",
    )(uf, perm_in, perm_out, tb, cmat, lev_r, lev_i)


def _merge_kernel(ya_ref, yg_ref, gt_ref, x_ref, wglu_ref, bglu_ref, wa_ref, ws_ref, wo_ref,
                  gffn_ref, x1_ref, h2_ref):
    yg = yg_ref[...]
    t = jnp.dot(yg.astype(BF16), wglu_ref[...], preferred_element_type=F32) + bglu_ref[...]
    ys = (yg * _sigmoid(t)).astype(BF16)
    ma = jnp.dot(ya_ref[...], wa_ref[...], preferred_element_type=F32)
    ms = jnp.dot(ys, ws_ref[...], preferred_element_type=F32)
    merged = gt_ref[:, :D_MODEL].astype(F32) * ma + gt_ref[:, D_MODEL:].astype(F32) * ms
    x1 = x_ref[...] + jnp.dot(merged.astype(BF16), wo_ref[...], preferred_element_type=F32)
    x1_ref[...] = x1
    h2_ref[...] = _rms(x1, gffn_ref[...]).astype(BF16)


def _merge(ya, yg, gates, x2, wglu, bglu, wa, ws, wo, gffn, layer):
    t = x2.shape[0]
    tm = MERGE_TM

    def wspec(rows, cols):
        return pl.BlockSpec((None, rows, cols), lambda i: (layer, 0, 0), pipeline_mode=pl.Buffered(1))

    return pl.pallas_call(
        _merge_kernel,
        grid=(t // tm,),
        in_specs=[
            pl.BlockSpec((tm, ATTN_WIDTH), lambda i: (i, 0)),
            pl.BlockSpec((tm, SSM_WIDTH), lambda i: (i, 0)),
            pl.BlockSpec((tm, 2 * D_MODEL), lambda i: (i, 0)),
            pl.BlockSpec((tm, D_MODEL), lambda i: (i, 0)),
            wspec(SSM_WIDTH, SSM_WIDTH),
            pl.BlockSpec((1, SSM_WIDTH), lambda i: (0, 0)),
            wspec(ATTN_WIDTH, D_MODEL),
            wspec(SSM_WIDTH, D_MODEL),
            wspec(D_MODEL, D_MODEL),
            pl.BlockSpec((1, D_MODEL), lambda i: (0, 0)),
        ],
        out_specs=[
            pl.BlockSpec((tm, D_MODEL), lambda i: (i, 0)),
            pl.BlockSpec((tm, D_MODEL), lambda i: (i, 0)),
        ],
        out_shape=[
            jax.ShapeDtypeStruct((t, D_MODEL), F32),
            jax.ShapeDtypeStruct((t, D_MODEL), BF16),
        ],
        compiler_params=pltpu.CompilerParams(
            dimension_semantics=("arbitrary",), vmem_limit_bytes=VMEM_LIMIT),
        name="merge_out",
    )(ya, yg, gates, x2, wglu, bglu, wa, ws, wo, gffn)


def _ffn_kernel(h_ref, x_ref, wg_ref, wu_ref, wo_ref, o_ref):
    k = pl.program_id(1)

    @pl.when(k == 0)
    def _():
        o_ref[...] = x_ref[...]

    for r in range(FFN_TM // FFN_SUB):
        rs = pl.ds(r * FFN_SUB, FFN_SUB)
        h = h_ref[rs, :]
        g = jnp.dot(h, wg_ref[...], preferred_element_type=F32)
        u = jnp.dot(h, wu_ref[...], preferred_element_type=F32)
        act = (g * _sigmoid(g) * u).astype(BF16)
        o_ref[rs, :] += jnp.dot(act, wo_ref[...], preferred_element_type=F32)


def _ffn(h2, x1, w_in_blk, w_out, layer):
    t = x1.shape[0]
    nk = D_FF // FFN_TF
    return pl.pallas_call(
        _ffn_kernel,
        grid=(t // FFN_TM, nk),
        in_specs=[
            pl.BlockSpec((FFN_TM, D_MODEL), lambda i, k: (i, 0), pipeline_mode=pl.Buffered(1)),
            pl.BlockSpec((FFN_TM, D_MODEL), lambda i, k: (i, 0), pipeline_mode=pl.Buffered(1)),
            pl.BlockSpec((None, None, D_MODEL, FFN_TF), lambda i, k: (layer, k, 0, 0)),
            pl.BlockSpec((None, None, D_MODEL, FFN_TF), lambda i, k: (layer, nk + k, 0, 0)),
            pl.BlockSpec((None, FFN_TF, D_MODEL), lambda i, k: (layer, k, 0)),
        ],
        out_specs=pl.BlockSpec((FFN_TM, D_MODEL), lambda i, k: (i, 0)),
        out_shape=jax.ShapeDtypeStruct((t, D_MODEL), F32),
        compiler_params=pltpu.CompilerParams(
            dimension_semantics=("arbitrary", "arbitrary"), vmem_limit_bytes=VMEM_LIMIT),
        name="swiglu_ffn",
    )(h2, x1, w_in_blk, w_in_blk, w_out)


def kernel(x, norm_mix_g, w_in, gate_bias, q_norm_g, k_norm_g, attn_sinks, ssm_lambda_re, ssm_lambda_im, ssm_log_dt, ssm_b_re, ssm_b_im, ssm_c_re, ssm_c_im, ssm_d, ssm_glu_w, ssm_glu_b, w_attn_branch, w_ssm_branch, w_out, norm_ffn_g, w_ffn_in, w_ffn_out):
    batch, seq, _ = x.shape
    t = batch * seq
    n_levels = (seq // SUB).bit_length() - 1
    x2 = x.reshape(t, D_MODEL).astype(F32)
    w_in_blk = _column_blocks(w_in, INPROJ_TN)
    wfi_blk = _column_blocks(w_ffn_in, FFN_TF)
    wglu_bf = ssm_glu_w.astype(BF16)
    wa_bf = w_attn_branch.astype(BF16)
    ws_bf = w_ssm_branch.astype(BF16)
    wo_bf = w_out.astype(BF16)
    wfo_bf = w_ffn_out.astype(BF16)
    seg = _segment_ones()
    bias_tab = _attn_bias_tables()
    perm_in, perm_out = _slab_permutations()
    tb, cmat, lev_r, lev_i = _ssm_prep(ssm_lambda_re, ssm_lambda_im, ssm_log_dt, ssm_b_re, ssm_b_im,
                                       ssm_c_re, ssm_c_im, ssm_d, n_levels)
    for l in range(DEPTH):
        zq, uf, gates = _inproj(x2, norm_mix_g[l].reshape(1, D_MODEL).astype(F32), w_in_blk, l,
                                gate_bias[l].reshape(1, 2 * D_MODEL).astype(F32), seg,
                                _qk_norm_tables(q_norm_g[l], k_norm_g[l]))
        ya = _attention(zq, attn_sinks[l].astype(F32), bias_tab, batch, seq)
        yg = _ssm(uf, perm_in, perm_out, tb, cmat, lev_r, lev_i, l, batch, seq)
        x1, h2 = _merge(ya, yg, gates, x2, wglu_bf, ssm_glu_b[l].reshape(1, SSM_WIDTH).astype(F32),
                        wa_bf, ws_bf, wo_bf, norm_ffn_g[l].reshape(1, D_MODEL).astype(F32), l)
        x2 = _ffn(h2, x1, wfi_blk, wfo_bf, l)
    return x2.reshape(batch, seq, D_MODEL).astype(x.dtype)
```

```python
import jax
import jax.numpy as jnp
from jax import lax
from jax.experimental import pallas as pl
from jax.experimental.pallas import tpu as pltpu

D_MODEL = 2048
DEPTH = 2
HEAD_DIM = 64
N_Q_HEADS = 16
N_KV_HEADS = 4
GQA_GROUP = N_Q_HEADS // N_KV_HEADS
ATTN_WIDTH = N_Q_HEADS * HEAD_DIM
KV_WIDTH = N_KV_HEADS * HEAD_DIM
WINDOW = 128
BLOCK = 128
SSM_WIDTH = D_MODEL // 2
SSM_GROUP_CH = 16
SSM_GROUPS = SSM_WIDTH // SSM_GROUP_CH
SSM_STATE = 64
D_FF = -(-8 * D_MODEL // (3 * 256)) * 256
OFF_K = ATTN_WIDTH
OFF_V = OFF_K + KV_WIDTH
OFF_U = OFF_V + KV_WIDTH
OFF_G = OFF_U + SSM_WIDTH
IN_WIDTH = OFF_G + 2 * D_MODEL
RMS_EPS = 1e-6

F32 = jnp.float32
BF16 = jnp.bfloat16
HIGHEST = lax.Precision.HIGHEST

LANES = 128
SUB = 16
SUB_W = SUB * SSM_GROUP_CH
STATE_W = 2 * SSM_STATE
TB_W = SUB_W + 2 * STATE_W
SLAB_GROUPS = LANES // SSM_GROUP_CH
N_SLABS = SSM_GROUPS // SLAB_GROUPS
SLAB_W = SLAB_GROUPS * SUB_W

VMEM_LIMIT = 56 * 1024 * 1024
FFN_VMEM_LIMIT = 60 * 1024 * 1024

INPROJ_TM = 256
INPROJ_TN = 512
MERGE_TM = 256
FFN_TM = 1024
FFN_TF = 512
FFN_SUB = 512


def _rms(x, g):
    return x * lax.rsqrt(jnp.mean(x * x, axis=-1, keepdims=True) + RMS_EPS) * g


def _sigmoid(x):
    return 0.5 * jnp.tanh(0.5 * x) + 0.5


N_QKV_BLOCKS = OFF_U // INPROJ_TN
N_MAIN_BLOCKS = OFF_G // INPROJ_TN
N_IN_BLOCKS = IN_WIDTH // INPROJ_TN


def _inproj_kernel(x_ref, g_ref, w_ref, b_ref, seg_ref, nt_ref, zq_ref, u_ref, gt_ref):
    h = _rms(x_ref[...], g_ref[...]).astype(BF16)
    for j in range(N_IN_BLOCKS):
        z = jnp.dot(h, w_ref[:, j * INPROJ_TN:(j + 1) * INPROJ_TN], preferred_element_type=F32)
        if j < N_QKV_BLOCKS:
            zz = z * z
            hi = zz.astype(BF16)
            lo = (zz - hi.astype(F32)).astype(BF16)
            ssq = (jnp.dot(hi, seg_ref[...], preferred_element_type=F32)
                   + jnp.dot(lo, seg_ref[...], preferred_element_type=F32))
            inv = lax.rsqrt(ssq * (1.0 / HEAD_DIM) + RMS_EPS)
            fac = jnp.where(nt_ref[j, 1:2, :] > 0.0, inv, 1.0) * nt_ref[j, 0:1, :]
            zq_ref[:, j * INPROJ_TN:(j + 1) * INPROJ_TN] = (z * fac).astype(BF16)
        elif j < N_MAIN_BLOCKS:
            c = j - N_QKV_BLOCKS
            u_ref[:, c * INPROJ_TN:(c + 1) * INPROJ_TN] = z
        else:
            cs = slice((j - N_MAIN_BLOCKS) * INPROJ_TN, (j - N_MAIN_BLOCKS + 1) * INPROJ_TN)
            gt_ref[:, cs] = _sigmoid(z + b_ref[:, cs]).astype(BF16)


def _inproj(x2, gain, w_bf, layer, bias, seg, ntab):
    t = x2.shape[0]
    tm = INPROJ_TM
    const2 = lambda i: (0, 0)
    return pl.pallas_call(
        _inproj_kernel,
        grid=(t // tm,),
        in_specs=[
            pl.BlockSpec((tm, D_MODEL), lambda i: (i, 0)),
            pl.BlockSpec((1, D_MODEL), const2),
            pl.BlockSpec((None, D_MODEL, IN_WIDTH), lambda i: (layer, 0, 0), pipeline_mode=pl.Buffered(1)),
            pl.BlockSpec((1, 2 * D_MODEL), const2),
            pl.BlockSpec((INPROJ_TN, INPROJ_TN), const2),
            pl.BlockSpec((N_QKV_BLOCKS, 2, INPROJ_TN), lambda i: (0, 0, 0)),
        ],
        out_specs=[
            pl.BlockSpec((tm, OFF_U), lambda i: (i, 0)),
            pl.BlockSpec((tm, SSM_WIDTH), lambda i: (i, 0)),
            pl.BlockSpec((tm, 2 * D_MODEL), lambda i: (i, 0)),
        ],
        out_shape=[
            jax.ShapeDtypeStruct((t, OFF_U), BF16),
            jax.ShapeDtypeStruct((t, SSM_WIDTH), F32),
            jax.ShapeDtypeStruct((t, 2 * D_MODEL), BF16),
        ],
        compiler_params=pltpu.CompilerParams(
            dimension_semantics=("arbitrary",), vmem_limit_bytes=VMEM_LIMIT),
        name="inproj",
    )(x2, gain, w_bf, bias, seg, ntab)


def _qk_norm_tables(q_gain, k_gain):
    qrow = jnp.tile(q_gain.astype(F32), INPROJ_TN // HEAD_DIM) * (HEAD_DIM ** -0.5)
    ones = jnp.ones((INPROJ_TN,), F32)
    kvrow = jnp.concatenate([jnp.tile(k_gain.astype(F32), N_KV_HEADS), jnp.ones((KV_WIDTH,), F32)])
    kvmask = jnp.concatenate([jnp.ones((KV_WIDTH,), F32), jnp.zeros((KV_WIDTH,), F32)])
    return jnp.stack([jnp.stack([qrow, ones]), jnp.stack([qrow, ones]), jnp.stack([kvrow, kvmask])])


def _segment_ones():
    r = jnp.arange(INPROJ_TN) // HEAD_DIM
    return (r[:, None] == r[None, :]).astype(BF16)


def _attn_bias_tables():
    t_loc = jnp.arange(BLOCK)[:, None]
    s_loc = jnp.arange(2 * BLOCK)[None, :] - BLOCK
    dist = (t_loc - s_loc).astype(F32)
    valid = (dist >= 0) & (dist < WINDOW)
    slopes = jnp.exp2(-8.0 * jnp.arange(1, N_Q_HEADS + 1, dtype=F32) / N_Q_HEADS)
    bias = -slopes[:, None, None] * dist[None]
    full = jnp.where(valid[None], bias, -jnp.inf)
    first = jnp.where((valid & (s_loc >= 0))[None], bias, -jnp.inf)
    return jnp.stack([first, full])


def _attn_kernel(sink_ref, bias_ref, q_ref, kvc_ref, kvp_ref, o_ref):
    kv_rows = 2 * BLOCK
    left_kv = lax.broadcasted_iota(jnp.int32, (kv_rows, LANES), 1) < HEAD_DIM
    left_q = lax.broadcasted_iota(jnp.int32, (BLOCK, LANES), 1) < HEAD_DIM
    zeros = jnp.zeros((kv_rows, LANES), BF16)
    ones_l = jnp.where(left_kv, 1.0, 0.0).astype(BF16)
    ones_r = jnp.where(left_kv, 0.0, 1.0).astype(BF16)
    contract_lanes = (((1,), (1,)), ((), ()))

    def slab(off):
        a = jnp.concatenate([kvp_ref[:, off:off + LANES], kvc_ref[:, off:off + LANES]], axis=0)
        return a, pltpu.roll(a.astype(F32), HEAD_DIM, axis=1).astype(BF16)

    for c in range(N_KV_HEADS // 2):
        k_slabs = slab(c * LANES)
        v_slabs = slab(KV_WIDTH + c * LANES)
        for side in range(2):
            kh = 2 * c + side
            k_l = jnp.where(left_kv, k_slabs[side], zeros)
            k_r = jnp.where(left_kv, zeros, k_slabs[1 - side])
            v_l = jnp.where(left_kv, v_slabs[side], zeros)
            v_r = jnp.where(left_kv, zeros, v_slabs[1 - side])
            kk = jnp.concatenate([k_l, k_r], axis=0)
            vv = jnp.concatenate([jnp.concatenate([v_l, ones_l], axis=1),
                                  jnp.concatenate([v_r, ones_r], axis=1)], axis=0)
            for pair in range(GQA_GROUP // 2):
                e = kh * GQA_GROUP + 2 * pair
                cols = slice(e * HEAD_DIM, (e + 2) * HEAD_DIM)
                s2 = lax.dot_general(q_ref[:, cols], kk, contract_lanes, preferred_element_type=F32)
                ps, ds = [], []
                for t in range(2):
                    s = s2[:, t * kv_rows:(t + 1) * kv_rows] + bias_ref[e + t]
                    sink = sink_ref[e + t]
                    m = jnp.maximum(jnp.max(s, axis=-1, keepdims=True), sink)
                    ps.append(jnp.exp(s - m).astype(BF16))
                    ds.append(jnp.exp(sink - m))
                r = jnp.dot(jnp.concatenate(ps, axis=1), vv, preferred_element_type=F32)
                denom = r[:, LANES:] + jnp.where(left_q, ds[0], ds[1])
                o_ref[:, cols] = (r[:, :LANES] / denom).astype(BF16)


def _attention(zq, sinks, bias_tab, batch, seq):
    t = zq.shape[0]
    nb = seq // BLOCK
    kv_col = OFF_K // (2 * KV_WIDTH)
    return pl.pallas_call(
        _attn_kernel,
        grid=(batch, nb),
        in_specs=[
            pl.BlockSpec(memory_space=pltpu.SMEM),
            pl.BlockSpec((None, N_Q_HEADS, BLOCK, 2 * BLOCK), lambda b, n: (jnp.minimum(n, 1), 0, 0, 0)),
            pl.BlockSpec((BLOCK, ATTN_WIDTH), lambda b, n: (b * nb + n, 0)),
            pl.BlockSpec((BLOCK, 2 * KV_WIDTH), lambda b, n: (b * nb + n, kv_col)),
            pl.BlockSpec((BLOCK, 2 * KV_WIDTH), lambda b, n: (b * nb + jnp.maximum(n - 1, 0), kv_col)),
        ],
        out_specs=pl.BlockSpec((BLOCK, ATTN_WIDTH), lambda b, n: (b * nb + n, 0)),
        out_shape=jax.ShapeDtypeStruct((t, ATTN_WIDTH), BF16),
        compiler_params=pltpu.CompilerParams(
            dimension_semantics=("arbitrary", "arbitrary"), vmem_limit_bytes=VMEM_LIMIT),
        name="swa_attention",
    )(sinks, bias_tab, zq, zq, zq)


def _cmul(ar, ai, br, bi):
    return ar * br - ai * bi, ar * bi + ai * br


def _pow_by_bits(exps, squares):
    pr = jnp.ones(exps.shape, F32)
    pi = jnp.zeros(exps.shape, F32)
    for b, (sr, si) in enumerate(squares):
        on = ((exps >> b) & 1) == 1
        pr, pi = _cmul(pr, pi, jnp.where(on, sr, 1.0), jnp.where(on, si, 0.0))
    return pr, pi


def _ssm_prep_kernel(lam_ref, lcr_ref, lci_ref, ldt_ref, btr_ref, bti_ref, ctr_ref, cti_ref, d_ref,
                     tb_ref, c_ref, levr_ref, levi_ref):
    n_levels = levr_ref.shape[1]
    tau_lane = lax.broadcasted_iota(jnp.int32, (SSM_STATE, SUB_W), 1) // SSM_GROUP_CH
    row_h = lax.broadcasted_iota(jnp.int32, (SSM_GROUP_CH, SUB_W), 0)
    lane_h = lax.broadcasted_iota(jnp.int32, (SSM_GROUP_CH, SUB_W), 1)
    tile_ch = (lane_h % SSM_GROUP_CH == row_h).astype(F32)
    row_m = lax.broadcasted_iota(jnp.int32, (SUB, SSM_STATE), 0)

    def squares(a, n):
        out = [a]
        for _ in range(n - 1):
            out.append(_cmul(*out[-1], *out[-1]))
        return out

    for g in range(SLAB_GROUPS):
        dt = jnp.exp(ldt_ref[g])

        def discretise(lr, li):
            mag = jnp.exp(lr * dt)
            return mag * jnp.cos(li * dt), mag * jnp.sin(li * dt)

        lr_row, li_row = lam_ref[g, 0:1, :], lam_ref[g, 1:2, :]
        sq_row = squares(discretise(lr_row, li_row), 5)
        sq_col = squares(discretise(lcr_ref[g], lci_ref[g]), 4)
        ar, ai = sq_row[0]
        den = lr_row * lr_row + li_row * li_row
        fr = ((ar - 1.0) * lr_row + ai * li_row) / den
        fi = (ai * lr_row - (ar - 1.0) * li_row) / den
        bbr, bbi = _cmul(fr, fi, btr_ref[g], bti_ref[g])

        e0 = _pow_by_bits(tau_lane, sq_col)
        e1 = _cmul(*e0, *sq_col[0])
        ctr = jnp.dot(ctr_ref[g], tile_ch, precision=HIGHEST, preferred_element_type=F32)
        cti = jnp.dot(cti_ref[g], tile_ch, precision=HIGHEST, preferred_element_type=F32)
        mr, mi = _cmul(*e0, ctr, cti)
        kt = (jnp.dot(bbr, mr, precision=HIGHEST, preferred_element_type=F32)
              - jnp.dot(bbi, mi, precision=HIGHEST, preferred_element_type=F32))
        kt = kt + jnp.where(lane_h == row_h, d_ref[g], 0.0)
        pw = _pow_by_bits(row_m, sq_row[:4])
        for j in range(SUB):
            rows = pl.ds(j * SSM_GROUP_CH, SSM_GROUP_CH)
            tj = kt if j == 0 else jnp.where(lane_h >= j * SSM_GROUP_CH,
                                             pltpu.roll(kt, j * SSM_GROUP_CH, axis=1), 0.0)
            tb_ref[g, rows, 0:SUB_W] = tj.astype(BF16)
            m = SUB - 1 - j
            br_, bi_ = _cmul(pw[0][m:m + 1, :], pw[1][m:m + 1, :], bbr, bbi)
            tb_ref[g, rows, SUB_W:SUB_W + STATE_W] = jnp.concatenate([br_, bi_], axis=1).astype(BF16)
            tb_ref[g, rows, SUB_W + STATE_W:TB_W] = jnp.concatenate([bi_, br_], axis=1).astype(BF16)
        c_ref[g, 0:SSM_STATE, :] = (ctr * e1[0] - cti * e1[1]).astype(BF16)
        c_ref[g, SSM_STATE:STATE_W, :] = (-ctr * e1[1] - cti * e1[0]).astype(BF16)
        lv = sq_row[4]
        lev_r, lev_i = [], []
        for _ in range(n_levels):
            lev_r.append(jnp.concatenate([lv[0], lv[0]], axis=1))
            lev_i.append(jnp.concatenate([-lv[1], lv[1]], axis=1))
            lv = _cmul(*lv, *lv)
        levr_ref[g] = jnp.concatenate(lev_r, axis=0)
        levi_ref[g] = jnp.concatenate(lev_i, axis=0)


def _ssm_prep(lam_re, lam_im, log_dt, b_re, b_im, c_re, c_im, d_skip, n_levels):
    dg = lam_re.shape[0] * SSM_GROUPS
    h_, p_ = SSM_GROUP_CH, SSM_STATE
    f = lambda a: a.astype(F32)
    lam_rows = jnp.stack([f(lam_re), f(lam_im)], axis=2).reshape(dg, 2, p_)
    btr = f(b_re).transpose(0, 1, 3, 2).reshape(dg, h_, p_)
    bti = f(b_im).transpose(0, 1, 3, 2).reshape(dg, h_, p_)
    ctr = f(c_re).transpose(0, 1, 3, 2).reshape(dg, p_, h_)
    cti = f(c_im).transpose(0, 1, 3, 2).reshape(dg, p_, h_)
    d_rows = jnp.pad(f(d_skip).reshape(dg, 1, h_), ((0, 0), (0, 0), (0, SUB_W - h_)))
    grp = lambda *shape: pl.BlockSpec((SLAB_GROUPS,) + shape, lambda s: (s,) + (0,) * len(shape))
    return pl.pallas_call(
        _ssm_prep_kernel,
        grid=(dg // SLAB_GROUPS,),
        in_specs=[grp(2, p_), grp(p_, 1), grp(p_, 1), grp(1, 1), grp(h_, p_), grp(h_, p_),
                  grp(p_, h_), grp(p_, h_), grp(1, SUB_W)],
        out_specs=[grp(SUB_W, TB_W), grp(STATE_W, SUB_W), grp(n_levels, STATE_W), grp(n_levels, STATE_W)],
        out_shape=[
            jax.ShapeDtypeStruct((dg, SUB_W, TB_W), BF16),
            jax.ShapeDtypeStruct((dg, STATE_W, SUB_W), BF16),
            jax.ShapeDtypeStruct((dg, n_levels, STATE_W), F32),
            jax.ShapeDtypeStruct((dg, n_levels, STATE_W), F32),
        ],
        compiler_params=pltpu.CompilerParams(
            dimension_semantics=("arbitrary",), vmem_limit_bytes=VMEM_LIMIT),
        name="s5_prep",
    )(lam_rows, f(lam_re).reshape(dg, p_, 1), f(lam_im).reshape(dg, p_, 1), f(log_dt).reshape(dg, 1, 1),
      btr, bti, ctr, cti, d_rows)


def _slab_permutations():
    r = jnp.arange(2 * LANES)
    jj, lane = r // LANES, r % LANES
    grp, ch = lane // SSM_GROUP_CH, lane % SSM_GROUP_CH
    jp = jnp.arange(SUB // 2)[:, None]
    col = grp[None] * SUB_W + (2 * jp + jj[None]) * SSM_GROUP_CH + ch[None]
    perm_in = (col[:, :, None] == jnp.arange(SLAB_W)[None, None, :]).astype(BF16)
    return perm_in, perm_in.transpose(0, 2, 1)


def _ssm_kernel(u_ref, pin_ref, pout_ref, tb_ref, c_ref, ar_ref, ai_ref, y_ref, xs_ref, ys_ref):
    rows = xs_ref.shape[0]
    acc = jnp.zeros((rows, SLAB_W), F32)
    for jp in range(SUB // 2):
        pair = jnp.concatenate([u_ref[pl.ds(2 * jp, rows, stride=SUB), :],
                                u_ref[pl.ds(2 * jp + 1, rows, stride=SUB), :]], axis=1)
        acc = acc + jnp.dot(pair.astype(BF16), pin_ref[jp], preferred_element_type=F32)
    xs_ref[...] = acc.astype(BF16)

    n_levels = rows.bit_length() - 1

    def shifted(a, sh):
        return jnp.concatenate([jnp.zeros((sh, a.shape[1]), a.dtype), a[:rows - sh]], axis=0)

    for g in range(SLAB_GROUPS):
        x = xs_ref[:, g * SUB_W:(g + 1) * SUB_W]
        r = jnp.dot(x, tb_ref[g], preferred_element_type=F32)
        yt = r[:, :SUB_W]
        s = r[:, SUB_W:SUB_W + STATE_W]
        w = r[:, SUB_W + STATE_W:]
        for k in range(n_levels):
            sh = 1 << k
            ar = ar_ref[g, k:k + 1, :]
            ai = ai_ref[g, k:k + 1, :]
            ps, pw = shifted(s, sh), shifted(w, sh)
            s, w = s + ar * ps + ai * pw, (w + ar * pw - ai * ps if k + 1 < n_levels else None)
        y = yt + jnp.dot(shifted(s, 1).astype(BF16), c_ref[g], preferred_element_type=F32)
        ys_ref[:, g * SUB_W:(g + 1) * SUB_W] = jax.nn.gelu(y).astype(BF16)

    ys = ys_ref[...]
    for ip in range(SUB // 2):
        o = jnp.dot(ys, pout_ref[ip], preferred_element_type=F32)
        y_ref[pl.ds(2 * ip, rows, stride=SUB), :] = o[:, :LANES]
        y_ref[pl.ds(2 * ip + 1, rows, stride=SUB), :] = o[:, LANES:]


def _ssm(uf, perm_in, perm_out, tb, cmat, lev_r, lev_i, layer, batch, seq):
    t = uf.shape[0]
    n_sub = seq // SUB
    n_levels = lev_r.shape[1]
    const3 = lambda b, s: (0, 0, 0)
    slab3 = lambda b, s: (layer * N_SLABS + s, 0, 0)
    return pl.pallas_call(
        _ssm_kernel,
        grid=(batch, N_SLABS),
        in_specs=[
            pl.BlockSpec((seq, LANES), lambda b, s: (b, s)),
            pl.BlockSpec((SUB // 2, 2 * LANES, SLAB_W), const3, pipeline_mode=pl.Buffered(1)),
            pl.BlockSpec((SUB // 2, SLAB_W, 2 * LANES), const3, pipeline_mode=pl.Buffered(1)),
            pl.BlockSpec((SLAB_GROUPS, SUB_W, TB_W), slab3),
            pl.BlockSpec((SLAB_GROUPS, STATE_W, SUB_W), slab3),
            pl.BlockSpec((SLAB_GROUPS, n_levels, STATE_W), slab3),
            pl.BlockSpec((SLAB_GROUPS, n_levels, STATE_W), slab3),
        ],
        out_specs=pl.BlockSpec((seq, LANES), lambda b, s: (b, s)),
        out_shape=jax.ShapeDtypeStruct((t, SSM_WIDTH), F32),
        scratch_shapes=[pltpu.VMEM((n_sub, SLAB_W), BF16), pltpu.VMEM((n_sub, SLAB_W), BF16)],
        compiler_params=pltpu.CompilerParams(
            dimension_semantics=("arbitrary", "arbitrary"), vmem_limit_bytes=VMEM_LIMIT),
        name="s5_scan",
    )(uf, perm_in, perm_out, tb, cmat, lev_r, lev_i)


def _merge_kernel(ya_ref, yg_ref, gt_ref, x_ref, wglu_ref, bglu_ref, wa_ref, ws_ref, wo_ref,
                  gffn_ref, x1_ref, h2_ref):
    yg = yg_ref[...]
    t = jnp.dot(yg.astype(BF16), wglu_ref[...], preferred_element_type=F32) + bglu_ref[...]
    ys = (yg * _sigmoid(t)).astype(BF16)
    ma = jnp.dot(ya_ref[...], wa_ref[...], preferred_element_type=F32)
    ms = jnp.dot(ys, ws_ref[...], preferred_element_type=F32)
    merged = gt_ref[:, :D_MODEL].astype(F32) * ma + gt_ref[:, D_MODEL:].astype(F32) * ms
    x1 = x_ref[...] + jnp.dot(merged.astype(BF16), wo_ref[...], preferred_element_type=F32)
    x1_ref[...] = x1
    h2_ref[...] = _rms(x1, gffn_ref[...]).astype(BF16)


def _merge(ya, yg, gates, x2, wglu, bglu, wa, ws, wo, gffn, layer):
    t = x2.shape[0]
    tm = MERGE_TM

    def wspec(rows, cols):
        return pl.BlockSpec((None, rows, cols), lambda i: (layer, 0, 0), pipeline_mode=pl.Buffered(1))

    return pl.pallas_call(
        _merge_kernel,
        grid=(t // tm,),
        in_specs=[
            pl.BlockSpec((tm, ATTN_WIDTH), lambda i: (i, 0)),
            pl.BlockSpec((tm, SSM_WIDTH), lambda i: (i, 0)),
            pl.BlockSpec((tm, 2 * D_MODEL), lambda i: (i, 0)),
            pl.BlockSpec((tm, D_MODEL), lambda i: (i, 0)),
            wspec(SSM_WIDTH, SSM_WIDTH),
            pl.BlockSpec((1, SSM_WIDTH), lambda i: (0, 0)),
            wspec(ATTN_WIDTH, D_MODEL),
            wspec(SSM_WIDTH, D_MODEL),
            wspec(D_MODEL, D_MODEL),
            pl.BlockSpec((1, D_MODEL), lambda i: (0, 0)),
        ],
        out_specs=[
            pl.BlockSpec((tm, D_MODEL), lambda i: (i, 0)),
            pl.BlockSpec((tm, D_MODEL), lambda i: (i, 0)),
        ],
        out_shape=[
            jax.ShapeDtypeStruct((t, D_MODEL), F32),
            jax.ShapeDtypeStruct((t, D_MODEL), BF16),
        ],
        compiler_params=pltpu.CompilerParams(
            dimension_semantics=("arbitrary",), vmem_limit_bytes=VMEM_LIMIT),
        name="merge_out",
    )(ya, yg, gates, x2, wglu, bglu, wa, ws, wo, gffn)


def _ffn_kernel(h_ref, x_ref, wg_ref, wu_ref, wo_ref, o_ref):
    k = pl.program_id(1)

    @pl.when(k == 0)
    def _():
        o_ref[...] = x_ref[...]

    for r in range(FFN_TM // FFN_SUB):
        rs = pl.ds(r * FFN_SUB, FFN_SUB)
        h = h_ref[rs, :]
        g = jnp.dot(h, wg_ref[...], preferred_element_type=F32)
        u = jnp.dot(h, wu_ref[...], preferred_element_type=F32)
        act = (g * _sigmoid(g) * u).astype(BF16)
        o_ref[rs, :] += jnp.dot(act, wo_ref[...], preferred_element_type=F32)


def _ffn(h2, x1, w_in, w_out, layer):
    t = x1.shape[0]
    nk = D_FF // FFN_TF
    return pl.pallas_call(
        _ffn_kernel,
        grid=(t // FFN_TM, nk),
        in_specs=[
            pl.BlockSpec((FFN_TM, D_MODEL), lambda i, k: (i, 0)),
            pl.BlockSpec((FFN_TM, D_MODEL), lambda i, k: (i, 0)),
            pl.BlockSpec((None, D_MODEL, FFN_TF), lambda i, k: (layer, 0, k)),
            pl.BlockSpec((None, D_MODEL, FFN_TF), lambda i, k: (layer, 0, nk + k)),
            pl.BlockSpec((None, FFN_TF, D_MODEL), lambda i, k: (layer, k, 0)),
        ],
        out_specs=pl.BlockSpec((FFN_TM, D_MODEL), lambda i, k: (i, 0)),
        out_shape=jax.ShapeDtypeStruct((t, D_MODEL), F32),
        compiler_params=pltpu.CompilerParams(
            dimension_semantics=("arbitrary", "arbitrary"), vmem_limit_bytes=FFN_VMEM_LIMIT),
        name="swiglu_ffn",
    )(h2, x1, w_in, w_in, w_out)


def kernel(x, norm_mix_g, w_in, gate_bias, q_norm_g, k_norm_g, attn_sinks, ssm_lambda_re, ssm_lambda_im, ssm_log_dt, ssm_b_re, ssm_b_im, ssm_c_re, ssm_c_im, ssm_d, ssm_glu_w, ssm_glu_b, w_attn_branch, w_ssm_branch, w_out, norm_ffn_g, w_ffn_in, w_ffn_out):
    batch, seq, _ = x.shape
    t = batch * seq
    n_levels = (seq // SUB).bit_length() - 1
    x2 = x.reshape(t, D_MODEL).astype(F32)
    w_in_bf = w_in.astype(BF16)
    wfi_bf = w_ffn_in.astype(BF16)
    wglu_bf = ssm_glu_w.astype(BF16)
    wa_bf = w_attn_branch.astype(BF16)
    ws_bf = w_ssm_branch.astype(BF16)
    wo_bf = w_out.astype(BF16)
    wfo_bf = w_ffn_out.astype(BF16)
    seg = _segment_ones()
    bias_tab = _attn_bias_tables()
    perm_in, perm_out = _slab_permutations()
    tb, cmat, lev_r, lev_i = _ssm_prep(ssm_lambda_re, ssm_lambda_im, ssm_log_dt, ssm_b_re, ssm_b_im,
                                       ssm_c_re, ssm_c_im, ssm_d, n_levels)
    for l in range(DEPTH):
        zq, uf, gates = _inproj(x2, norm_mix_g[l].reshape(1, D_MODEL).astype(F32), w_in_bf, l,
                                gate_bias[l].reshape(1, 2 * D_MODEL).astype(F32), seg,
                                _qk_norm_tables(q_norm_g[l], k_norm_g[l]))
        ya = _attention(zq, attn_sinks[l].astype(F32), bias_tab, batch, seq)
        yg = _ssm(uf, perm_in, perm_out, tb, cmat, lev_r, lev_i, l, batch, seq)
        x1, h2 = _merge(ya, yg, gates, x2, wglu_bf, ssm_glu_b[l].reshape(1, SSM_WIDTH).astype(F32),
                        wa_bf, ws_bf, wo_bf, norm_ffn_g[l].reshape(1, D_MODEL).astype(F32), l)
        x2 = _ffn(h2, x1, wfi_bf, wfo_bf, l)
    return x2.reshape(batch, seq, D_MODEL).astype(x.dtype)
```

```python
import jax
import jax.numpy as jnp
from jax import lax
from jax.experimental import pallas as pl
from jax.experimental.pallas import tpu as pltpu

D_MODEL = 2048
DEPTH = 2
HEAD_DIM = 64
N_Q_HEADS = 16
N_KV_HEADS = 4
GQA_GROUP = N_Q_HEADS // N_KV_HEADS
ATTN_WIDTH = N_Q_HEADS * HEAD_DIM
KV_WIDTH = N_KV_HEADS * HEAD_DIM
WINDOW = 128
BLOCK = 128
SSM_WIDTH = D_MODEL // 2
SSM_GROUP_CH = 16
SSM_GROUPS = SSM_WIDTH // SSM_GROUP_CH
SSM_STATE = 64
D_FF = -(-8 * D_MODEL // (3 * 256)) * 256
OFF_K = ATTN_WIDTH
OFF_V = OFF_K + KV_WIDTH
OFF_U = OFF_V + KV_WIDTH
OFF_G = OFF_U + SSM_WIDTH
IN_WIDTH = OFF_G + 2 * D_MODEL
RMS_EPS = 1e-6

F32 = jnp.float32
BF16 = jnp.bfloat16
HIGHEST = lax.Precision.HIGHEST

LANES = 128
SUB = 16
SUB_W = SUB * SSM_GROUP_CH
STATE_W = 2 * SSM_STATE
TB_W = SUB_W + 2 * STATE_W
SLAB_GROUPS = LANES // SSM_GROUP_CH
N_SLABS = SSM_GROUPS // SLAB_GROUPS
SLAB_W = SLAB_GROUPS * SUB_W

VMEM_LIMIT = 56 * 1024 * 1024
FFN_VMEM_LIMIT = 60 * 1024 * 1024

INPROJ_TM = 256
INPROJ_TN = 512
MERGE_TM = 256
FFN_TM = 1024
FFN_TF = 512
FFN_SUB = 512


def _rms(x, g):
    return x * lax.rsqrt(jnp.mean(x * x, axis=-1, keepdims=True) + RMS_EPS) * g


def _sigmoid(x):
    return 0.5 * jnp.tanh(0.5 * x) + 0.5


N_QKV_BLOCKS = OFF_U // INPROJ_TN
N_MAIN_BLOCKS = OFF_G // INPROJ_TN
N_IN_BLOCKS = IN_WIDTH // INPROJ_TN


def _inproj_kernel(x_ref, g_ref, w_ref, b_ref, seg_ref, nt_ref, zq_ref, u_ref, gt_ref):
    h = _rms(x_ref[...], g_ref[...]).astype(BF16)
    for j in range(N_IN_BLOCKS):
        z = jnp.dot(h, w_ref[:, j * INPROJ_TN:(j + 1) * INPROJ_TN], preferred_element_type=F32)
        if j < N_QKV_BLOCKS:
            zz = z * z
            hi = zz.astype(BF16)
            lo = (zz - hi.astype(F32)).astype(BF16)
            ssq = (jnp.dot(hi, seg_ref[...], preferred_element_type=F32)
                   + jnp.dot(lo, seg_ref[...], preferred_element_type=F32))
            inv = lax.rsqrt(ssq * (1.0 / HEAD_DIM) + RMS_EPS)
            fac = jnp.where(nt_ref[j, 1:2, :] > 0.0, inv, 1.0) * nt_ref[j, 0:1, :]
            zq_ref[:, j * INPROJ_TN:(j + 1) * INPROJ_TN] = (z * fac).astype(BF16)
        elif j < N_MAIN_BLOCKS:
            c = j - N_QKV_BLOCKS
            u_ref[:, c * INPROJ_TN:(c + 1) * INPROJ_TN] = z
        else:
            cs = slice((j - N_MAIN_BLOCKS) * INPROJ_TN, (j - N_MAIN_BLOCKS + 1) * INPROJ_TN)
            gt_ref[:, cs] = _sigmoid(z + b_ref[:, cs]).astype(BF16)


def _inproj(x2, gain, w_bf, layer, bias, seg, ntab):
    t = x2.shape[0]
    tm = INPROJ_TM
    const2 = lambda i: (0, 0)
    return pl.pallas_call(
        _inproj_kernel,
        grid=(t // tm,),
        in_specs=[
            pl.BlockSpec((tm, D_MODEL), lambda i: (i, 0)),
            pl.BlockSpec((1, D_MODEL), const2),
            pl.BlockSpec((None, D_MODEL, IN_WIDTH), lambda i: (layer, 0, 0), pipeline_mode=pl.Buffered(1)),
            pl.BlockSpec((1, 2 * D_MODEL), const2),
            pl.BlockSpec((INPROJ_TN, INPROJ_TN), const2),
            pl.BlockSpec((N_QKV_BLOCKS, 2, INPROJ_TN), lambda i: (0, 0, 0)),
        ],
        out_specs=[
            pl.BlockSpec((tm, OFF_U), lambda i: (i, 0)),
            pl.BlockSpec((tm, SSM_WIDTH), lambda i: (i, 0)),
            pl.BlockSpec((tm, 2 * D_MODEL), lambda i: (i, 0)),
        ],
        out_shape=[
            jax.ShapeDtypeStruct((t, OFF_U), BF16),
            jax.ShapeDtypeStruct((t, SSM_WIDTH), F32),
            jax.ShapeDtypeStruct((t, 2 * D_MODEL), BF16),
        ],
        compiler_params=pltpu.CompilerParams(
            dimension_semantics=("arbitrary",), vmem_limit_bytes=VMEM_LIMIT),
        name="inproj",
    )(x2, gain, w_bf, bias, seg, ntab)


def _qk_norm_tables(q_gain, k_gain):
    qrow = jnp.tile(q_gain.astype(F32), INPROJ_TN // HEAD_DIM) * (HEAD_DIM ** -0.5)
    ones = jnp.ones((INPROJ_TN,), F32)
    kvrow = jnp.concatenate([jnp.tile(k_gain.astype(F32), N_KV_HEADS), jnp.ones((KV_WIDTH,), F32)])
    kvmask = jnp.concatenate([jnp.ones((KV_WIDTH,), F32), jnp.zeros((KV_WIDTH,), F32)])
    return jnp.stack([jnp.stack([qrow, ones]), jnp.stack([qrow, ones]), jnp.stack([kvrow, kvmask])])


def _segment_ones():
    r = jnp.arange(INPROJ_TN) // HEAD_DIM
    return (r[:, None] == r[None, :]).astype(BF16)


def _attn_bias_tables():
    t_loc = jnp.arange(BLOCK)[:, None]
    s_loc = jnp.arange(2 * BLOCK)[None, :] - BLOCK
    dist = (t_loc - s_loc).astype(F32)
    valid = (dist >= 0) & (dist < WINDOW)
    slopes = jnp.exp2(-8.0 * jnp.arange(1, N_Q_HEADS + 1, dtype=F32) / N_Q_HEADS)
    bias = -slopes[:, None, None] * dist[None]
    full = jnp.where(valid[None], bias, -jnp.inf)
    first = jnp.where((valid & (s_loc >= 0))[None], bias, -jnp.inf)
    return jnp.stack([first, full])


def _attn_kernel(sink_ref, bias_ref, q_ref, kvc_ref, kvp_ref, o_ref):
    kv_rows = 2 * BLOCK
    left_kv = lax.broadcasted_iota(jnp.int32, (kv_rows, LANES), 1) < HEAD_DIM
    left_q = lax.broadcasted_iota(jnp.int32, (BLOCK, LANES), 1) < HEAD_DIM
    zeros = jnp.zeros((kv_rows, LANES), BF16)
    ones_l = jnp.where(left_kv, 1.0, 0.0).astype(BF16)
    ones_r = jnp.where(left_kv, 0.0, 1.0).astype(BF16)
    contract_lanes = (((1,), (1,)), ((), ()))

    def slab(off):
        a = jnp.concatenate([kvp_ref[:, off:off + LANES], kvc_ref[:, off:off + LANES]], axis=0)
        return a, pltpu.roll(a.astype(F32), HEAD_DIM, axis=1).astype(BF16)

    for c in range(N_KV_HEADS // 2):
        k_slabs = slab(c * LANES)
        v_slabs = slab(KV_WIDTH + c * LANES)
        for side in range(2):
            kh = 2 * c + side
            k_l = jnp.where(left_kv, k_slabs[side], zeros)
            k_r = jnp.where(left_kv, zeros, k_slabs[1 - side])
            v_l = jnp.where(left_kv, v_slabs[side], zeros)
            v_r = jnp.where(left_kv, zeros, v_slabs[1 - side])
            kk = jnp.concatenate([k_l, k_r], axis=0)
            vv = jnp.concatenate([jnp.concatenate([v_l, ones_l], axis=1),
                                  jnp.concatenate([v_r, ones_r], axis=1)], axis=0)
            for pair in range(GQA_GROUP // 2):
                e = kh * GQA_GROUP + 2 * pair
                cols = slice(e * HEAD_DIM, (e + 2) * HEAD_DIM)
                s2 = lax.dot_general(q_ref[:, cols], kk, contract_lanes, preferred_element_type=F32)
                ps, ds = [], []
                for t in range(2):
                    s = s2[:, t * kv_rows:(t + 1) * kv_rows] + bias_ref[e + t]
                    sink = sink_ref[e + t]
                    m = jnp.maximum(jnp.max(s, axis=-1, keepdims=True), sink)
                    ps.append(jnp.exp(s - m).astype(BF16))
                    ds.append(jnp.exp(sink - m))
                r = jnp.dot(jnp.concatenate(ps, axis=1), vv, preferred_element_type=F32)
                denom = r[:, LANES:] + jnp.where(left_q, ds[0], ds[1])
                o_ref[:, cols] = (r[:, :LANES] / denom).astype(BF16)


def _attention(zq, sinks, bias_tab, batch, seq):
    t = zq.shape[0]
    nb = seq // BLOCK
    kv_col = OFF_K // (2 * KV_WIDTH)
    return pl.pallas_call(
        _attn_kernel,
        grid=(batch, nb),
        in_specs=[
            pl.BlockSpec(memory_space=pltpu.SMEM),
            pl.BlockSpec((None, N_Q_HEADS, BLOCK, 2 * BLOCK), lambda b, n: (jnp.minimum(n, 1), 0, 0, 0)),
            pl.BlockSpec((BLOCK, ATTN_WIDTH), lambda b, n: (b * nb + n, 0)),
            pl.BlockSpec((BLOCK, 2 * KV_WIDTH), lambda b, n: (b * nb + n, kv_col)),
            pl.BlockSpec((BLOCK, 2 * KV_WIDTH), lambda b, n: (b * nb + jnp.maximum(n - 1, 0), kv_col)),
        ],
        out_specs=pl.BlockSpec((BLOCK, ATTN_WIDTH), lambda b, n: (b * nb + n, 0)),
        out_shape=jax.ShapeDtypeStruct((t, ATTN_WIDTH), BF16),
        compiler_params=pltpu.CompilerParams(
            dimension_semantics=("arbitrary", "arbitrary"), vmem_limit_bytes=VMEM_LIMIT),
        name="swa_attention",
    )(sinks, bias_tab, zq, zq, zq)


def _cmul(ar, ai, br, bi):
    return ar * br - ai * bi, ar * bi + ai * br


def _pow_by_bits(exps, squares):
    pr = jnp.ones(exps.shape, F32)
    pi = jnp.zeros(exps.shape, F32)
    for b, (sr, si) in enumerate(squares):
        on = ((exps >> b) & 1) == 1
        pr, pi = _cmul(pr, pi, jnp.where(on, sr, 1.0), jnp.where(on, si, 0.0))
    return pr, pi


def _ssm_prep_kernel(lam_ref, ldt_ref, btr_ref, bti_ref, ctr_ref, cti_ref, d_ref,
                     tb_ref, c_ref, levr_ref, levi_ref):
    n_levels = levr_ref.shape[1]
    tau_lane = lax.broadcasted_iota(jnp.int32, (SSM_STATE, SUB_W), 1) // SSM_GROUP_CH
    row_h = lax.broadcasted_iota(jnp.int32, (SSM_GROUP_CH, SUB_W), 0)
    lane_h = lax.broadcasted_iota(jnp.int32, (SSM_GROUP_CH, SUB_W), 1)
    tile_ch = (lane_h % SSM_GROUP_CH == row_h).astype(F32)
    row_m = lax.broadcasted_iota(jnp.int32, (SUB, SSM_STATE), 0)
    eye_p = (lax.broadcasted_iota(jnp.int32, (SSM_STATE, SSM_STATE), 0)
             == lax.broadcasted_iota(jnp.int32, (SSM_STATE, SSM_STATE), 1))

    def to_col(v):
        return jnp.sum(jnp.where(eye_p, v, 0.0), axis=1, keepdims=True)

    def squares(a, n):
        out = [a]
        for _ in range(n - 1):
            out.append(_cmul(*out[-1], *out[-1]))
        return out

    for g in range(SLAB_GROUPS):
        dt = jnp.exp(ldt_ref[g])

        def discretise(lr, li):
            mag = jnp.exp(lr * dt)
            return mag * jnp.cos(li * dt), mag * jnp.sin(li * dt)

        lr_row, li_row = lam_ref[g, 0:1, :], lam_ref[g, 1:2, :]
        a_row = discretise(lr_row, li_row)
        sq_row = squares(a_row, 5)
        sq_col = squares((to_col(a_row[0]), to_col(a_row[1])), 4)
        ar, ai = sq_row[0]
        den = lr_row * lr_row + li_row * li_row
        fr = ((ar - 1.0) * lr_row + ai * li_row) / den
        fi = (ai * lr_row - (ar - 1.0) * li_row) / den
        bbr, bbi = _cmul(fr, fi, btr_ref[g], bti_ref[g])

        e0 = _pow_by_bits(tau_lane, sq_col)
        e1 = _cmul(*e0, *sq_col[0])
        ctr = jnp.dot(ctr_ref[g], tile_ch, precision=HIGHEST, preferred_element_type=F32)
        cti = jnp.dot(cti_ref[g], tile_ch, precision=HIGHEST, preferred_element_type=F32)
        mr, mi = _cmul(*e0, ctr, cti)
        kt = (jnp.dot(bbr, mr, precision=HIGHEST, preferred_element_type=F32)
              - jnp.dot(bbi, mi, precision=HIGHEST, preferred_element_type=F32))
        kt = kt + jnp.where(lane_h == row_h, d_ref[g], 0.0)
        pw = _pow_by_bits(row_m, sq_row[:4])
        for j in range(SUB):
            rows = pl.ds(j * SSM_GROUP_CH, SSM_GROUP_CH)
            tj = kt if j == 0 else jnp.where(lane_h >= j * SSM_GROUP_CH,
                                             pltpu.roll(kt, j * SSM_GROUP_CH, axis=1), 0.0)
            tb_ref[g, rows, 0:SUB_W] = tj.astype(BF16)
            m = SUB - 1 - j
            br_, bi_ = _cmul(pw[0][m:m + 1, :], pw[1][m:m + 1, :], bbr, bbi)
            tb_ref[g, rows, SUB_W:SUB_W + STATE_W] = jnp.concatenate([br_, bi_], axis=1).astype(BF16)
            tb_ref[g, rows, SUB_W + STATE_W:TB_W] = jnp.concatenate([bi_, br_], axis=1).astype(BF16)
        c_ref[g, 0:SSM_STATE, :] = (ctr * e1[0] - cti * e1[1]).astype(BF16)
        c_ref[g, SSM_STATE:STATE_W, :] = (-ctr * e1[1] - cti * e1[0]).astype(BF16)
        lv = sq_row[4]
        lev_r, lev_i = [], []
        for _ in range(n_levels):
            lev_r.append(jnp.concatenate([lv[0], lv[0]], axis=1))
            lev_i.append(jnp.concatenate([-lv[1], lv[1]], axis=1))
            lv = _cmul(*lv, *lv)
        levr_ref[g] = jnp.concatenate(lev_r, axis=0)
        levi_ref[g] = jnp.concatenate(lev_i, axis=0)


def _ssm_prep(lam_re, lam_im, log_dt, b_re, b_im, c_re, c_im, d_skip, n_levels):
    dg = lam_re.shape[0] * SSM_GROUPS
    h_, p_ = SSM_GROUP_CH, SSM_STATE
    f = lambda a: a.astype(F32)
    lam_rows = jnp.stack([f(lam_re), f(lam_im)], axis=2).reshape(dg, 2, p_)
    btr = f(b_re).transpose(0, 1, 3, 2).reshape(dg, h_, p_)
    bti = f(b_im).transpose(0, 1, 3, 2).reshape(dg, h_, p_)
    ctr = f(c_re).transpose(0, 1, 3, 2).reshape(dg, p_, h_)
    cti = f(c_im).transpose(0, 1, 3, 2).reshape(dg, p_, h_)
    d_rows = jnp.pad(f(d_skip).reshape(dg, 1, h_), ((0, 0), (0, 0), (0, SUB_W - h_)))
    grp = lambda *shape: pl.BlockSpec((SLAB_GROUPS,) + shape, lambda s: (s,) + (0,) * len(shape))
    return pl.pallas_call(
        _ssm_prep_kernel,
        grid=(dg // SLAB_GROUPS,),
        in_specs=[grp(2, p_), grp(1, 1), grp(h_, p_), grp(h_, p_),
                  grp(p_, h_), grp(p_, h_), grp(1, SUB_W)],
        out_specs=[grp(SUB_W, TB_W), grp(STATE_W, SUB_W), grp(n_levels, STATE_W), grp(n_levels, STATE_W)],
        out_shape=[
            jax.ShapeDtypeStruct((dg, SUB_W, TB_W), BF16),
            jax.ShapeDtypeStruct((dg, STATE_W, SUB_W), BF16),
            jax.ShapeDtypeStruct((dg, n_levels, STATE_W), F32),
            jax.ShapeDtypeStruct((dg, n_levels, STATE_W), F32),
        ],
        compiler_params=pltpu.CompilerParams(
            dimension_semantics=("arbitrary",), vmem_limit_bytes=VMEM_LIMIT),
        name="s5_prep",
    )(lam_rows, f(log_dt).reshape(dg, 1, 1), btr, bti, ctr, cti, d_rows)


def _slab_permutations():
    r = jnp.arange(SUB * LANES)
    j, lane = r // LANES, r % LANES
    grp, ch = lane // SSM_GROUP_CH, lane % SSM_GROUP_CH
    col = grp * SUB_W + j * SSM_GROUP_CH + ch
    full = (col[:, None] == jnp.arange(SLAB_W)[None, :]).astype(BF16)
    split = lambda m, n: m.reshape(m.shape[0], n, m.shape[1] // n).transpose(1, 0, 2)
    return split(full, SLAB_GROUPS), split(full.T, SUB // 2)


S5_STAGES = 3


def _ssm_kernel(u_ref, pin_ref, pout_ref, tb_ref, c_ref, ar_ref, ai_ref, y_ref, xs_ref, ys_ref):
    t = pl.program_id(0)
    rows = xs_ref.shape[1]
    n_levels = rows.bit_length() - 1
    cur, prv = t % 2, (t + 1) % 2

    @pl.when(t == 0)
    def _():
        xs_ref[...] = jnp.zeros(xs_ref.shape, BF16)
        ys_ref[...] = jnp.zeros(ys_ref.shape, BF16)

    def shifted(a, sh):
        return jnp.concatenate([jnp.zeros((sh, a.shape[1]), a.dtype), a[:rows - sh]], axis=0)

    ys_done = ys_ref[cur]
    u_all = jnp.concatenate([u_ref[pl.ds(j, rows, stride=SUB), :] for j in range(SUB)],
                            axis=1).astype(BF16)
    def scan_inputs(g):
        return jnp.dot(xs_ref[prv, :, g * SUB_W:(g + 1) * SUB_W], tb_ref[g], preferred_element_type=F32)

    r_next = scan_inputs(0)
    for g in range(SLAB_GROUPS):
        r = r_next
        if g + 1 < SLAB_GROUPS:
            r_next = scan_inputs(g + 1)
        xs_ref[cur, :, g * SUB_W:(g + 1) * SUB_W] = jnp.dot(
            u_all, pin_ref[g], preferred_element_type=F32).astype(BF16)
        o = jnp.dot(ys_done, pout_ref[g], preferred_element_type=F32)
        y_ref[pl.ds(2 * g, rows, stride=SUB), :] = o[:, :LANES]
        y_ref[pl.ds(2 * g + 1, rows, stride=SUB), :] = o[:, LANES:]
        yt = r[:, :SUB_W]
        s = r[:, SUB_W:SUB_W + STATE_W]
        w = r[:, SUB_W + STATE_W:]
        for k in range(n_levels):
            sh = 1 << k
            ar = ar_ref[g, k:k + 1, :]
            ai = ai_ref[g, k:k + 1, :]
            ps, pw = shifted(s, sh), shifted(w, sh)
            s, w = s + ar * ps + ai * pw, (w + ar * pw - ai * ps if k + 1 < n_levels else None)
        y = yt + jnp.dot(shifted(s, 1).astype(BF16), c_ref[g], preferred_element_type=F32)
        ys_ref[prv, :, g * SUB_W:(g + 1) * SUB_W] = jax.nn.gelu(y).astype(BF16)


def _ssm(uf, perm_in, perm_out, tb, cmat, lev_r, lev_i, layer, batch, seq):
    t = uf.shape[0]
    n_sub = seq // SUB
    n_levels = lev_r.shape[1]
    n_items = batch * N_SLABS
    item = lambda t, stage: jnp.clip(t - stage, 0, n_items - 1)
    const3 = lambda t: (0, 0, 0)
    slab3 = lambda t: (layer * N_SLABS + item(t, 1) % N_SLABS, 0, 0)
    return pl.pallas_call(
        _ssm_kernel,
        grid=(n_items + S5_STAGES - 1,),
        in_specs=[
            pl.BlockSpec((seq, LANES), lambda t: (item(t, 0) // N_SLABS, item(t, 0) % N_SLABS)),
            pl.BlockSpec((SLAB_GROUPS, SUB * LANES, SUB_W), const3, pipeline_mode=pl.Buffered(1)),
            pl.BlockSpec((SUB // 2, SLAB_W, 2 * LANES), const3, pipeline_mode=pl.Buffered(1)),
            pl.BlockSpec((SLAB_GROUPS, SUB_W, TB_W), slab3),
            pl.BlockSpec((SLAB_GROUPS, STATE_W, SUB_W), slab3),
            pl.BlockSpec((SLAB_GROUPS, n_levels, STATE_W), slab3),
            pl.BlockSpec((SLAB_GROUPS, n_levels, STATE_W), slab3),
        ],
        out_specs=pl.BlockSpec((seq, LANES), lambda t: (item(t, 2) // N_SLABS, item(t, 2) % N_SLABS)),
        out_shape=jax.ShapeDtypeStruct((t, SSM_WIDTH), F32),
        scratch_shapes=[pltpu.VMEM((2, n_sub, SLAB_W), BF16), pltpu.VMEM((2, n_sub, SLAB_W), BF16)],
        compiler_params=pltpu.CompilerParams(
            dimension_semantics=("arbitrary",), vmem_limit_bytes=VMEM_LIMIT),
        name="s5_scan",
    )(uf, perm_in, perm_out, tb, cmat, lev_r, lev_i)


def _merge_kernel(ya_ref, yg_ref, gt_ref, x_ref, wglu_ref, bglu_ref, wa_ref, ws_ref, wo_ref,
                  gffn_ref, x1_ref, h2_ref):
    yg = yg_ref[...]
    t = jnp.dot(yg.astype(BF16), wglu_ref[...], preferred_element_type=F32) + bglu_ref[...]
    ys = (yg * _sigmoid(t)).astype(BF16)
    ma = jnp.dot(ya_ref[...], wa_ref[...], preferred_element_type=F32)
    ms = jnp.dot(ys, ws_ref[...], preferred_element_type=F32)
    merged = gt_ref[:, :D_MODEL].astype(F32) * ma + gt_ref[:, D_MODEL:].astype(F32) * ms
    x1 = x_ref[...] + jnp.dot(merged.astype(BF16), wo_ref[...], preferred_element_type=F32)
    x1_ref[...] = x1
    h2_ref[...] = _rms(x1, gffn_ref[...]).astype(BF16)


def _merge(ya, yg, gates, x2, wglu, bglu, wa, ws, wo, gffn, layer):
    t = x2.shape[0]
    tm = MERGE_TM

    def wspec(rows, cols):
        return pl.BlockSpec((None, rows, cols), lambda i: (layer, 0, 0), pipeline_mode=pl.Buffered(1))

    return pl.pallas_call(
        _merge_kernel,
        grid=(t // tm,),
        in_specs=[
            pl.BlockSpec((tm, ATTN_WIDTH), lambda i: (i, 0)),
            pl.BlockSpec((tm, SSM_WIDTH), lambda i: (i, 0)),
            pl.BlockSpec((tm, 2 * D_MODEL), lambda i: (i, 0)),
            pl.BlockSpec((tm, D_MODEL), lambda i: (i, 0)),
            wspec(SSM_WIDTH, SSM_WIDTH),
            pl.BlockSpec((1, SSM_WIDTH), lambda i: (0, 0)),
            wspec(ATTN_WIDTH, D_MODEL),
            wspec(SSM_WIDTH, D_MODEL),
            wspec(D_MODEL, D_MODEL),
            pl.BlockSpec((1, D_MODEL), lambda i: (0, 0)),
        ],
        out_specs=[
            pl.BlockSpec((tm, D_MODEL), lambda i: (i, 0)),
            pl.BlockSpec((tm, D_MODEL), lambda i: (i, 0)),
        ],
        out_shape=[
            jax.ShapeDtypeStruct((t, D_MODEL), F32),
            jax.ShapeDtypeStruct((t, D_MODEL), BF16),
        ],
        compiler_params=pltpu.CompilerParams(
            dimension_semantics=("arbitrary",), vmem_limit_bytes=VMEM_LIMIT),
        name="merge_out",
    )(ya, yg, gates, x2, wglu, bglu, wa, ws, wo, gffn)


def _ffn_kernel(h_ref, x_ref, wg_ref, wu_ref, wo_ref, o_ref):
    k = pl.program_id(1)

    @pl.when(k == 0)
    def _():
        o_ref[...] = x_ref[...]

    for r in range(FFN_TM // FFN_SUB):
        rs = pl.ds(r * FFN_SUB, FFN_SUB)
        h = h_ref[rs, :]
        g = jnp.dot(h, wg_ref[...], preferred_element_type=F32)
        u = jnp.dot(h, wu_ref[...], preferred_element_type=F32)
        act = (g * _sigmoid(g) * u).astype(BF16)
        o_ref[rs, :] += jnp.dot(act, wo_ref[...], preferred_element_type=F32)


def _ffn(h2, x1, w_in, w_out, layer):
    t = x1.shape[0]
    nk = D_FF // FFN_TF
    return pl.pallas_call(
        _ffn_kernel,
        grid=(t // FFN_TM, nk),
        in_specs=[
            pl.BlockSpec((FFN_TM, D_MODEL), lambda i, k: (i, 0)),
            pl.BlockSpec((FFN_TM, D_MODEL), lambda i, k: (i, 0)),
            pl.BlockSpec((None, D_MODEL, FFN_TF), lambda i, k: (layer, 0, k)),
            pl.BlockSpec((None, D_MODEL, FFN_TF), lambda i, k: (layer, 0, nk + k)),
            pl.BlockSpec((None, FFN_TF, D_MODEL), lambda i, k: (layer, k, 0)),
        ],
        out_specs=pl.BlockSpec((FFN_TM, D_MODEL), lambda i, k: (i, 0)),
        out_shape=jax.ShapeDtypeStruct((t, D_MODEL), F32),
        compiler_params=pltpu.CompilerParams(
            dimension_semantics=("arbitrary", "arbitrary"), vmem_limit_bytes=FFN_VMEM_LIMIT),
        name="swiglu_ffn",
    )(h2, x1, w_in, w_in, w_out)


def kernel(x, norm_mix_g, w_in, gate_bias, q_norm_g, k_norm_g, attn_sinks, ssm_lambda_re, ssm_lambda_im, ssm_log_dt, ssm_b_re, ssm_b_im, ssm_c_re, ssm_c_im, ssm_d, ssm_glu_w, ssm_glu_b, w_attn_branch, w_ssm_branch, w_out, norm_ffn_g, w_ffn_in, w_ffn_out):
    batch, seq, _ = x.shape
    t = batch * seq
    n_levels = (seq // SUB).bit_length() - 1
    x2 = x.reshape(t, D_MODEL).astype(F32)
    w_in_bf = w_in.astype(BF16)
    wfi_bf = w_ffn_in.astype(BF16)
    wglu_bf = ssm_glu_w.astype(BF16)
    wa_bf = w_attn_branch.astype(BF16)
    ws_bf = w_ssm_branch.astype(BF16)
    wo_bf = w_out.astype(BF16)
    wfo_bf = w_ffn_out.astype(BF16)
    seg = _segment_ones()
    bias_tab = _attn_bias_tables()
    perm_in, perm_out = _slab_permutations()
    tb, cmat, lev_r, lev_i = _ssm_prep(ssm_lambda_re, ssm_lambda_im, ssm_log_dt, ssm_b_re, ssm_b_im,
                                       ssm_c_re, ssm_c_im, ssm_d, n_levels)
    for l in range(DEPTH):
        zq, uf, gates = _inproj(x2, norm_mix_g[l].reshape(1, D_MODEL).astype(F32), w_in_bf, l,
                                gate_bias[l].reshape(1, 2 * D_MODEL).astype(F32), seg,
                                _qk_norm_tables(q_norm_g[l], k_norm_g[l]))
        ya = _attention(zq, attn_sinks[l].astype(F32), bias_tab, batch, seq)
        yg = _ssm(uf, perm_in, perm_out, tb, cmat, lev_r, lev_i, l, batch, seq)
        x1, h2 = _merge(ya, yg, gates, x2, wglu_bf, ssm_glu_b[l].reshape(1, SSM_WIDTH).astype(F32),
                        wa_bf, ws_bf, wo_bf, norm_ffn_g[l].reshape(1, D_MODEL).astype(F32), l)
        x2 = _ffn(h2, x1, wfi_bf, wfo_bf, l)
    return x2.reshape(batch, seq, D_MODEL).astype(x.dtype)
```

```python
import jax
import jax.numpy as jnp
from jax import lax
from jax.experimental import pallas as pl
from jax.experimental.pallas import tpu as pltpu

D_MODEL = 2048
DEPTH = 2
HEAD_DIM = 64
N_Q_HEADS = 16
N_KV_HEADS = 4
GQA_GROUP = N_Q_HEADS // N_KV_HEADS
ATTN_WIDTH = N_Q_HEADS * HEAD_DIM
KV_WIDTH = N_KV_HEADS * HEAD_DIM
WINDOW = 128
BLOCK = 128
SSM_WIDTH = D_MODEL // 2
SSM_GROUP_CH = 16
SSM_GROUPS = SSM_WIDTH // SSM_GROUP_CH
SSM_STATE = 64
D_FF = -(-8 * D_MODEL // (3 * 256)) * 256
OFF_K = ATTN_WIDTH
OFF_V = OFF_K + KV_WIDTH
OFF_U = OFF_V + KV_WIDTH
OFF_G = OFF_U + SSM_WIDTH
IN_WIDTH = OFF_G + 2 * D_MODEL
RMS_EPS = 1e-6

F32 = jnp.float32
BF16 = jnp.bfloat16
HIGHEST = lax.Precision.HIGHEST

LANES = 128
BF16_SUBLANES = 16
SUB = 16
SUB_W = SUB * SSM_GROUP_CH
STATE_W = 2 * SSM_STATE
TB_W = SUB_W + 2 * STATE_W
SLAB_GROUPS = LANES // SSM_GROUP_CH
N_SLABS = SSM_GROUPS // SLAB_GROUPS
SLAB_W = SLAB_GROUPS * SUB_W

VMEM_LIMIT = 56 * 1024 * 1024
FFN_VMEM_LIMIT = 60 * 1024 * 1024

INPROJ_TM = 256
INPROJ_TN = 512
MERGE_TM = 256
FFN_TM = 1024
FFN_TF = 512
FFN_SUB = 512


def _rms(x, g):
    return x * lax.rsqrt(jnp.mean(x * x, axis=-1, keepdims=True) + RMS_EPS) * g


def _sigmoid(x):
    return 0.5 * jnp.tanh(0.5 * x) + 0.5


class _RowCast:
    def __init__(self, stacked, layer, n_steps, step_of):
        _, rows, cols = stacked.shape
        blk = rows // n_steps
        assert blk * n_steps == rows and blk % BF16_SUBLANES == 0, (rows, n_steps)
        self.operand = stacked
        self.in_spec = pl.BlockSpec((None, blk, cols), lambda *ids: (layer, step_of(*ids), 0))
        self.out_spec = pl.BlockSpec((blk, cols), lambda *ids: (step_of(*ids), 0))
        self.out_shape = jax.ShapeDtypeStruct((rows, cols), BF16)


def _cast_blocks(src_refs, dst_refs):
    for src, dst in zip(src_refs, dst_refs):
        dst[...] = src[...].astype(BF16)


N_QKV_BLOCKS = OFF_U // INPROJ_TN
N_MAIN_BLOCKS = OFF_G // INPROJ_TN
N_IN_BLOCKS = IN_WIDTH // INPROJ_TN


def _inproj_kernel(x_ref, g_ref, w_ref, b_ref, seg_ref, nt_ref, *refs):
    n_cast = (len(refs) - 3) // 2
    zq_ref, u_ref, gt_ref = refs[n_cast:n_cast + 3]
    _cast_blocks(refs[:n_cast], refs[n_cast + 3:])
    h = _rms(x_ref[...], g_ref[...]).astype(BF16)
    for j in range(N_IN_BLOCKS):
        z = jnp.dot(h, w_ref[:, j * INPROJ_TN:(j + 1) * INPROJ_TN], preferred_element_type=F32)
        if j < N_QKV_BLOCKS:
            zz = z * z
            hi = zz.astype(BF16)
            lo = (zz - hi.astype(F32)).astype(BF16)
            ssq = (jnp.dot(hi, seg_ref[...], preferred_element_type=F32)
                   + jnp.dot(lo, seg_ref[...], preferred_element_type=F32))
            inv = lax.rsqrt(ssq * (1.0 / HEAD_DIM) + RMS_EPS)
            fac = jnp.where(nt_ref[j, 1:2, :] > 0.0, inv, 1.0) * nt_ref[j, 0:1, :]
            zq_ref[:, j * INPROJ_TN:(j + 1) * INPROJ_TN] = (z * fac).astype(BF16)
        elif j < N_MAIN_BLOCKS:
            c = j - N_QKV_BLOCKS
            u_ref[:, c * INPROJ_TN:(c + 1) * INPROJ_TN] = z
        else:
            cs = slice((j - N_MAIN_BLOCKS) * INPROJ_TN, (j - N_MAIN_BLOCKS + 1) * INPROJ_TN)
            gt_ref[:, cs] = _sigmoid(z + b_ref[:, cs]).astype(BF16)


def _inproj(x2, gain, w_bf, layer, bias, seg, ntab, cast_weights):
    t = x2.shape[0]
    tm = INPROJ_TM
    const2 = lambda i: (0, 0)
    casts = [_RowCast(w, layer, t // tm, lambda i: i) for w in cast_weights]
    outs = pl.pallas_call(
        _inproj_kernel,
        grid=(t // tm,),
        in_specs=[
            pl.BlockSpec((tm, D_MODEL), lambda i: (i, 0)),
            pl.BlockSpec((1, D_MODEL), const2),
            pl.BlockSpec((None, D_MODEL, IN_WIDTH), lambda i: (layer, 0, 0), pipeline_mode=pl.Buffered(1)),
            pl.BlockSpec((1, 2 * D_MODEL), const2),
            pl.BlockSpec((INPROJ_TN, INPROJ_TN), const2),
            pl.BlockSpec((N_QKV_BLOCKS, 2, INPROJ_TN), lambda i: (0, 0, 0)),
        ] + [c.in_spec for c in casts],
        out_specs=[
            pl.BlockSpec((tm, OFF_U), lambda i: (i, 0)),
            pl.BlockSpec((tm, SSM_WIDTH), lambda i: (i, 0)),
            pl.BlockSpec((tm, 2 * D_MODEL), lambda i: (i, 0)),
        ] + [c.out_spec for c in casts],
        out_shape=[
            jax.ShapeDtypeStruct((t, OFF_U), BF16),
            jax.ShapeDtypeStruct((t, SSM_WIDTH), F32),
            jax.ShapeDtypeStruct((t, 2 * D_MODEL), BF16),
        ] + [c.out_shape for c in casts],
        compiler_params=pltpu.CompilerParams(
            dimension_semantics=("arbitrary",), vmem_limit_bytes=VMEM_LIMIT),
        name="inproj",
    )(x2, gain, w_bf, bias, seg, ntab, *[c.operand for c in casts])
    return outs[:3], outs[3:]


def _qk_norm_tables(q_gain, k_gain):
    qrow = jnp.tile(q_gain.astype(F32), INPROJ_TN // HEAD_DIM) * (HEAD_DIM ** -0.5)
    ones = jnp.ones((INPROJ_TN,), F32)
    kvrow = jnp.concatenate([jnp.tile(k_gain.astype(F32), N_KV_HEADS), jnp.ones((KV_WIDTH,), F32)])
    kvmask = jnp.concatenate([jnp.ones((KV_WIDTH,), F32), jnp.zeros((KV_WIDTH,), F32)])
    return jnp.stack([jnp.stack([qrow, ones]), jnp.stack([qrow, ones]), jnp.stack([kvrow, kvmask])])


def _segment_ones():
    r = jnp.arange(INPROJ_TN) // HEAD_DIM
    return (r[:, None] == r[None, :]).astype(BF16)


def _attn_bias_tables():
    t_loc = jnp.arange(BLOCK)[:, None]
    s_loc = jnp.arange(2 * BLOCK)[None, :] - BLOCK
    dist = (t_loc - s_loc).astype(F32)
    valid = (dist >= 0) & (dist < WINDOW)
    slopes = jnp.exp2(-8.0 * jnp.arange(1, N_Q_HEADS + 1, dtype=F32) / N_Q_HEADS)
    bias = -slopes[:, None, None] * dist[None]
    full = jnp.where(valid[None], bias, -jnp.inf)
    first = jnp.where((valid & (s_loc >= 0))[None], bias, -jnp.inf)
    return jnp.stack([first, full])


def _attn_kernel(sink_ref, bias_ref, q_ref, kvc_ref, kvp_ref, o_ref):
    kv_rows = 2 * BLOCK
    left_kv = lax.broadcasted_iota(jnp.int32, (kv_rows, LANES), 1) < HEAD_DIM
    left_q = lax.broadcasted_iota(jnp.int32, (BLOCK, LANES), 1) < HEAD_DIM
    zeros = jnp.zeros((kv_rows, LANES), BF16)
    ones_l = jnp.where(left_kv, 1.0, 0.0).astype(BF16)
    ones_r = jnp.where(left_kv, 0.0, 1.0).astype(BF16)
    contract_lanes = (((1,), (1,)), ((), ()))

    def slab(off):
        a = jnp.concatenate([kvp_ref[:, off:off + LANES], kvc_ref[:, off:off + LANES]], axis=0)
        return a, pltpu.roll(a.astype(F32), HEAD_DIM, axis=1).astype(BF16)

    for c in range(N_KV_HEADS // 2):
        k_slabs = slab(c * LANES)
        v_slabs = slab(KV_WIDTH + c * LANES)
        for side in range(2):
            kh = 2 * c + side
            k_l = jnp.where(left_kv, k_slabs[side], zeros)
            k_r = jnp.where(left_kv, zeros, k_slabs[1 - side])
            v_l = jnp.where(left_kv, v_slabs[side], zeros)
            v_r = jnp.where(left_kv, zeros, v_slabs[1 - side])
            kk = jnp.concatenate([k_l, k_r], axis=0)
            vv = jnp.concatenate([jnp.concatenate([v_l, ones_l], axis=1),
                                  jnp.concatenate([v_r, ones_r], axis=1)], axis=0)
            for pair in range(GQA_GROUP // 2):
                e = kh * GQA_GROUP + 2 * pair
                cols = slice(e * HEAD_DIM, (e + 2) * HEAD_DIM)
                s2 = lax.dot_general(q_ref[:, cols], kk, contract_lanes, preferred_element_type=F32)
                ps, ds = [], []
                for t in range(2):
                    s = s2[:, t * kv_rows:(t + 1) * kv_rows] + bias_ref[e + t]
                    sink = sink_ref[e + t]
                    m = jnp.maximum(jnp.max(s, axis=-1, keepdims=True), sink)
                    ps.append(jnp.exp(s - m).astype(BF16))
                    ds.append(jnp.exp(sink - m))
                r = jnp.dot(jnp.concatenate(ps, axis=1), vv, preferred_element_type=F32)
                denom = r[:, LANES:] + jnp.where(left_q, ds[0], ds[1])
                o_ref[:, cols] = (r[:, :LANES] / denom).astype(BF16)


def _attention(zq, sinks, bias_tab, batch, seq):
    t = zq.shape[0]
    nb = seq // BLOCK
    kv_col = OFF_K // (2 * KV_WIDTH)
    return pl.pallas_call(
        _attn_kernel,
        grid=(batch, nb),
        in_specs=[
            pl.BlockSpec(memory_space=pltpu.SMEM),
            pl.BlockSpec((None, N_Q_HEADS, BLOCK, 2 * BLOCK), lambda b, n: (jnp.minimum(n, 1), 0, 0, 0)),
            pl.BlockSpec((BLOCK, ATTN_WIDTH), lambda b, n: (b * nb + n, 0)),
            pl.BlockSpec((BLOCK, 2 * KV_WIDTH), lambda b, n: (b * nb + n, kv_col)),
            pl.BlockSpec((BLOCK, 2 * KV_WIDTH), lambda b, n: (b * nb + jnp.maximum(n - 1, 0), kv_col)),
        ],
        out_specs=pl.BlockSpec((BLOCK, ATTN_WIDTH), lambda b, n: (b * nb + n, 0)),
        out_shape=jax.ShapeDtypeStruct((t, ATTN_WIDTH), BF16),
        compiler_params=pltpu.CompilerParams(
            dimension_semantics=("arbitrary", "arbitrary"), vmem_limit_bytes=VMEM_LIMIT),
        name="swa_attention",
    )(sinks, bias_tab, zq, zq, zq)


def _cmul(ar, ai, br, bi):
    return ar * br - ai * bi, ar * bi + ai * br


def _pow_by_bits(exps, squares):
    pr = jnp.ones(exps.shape, F32)
    pi = jnp.zeros(exps.shape, F32)
    for b, (sr, si) in enumerate(squares):
        on = ((exps >> b) & 1) == 1
        pr, pi = _cmul(pr, pi, jnp.where(on, sr, 1.0), jnp.where(on, si, 0.0))
    return pr, pi


def _ssm_prep_kernel(lam_ref, ldt_ref, btr_ref, bti_ref, ctr_ref, cti_ref, d_ref, *refs):
    n_cast = (len(refs) - 4) // 2
    tb_ref, c_ref, levr_ref, levi_ref = refs[n_cast:n_cast + 4]
    _cast_blocks(refs[:n_cast], refs[n_cast + 4:])
    n_levels = levr_ref.shape[1]
    tau_lane = lax.broadcasted_iota(jnp.int32, (SSM_STATE, SUB_W), 1) // SSM_GROUP_CH
    row_h = lax.broadcasted_iota(jnp.int32, (SSM_GROUP_CH, SUB_W), 0)
    lane_h = lax.broadcasted_iota(jnp.int32, (SSM_GROUP_CH, SUB_W), 1)
    tile_ch = (lane_h % SSM_GROUP_CH == row_h).astype(F32)
    row_m = lax.broadcasted_iota(jnp.int32, (SUB, SSM_STATE), 0)
    eye_p = (lax.broadcasted_iota(jnp.int32, (SSM_STATE, SSM_STATE), 0)
             == lax.broadcasted_iota(jnp.int32, (SSM_STATE, SSM_STATE), 1))

    def to_col(v):
        return jnp.sum(jnp.where(eye_p, v, 0.0), axis=1, keepdims=True)

    def squares(a, n):
        out = [a]
        for _ in range(n - 1):
            out.append(_cmul(*out[-1], *out[-1]))
        return out

    for g in range(SLAB_GROUPS):
        dt = jnp.exp(ldt_ref[g])

        def discretise(lr, li):
            mag = jnp.exp(lr * dt)
            return mag * jnp.cos(li * dt), mag * jnp.sin(li * dt)

        lr_row, li_row = lam_ref[g, 0:1, :], lam_ref[g, 1:2, :]
        a_row = discretise(lr_row, li_row)
        sq_row = squares(a_row, 5)
        sq_col = squares((to_col(a_row[0]), to_col(a_row[1])), 4)
        ar, ai = sq_row[0]
        den = lr_row * lr_row + li_row * li_row
        fr = ((ar - 1.0) * lr_row + ai * li_row) / den
        fi = (ai * lr_row - (ar - 1.0) * li_row) / den
        bbr, bbi = _cmul(fr, fi, btr_ref[g], bti_ref[g])

        e0 = _pow_by_bits(tau_lane, sq_col)
        e1 = _cmul(*e0, *sq_col[0])
        ctr = jnp.dot(ctr_ref[g], tile_ch, precision=HIGHEST, preferred_element_type=F32)
        cti = jnp.dot(cti_ref[g], tile_ch, precision=HIGHEST, preferred_element_type=F32)
        mr, mi = _cmul(*e0, ctr, cti)
        kt = (jnp.dot(bbr, mr, precision=HIGHEST, preferred_element_type=F32)
              - jnp.dot(bbi, mi, precision=HIGHEST, preferred_element_type=F32))
        kt = kt + jnp.where(lane_h == row_h, d_ref[g], 0.0)
        pw = _pow_by_bits(row_m, sq_row[:4])
        for j in range(SUB):
            rows = pl.ds(j * SSM_GROUP_CH, SSM_GROUP_CH)
            tj = kt if j == 0 else jnp.where(lane_h >= j * SSM_GROUP_CH,
                                             pltpu.roll(kt, j * SSM_GROUP_CH, axis=1), 0.0)
            tb_ref[g, rows, 0:SUB_W] = tj.astype(BF16)
            m = SUB - 1 - j
            br_, bi_ = _cmul(pw[0][m:m + 1, :], pw[1][m:m + 1, :], bbr, bbi)
            tb_ref[g, rows, SUB_W:SUB_W + STATE_W] = jnp.concatenate([br_, bi_], axis=1).astype(BF16)
            tb_ref[g, rows, SUB_W + STATE_W:TB_W] = jnp.concatenate([bi_, br_], axis=1).astype(BF16)
        c_ref[g, 0:SSM_STATE, :] = (ctr * e1[0] - cti * e1[1]).astype(BF16)
        c_ref[g, SSM_STATE:STATE_W, :] = (-ctr * e1[1] - cti * e1[0]).astype(BF16)
        lv = sq_row[4]
        lev_r, lev_i = [], []
        for _ in range(n_levels):
            lev_r.append(jnp.concatenate([lv[0], lv[0]], axis=1))
            lev_i.append(jnp.concatenate([-lv[1], lv[1]], axis=1))
            lv = _cmul(*lv, *lv)
        levr_ref[g] = jnp.concatenate(lev_r, axis=0)
        levi_ref[g] = jnp.concatenate(lev_i, axis=0)


def _ssm_prep(lam_re, lam_im, log_dt, b_re, b_im, c_re, c_im, d_skip, n_levels, cast_weights):
    dg = lam_re.shape[0] * SSM_GROUPS
    flat = lambda w: w.reshape((1, w.shape[0] * w.shape[1]) + w.shape[2:])
    casts = [_RowCast(flat(w), 0, dg // SLAB_GROUPS, lambda s: s) for w in cast_weights]
    h_, p_ = SSM_GROUP_CH, SSM_STATE
    f = lambda a: a.astype(F32)
    lam_rows = jnp.stack([f(lam_re), f(lam_im)], axis=2).reshape(dg, 2, p_)
    btr = f(b_re).transpose(0, 1, 3, 2).reshape(dg, h_, p_)
    bti = f(b_im).transpose(0, 1, 3, 2).reshape(dg, h_, p_)
    ctr = f(c_re).transpose(0, 1, 3, 2).reshape(dg, p_, h_)
    cti = f(c_im).transpose(0, 1, 3, 2).reshape(dg, p_, h_)
    d_rows = jnp.pad(f(d_skip).reshape(dg, 1, h_), ((0, 0), (0, 0), (0, SUB_W - h_)))
    grp = lambda *shape: pl.BlockSpec((SLAB_GROUPS,) + shape, lambda s: (s,) + (0,) * len(shape))
    outs = pl.pallas_call(
        _ssm_prep_kernel,
        grid=(dg // SLAB_GROUPS,),
        in_specs=[grp(2, p_), grp(1, 1), grp(h_, p_), grp(h_, p_),
                  grp(p_, h_), grp(p_, h_), grp(1, SUB_W)] + [c.in_spec for c in casts],
        out_specs=[grp(SUB_W, TB_W), grp(STATE_W, SUB_W), grp(n_levels, STATE_W),
                   grp(n_levels, STATE_W)] + [c.out_spec for c in casts],
        out_shape=[
            jax.ShapeDtypeStruct((dg, SUB_W, TB_W), BF16),
            jax.ShapeDtypeStruct((dg, STATE_W, SUB_W), BF16),
            jax.ShapeDtypeStruct((dg, n_levels, STATE_W), F32),
            jax.ShapeDtypeStruct((dg, n_levels, STATE_W), F32),
        ] + [c.out_shape for c in casts],
        compiler_params=pltpu.CompilerParams(
            dimension_semantics=("arbitrary",), vmem_limit_bytes=VMEM_LIMIT),
        name="s5_prep",
    )(lam_rows, f(log_dt).reshape(dg, 1, 1), btr, bti, ctr, cti, d_rows, *[c.operand for c in casts])
    return outs[:4], [o.reshape(w.shape) for o, w in zip(outs[4:], cast_weights)]


def _slab_permutations():
    r = jnp.arange(SUB * LANES)
    j, lane = r // LANES, r % LANES
    grp, ch = lane // SSM_GROUP_CH, lane % SSM_GROUP_CH
    col = grp * SUB_W + j * SSM_GROUP_CH + ch
    full = (col[:, None] == jnp.arange(SLAB_W)[None, :]).astype(BF16)
    split = lambda m, n: m.reshape(m.shape[0], n, m.shape[1] // n).transpose(1, 0, 2)
    return split(full, SLAB_GROUPS), split(full.T, SUB // 2)


S5_STAGES = 3


def _ssm_kernel(u_ref, pin_ref, pout_ref, tb_ref, c_ref, ar_ref, ai_ref, y_ref, xs_ref, ys_ref):
    t = pl.program_id(0)
    rows = xs_ref.shape[1]
    n_levels = rows.bit_length() - 1
    cur, prv = t % 2, (t + 1) % 2

    @pl.when(t == 0)
    def _():
        xs_ref[...] = jnp.zeros(xs_ref.shape, BF16)
        ys_ref[...] = jnp.zeros(ys_ref.shape, BF16)

    def shifted(a, sh):
        return jnp.concatenate([jnp.zeros((sh, a.shape[1]), a.dtype), a[:rows - sh]], axis=0)

    ys_done = ys_ref[cur]
    u_all = jnp.concatenate([u_ref[pl.ds(j, rows, stride=SUB), :] for j in range(SUB)],
                            axis=1).astype(BF16)
    def scan_inputs(g):
        return jnp.dot(xs_ref[prv, :, g * SUB_W:(g + 1) * SUB_W], tb_ref[g], preferred_element_type=F32)

    r_next = scan_inputs(0)
    for g in range(SLAB_GROUPS):
        r = r_next
        if g + 1 < SLAB_GROUPS:
            r_next = scan_inputs(g + 1)
        xs_ref[cur, :, g * SUB_W:(g + 1) * SUB_W] = jnp.dot(
            u_all, pin_ref[g], preferred_element_type=F32).astype(BF16)
        o = jnp.dot(ys_done, pout_ref[g], preferred_element_type=F32)
        y_ref[pl.ds(2 * g, rows, stride=SUB), :] = o[:, :LANES]
        y_ref[pl.ds(2 * g + 1, rows, stride=SUB), :] = o[:, LANES:]
        yt = r[:, :SUB_W]
        s = r[:, SUB_W:SUB_W + STATE_W]
        w = r[:, SUB_W + STATE_W:]
        for k in range(n_levels):
            sh = 1 << k
            ar = ar_ref[g, k:k + 1, :]
            ai = ai_ref[g, k:k + 1, :]
            ps, pw = shifted(s, sh), shifted(w, sh)
            s, w = s + ar * ps + ai * pw, (w + ar * pw - ai * ps if k + 1 < n_levels else None)
        y = yt + jnp.dot(shifted(s, 1).astype(BF16), c_ref[g], preferred_element_type=F32)
        ys_ref[prv, :, g * SUB_W:(g + 1) * SUB_W] = jax.nn.gelu(y).astype(BF16)


def _ssm(uf, perm_in, perm_out, tb, cmat, lev_r, lev_i, layer, batch, seq):
    t = uf.shape[0]
    n_sub = seq // SUB
    n_levels = lev_r.shape[1]
    n_items = batch * N_SLABS
    item = lambda t, stage: jnp.clip(t - stage, 0, n_items - 1)
    const3 = lambda t: (0, 0, 0)
    slab3 = lambda t: (layer * N_SLABS + item(t, 1) % N_SLABS, 0, 0)
    return pl.pallas_call(
        _ssm_kernel,
        grid=(n_items + S5_STAGES - 1,),
        in_specs=[
            pl.BlockSpec((seq, LANES), lambda t: (item(t, 0) // N_SLABS, item(t, 0) % N_SLABS)),
            pl.BlockSpec((SLAB_GROUPS, SUB * LANES, SUB_W), const3, pipeline_mode=pl.Buffered(1)),
            pl.BlockSpec((SUB // 2, SLAB_W, 2 * LANES), const3, pipeline_mode=pl.Buffered(1)),
            pl.BlockSpec((SLAB_GROUPS, SUB_W, TB_W), slab3),
            pl.BlockSpec((SLAB_GROUPS, STATE_W, SUB_W), slab3),
            pl.BlockSpec((SLAB_GROUPS, n_levels, STATE_W), slab3),
            pl.BlockSpec((SLAB_GROUPS, n_levels, STATE_W), slab3),
        ],
        out_specs=pl.BlockSpec((seq, LANES), lambda t: (item(t, 2) // N_SLABS, item(t, 2) % N_SLABS)),
        out_shape=jax.ShapeDtypeStruct((t, SSM_WIDTH), F32),
        scratch_shapes=[pltpu.VMEM((2, n_sub, SLAB_W), BF16), pltpu.VMEM((2, n_sub, SLAB_W), BF16)],
        compiler_params=pltpu.CompilerParams(
            dimension_semantics=("arbitrary",), vmem_limit_bytes=VMEM_LIMIT),
        name="s5_scan",
    )(uf, perm_in, perm_out, tb, cmat, lev_r, lev_i)


def _merge_kernel(ya_ref, yg_ref, gt_ref, x_ref, wglu_ref, bglu_ref, wa_ref, ws_ref, wo_ref,
                  gffn_ref, *refs):
    n_cast = (len(refs) - 2) // 2
    x1_ref, h2_ref = refs[n_cast:n_cast + 2]
    _cast_blocks(refs[:n_cast], refs[n_cast + 2:])
    yg = yg_ref[...]
    t = jnp.dot(yg.astype(BF16), wglu_ref[...], preferred_element_type=F32) + bglu_ref[...]
    ys = (yg * _sigmoid(t)).astype(BF16)
    ma = jnp.dot(ya_ref[...], wa_ref[...], preferred_element_type=F32)
    ms = jnp.dot(ys, ws_ref[...], preferred_element_type=F32)
    merged = gt_ref[:, :D_MODEL].astype(F32) * ma + gt_ref[:, D_MODEL:].astype(F32) * ms
    x1 = x_ref[...] + jnp.dot(merged.astype(BF16), wo_ref[...], preferred_element_type=F32)
    x1_ref[...] = x1
    h2_ref[...] = _rms(x1, gffn_ref[...]).astype(BF16)


def _merge(ya, yg, gates, x2, wglu, bglu, wa, ws, wo, gffn, layer, cast_weights):
    t = x2.shape[0]
    tm = MERGE_TM
    casts = [_RowCast(w, layer, t // tm, lambda i: i) for w in cast_weights]

    def wspec(rows, cols):
        return pl.BlockSpec((rows, cols), lambda i: (0, 0), pipeline_mode=pl.Buffered(1))

    outs = pl.pallas_call(
        _merge_kernel,
        grid=(t // tm,),
        in_specs=[
            pl.BlockSpec((tm, ATTN_WIDTH), lambda i: (i, 0)),
            pl.BlockSpec((tm, SSM_WIDTH), lambda i: (i, 0)),
            pl.BlockSpec((tm, 2 * D_MODEL), lambda i: (i, 0)),
            pl.BlockSpec((tm, D_MODEL), lambda i: (i, 0)),
            wspec(SSM_WIDTH, SSM_WIDTH),
            pl.BlockSpec((1, SSM_WIDTH), lambda i: (0, 0)),
            wspec(ATTN_WIDTH, D_MODEL),
            wspec(SSM_WIDTH, D_MODEL),
            wspec(D_MODEL, D_MODEL),
            pl.BlockSpec((1, D_MODEL), lambda i: (0, 0)),
        ] + [c.in_spec for c in casts],
        out_specs=[
            pl.BlockSpec((tm, D_MODEL), lambda i: (i, 0)),
            pl.BlockSpec((tm, D_MODEL), lambda i: (i, 0)),
        ] + [c.out_spec for c in casts],
        out_shape=[
            jax.ShapeDtypeStruct((t, D_MODEL), F32),
            jax.ShapeDtypeStruct((t, D_MODEL), BF16),
        ] + [c.out_shape for c in casts],
        compiler_params=pltpu.CompilerParams(
            dimension_semantics=("arbitrary",), vmem_limit_bytes=VMEM_LIMIT),
        name="merge_out",
    )(ya, yg, gates, x2, wglu, bglu, wa, ws, wo, gffn, *[c.operand for c in casts])
    return outs[:2], outs[2:]


def _ffn_kernel(h_ref, x_ref, wg_ref, wu_ref, wo_ref, o_ref):
    k = pl.program_id(1)

    @pl.when(k == 0)
    def _():
        o_ref[...] = x_ref[...]

    for r in range(FFN_TM // FFN_SUB):
        rs = pl.ds(r * FFN_SUB, FFN_SUB)
        h = h_ref[rs, :]
        g = jnp.dot(h, wg_ref[...], preferred_element_type=F32)
        u = jnp.dot(h, wu_ref[...], preferred_element_type=F32)
        act = (g * _sigmoid(g) * u).astype(BF16)
        o_ref[rs, :] += jnp.dot(act, wo_ref[...], preferred_element_type=F32)


def _ffn(h2, x1, w_in, w_out):
    t = x1.shape[0]
    nk = D_FF // FFN_TF
    return pl.pallas_call(
        _ffn_kernel,
        grid=(t // FFN_TM, nk),
        in_specs=[
            pl.BlockSpec((FFN_TM, D_MODEL), lambda i, k: (i, 0)),
            pl.BlockSpec((FFN_TM, D_MODEL), lambda i, k: (i, 0)),
            pl.BlockSpec((D_MODEL, FFN_TF), lambda i, k: (0, k)),
            pl.BlockSpec((D_MODEL, FFN_TF), lambda i, k: (0, nk + k)),
            pl.BlockSpec((FFN_TF, D_MODEL), lambda i, k: (k, 0)),
        ],
        out_specs=pl.BlockSpec((FFN_TM, D_MODEL), lambda i, k: (i, 0)),
        out_shape=jax.ShapeDtypeStruct((t, D_MODEL), F32),
        compiler_params=pltpu.CompilerParams(
            dimension_semantics=("arbitrary", "arbitrary"), vmem_limit_bytes=FFN_VMEM_LIMIT),
        name="swiglu_ffn",
    )(h2, x1, w_in, w_in, w_out)


def kernel(x, norm_mix_g, w_in, gate_bias, q_norm_g, k_norm_g, attn_sinks, ssm_lambda_re, ssm_lambda_im, ssm_log_dt, ssm_b_re, ssm_b_im, ssm_c_re, ssm_c_im, ssm_d, ssm_glu_w, ssm_glu_b, w_attn_branch, w_ssm_branch, w_out, norm_ffn_g, w_ffn_in, w_ffn_out):
    batch, seq, _ = x.shape
    t = batch * seq
    n_levels = (seq // SUB).bit_length() - 1
    x2 = x.reshape(t, D_MODEL).astype(F32)
    f32 = lambda w: w.astype(F32)
    seg = _segment_ones()
    bias_tab = _attn_bias_tables()
    perm_in, perm_out = _slab_permutations()
    (tb, cmat, lev_r, lev_i), (w_in_bf,) = _ssm_prep(
        ssm_lambda_re, ssm_lambda_im, ssm_log_dt, ssm_b_re, ssm_b_im, ssm_c_re, ssm_c_im, ssm_d,
        n_levels, [f32(w_in)])
    merge_weights = [f32(ssm_glu_w), f32(w_attn_branch), f32(w_ssm_branch), f32(w_out)]
    ffn_weights = [f32(w_ffn_in), f32(w_ffn_out)]
    for l in range(DEPTH):
        (zq, uf, gates), (wglu_bf, wa_bf, ws_bf, wo_bf) = _inproj(
            x2, norm_mix_g[l].reshape(1, D_MODEL).astype(F32), w_in_bf, l,
            gate_bias[l].reshape(1, 2 * D_MODEL).astype(F32), seg,
            _qk_norm_tables(q_norm_g[l], k_norm_g[l]), merge_weights)
        ya = _attention(zq, attn_sinks[l].astype(F32), bias_tab, batch, seq)
        yg = _ssm(uf, perm_in, perm_out, tb, cmat, lev_r, lev_i, l, batch, seq)
        (x1, h2), (wfi_bf, wfo_bf) = _merge(
            ya, yg, gates, x2, wglu_bf, ssm_glu_b[l].reshape(1, SSM_WIDTH).astype(F32),
            wa_bf, ws_bf, wo_bf, norm_ffn_g[l].reshape(1, D_MODEL).astype(F32), l, ffn_weights)
        x2 = _ffn(h2, x1, wfi_bf, wfo_bf)
    return x2.reshape(batch, seq, D_MODEL).astype(x.dtype)
```

```python
import jax
import jax.numpy as jnp
from jax import lax
from jax.experimental import pallas as pl
from jax.experimental.pallas import tpu as pltpu

D_MODEL = 2048
DEPTH = 2
HEAD_DIM = 64
N_Q_HEADS = 16
N_KV_HEADS = 4
GQA_GROUP = N_Q_HEADS // N_KV_HEADS
ATTN_WIDTH = N_Q_HEADS * HEAD_DIM
KV_WIDTH = N_KV_HEADS * HEAD_DIM
WINDOW = 128
BLOCK = 128
SSM_WIDTH = D_MODEL // 2
SSM_GROUP_CH = 16
SSM_GROUPS = SSM_WIDTH // SSM_GROUP_CH
SSM_STATE = 64
D_FF = -(-8 * D_MODEL // (3 * 256)) * 256
OFF_K = ATTN_WIDTH
OFF_V = OFF_K + KV_WIDTH
OFF_U = OFF_V + KV_WIDTH
OFF_G = OFF_U + SSM_WIDTH
IN_WIDTH = OFF_G + 2 * D_MODEL
RMS_EPS = 1e-6

F32 = jnp.float32
BF16 = jnp.bfloat16
HIGHEST = lax.Precision.HIGHEST

LANES = 128
BF16_SUBLANES = 16
SUB = 16
SUB_W = SUB * SSM_GROUP_CH
STATE_W = 2 * SSM_STATE
TB_W = SUB_W + 2 * STATE_W
SLAB_GROUPS = LANES // SSM_GROUP_CH
N_SLABS = SSM_GROUPS // SLAB_GROUPS
SLAB_W = SLAB_GROUPS * SUB_W

VMEM_LIMIT = 56 * 1024 * 1024
FFN_VMEM_LIMIT = 60 * 1024 * 1024

INPROJ_TM = 256
INPROJ_TN = 512
MERGE_TM = 256
FFN_TM = 1024
FFN_TF = 512
FFN_SUB = 512


def _rms(x, g):
    return x * lax.rsqrt(jnp.mean(x * x, axis=-1, keepdims=True) + RMS_EPS) * g


def _sigmoid(x):
    return 0.5 * jnp.tanh(0.5 * x) + 0.5


class _RowCast:
    def __init__(self, stacked, layer, n_steps, step_of):
        _, rows, cols = stacked.shape
        blk = rows // n_steps
        assert blk * n_steps == rows and blk % BF16_SUBLANES == 0, (rows, n_steps)
        self.operand = stacked
        self.in_spec = pl.BlockSpec((None, blk, cols), lambda *ids: (layer, step_of(*ids), 0))
        self.out_spec = pl.BlockSpec((blk, cols), lambda *ids: (step_of(*ids), 0))
        self.out_shape = jax.ShapeDtypeStruct((rows, cols), BF16)


def _cast_blocks(src_refs, dst_refs):
    for src, dst in zip(src_refs, dst_refs):
        dst[...] = src[...].astype(BF16)


N_QKV_BLOCKS = OFF_U // INPROJ_TN
N_MAIN_BLOCKS = OFF_G // INPROJ_TN
N_IN_BLOCKS = IN_WIDTH // INPROJ_TN


def _inproj_kernel(x_ref, g_ref, w_ref, b_ref, seg_ref, nt_ref, *refs):
    n_cast = (len(refs) - 3) // 2
    zq_ref, u_ref, gt_ref = refs[n_cast:n_cast + 3]
    _cast_blocks(refs[:n_cast], refs[n_cast + 3:])
    h = _rms(x_ref[...], g_ref[...]).astype(BF16)
    for j in range(N_IN_BLOCKS):
        z = jnp.dot(h, w_ref[:, j * INPROJ_TN:(j + 1) * INPROJ_TN], preferred_element_type=F32)
        if j < N_QKV_BLOCKS:
            ssq = jnp.dot((z * z).astype(BF16), seg_ref[...], preferred_element_type=F32)
            inv = lax.rsqrt(ssq * (1.0 / HEAD_DIM) + RMS_EPS)
            fac = jnp.where(nt_ref[j, 1:2, :] > 0.0, inv, 1.0) * nt_ref[j, 0:1, :]
            zq_ref[:, j * INPROJ_TN:(j + 1) * INPROJ_TN] = (z * fac).astype(BF16)
        elif j < N_MAIN_BLOCKS:
            c = j - N_QKV_BLOCKS
            u_ref[:, c * INPROJ_TN:(c + 1) * INPROJ_TN] = z
        else:
            cs = slice((j - N_MAIN_BLOCKS) * INPROJ_TN, (j - N_MAIN_BLOCKS + 1) * INPROJ_TN)
            gt_ref[:, cs] = _sigmoid(z + b_ref[:, cs]).astype(BF16)


def _inproj(x2, gain, w_bf, bias, seg, ntab, cast_weights):
    t = x2.shape[0]
    tm = INPROJ_TM
    const2 = lambda i: (0, 0)
    casts = [_RowCast(w, wl, t // tm, lambda i: i) for w, wl in cast_weights]
    outs = pl.pallas_call(
        _inproj_kernel,
        grid=(t // tm,),
        in_specs=[
            pl.BlockSpec((tm, D_MODEL), lambda i: (i, 0)),
            pl.BlockSpec((1, D_MODEL), const2),
            pl.BlockSpec((D_MODEL, IN_WIDTH), const2, pipeline_mode=pl.Buffered(1)),
            pl.BlockSpec((1, 2 * D_MODEL), const2),
            pl.BlockSpec((INPROJ_TN, INPROJ_TN), const2),
            pl.BlockSpec((N_QKV_BLOCKS, 2, INPROJ_TN), lambda i: (0, 0, 0)),
        ] + [c.in_spec for c in casts],
        out_specs=[
            pl.BlockSpec((tm, OFF_U), lambda i: (i, 0)),
            pl.BlockSpec((tm, SSM_WIDTH), lambda i: (i, 0)),
            pl.BlockSpec((tm, 2 * D_MODEL), lambda i: (i, 0)),
        ] + [c.out_spec for c in casts],
        out_shape=[
            jax.ShapeDtypeStruct((t, OFF_U), BF16),
            jax.ShapeDtypeStruct((t, SSM_WIDTH), F32),
            jax.ShapeDtypeStruct((t, 2 * D_MODEL), BF16),
        ] + [c.out_shape for c in casts],
        compiler_params=pltpu.CompilerParams(
            dimension_semantics=("arbitrary",), vmem_limit_bytes=VMEM_LIMIT),
        name="inproj",
    )(x2, gain, w_bf, bias, seg, ntab, *[c.operand for c in casts])
    return outs[:3], outs[3:]


def _qk_norm_tables(q_gain, k_gain):
    qrow = jnp.tile(q_gain.astype(F32), INPROJ_TN // HEAD_DIM) * (HEAD_DIM ** -0.5)
    ones = jnp.ones((INPROJ_TN,), F32)
    kvrow = jnp.concatenate([jnp.tile(k_gain.astype(F32), N_KV_HEADS), jnp.ones((KV_WIDTH,), F32)])
    kvmask = jnp.concatenate([jnp.ones((KV_WIDTH,), F32), jnp.zeros((KV_WIDTH,), F32)])
    return jnp.stack([jnp.stack([qrow, ones]), jnp.stack([qrow, ones]), jnp.stack([kvrow, kvmask])])


def _segment_ones():
    r = jnp.arange(INPROJ_TN) // HEAD_DIM
    return (r[:, None] == r[None, :]).astype(BF16)


def _attn_bias_tables():
    t_loc = jnp.arange(BLOCK)[:, None]
    s_loc = jnp.arange(2 * BLOCK)[None, :] - BLOCK
    dist = (t_loc - s_loc).astype(F32)
    valid = (dist >= 0) & (dist < WINDOW)
    slopes = jnp.exp2(-8.0 * jnp.arange(1, N_Q_HEADS + 1, dtype=F32) / N_Q_HEADS)
    bias = -slopes[:, None, None] * dist[None]
    full = jnp.where(valid[None], bias, -jnp.inf)
    first = jnp.where((valid & (s_loc >= 0))[None], bias, -jnp.inf)
    return jnp.stack([first, full])


def _attn_kernel(sink_ref, bias_ref, q_ref, kvc_ref, kvp_ref, o_ref):
    kv_rows = 2 * BLOCK
    left_kv = lax.broadcasted_iota(jnp.int32, (kv_rows, LANES), 1) < HEAD_DIM
    left_q = lax.broadcasted_iota(jnp.int32, (BLOCK, LANES), 1) < HEAD_DIM
    zeros = jnp.zeros((kv_rows, LANES), BF16)
    ones_l = jnp.where(left_kv, 1.0, 0.0).astype(BF16)
    ones_r = jnp.where(left_kv, 0.0, 1.0).astype(BF16)
    contract_lanes = (((1,), (1,)), ((), ()))

    def slab(off):
        a = jnp.concatenate([kvp_ref[:, off:off + LANES], kvc_ref[:, off:off + LANES]], axis=0)
        return a, pltpu.roll(a.astype(F32), HEAD_DIM, axis=1).astype(BF16)

    for c in range(N_KV_HEADS // 2):
        k_slabs = slab(c * LANES)
        v_slabs = slab(KV_WIDTH + c * LANES)
        for side in range(2):
            kh = 2 * c + side
            k_l = jnp.where(left_kv, k_slabs[side], zeros)
            k_r = jnp.where(left_kv, zeros, k_slabs[1 - side])
            v_l = jnp.where(left_kv, v_slabs[side], zeros)
            v_r = jnp.where(left_kv, zeros, v_slabs[1 - side])
            kk = jnp.concatenate([k_l, k_r], axis=0)
            vv = jnp.concatenate([jnp.concatenate([v_l, ones_l], axis=1),
                                  jnp.concatenate([v_r, ones_r], axis=1)], axis=0)
            for pair in range(GQA_GROUP // 2):
                e = kh * GQA_GROUP + 2 * pair
                cols = slice(e * HEAD_DIM, (e + 2) * HEAD_DIM)
                s2 = lax.dot_general(q_ref[:, cols], kk, contract_lanes, preferred_element_type=F32)
                ps, ds = [], []
                for t in range(2):
                    s = s2[:, t * kv_rows:(t + 1) * kv_rows] + bias_ref[e + t]
                    sink = sink_ref[e + t]
                    m = jnp.maximum(jnp.max(s, axis=-1, keepdims=True), sink)
                    ps.append(jnp.exp(s - m).astype(BF16))
                    ds.append(jnp.exp(sink - m))
                r = jnp.dot(jnp.concatenate(ps, axis=1), vv, preferred_element_type=F32)
                denom = r[:, LANES:] + jnp.where(left_q, ds[0], ds[1])
                o_ref[:, cols] = (r[:, :LANES] / denom).astype(BF16)


def _attention(zq, sinks, bias_tab, batch, seq):
    t = zq.shape[0]
    nb = seq // BLOCK
    kv_col = OFF_K // (2 * KV_WIDTH)
    return pl.pallas_call(
        _attn_kernel,
        grid=(batch, nb),
        in_specs=[
            pl.BlockSpec(memory_space=pltpu.SMEM),
            pl.BlockSpec((None, N_Q_HEADS, BLOCK, 2 * BLOCK), lambda b, n: (jnp.minimum(n, 1), 0, 0, 0)),
            pl.BlockSpec((BLOCK, ATTN_WIDTH), lambda b, n: (b * nb + n, 0)),
            pl.BlockSpec((BLOCK, 2 * KV_WIDTH), lambda b, n: (b * nb + n, kv_col)),
            pl.BlockSpec((BLOCK, 2 * KV_WIDTH), lambda b, n: (b * nb + jnp.maximum(n - 1, 0), kv_col)),
        ],
        out_specs=pl.BlockSpec((BLOCK, ATTN_WIDTH), lambda b, n: (b * nb + n, 0)),
        out_shape=jax.ShapeDtypeStruct((t, ATTN_WIDTH), BF16),
        compiler_params=pltpu.CompilerParams(
            dimension_semantics=("arbitrary", "arbitrary"), vmem_limit_bytes=VMEM_LIMIT),
        name="swa_attention",
    )(sinks, bias_tab, zq, zq, zq)


def _cmul(ar, ai, br, bi):
    return ar * br - ai * bi, ar * bi + ai * br


def _pow_by_bits(exps, squares):
    pr = jnp.ones(exps.shape, F32)
    pi = jnp.zeros(exps.shape, F32)
    for b, (sr, si) in enumerate(squares):
        on = ((exps >> b) & 1) == 1
        pr, pi = _cmul(pr, pi, jnp.where(on, sr, 1.0), jnp.where(on, si, 0.0))
    return pr, pi


def _ssm_prep_kernel(lam_ref, ldt_ref, btr_ref, bti_ref, ctr_ref, cti_ref, d_ref, *refs):
    n_cast = (len(refs) - 4) // 2
    tb_ref, c_ref, levr_ref, levi_ref = refs[n_cast:n_cast + 4]
    _cast_blocks(refs[:n_cast], refs[n_cast + 4:])
    n_levels = levr_ref.shape[1]
    tau_lane = lax.broadcasted_iota(jnp.int32, (SSM_STATE, SUB_W), 1) // SSM_GROUP_CH
    row_h = lax.broadcasted_iota(jnp.int32, (SSM_GROUP_CH, SUB_W), 0)
    lane_h = lax.broadcasted_iota(jnp.int32, (SSM_GROUP_CH, SUB_W), 1)
    tile_ch = (lane_h % SSM_GROUP_CH == row_h).astype(F32)
    row_m = lax.broadcasted_iota(jnp.int32, (SUB, SSM_STATE), 0)
    eye_p = (lax.broadcasted_iota(jnp.int32, (SSM_STATE, SSM_STATE), 0)
             == lax.broadcasted_iota(jnp.int32, (SSM_STATE, SSM_STATE), 1))

    def to_col(v):
        return jnp.sum(jnp.where(eye_p, v, 0.0), axis=1, keepdims=True)

    def squares(a, n):
        out = [a]
        for _ in range(n - 1):
            out.append(_cmul(*out[-1], *out[-1]))
        return out

    for g in range(SLAB_GROUPS):
        dt = jnp.exp(ldt_ref[g])

        def discretise(lr, li):
            mag = jnp.exp(lr * dt)
            return mag * jnp.cos(li * dt), mag * jnp.sin(li * dt)

        lr_row, li_row = lam_ref[g, 0:1, :], lam_ref[g, 1:2, :]
        a_row = discretise(lr_row, li_row)
        sq_row = squares(a_row, 5)
        sq_col = squares((to_col(a_row[0]), to_col(a_row[1])), 4)
        ar, ai = sq_row[0]
        den = lr_row * lr_row + li_row * li_row
        fr = ((ar - 1.0) * lr_row + ai * li_row) / den
        fi = (ai * lr_row - (ar - 1.0) * li_row) / den
        bbr, bbi = _cmul(fr, fi, btr_ref[g], bti_ref[g])

        e0 = _pow_by_bits(tau_lane, sq_col)
        e1 = _cmul(*e0, *sq_col[0])
        ctr = jnp.dot(ctr_ref[g], tile_ch, precision=HIGHEST, preferred_element_type=F32)
        cti = jnp.dot(cti_ref[g], tile_ch, precision=HIGHEST, preferred_element_type=F32)
        mr, mi = _cmul(*e0, ctr, cti)
        kt = (jnp.dot(bbr, mr, precision=HIGHEST, preferred_element_type=F32)
              - jnp.dot(bbi, mi, precision=HIGHEST, preferred_element_type=F32))
        kt = kt + jnp.where(lane_h == row_h, d_ref[g], 0.0)
        pw = _pow_by_bits(row_m, sq_row[:4])
        for j in range(SUB):
            rows = pl.ds(j * SSM_GROUP_CH, SSM_GROUP_CH)
            tj = kt if j == 0 else jnp.where(lane_h >= j * SSM_GROUP_CH,
                                             pltpu.roll(kt, j * SSM_GROUP_CH, axis=1), 0.0)
            tb_ref[g, rows, 0:SUB_W] = tj.astype(BF16)
            m = SUB - 1 - j
            br_, bi_ = _cmul(pw[0][m:m + 1, :], pw[1][m:m + 1, :], bbr, bbi)
            tb_ref[g, rows, SUB_W:SUB_W + STATE_W] = jnp.concatenate([br_, bi_], axis=1).astype(BF16)
            tb_ref[g, rows, SUB_W + STATE_W:TB_W] = jnp.concatenate([bi_, br_], axis=1).astype(BF16)
        c_ref[g, 0:SSM_STATE, :] = (ctr * e1[0] - cti * e1[1]).astype(BF16)
        c_ref[g, SSM_STATE:STATE_W, :] = (-ctr * e1[1] - cti * e1[0]).astype(BF16)
        lv = sq_row[4]
        lev_r, lev_i = [], []
        for _ in range(n_levels):
            lev_r.append(jnp.concatenate([lv[0], lv[0]], axis=1))
            lev_i.append(jnp.concatenate([-lv[1], lv[1]], axis=1))
            lv = _cmul(*lv, *lv)
        levr_ref[g] = jnp.concatenate(lev_r, axis=0)
        levi_ref[g] = jnp.concatenate(lev_i, axis=0)


def _ssm_prep(lam_re, lam_im, log_dt, b_re, b_im, c_re, c_im, d_skip, n_levels, cast_weights):
    dg = lam_re.shape[0] * SSM_GROUPS
    casts = [_RowCast(w, wl, dg // SLAB_GROUPS, lambda s: s) for w, wl in cast_weights]
    h_, p_ = SSM_GROUP_CH, SSM_STATE
    f = lambda a: a.astype(F32)
    lam_rows = jnp.stack([f(lam_re), f(lam_im)], axis=2).reshape(dg, 2, p_)
    btr = f(b_re).transpose(0, 1, 3, 2).reshape(dg, h_, p_)
    bti = f(b_im).transpose(0, 1, 3, 2).reshape(dg, h_, p_)
    ctr = f(c_re).transpose(0, 1, 3, 2).reshape(dg, p_, h_)
    cti = f(c_im).transpose(0, 1, 3, 2).reshape(dg, p_, h_)
    d_rows = jnp.pad(f(d_skip).reshape(dg, 1, h_), ((0, 0), (0, 0), (0, SUB_W - h_)))
    grp = lambda *shape: pl.BlockSpec((SLAB_GROUPS,) + shape, lambda s: (s,) + (0,) * len(shape))
    outs = pl.pallas_call(
        _ssm_prep_kernel,
        grid=(dg // SLAB_GROUPS,),
        in_specs=[grp(2, p_), grp(1, 1), grp(h_, p_), grp(h_, p_),
                  grp(p_, h_), grp(p_, h_), grp(1, SUB_W)] + [c.in_spec for c in casts],
        out_specs=[grp(SUB_W, TB_W), grp(STATE_W, SUB_W), grp(n_levels, STATE_W),
                   grp(n_levels, STATE_W)] + [c.out_spec for c in casts],
        out_shape=[
            jax.ShapeDtypeStruct((dg, SUB_W, TB_W), BF16),
            jax.ShapeDtypeStruct((dg, STATE_W, SUB_W), BF16),
            jax.ShapeDtypeStruct((dg, n_levels, STATE_W), F32),
            jax.ShapeDtypeStruct((dg, n_levels, STATE_W), F32),
        ] + [c.out_shape for c in casts],
        compiler_params=pltpu.CompilerParams(
            dimension_semantics=("arbitrary",), vmem_limit_bytes=VMEM_LIMIT),
        name="s5_prep",
    )(lam_rows, f(log_dt).reshape(dg, 1, 1), btr, bti, ctr, cti, d_rows, *[c.operand for c in casts])
    return outs[:4], outs[4:]


def _slab_permutations():
    r = jnp.arange(SUB * LANES)
    j, lane = r // LANES, r % LANES
    grp, ch = lane // SSM_GROUP_CH, lane % SSM_GROUP_CH
    col = grp * SUB_W + j * SSM_GROUP_CH + ch
    full = (col[:, None] == jnp.arange(SLAB_W)[None, :]).astype(BF16)
    split = lambda m, n: m.reshape(m.shape[0], n, m.shape[1] // n).transpose(1, 0, 2)
    return split(full, SLAB_GROUPS), split(full.T, SUB // 2)


S5_STAGES = 3


def _ssm_kernel(u_ref, pin_ref, pout_ref, tb_ref, c_ref, ar_ref, ai_ref, y_ref, xs_ref, ys_ref):
    t = pl.program_id(0)
    rows = xs_ref.shape[1]
    n_levels = rows.bit_length() - 1
    cur, prv = t % 2, (t + 1) % 2

    @pl.when(t == 0)
    def _():
        xs_ref[...] = jnp.zeros(xs_ref.shape, BF16)
        ys_ref[...] = jnp.zeros(ys_ref.shape, BF16)

    def shifted(a, sh):
        return jnp.concatenate([jnp.zeros((sh, a.shape[1]), a.dtype), a[:rows - sh]], axis=0)

    ys_done = ys_ref[cur]
    u_all = jnp.concatenate([u_ref[pl.ds(j, rows, stride=SUB), :] for j in range(SUB)],
                            axis=1).astype(BF16)
    def scan_inputs(g):
        return jnp.dot(xs_ref[prv, :, g * SUB_W:(g + 1) * SUB_W], tb_ref[g], preferred_element_type=F32)

    r_next = scan_inputs(0)
    for g in range(SLAB_GROUPS):
        r = r_next
        if g + 1 < SLAB_GROUPS:
            r_next = scan_inputs(g + 1)
        xs_ref[cur, :, g * SUB_W:(g + 1) * SUB_W] = jnp.dot(
            u_all, pin_ref[g], preferred_element_type=F32).astype(BF16)
        o = jnp.dot(ys_done, pout_ref[g], preferred_element_type=F32)
        y_ref[pl.ds(2 * g, rows, stride=SUB), :] = o[:, :LANES]
        y_ref[pl.ds(2 * g + 1, rows, stride=SUB), :] = o[:, LANES:]
        yt = r[:, :SUB_W]
        s = r[:, SUB_W:SUB_W + STATE_W]
        w = r[:, SUB_W + STATE_W:]
        for k in range(n_levels):
            sh = 1 << k
            ar = ar_ref[g, k:k + 1, :]
            ai = ai_ref[g, k:k + 1, :]
            ps, pw = shifted(s, sh), shifted(w, sh)
            s, w = s + ar * ps + ai * pw, (w + ar * pw - ai * ps if k + 1 < n_levels else None)
        y = yt + jnp.dot(shifted(s, 1).astype(BF16), c_ref[g], preferred_element_type=F32)
        ys_ref[prv, :, g * SUB_W:(g + 1) * SUB_W] = jax.nn.gelu(y).astype(BF16)


def _ssm(uf, perm_in, perm_out, tb, cmat, lev_r, lev_i, layer, batch, seq):
    t = uf.shape[0]
    n_sub = seq // SUB
    n_levels = lev_r.shape[1]
    n_items = batch * N_SLABS
    item = lambda t, stage: jnp.clip(t - stage, 0, n_items - 1)
    const3 = lambda t: (0, 0, 0)
    slab3 = lambda t: (layer * N_SLABS + item(t, 1) % N_SLABS, 0, 0)
    return pl.pallas_call(
        _ssm_kernel,
        grid=(n_items + S5_STAGES - 1,),
        in_specs=[
            pl.BlockSpec((seq, LANES), lambda t: (item(t, 0) // N_SLABS, item(t, 0) % N_SLABS)),
            pl.BlockSpec((SLAB_GROUPS, SUB * LANES, SUB_W), const3, pipeline_mode=pl.Buffered(1)),
            pl.BlockSpec((SUB // 2, SLAB_W, 2 * LANES), const3, pipeline_mode=pl.Buffered(1)),
            pl.BlockSpec((SLAB_GROUPS, SUB_W, TB_W), slab3),
            pl.BlockSpec((SLAB_GROUPS, STATE_W, SUB_W), slab3),
            pl.BlockSpec((SLAB_GROUPS, n_levels, STATE_W), slab3),
            pl.BlockSpec((SLAB_GROUPS, n_levels, STATE_W), slab3),
        ],
        out_specs=pl.BlockSpec((seq, LANES), lambda t: (item(t, 2) // N_SLABS, item(t, 2) % N_SLABS)),
        out_shape=jax.ShapeDtypeStruct((t, SSM_WIDTH), F32),
        scratch_shapes=[pltpu.VMEM((2, n_sub, SLAB_W), BF16), pltpu.VMEM((2, n_sub, SLAB_W), BF16)],
        compiler_params=pltpu.CompilerParams(
            dimension_semantics=("arbitrary",), vmem_limit_bytes=VMEM_LIMIT),
        name="s5_scan",
    )(uf, perm_in, perm_out, tb, cmat, lev_r, lev_i)


def _merge_kernel(ya_ref, yg_ref, gt_ref, x_ref, wglu_ref, bglu_ref, wa_ref, ws_ref, wo_ref,
                  gffn_ref, *refs):
    n_cast = (len(refs) - 2) // 2
    x1_ref, h2_ref = refs[n_cast:n_cast + 2]
    _cast_blocks(refs[:n_cast], refs[n_cast + 2:])
    yg = yg_ref[...]
    t = jnp.dot(yg.astype(BF16), wglu_ref[...], preferred_element_type=F32) + bglu_ref[...]
    ys = (yg * _sigmoid(t)).astype(BF16)
    ma = jnp.dot(ya_ref[...], wa_ref[...], preferred_element_type=F32)
    ms = jnp.dot(ys, ws_ref[...], preferred_element_type=F32)
    merged = gt_ref[:, :D_MODEL].astype(F32) * ma + gt_ref[:, D_MODEL:].astype(F32) * ms
    x1 = x_ref[...] + jnp.dot(merged.astype(BF16), wo_ref[...], preferred_element_type=F32)
    x1_ref[...] = x1
    h2_ref[...] = _rms(x1, gffn_ref[...]).astype(BF16)


def _merge(ya, yg, gates, x2, wglu, bglu, wa, ws, wo, gffn, cast_weights):
    t = x2.shape[0]
    tm = MERGE_TM
    casts = [_RowCast(w, wl, t // tm, lambda i: i) for w, wl in cast_weights]

    def wspec(rows, cols):
        return pl.BlockSpec((rows, cols), lambda i: (0, 0), pipeline_mode=pl.Buffered(1))

    outs = pl.pallas_call(
        _merge_kernel,
        grid=(t // tm,),
        in_specs=[
            pl.BlockSpec((tm, ATTN_WIDTH), lambda i: (i, 0)),
            pl.BlockSpec((tm, SSM_WIDTH), lambda i: (i, 0)),
            pl.BlockSpec((tm, 2 * D_MODEL), lambda i: (i, 0)),
            pl.BlockSpec((tm, D_MODEL), lambda i: (i, 0)),
            wspec(SSM_WIDTH, SSM_WIDTH),
            pl.BlockSpec((1, SSM_WIDTH), lambda i: (0, 0)),
            wspec(ATTN_WIDTH, D_MODEL),
            wspec(SSM_WIDTH, D_MODEL),
            wspec(D_MODEL, D_MODEL),
            pl.BlockSpec((1, D_MODEL), lambda i: (0, 0)),
        ] + [c.in_spec for c in casts],
        out_specs=[
            pl.BlockSpec((tm, D_MODEL), lambda i: (i, 0)),
            pl.BlockSpec((tm, D_MODEL), lambda i: (i, 0)),
        ] + [c.out_spec for c in casts],
        out_shape=[
            jax.ShapeDtypeStruct((t, D_MODEL), F32),
            jax.ShapeDtypeStruct((t, D_MODEL), BF16),
        ] + [c.out_shape for c in casts],
        compiler_params=pltpu.CompilerParams(
            dimension_semantics=("arbitrary",), vmem_limit_bytes=VMEM_LIMIT),
        name="merge_out",
    )(ya, yg, gates, x2, wglu, bglu, wa, ws, wo, gffn, *[c.operand for c in casts])
    return outs[:2], outs[2:]


def _ffn_kernel(h_ref, x_ref, wg_ref, wu_ref, wo_ref, o_ref):
    k = pl.program_id(1)

    def accumulate(base_ref):
        for r in range(FFN_TM // FFN_SUB):
            rs = pl.ds(r * FFN_SUB, FFN_SUB)
            h = h_ref[rs, :]
            g = jnp.dot(h, wg_ref[...], preferred_element_type=F32)
            u = jnp.dot(h, wu_ref[...], preferred_element_type=F32)
            act = (g * _sigmoid(g) * u).astype(BF16)
            o_ref[rs, :] = base_ref[rs, :] + jnp.dot(act, wo_ref[...], preferred_element_type=F32)

    @pl.when(k == 0)
    def _():
        accumulate(x_ref)

    @pl.when(k > 0)
    def _():
        accumulate(o_ref)


def _ffn(h2, x1, w_in, w_out):
    t = x1.shape[0]
    nk = D_FF // FFN_TF
    return pl.pallas_call(
        _ffn_kernel,
        grid=(t // FFN_TM, nk),
        in_specs=[
            pl.BlockSpec((FFN_TM, D_MODEL), lambda i, k: (i, 0)),
            pl.BlockSpec((FFN_TM, D_MODEL), lambda i, k: (i, 0)),
            pl.BlockSpec((D_MODEL, FFN_TF), lambda i, k: (0, k)),
            pl.BlockSpec((D_MODEL, FFN_TF), lambda i, k: (0, nk + k)),
            pl.BlockSpec((FFN_TF, D_MODEL), lambda i, k: (k, 0)),
        ],
        out_specs=pl.BlockSpec((FFN_TM, D_MODEL), lambda i, k: (i, 0)),
        out_shape=jax.ShapeDtypeStruct((t, D_MODEL), F32),
        compiler_params=pltpu.CompilerParams(
            dimension_semantics=("arbitrary", "arbitrary"), vmem_limit_bytes=FFN_VMEM_LIMIT),
        name="swiglu_ffn",
    )(h2, x1, w_in, w_in, w_out)


def kernel(x, norm_mix_g, w_in, gate_bias, q_norm_g, k_norm_g, attn_sinks, ssm_lambda_re, ssm_lambda_im, ssm_log_dt, ssm_b_re, ssm_b_im, ssm_c_re, ssm_c_im, ssm_d, ssm_glu_w, ssm_glu_b, w_attn_branch, w_ssm_branch, w_out, norm_ffn_g, w_ffn_in, w_ffn_out):
    batch, seq, _ = x.shape
    t = batch * seq
    n_levels = (seq // SUB).bit_length() - 1
    x2 = x.reshape(t, D_MODEL).astype(F32)
    f32 = lambda w: w.astype(F32)
    seg = _segment_ones()
    bias_tab = _attn_bias_tables()
    perm_in, perm_out = _slab_permutations()
    w_in_f = f32(w_in)
    (tb, cmat, lev_r, lev_i), (w_in_bf,) = _ssm_prep(
        ssm_lambda_re, ssm_lambda_im, ssm_log_dt, ssm_b_re, ssm_b_im, ssm_c_re, ssm_c_im, ssm_d,
        n_levels, [(w_in_f, 0)])
    merge_weights = [f32(ssm_glu_w), f32(w_attn_branch), f32(w_ssm_branch), f32(w_out)]
    ffn_weights = [f32(w_ffn_in), f32(w_ffn_out)]
    for l in range(DEPTH):
        next_w_in = [(w_in_f, l + 1)] if l + 1 < DEPTH else []
        (zq, uf, gates), cast_out = _inproj(
            x2, norm_mix_g[l].reshape(1, D_MODEL).astype(F32), w_in_bf,
            gate_bias[l].reshape(1, 2 * D_MODEL).astype(F32), seg,
            _qk_norm_tables(q_norm_g[l], k_norm_g[l]), [(w, l) for w in merge_weights] + next_w_in)
        wglu_bf, wa_bf, ws_bf, wo_bf = cast_out[:4]
        w_in_bf = cast_out[4] if next_w_in else None
        ya = _attention(zq, attn_sinks[l].astype(F32), bias_tab, batch, seq)
        yg = _ssm(uf, perm_in, perm_out, tb, cmat, lev_r, lev_i, l, batch, seq)
        (x1, h2), (wfi_bf, wfo_bf) = _merge(
            ya, yg, gates, x2, wglu_bf, ssm_glu_b[l].reshape(1, SSM_WIDTH).astype(F32),
            wa_bf, ws_bf, wo_bf, norm_ffn_g[l].reshape(1, D_MODEL).astype(F32),
            [(w, l) for w in ffn_weights])
        x2 = _ffn(h2, x1, wfi_bf, wfo_bf)
    return x2.reshape(batch, seq, D_MODEL).astype(x.dtype)
```

```python
import jax
import jax.numpy as jnp
from jax import lax
from jax.experimental import pallas as pl
from jax.experimental.pallas import tpu as pltpu

D_MODEL = 2048
DEPTH = 2
HEAD_DIM = 64
N_Q_HEADS = 16
N_KV_HEADS = 4
GQA_GROUP = N_Q_HEADS // N_KV_HEADS
ATTN_WIDTH = N_Q_HEADS * HEAD_DIM
KV_WIDTH = N_KV_HEADS * HEAD_DIM
WINDOW = 128
BLOCK = 128
SSM_WIDTH = D_MODEL // 2
SSM_GROUP_CH = 16
SSM_GROUPS = SSM_WIDTH // SSM_GROUP_CH
SSM_STATE = 64
D_FF = -(-8 * D_MODEL // (3 * 256)) * 256
OFF_K = ATTN_WIDTH
OFF_V = OFF_K + KV_WIDTH
OFF_U = OFF_V + KV_WIDTH
OFF_G = OFF_U + SSM_WIDTH
IN_WIDTH = OFF_G + 2 * D_MODEL
RMS_EPS = 1e-6

F32 = jnp.float32
BF16 = jnp.bfloat16
HIGHEST = lax.Precision.HIGHEST

LANES = 128
BF16_SUBLANES = 16
SUB = 16
SUB_W = SUB * SSM_GROUP_CH
STATE_W = 2 * SSM_STATE
TB_W = SUB_W + 2 * STATE_W
SLAB_GROUPS = LANES // SSM_GROUP_CH
N_SLABS = SSM_GROUPS // SLAB_GROUPS
SLAB_W = SLAB_GROUPS * SUB_W

VMEM_LIMIT = 56 * 1024 * 1024
FFN_VMEM_LIMIT = 60 * 1024 * 1024

INPROJ_TM = 256
INPROJ_TN = 512
MERGE_TM = 256
FFN_TM = 1024
FFN_TF = 512
FFN_SUB = 512


def _rms(x, g):
    return x * lax.rsqrt(jnp.mean(x * x, axis=-1, keepdims=True) + RMS_EPS) * g


def _sigmoid(x):
    return 0.5 * jnp.tanh(0.5 * x) + 0.5


class _RowCast:
    def __init__(self, stacked, layer, n_steps, step_of):
        _, rows, cols = stacked.shape
        blk = rows // n_steps
        assert blk * n_steps == rows and blk % BF16_SUBLANES == 0, (rows, n_steps)
        self.operand = stacked
        self.in_spec = pl.BlockSpec((None, blk, cols), lambda *ids: (layer, step_of(*ids), 0))
        self.out_spec = pl.BlockSpec((blk, cols), lambda *ids: (step_of(*ids), 0))
        self.out_shape = jax.ShapeDtypeStruct((rows, cols), BF16)


def _cast_blocks(src_refs, dst_refs):
    for src, dst in zip(src_refs, dst_refs):
        dst[...] = src[...].astype(BF16)


N_QKV_BLOCKS = OFF_U // INPROJ_TN
N_MAIN_BLOCKS = OFF_G // INPROJ_TN
N_IN_BLOCKS = IN_WIDTH // INPROJ_TN


QBLOCKS_PER_TILE = INPROJ_TM // BLOCK
ATTN_PROBLEMS_PER_GAP = 2


def _inproj_attn_kernel(sink_ref, x_ref, g_ref, w_ref, b_ref, seg_ref, nt_ref, bias0_ref, bias_ref, *refs):
    n_cast = (len(refs) - 5) // 2
    u_ref, gt_ref, ya_ref = refs[n_cast:n_cast + 3]
    q_ref, kv_ref = refs[2 * n_cast + 3:]
    _cast_blocks(refs[:n_cast], refs[n_cast + 3:2 * n_cast + 3])

    @pl.when(pl.program_id(0) == 0)
    def _():
        kv_ref[...] = jnp.zeros(kv_ref.shape, BF16)

    kv_ref[0:BLOCK, :] = kv_ref[INPROJ_TM:INPROJ_TM + BLOCK, :]
    h = _rms(x_ref[...], g_ref[...]).astype(BF16)

    def project(j):
        return jnp.dot(h, w_ref[:, j * INPROJ_TN:(j + 1) * INPROJ_TN], preferred_element_type=F32)

    for j in range(N_QKV_BLOCKS):
        z = project(j)
        ssq = jnp.dot((z * z).astype(BF16), seg_ref[...], preferred_element_type=F32)
        inv = lax.rsqrt(ssq * (1.0 / HEAD_DIM) + RMS_EPS)
        fac = jnp.where(nt_ref[j, 1:2, :] > 0.0, inv, 1.0) * nt_ref[j, 0:1, :]
        if (j + 1) * INPROJ_TN <= ATTN_WIDTH:
            q_ref[:, j * INPROJ_TN:(j + 1) * INPROJ_TN] = (z * fac).astype(BF16)
        else:
            kv_ref[BLOCK:, :] = (z * fac).astype(BF16)

    def other_block(j):
        z = project(j)
        if j < N_MAIN_BLOCKS:
            c = j - N_QKV_BLOCKS
            u_ref[:, c * INPROJ_TN:(c + 1) * INPROJ_TN] = z
        else:
            cs = slice((j - N_MAIN_BLOCKS) * INPROJ_TN, (j - N_MAIN_BLOCKS + 1) * INPROJ_TN)
            gt_ref[:, cs] = _sigmoid(z + b_ref[:, cs]).astype(BF16)

    kv_rows = 2 * BLOCK
    all_rows = kv_ref.shape[0]
    left_kv = lax.broadcasted_iota(jnp.int32, (all_rows, LANES), 1) < HEAD_DIM
    left_q = lax.broadcasted_iota(jnp.int32, (BLOCK, LANES), 1) < HEAD_DIM
    zeros = jnp.zeros((all_rows, LANES), BF16)
    left_ones = lax.broadcasted_iota(jnp.int32, (kv_rows, LANES), 1) < HEAD_DIM
    ones_l = jnp.where(left_ones, 1.0, 0.0).astype(BF16)
    ones_r = jnp.where(left_ones, 0.0, 1.0).astype(BF16)
    contract_lanes = (((1,), (1,)), ((), ()))

    def slab(off):
        a = kv_ref[:, off:off + LANES]
        return a, pltpu.roll(a.astype(F32), HEAD_DIM, axis=1).astype(BF16)

    problems = []
    for c in range(N_KV_HEADS // 2):
        k_slabs = slab(c * LANES)
        v_slabs = slab(KV_WIDTH + c * LANES)
        for side in range(2):
            kh = 2 * c + side
            k_l = jnp.where(left_kv, k_slabs[side], zeros)
            k_r = jnp.where(left_kv, zeros, k_slabs[1 - side])
            v_l = jnp.where(left_kv, v_slabs[side], zeros)
            v_r = jnp.where(left_kv, zeros, v_slabs[1 - side])
            for qb in range(QBLOCKS_PER_TILE):
                rows = slice(qb * BLOCK, qb * BLOCK + kv_rows)
                kk = jnp.concatenate([k_l[rows], k_r[rows]], axis=0)
                vv = jnp.concatenate([jnp.concatenate([v_l[rows], ones_l], axis=1),
                                      jnp.concatenate([v_r[rows], ones_r], axis=1)], axis=0)
                for pair in range(GQA_GROUP // 2):
                    problems.append((qb, kh * GQA_GROUP + 2 * pair, kk, vv))

    def scores(qb, e, kk):
        q2 = q_ref[qb * BLOCK:(qb + 1) * BLOCK, e * HEAD_DIM:(e + 2) * HEAD_DIM]
        return lax.dot_general(q2, kk, contract_lanes, preferred_element_type=F32)

    def finish(qb, e, s2, vv):
        bias = bias0_ref if qb == 0 else bias_ref
        ps, ds = [], []
        for t in range(2):
            s = s2[:, t * kv_rows:(t + 1) * kv_rows] + bias[e + t]
            sink = sink_ref[e + t]
            m = jnp.maximum(jnp.max(s, axis=-1, keepdims=True), sink)
            ps.append(jnp.exp(s - m).astype(BF16))
            ds.append(jnp.exp(sink - m))
        r = jnp.dot(jnp.concatenate(ps, axis=1), vv, preferred_element_type=F32)
        denom = r[:, LANES:] + jnp.where(left_q, ds[0], ds[1])
        ya_ref[qb * BLOCK:(qb + 1) * BLOCK, e * HEAD_DIM:(e + 2) * HEAD_DIM] = (r[:, :LANES] / denom).astype(BF16)

    others = list(range(N_QKV_BLOCKS, N_IN_BLOCKS))
    for first in range(0, len(problems), ATTN_PROBLEMS_PER_GAP):
        group = problems[first:first + ATTN_PROBLEMS_PER_GAP]
        s2s = [scores(qb, e, kk) for qb, e, kk, _ in group]
        if others:
            other_block(others.pop(0))
        for (qb, e, _, vv), s2 in zip(group, s2s):
            finish(qb, e, s2, vv)
    for j in others:
        other_block(j)


def _inproj_attn(x2, gain, w_bf, bias, seg, ntab, sinks, bias_tab, seq, cast_weights):
    t = x2.shape[0]
    tm = INPROJ_TM
    tiles_per_seq = seq // tm
    const2 = lambda i: (0, 0)
    bias_shape = (None, N_Q_HEADS, BLOCK, 2 * BLOCK)
    casts = [_RowCast(w, wl, t // tm, lambda i: i) for w, wl in cast_weights]
    outs = pl.pallas_call(
        _inproj_attn_kernel,
        grid=(t // tm,),
        in_specs=[
            pl.BlockSpec(memory_space=pltpu.SMEM),
            pl.BlockSpec((tm, D_MODEL), lambda i: (i, 0)),
            pl.BlockSpec((1, D_MODEL), const2),
            pl.BlockSpec((D_MODEL, IN_WIDTH), const2, pipeline_mode=pl.Buffered(1)),
            pl.BlockSpec((1, 2 * D_MODEL), const2),
            pl.BlockSpec((INPROJ_TN, INPROJ_TN), const2),
            pl.BlockSpec((N_QKV_BLOCKS, 2, INPROJ_TN), lambda i: (0, 0, 0)),
            pl.BlockSpec(bias_shape, lambda i: (jnp.minimum(i % tiles_per_seq, 1), 0, 0, 0)),
            pl.BlockSpec(bias_shape, lambda i: (1, 0, 0, 0), pipeline_mode=pl.Buffered(1)),
        ] + [c.in_spec for c in casts],
        out_specs=[
            pl.BlockSpec((tm, SSM_WIDTH), lambda i: (i, 0)),
            pl.BlockSpec((tm, 2 * D_MODEL), lambda i: (i, 0)),
            pl.BlockSpec((tm, ATTN_WIDTH), lambda i: (i, 0)),
        ] + [c.out_spec for c in casts],
        out_shape=[
            jax.ShapeDtypeStruct((t, SSM_WIDTH), F32),
            jax.ShapeDtypeStruct((t, 2 * D_MODEL), BF16),
            jax.ShapeDtypeStruct((t, ATTN_WIDTH), BF16),
        ] + [c.out_shape for c in casts],
        scratch_shapes=[pltpu.VMEM((tm, ATTN_WIDTH), BF16), pltpu.VMEM((BLOCK + tm, 2 * KV_WIDTH), BF16)],
        compiler_params=pltpu.CompilerParams(
            dimension_semantics=("arbitrary",), vmem_limit_bytes=VMEM_LIMIT),
        name="inproj_attn",
    )(sinks, x2, gain, w_bf, bias, seg, ntab, bias_tab, bias_tab, *[c.operand for c in casts])
    return outs[:3], outs[3:]


def _qk_norm_tables(q_gain, k_gain):
    qrow = jnp.tile(q_gain.astype(F32), INPROJ_TN // HEAD_DIM) * (HEAD_DIM ** -0.5)
    ones = jnp.ones((INPROJ_TN,), F32)
    kvrow = jnp.concatenate([jnp.tile(k_gain.astype(F32), N_KV_HEADS), jnp.ones((KV_WIDTH,), F32)])
    kvmask = jnp.concatenate([jnp.ones((KV_WIDTH,), F32), jnp.zeros((KV_WIDTH,), F32)])
    return jnp.stack([jnp.stack([qrow, ones]), jnp.stack([qrow, ones]), jnp.stack([kvrow, kvmask])])


def _segment_ones():
    r = jnp.arange(INPROJ_TN) // HEAD_DIM
    return (r[:, None] == r[None, :]).astype(BF16)


def _attn_bias_tables():
    t_loc = jnp.arange(BLOCK)[:, None]
    s_loc = jnp.arange(2 * BLOCK)[None, :] - BLOCK
    dist = (t_loc - s_loc).astype(F32)
    valid = (dist >= 0) & (dist < WINDOW)
    slopes = jnp.exp2(-8.0 * jnp.arange(1, N_Q_HEADS + 1, dtype=F32) / N_Q_HEADS)
    bias = -slopes[:, None, None] * dist[None]
    full = jnp.where(valid[None], bias, -jnp.inf)
    first = jnp.where((valid & (s_loc >= 0))[None], bias, -jnp.inf)
    return jnp.stack([first, full])


def _cmul(ar, ai, br, bi):
    return ar * br - ai * bi, ar * bi + ai * br


def _pow_by_bits(exps, squares):
    pr = jnp.ones(exps.shape, F32)
    pi = jnp.zeros(exps.shape, F32)
    for b, (sr, si) in enumerate(squares):
        on = ((exps >> b) & 1) == 1
        pr, pi = _cmul(pr, pi, jnp.where(on, sr, 1.0), jnp.where(on, si, 0.0))
    return pr, pi


def _ssm_prep_kernel(lam_ref, ldt_ref, btr_ref, bti_ref, ctr_ref, cti_ref, d_ref, *refs):
    n_cast = (len(refs) - 4) // 2
    tb_ref, c_ref, levr_ref, levi_ref = refs[n_cast:n_cast + 4]
    _cast_blocks(refs[:n_cast], refs[n_cast + 4:])
    n_levels = levr_ref.shape[1]
    tau_lane = lax.broadcasted_iota(jnp.int32, (SSM_STATE, SUB_W), 1) // SSM_GROUP_CH
    row_h = lax.broadcasted_iota(jnp.int32, (SSM_GROUP_CH, SUB_W), 0)
    lane_h = lax.broadcasted_iota(jnp.int32, (SSM_GROUP_CH, SUB_W), 1)
    tile_ch = (lane_h % SSM_GROUP_CH == row_h).astype(F32)
    row_m = lax.broadcasted_iota(jnp.int32, (SUB, SSM_STATE), 0)
    eye_p = (lax.broadcasted_iota(jnp.int32, (SSM_STATE, SSM_STATE), 0)
             == lax.broadcasted_iota(jnp.int32, (SSM_STATE, SSM_STATE), 1))

    def to_col(v):
        return jnp.sum(jnp.where(eye_p, v, 0.0), axis=1, keepdims=True)

    def squares(a, n):
        out = [a]
        for _ in range(n - 1):
            out.append(_cmul(*out[-1], *out[-1]))
        return out

    for g in range(SLAB_GROUPS):
        dt = jnp.exp(ldt_ref[g])

        def discretise(lr, li):
            mag = jnp.exp(lr * dt)
            return mag * jnp.cos(li * dt), mag * jnp.sin(li * dt)

        lr_row, li_row = lam_ref[g, 0:1, :], lam_ref[g, 1:2, :]
        a_row = discretise(lr_row, li_row)
        sq_row = squares(a_row, 5)
        sq_col = squares((to_col(a_row[0]), to_col(a_row[1])), 4)
        ar, ai = sq_row[0]
        den = lr_row * lr_row + li_row * li_row
        fr = ((ar - 1.0) * lr_row + ai * li_row) / den
        fi = (ai * lr_row - (ar - 1.0) * li_row) / den
        bbr, bbi = _cmul(fr, fi, btr_ref[g], bti_ref[g])

        e0 = _pow_by_bits(tau_lane, sq_col)
        e1 = _cmul(*e0, *sq_col[0])
        ctr = jnp.dot(ctr_ref[g], tile_ch, precision=HIGHEST, preferred_element_type=F32)
        cti = jnp.dot(cti_ref[g], tile_ch, precision=HIGHEST, preferred_element_type=F32)
        mr, mi = _cmul(*e0, ctr, cti)
        kt = (jnp.dot(bbr, mr, precision=HIGHEST, preferred_element_type=F32)
              - jnp.dot(bbi, mi, precision=HIGHEST, preferred_element_type=F32))
        kt = kt + jnp.where(lane_h == row_h, d_ref[g], 0.0)
        pw = _pow_by_bits(row_m, sq_row[:4])
        for j in range(SUB):
            rows = pl.ds(j * SSM_GROUP_CH, SSM_GROUP_CH)
            tj = kt if j == 0 else jnp.where(lane_h >= j * SSM_GROUP_CH,
                                             pltpu.roll(kt, j * SSM_GROUP_CH, axis=1), 0.0)
            tb_ref[g, rows, 0:SUB_W] = tj.astype(BF16)
            m = SUB - 1 - j
            br_, bi_ = _cmul(pw[0][m:m + 1, :], pw[1][m:m + 1, :], bbr, bbi)
            tb_ref[g, rows, SUB_W:SUB_W + STATE_W] = jnp.concatenate([br_, bi_], axis=1).astype(BF16)
            tb_ref[g, rows, SUB_W + STATE_W:TB_W] = jnp.concatenate([bi_, br_], axis=1).astype(BF16)
        c_ref[g, 0:SSM_STATE, :] = (ctr * e1[0] - cti * e1[1]).astype(BF16)
        c_ref[g, SSM_STATE:STATE_W, :] = (-ctr * e1[1] - cti * e1[0]).astype(BF16)
        lv = sq_row[4]
        lev_r, lev_i = [], []
        for _ in range(n_levels):
            lev_r.append(jnp.concatenate([lv[0], lv[0]], axis=1))
            lev_i.append(jnp.concatenate([-lv[1], lv[1]], axis=1))
            lv = _cmul(*lv, *lv)
        levr_ref[g] = jnp.concatenate(lev_r, axis=0)
        levi_ref[g] = jnp.concatenate(lev_i, axis=0)


def _ssm_prep(lam_re, lam_im, log_dt, b_re, b_im, c_re, c_im, d_skip, n_levels, cast_weights):
    dg = lam_re.shape[0] * SSM_GROUPS
    casts = [_RowCast(w, wl, dg // SLAB_GROUPS, lambda s: s) for w, wl in cast_weights]
    h_, p_ = SSM_GROUP_CH, SSM_STATE
    f = lambda a: a.astype(F32)
    lam_rows = jnp.stack([f(lam_re), f(lam_im)], axis=2).reshape(dg, 2, p_)
    btr = f(b_re).transpose(0, 1, 3, 2).reshape(dg, h_, p_)
    bti = f(b_im).transpose(0, 1, 3, 2).reshape(dg, h_, p_)
    ctr = f(c_re).transpose(0, 1, 3, 2).reshape(dg, p_, h_)
    cti = f(c_im).transpose(0, 1, 3, 2).reshape(dg, p_, h_)
    d_rows = jnp.pad(f(d_skip).reshape(dg, 1, h_), ((0, 0), (0, 0), (0, SUB_W - h_)))
    grp = lambda *shape: pl.BlockSpec((SLAB_GROUPS,) + shape, lambda s: (s,) + (0,) * len(shape))
    outs = pl.pallas_call(
        _ssm_prep_kernel,
        grid=(dg // SLAB_GROUPS,),
        in_specs=[grp(2, p_), grp(1, 1), grp(h_, p_), grp(h_, p_),
                  grp(p_, h_), grp(p_, h_), grp(1, SUB_W)] + [c.in_spec for c in casts],
        out_specs=[grp(SUB_W, TB_W), grp(STATE_W, SUB_W), grp(n_levels, STATE_W),
                   grp(n_levels, STATE_W)] + [c.out_spec for c in casts],
        out_shape=[
            jax.ShapeDtypeStruct((dg, SUB_W, TB_W), BF16),
            jax.ShapeDtypeStruct((dg, STATE_W, SUB_W), BF16),
            jax.ShapeDtypeStruct((dg, n_levels, STATE_W), F32),
            jax.ShapeDtypeStruct((dg, n_levels, STATE_W), F32),
        ] + [c.out_shape for c in casts],
        compiler_params=pltpu.CompilerParams(
            dimension_semantics=("arbitrary",), vmem_limit_bytes=VMEM_LIMIT),
        name="s5_prep",
    )(lam_rows, f(log_dt).reshape(dg, 1, 1), btr, bti, ctr, cti, d_rows, *[c.operand for c in casts])
    return outs[:4], outs[4:]


def _slab_permutations():
    r = jnp.arange(SUB * LANES)
    j, lane = r // LANES, r % LANES
    grp, ch = lane // SSM_GROUP_CH, lane % SSM_GROUP_CH
    col = grp * SUB_W + j * SSM_GROUP_CH + ch
    full = (col[:, None] == jnp.arange(SLAB_W)[None, :]).astype(BF16)
    split = lambda m, n: m.reshape(m.shape[0], n, m.shape[1] // n).transpose(1, 0, 2)
    return split(full, SLAB_GROUPS), split(full.T, SUB // 2)


S5_STAGES = 3


def _ssm_kernel(u_ref, pin_ref, pout_ref, tb_ref, c_ref, ar_ref, ai_ref, y_ref, xs_ref, ys_ref):
    t = pl.program_id(0)
    rows = xs_ref.shape[1]
    n_levels = rows.bit_length() - 1
    cur, prv = t % 2, (t + 1) % 2

    @pl.when(t == 0)
    def _():
        xs_ref[...] = jnp.zeros(xs_ref.shape, BF16)
        ys_ref[...] = jnp.zeros(ys_ref.shape, BF16)

    def shifted(a, sh):
        return jnp.concatenate([jnp.zeros((sh, a.shape[1]), a.dtype), a[:rows - sh]], axis=0)

    ys_done = ys_ref[cur]
    u_all = jnp.concatenate([u_ref[pl.ds(j, rows, stride=SUB), :] for j in range(SUB)],
                            axis=1).astype(BF16)
    def scan_inputs(g):
        return jnp.dot(xs_ref[prv, :, g * SUB_W:(g + 1) * SUB_W], tb_ref[g], preferred_element_type=F32)

    r_next = scan_inputs(0)
    for g in range(SLAB_GROUPS):
        r = r_next
        if g + 1 < SLAB_GROUPS:
            r_next = scan_inputs(g + 1)
        xs_ref[cur, :, g * SUB_W:(g + 1) * SUB_W] = jnp.dot(
            u_all, pin_ref[g], preferred_element_type=F32).astype(BF16)
        o = jnp.dot(ys_done, pout_ref[g], preferred_element_type=F32)
        y_ref[pl.ds(2 * g, rows, stride=SUB), :] = o[:, :LANES]
        y_ref[pl.ds(2 * g + 1, rows, stride=SUB), :] = o[:, LANES:]
        yt = r[:, :SUB_W]
        s = r[:, SUB_W:SUB_W + STATE_W]
        w = r[:, SUB_W + STATE_W:]
        for k in range(n_levels):
            sh = 1 << k
            ar = ar_ref[g, k:k + 1, :]
            ai = ai_ref[g, k:k + 1, :]
            ps, pw = shifted(s, sh), shifted(w, sh)
            s, w = s + ar * ps + ai * pw, (w + ar * pw - ai * ps if k + 1 < n_levels else None)
        y = yt + jnp.dot(shifted(s, 1).astype(BF16), c_ref[g], preferred_element_type=F32)
        ys_ref[prv, :, g * SUB_W:(g + 1) * SUB_W] = jax.nn.gelu(y).astype(BF16)


def _ssm(uf, perm_in, perm_out, tb, cmat, lev_r, lev_i, layer, batch, seq):
    t = uf.shape[0]
    n_sub = seq // SUB
    n_levels = lev_r.shape[1]
    n_items = batch * N_SLABS
    item = lambda t, stage: jnp.clip(t - stage, 0, n_items - 1)
    const3 = lambda t: (0, 0, 0)
    slab3 = lambda t: (layer * N_SLABS + item(t, 1) % N_SLABS, 0, 0)
    return pl.pallas_call(
        _ssm_kernel,
        grid=(n_items + S5_STAGES - 1,),
        in_specs=[
            pl.BlockSpec((seq, LANES), lambda t: (item(t, 0) // N_SLABS, item(t, 0) % N_SLABS)),
            pl.BlockSpec((SLAB_GROUPS, SUB * LANES, SUB_W), const3, pipeline_mode=pl.Buffered(1)),
            pl.BlockSpec((SUB // 2, SLAB_W, 2 * LANES), const3, pipeline_mode=pl.Buffered(1)),
            pl.BlockSpec((SLAB_GROUPS, SUB_W, TB_W), slab3),
            pl.BlockSpec((SLAB_GROUPS, STATE_W, SUB_W), slab3),
            pl.BlockSpec((SLAB_GROUPS, n_levels, STATE_W), slab3),
            pl.BlockSpec((SLAB_GROUPS, n_levels, STATE_W), slab3),
        ],
        out_specs=pl.BlockSpec((seq, LANES), lambda t: (item(t, 2) // N_SLABS, item(t, 2) % N_SLABS)),
        out_shape=jax.ShapeDtypeStruct((t, SSM_WIDTH), F32),
        scratch_shapes=[pltpu.VMEM((2, n_sub, SLAB_W), BF16), pltpu.VMEM((2, n_sub, SLAB_W), BF16)],
        compiler_params=pltpu.CompilerParams(
            dimension_semantics=("arbitrary",), vmem_limit_bytes=VMEM_LIMIT),
        name="s5_scan",
    )(uf, perm_in, perm_out, tb, cmat, lev_r, lev_i)


def _merge_kernel(ya_ref, yg_ref, gt_ref, x_ref, wglu_ref, bglu_ref, wa_ref, ws_ref, wo_ref,
                  gffn_ref, *refs):
    n_cast = (len(refs) - 2) // 2
    x1_ref, h2_ref = refs[n_cast:n_cast + 2]
    _cast_blocks(refs[:n_cast], refs[n_cast + 2:])
    yg = yg_ref[...]
    t = jnp.dot(yg.astype(BF16), wglu_ref[...], preferred_element_type=F32) + bglu_ref[...]
    ys = (yg * _sigmoid(t)).astype(BF16)
    ma = jnp.dot(ya_ref[...], wa_ref[...], preferred_element_type=F32)
    ms = jnp.dot(ys, ws_ref[...], preferred_element_type=F32)
    merged = gt_ref[:, :D_MODEL].astype(F32) * ma + gt_ref[:, D_MODEL:].astype(F32) * ms
    x1 = x_ref[...] + jnp.dot(merged.astype(BF16), wo_ref[...], preferred_element_type=F32)
    x1_ref[...] = x1
    h2_ref[...] = _rms(x1, gffn_ref[...]).astype(BF16)


def _merge(ya, yg, gates, x2, wglu, bglu, wa, ws, wo, gffn, cast_weights):
    t = x2.shape[0]
    tm = MERGE_TM
    casts = [_RowCast(w, wl, t // tm, lambda i: i) for w, wl in cast_weights]

    def wspec(rows, cols):
        return pl.BlockSpec((rows, cols), lambda i: (0, 0), pipeline_mode=pl.Buffered(1))

    outs = pl.pallas_call(
        _merge_kernel,
        grid=(t // tm,),
        in_specs=[
            pl.BlockSpec((tm, ATTN_WIDTH), lambda i: (i, 0)),
            pl.BlockSpec((tm, SSM_WIDTH), lambda i: (i, 0)),
            pl.BlockSpec((tm, 2 * D_MODEL), lambda i: (i, 0)),
            pl.BlockSpec((tm, D_MODEL), lambda i: (i, 0)),
            wspec(SSM_WIDTH, SSM_WIDTH),
            pl.BlockSpec((1, SSM_WIDTH), lambda i: (0, 0)),
            wspec(ATTN_WIDTH, D_MODEL),
            wspec(SSM_WIDTH, D_MODEL),
            wspec(D_MODEL, D_MODEL),
            pl.BlockSpec((1, D_MODEL), lambda i: (0, 0)),
        ] + [c.in_spec for c in casts],
        out_specs=[
            pl.BlockSpec((tm, D_MODEL), lambda i: (i, 0)),
            pl.BlockSpec((tm, D_MODEL), lambda i: (i, 0)),
        ] + [c.out_spec for c in casts],
        out_shape=[
            jax.ShapeDtypeStruct((t, D_MODEL), F32),
            jax.ShapeDtypeStruct((t, D_MODEL), BF16),
        ] + [c.out_shape for c in casts],
        compiler_params=pltpu.CompilerParams(
            dimension_semantics=("arbitrary",), vmem_limit_bytes=VMEM_LIMIT),
        name="merge_out",
    )(ya, yg, gates, x2, wglu, bglu, wa, ws, wo, gffn, *[c.operand for c in casts])
    return outs[:2], outs[2:]


def _ffn_kernel(h_ref, x_ref, wg_ref, wu_ref, wo_ref, o_ref):
    k = pl.program_id(1)

    def accumulate(base_ref):
        for r in range(FFN_TM // FFN_SUB):
            rs = pl.ds(r * FFN_SUB, FFN_SUB)
            h = h_ref[rs, :]
            g = jnp.dot(h, wg_ref[...], preferred_element_type=F32)
            u = jnp.dot(h, wu_ref[...], preferred_element_type=F32)
            act = (g * _sigmoid(g) * u).astype(BF16)
            o_ref[rs, :] = base_ref[rs, :] + jnp.dot(act, wo_ref[...], preferred_element_type=F32)

    @pl.when(k == 0)
    def _():
        accumulate(x_ref)

    @pl.when(k > 0)
    def _():
        accumulate(o_ref)


def _ffn(h2, x1, w_in, w_out):
    t = x1.shape[0]
    nk = D_FF // FFN_TF
    return pl.pallas_call(
        _ffn_kernel,
        grid=(t // FFN_TM, nk),
        in_specs=[
            pl.BlockSpec((FFN_TM, D_MODEL), lambda i, k: (i, 0)),
            pl.BlockSpec((FFN_TM, D_MODEL), lambda i, k: (i, 0)),
            pl.BlockSpec((D_MODEL, FFN_TF), lambda i, k: (0, k)),
            pl.BlockSpec((D_MODEL, FFN_TF), lambda i, k: (0, nk + k)),
            pl.BlockSpec((FFN_TF, D_MODEL), lambda i, k: (k, 0)),
        ],
        out_specs=pl.BlockSpec((FFN_TM, D_MODEL), lambda i, k: (i, 0)),
        out_shape=jax.ShapeDtypeStruct((t, D_MODEL), F32),
        compiler_params=pltpu.CompilerParams(
            dimension_semantics=("arbitrary", "arbitrary"), vmem_limit_bytes=FFN_VMEM_LIMIT),
        name="swiglu_ffn",
    )(h2, x1, w_in, w_in, w_out)


def kernel(x, norm_mix_g, w_in, gate_bias, q_norm_g, k_norm_g, attn_sinks, ssm_lambda_re, ssm_lambda_im, ssm_log_dt, ssm_b_re, ssm_b_im, ssm_c_re, ssm_c_im, ssm_d, ssm_glu_w, ssm_glu_b, w_attn_branch, w_ssm_branch, w_out, norm_ffn_g, w_ffn_in, w_ffn_out):
    batch, seq, _ = x.shape
    t = batch * seq
    n_levels = (seq // SUB).bit_length() - 1
    x2 = x.reshape(t, D_MODEL).astype(F32)
    f32 = lambda w: w.astype(F32)
    seg = _segment_ones()
    bias_tab = _attn_bias_tables()
    perm_in, perm_out = _slab_permutations()
    w_in_f = f32(w_in)
    (tb, cmat, lev_r, lev_i), (w_in_bf,) = _ssm_prep(
        ssm_lambda_re, ssm_lambda_im, ssm_log_dt, ssm_b_re, ssm_b_im, ssm_c_re, ssm_c_im, ssm_d,
        n_levels, [(w_in_f, 0)])
    merge_weights = [f32(ssm_glu_w), f32(w_attn_branch), f32(w_ssm_branch), f32(w_out)]
    ffn_weights = [f32(w_ffn_in), f32(w_ffn_out)]
    for l in range(DEPTH):
        next_w_in = [(w_in_f, l + 1)] if l + 1 < DEPTH else []
        (uf, gates, ya), cast_out = _inproj_attn(
            x2, norm_mix_g[l].reshape(1, D_MODEL).astype(F32), w_in_bf,
            gate_bias[l].reshape(1, 2 * D_MODEL).astype(F32), seg,
            _qk_norm_tables(q_norm_g[l], k_norm_g[l]), attn_sinks[l].astype(F32), bias_tab, seq,
            [(w, l) for w in merge_weights] + next_w_in)
        wglu_bf, wa_bf, ws_bf, wo_bf = cast_out[:4]
        w_in_bf = cast_out[4] if next_w_in else None
        yg = _ssm(uf, perm_in, perm_out, tb, cmat, lev_r, lev_i, l, batch, seq)
        (x1, h2), (wfi_bf, wfo_bf) = _merge(
            ya, yg, gates, x2, wglu_bf, ssm_glu_b[l].reshape(1, SSM_WIDTH).astype(F32),
            wa_bf, ws_bf, wo_bf, norm_ffn_g[l].reshape(1, D_MODEL).astype(F32),
            [(w, l) for w in ffn_weights])
        x2 = _ffn(h2, x1, wfi_bf, wfo_bf)
    return x2.reshape(batch, seq, D_MODEL).astype(x.dtype)
```

```python
import jax
import jax.numpy as jnp
from jax import lax
from jax.experimental import pallas as pl
from jax.experimental.pallas import tpu as pltpu

D_MODEL = 2048
DEPTH = 2
HEAD_DIM = 64
N_Q_HEADS = 16
N_KV_HEADS = 4
GQA_GROUP = N_Q_HEADS // N_KV_HEADS
ATTN_WIDTH = N_Q_HEADS * HEAD_DIM
KV_WIDTH = N_KV_HEADS * HEAD_DIM
WINDOW = 128
BLOCK = 128
SSM_WIDTH = D_MODEL // 2
SSM_GROUP_CH = 16
SSM_GROUPS = SSM_WIDTH // SSM_GROUP_CH
SSM_STATE = 64
D_FF = -(-8 * D_MODEL // (3 * 256)) * 256
OFF_K = ATTN_WIDTH
OFF_V = OFF_K + KV_WIDTH
OFF_U = OFF_V + KV_WIDTH
OFF_G = OFF_U + SSM_WIDTH
IN_WIDTH = OFF_G + 2 * D_MODEL
RMS_EPS = 1e-6

F32 = jnp.float32
BF16 = jnp.bfloat16
HIGHEST = lax.Precision.HIGHEST

LANES = 128
BF16_SUBLANES = 16
SUB = 16
SUB_W = SUB * SSM_GROUP_CH
STATE_W = 2 * SSM_STATE
TB_W = SUB_W + 2 * STATE_W
SLAB_GROUPS = LANES // SSM_GROUP_CH
N_SLABS = SSM_GROUPS // SLAB_GROUPS
SLAB_W = SLAB_GROUPS * SUB_W

VMEM_LIMIT = 56 * 1024 * 1024
FFN_VMEM_LIMIT = 60 * 1024 * 1024

INPROJ_TM = 256
INPROJ_TN = 512
MERGE_TM = 256
FFN_TM = 1024
FFN_TF = 512
FFN_SUB = 512


def _rms(x, g):
    return x * lax.rsqrt(jnp.mean(x * x, axis=-1, keepdims=True) + RMS_EPS) * g


def _sigmoid(x):
    return 0.5 * jnp.tanh(0.5 * x) + 0.5


class _RowCast:
    def __init__(self, stacked, layer, n_steps, step_of):
        _, rows, cols = stacked.shape
        blk = rows // n_steps
        assert blk * n_steps == rows and blk % BF16_SUBLANES == 0, (rows, n_steps)
        self.operand = stacked
        self.in_spec = pl.BlockSpec((None, blk, cols), lambda *ids: (layer, step_of(*ids), 0))
        self.out_spec = pl.BlockSpec((blk, cols), lambda *ids: (step_of(*ids), 0))
        self.out_shape = jax.ShapeDtypeStruct((rows, cols), BF16)


def _cast_blocks(src_refs, dst_refs):
    for src, dst in zip(src_refs, dst_refs):
        dst[...] = src[...].astype(BF16)


N_QKV_BLOCKS = OFF_U // INPROJ_TN
N_MAIN_BLOCKS = OFF_G // INPROJ_TN
N_IN_BLOCKS = IN_WIDTH // INPROJ_TN


QBLOCKS_PER_TILE = INPROJ_TM // BLOCK
ATTN_PROBLEMS_PER_GAP = 2


def _inproj_attn_kernel(sink_ref, x_ref, g_ref, w_ref, b_ref, seg_ref, nt_ref, bias0_ref, bias_ref, *refs):
    n_cast = (len(refs) - 5) // 2
    u_ref, gt_ref, ya_ref = refs[n_cast:n_cast + 3]
    q_ref, kv_ref = refs[2 * n_cast + 3:]

    @pl.when(pl.program_id(0) == 0)
    def _():
        kv_ref[...] = jnp.zeros(kv_ref.shape, BF16)

    kv_ref[0:BLOCK, :] = kv_ref[INPROJ_TM:INPROJ_TM + BLOCK, :]
    h = _rms(x_ref[...], g_ref[...]).astype(BF16)

    def project(j):
        return jnp.dot(h, w_ref[:, j * INPROJ_TN:(j + 1) * INPROJ_TN], preferred_element_type=F32)

    for j in range(N_QKV_BLOCKS):
        z = project(j)
        ssq = jnp.dot((z * z).astype(BF16), seg_ref[...], preferred_element_type=F32)
        inv = lax.rsqrt(ssq * (1.0 / HEAD_DIM) + RMS_EPS)
        fac = jnp.where(nt_ref[j, 1:2, :] > 0.0, inv, 1.0) * nt_ref[j, 0:1, :]
        if (j + 1) * INPROJ_TN <= ATTN_WIDTH:
            q_ref[:, j * INPROJ_TN:(j + 1) * INPROJ_TN] = (z * fac).astype(BF16)
        else:
            kv_ref[BLOCK:, :] = (z * fac).astype(BF16)

    def other_block(j):
        z = project(j)
        if j < N_MAIN_BLOCKS:
            c = j - N_QKV_BLOCKS
            u_ref[:, c * INPROJ_TN:(c + 1) * INPROJ_TN] = z
        else:
            cs = slice((j - N_MAIN_BLOCKS) * INPROJ_TN, (j - N_MAIN_BLOCKS + 1) * INPROJ_TN)
            gt_ref[:, cs] = _sigmoid(z + b_ref[:, cs]).astype(BF16)

    kv_rows = 2 * BLOCK
    all_rows = kv_ref.shape[0]
    left_kv = lax.broadcasted_iota(jnp.int32, (all_rows, LANES), 1) < HEAD_DIM
    left_q = lax.broadcasted_iota(jnp.int32, (BLOCK, LANES), 1) < HEAD_DIM
    zeros = jnp.zeros((all_rows, LANES), BF16)
    left_ones = lax.broadcasted_iota(jnp.int32, (kv_rows, LANES), 1) < HEAD_DIM
    ones_l = jnp.where(left_ones, 1.0, 0.0).astype(BF16)
    ones_r = jnp.where(left_ones, 0.0, 1.0).astype(BF16)
    contract_lanes = (((1,), (1,)), ((), ()))

    def slab(off):
        a = kv_ref[:, off:off + LANES]
        return a, pltpu.roll(a.astype(F32), HEAD_DIM, axis=1).astype(BF16)

    problems = []
    for c in range(N_KV_HEADS // 2):
        k_slabs = slab(c * LANES)
        v_slabs = slab(KV_WIDTH + c * LANES)
        for side in range(2):
            kh = 2 * c + side
            k_l = jnp.where(left_kv, k_slabs[side], zeros)
            k_r = jnp.where(left_kv, zeros, k_slabs[1 - side])
            v_l = jnp.where(left_kv, v_slabs[side], zeros)
            v_r = jnp.where(left_kv, zeros, v_slabs[1 - side])
            for qb in range(QBLOCKS_PER_TILE):
                rows = slice(qb * BLOCK, qb * BLOCK + kv_rows)
                kk = jnp.concatenate([k_l[rows], k_r[rows]], axis=0)
                vv = jnp.concatenate([jnp.concatenate([v_l[rows], ones_l], axis=1),
                                      jnp.concatenate([v_r[rows], ones_r], axis=1)], axis=0)
                for pair in range(GQA_GROUP // 2):
                    problems.append((qb, kh * GQA_GROUP + 2 * pair, kk, vv))

    def scores(qb, e, kk):
        q2 = q_ref[qb * BLOCK:(qb + 1) * BLOCK, e * HEAD_DIM:(e + 2) * HEAD_DIM]
        return lax.dot_general(q2, kk, contract_lanes, preferred_element_type=F32)

    def finish(qb, e, s2, vv):
        bias = bias0_ref if qb == 0 else bias_ref
        ps, ds = [], []
        for t in range(2):
            s = s2[:, t * kv_rows:(t + 1) * kv_rows] + bias[e + t]
            sink = sink_ref[e + t]
            m = jnp.maximum(jnp.max(s, axis=-1, keepdims=True), sink)
            ps.append(jnp.exp(s - m).astype(BF16))
            ds.append(jnp.exp(sink - m))
        r = jnp.dot(jnp.concatenate(ps, axis=1), vv, preferred_element_type=F32)
        denom = r[:, LANES:] + jnp.where(left_q, ds[0], ds[1])
        ya_ref[qb * BLOCK:(qb + 1) * BLOCK, e * HEAD_DIM:(e + 2) * HEAD_DIM] = (r[:, :LANES] / denom).astype(BF16)

    others = list(range(N_QKV_BLOCKS, N_IN_BLOCKS))
    for first in range(0, len(problems), ATTN_PROBLEMS_PER_GAP):
        group = problems[first:first + ATTN_PROBLEMS_PER_GAP]
        s2s = [scores(qb, e, kk) for qb, e, kk, _ in group]
        if others:
            other_block(others.pop(0))
        for (qb, e, _, vv), s2 in zip(group, s2s):
            finish(qb, e, s2, vv)
    for j in others:
        other_block(j)
    _cast_blocks(refs[:n_cast], refs[n_cast + 3:2 * n_cast + 3])


def _inproj_attn(x2, gain, w_bf, bias, seg, ntab, sinks, bias_tab, seq, cast_weights):
    t = x2.shape[0]
    tm = INPROJ_TM
    tiles_per_seq = seq // tm
    const2 = lambda i: (0, 0)
    bias_shape = (None, N_Q_HEADS, BLOCK, 2 * BLOCK)
    casts = [_RowCast(w, wl, t // tm, lambda i: i) for w, wl in cast_weights]
    outs = pl.pallas_call(
        _inproj_attn_kernel,
        grid=(t // tm,),
        in_specs=[
            pl.BlockSpec(memory_space=pltpu.SMEM),
            pl.BlockSpec((tm, D_MODEL), lambda i: (i, 0)),
            pl.BlockSpec((1, D_MODEL), const2),
            pl.BlockSpec((D_MODEL, IN_WIDTH), const2, pipeline_mode=pl.Buffered(1)),
            pl.BlockSpec((1, 2 * D_MODEL), const2),
            pl.BlockSpec((INPROJ_TN, INPROJ_TN), const2),
            pl.BlockSpec((N_QKV_BLOCKS, 2, INPROJ_TN), lambda i: (0, 0, 0)),
            pl.BlockSpec(bias_shape, lambda i: (jnp.minimum(i % tiles_per_seq, 1), 0, 0, 0)),
            pl.BlockSpec(bias_shape, lambda i: (1, 0, 0, 0), pipeline_mode=pl.Buffered(1)),
        ] + [c.in_spec for c in casts],
        out_specs=[
            pl.BlockSpec((tm, SSM_WIDTH), lambda i: (i, 0)),
            pl.BlockSpec((tm, 2 * D_MODEL), lambda i: (i, 0)),
            pl.BlockSpec((tm, ATTN_WIDTH), lambda i: (i, 0)),
        ] + [c.out_spec for c in casts],
        out_shape=[
            jax.ShapeDtypeStruct((t, SSM_WIDTH), F32),
            jax.ShapeDtypeStruct((t, 2 * D_MODEL), BF16),
            jax.ShapeDtypeStruct((t, ATTN_WIDTH), BF16),
        ] + [c.out_shape for c in casts],
        scratch_shapes=[pltpu.VMEM((tm, ATTN_WIDTH), BF16), pltpu.VMEM((BLOCK + tm, 2 * KV_WIDTH), BF16)],
        compiler_params=pltpu.CompilerParams(
            dimension_semantics=("arbitrary",), vmem_limit_bytes=VMEM_LIMIT),
        name="inproj_attn",
    )(sinks, x2, gain, w_bf, bias, seg, ntab, bias_tab, bias_tab, *[c.operand for c in casts])
    return outs[:3], outs[3:]


def _qk_norm_tables(q_gain, k_gain):
    qrow = jnp.tile(q_gain.astype(F32), INPROJ_TN // HEAD_DIM) * (HEAD_DIM ** -0.5)
    ones = jnp.ones((INPROJ_TN,), F32)
    kvrow = jnp.concatenate([jnp.tile(k_gain.astype(F32), N_KV_HEADS), jnp.ones((KV_WIDTH,), F32)])
    kvmask = jnp.concatenate([jnp.ones((KV_WIDTH,), F32), jnp.zeros((KV_WIDTH,), F32)])
    return jnp.stack([jnp.stack([qrow, ones]), jnp.stack([qrow, ones]), jnp.stack([kvrow, kvmask])])


def _segment_ones():
    r = jnp.arange(INPROJ_TN) // HEAD_DIM
    return (r[:, None] == r[None, :]).astype(BF16)


def _attn_bias_tables():
    t_loc = jnp.arange(BLOCK)[:, None]
    s_loc = jnp.arange(2 * BLOCK)[None, :] - BLOCK
    dist = (t_loc - s_loc).astype(F32)
    valid = (dist >= 0) & (dist < WINDOW)
    slopes = jnp.exp2(-8.0 * jnp.arange(1, N_Q_HEADS + 1, dtype=F32) / N_Q_HEADS)
    bias = -slopes[:, None, None] * dist[None]
    full = jnp.where(valid[None], bias, -jnp.inf)
    first = jnp.where((valid & (s_loc >= 0))[None], bias, -jnp.inf)
    return jnp.stack([first, full])


def _cmul(ar, ai, br, bi):
    return ar * br - ai * bi, ar * bi + ai * br


def _pow_by_bits(exps, squares):
    pr = pi = None
    for b, (sr, si) in enumerate(squares):
        on = ((exps >> b) & 1) == 1
        fr, fi = jnp.where(on, sr, 1.0), jnp.where(on, si, 0.0)
        pr, pi = (fr, fi) if pr is None else _cmul(pr, pi, fr, fi)
    return pr, pi


def _ssm_prep_kernel(lam_ref, ldt_ref, btr_ref, bti_ref, ctr_ref, cti_ref, d_ref, *refs):
    n_cast = (len(refs) - 4) // 2
    tb_ref, c_ref, levr_ref, levi_ref = refs[n_cast:n_cast + 4]
    _cast_blocks(refs[:n_cast], refs[n_cast + 4:])
    n_levels = levr_ref.shape[1]
    tau_lane = lax.broadcasted_iota(jnp.int32, (SSM_STATE, SUB_W), 1) // SSM_GROUP_CH
    row_h = lax.broadcasted_iota(jnp.int32, (SSM_GROUP_CH, SUB_W), 0)
    lane_h = lax.broadcasted_iota(jnp.int32, (SSM_GROUP_CH, SUB_W), 1)
    tile_ch = (lane_h % SSM_GROUP_CH == row_h).astype(F32)
    row_m = lax.broadcasted_iota(jnp.int32, (SUB, SSM_STATE), 0)
    eye_p = (lax.broadcasted_iota(jnp.int32, (SSM_STATE, SSM_STATE), 0)
             == lax.broadcasted_iota(jnp.int32, (SSM_STATE, SSM_STATE), 1))

    def to_col(v):
        return jnp.sum(jnp.where(eye_p, v, 0.0), axis=1, keepdims=True)

    def squares(a, n):
        out = [a]
        for _ in range(n - 1):
            out.append(_cmul(*out[-1], *out[-1]))
        return out

    for g in range(SLAB_GROUPS):
        dt = jnp.exp(ldt_ref[g])

        def discretise(lr, li):
            mag = jnp.exp(lr * dt)
            return mag * jnp.cos(li * dt), mag * jnp.sin(li * dt)

        lr_row, li_row = lam_ref[g, 0:1, :], lam_ref[g, 1:2, :]
        a_row = discretise(lr_row, li_row)
        sq_row = squares(a_row, 5)
        sq_col = squares((to_col(a_row[0]), to_col(a_row[1])), 4)
        ar, ai = sq_row[0]
        den = lr_row * lr_row + li_row * li_row
        fr = ((ar - 1.0) * lr_row + ai * li_row) / den
        fi = (ai * lr_row - (ar - 1.0) * li_row) / den
        bbr, bbi = _cmul(fr, fi, btr_ref[g], bti_ref[g])

        e0 = _pow_by_bits(tau_lane, sq_col)
        e1 = _cmul(*e0, *sq_col[0])
        ctr = jnp.dot(ctr_ref[g], tile_ch, precision=HIGHEST, preferred_element_type=F32)
        cti = jnp.dot(cti_ref[g], tile_ch, precision=HIGHEST, preferred_element_type=F32)
        mr, mi = _cmul(*e0, ctr, cti)
        kt = (jnp.dot(bbr, mr, precision=HIGHEST, preferred_element_type=F32)
              - jnp.dot(bbi, mi, precision=HIGHEST, preferred_element_type=F32))
        kt = kt + jnp.where(lane_h == row_h, d_ref[g], 0.0)
        pw = _pow_by_bits(row_m, sq_row[:4])
        for j in range(SUB):
            rows = pl.ds(j * SSM_GROUP_CH, SSM_GROUP_CH)
            tj = kt if j == 0 else jnp.where(lane_h >= j * SSM_GROUP_CH,
                                             pltpu.roll(kt, j * SSM_GROUP_CH, axis=1), 0.0)
            tb_ref[g, rows, 0:SUB_W] = tj.astype(BF16)
            m = SUB - 1 - j
            br_, bi_ = _cmul(pw[0][m:m + 1, :], pw[1][m:m + 1, :], bbr, bbi)
            tb_ref[g, rows, SUB_W:SUB_W + STATE_W] = jnp.concatenate([br_, bi_], axis=1).astype(BF16)
            tb_ref[g, rows, SUB_W + STATE_W:TB_W] = jnp.concatenate([bi_, br_], axis=1).astype(BF16)
        c_ref[g, 0:SSM_STATE, :] = (ctr * e1[0] - cti * e1[1]).astype(BF16)
        c_ref[g, SSM_STATE:STATE_W, :] = (-ctr * e1[1] - cti * e1[0]).astype(BF16)
        lv = sq_row[4]
        lev_r, lev_i = [], []
        for _ in range(n_levels):
            lev_r.append(jnp.concatenate([lv[0], lv[0]], axis=1))
            lev_i.append(jnp.concatenate([-lv[1], lv[1]], axis=1))
            lv = _cmul(*lv, *lv)
        levr_ref[g] = jnp.concatenate(lev_r, axis=0)
        levi_ref[g] = jnp.concatenate(lev_i, axis=0)


def _ssm_prep(lam_re, lam_im, log_dt, b_re, b_im, c_re, c_im, d_skip, n_levels, cast_weights):
    dg = lam_re.shape[0] * SSM_GROUPS
    casts = [_RowCast(w, wl, dg // SLAB_GROUPS, lambda s: s) for w, wl in cast_weights]
    h_, p_ = SSM_GROUP_CH, SSM_STATE
    f = lambda a: a.astype(F32)
    lam_rows = jnp.stack([f(lam_re), f(lam_im)], axis=2).reshape(dg, 2, p_)
    btr = f(b_re).transpose(0, 1, 3, 2).reshape(dg, h_, p_)
    bti = f(b_im).transpose(0, 1, 3, 2).reshape(dg, h_, p_)
    ctr = f(c_re).transpose(0, 1, 3, 2).reshape(dg, p_, h_)
    cti = f(c_im).transpose(0, 1, 3, 2).reshape(dg, p_, h_)
    d_rows = jnp.pad(f(d_skip).reshape(dg, 1, h_), ((0, 0), (0, 0), (0, SUB_W - h_)))
    grp = lambda *shape: pl.BlockSpec((SLAB_GROUPS,) + shape, lambda s: (s,) + (0,) * len(shape))
    outs = pl.pallas_call(
        _ssm_prep_kernel,
        grid=(dg // SLAB_GROUPS,),
        in_specs=[grp(2, p_), grp(1, 1), grp(h_, p_), grp(h_, p_),
                  grp(p_, h_), grp(p_, h_), grp(1, SUB_W)] + [c.in_spec for c in casts],
        out_specs=[grp(SUB_W, TB_W), grp(STATE_W, SUB_W), grp(n_levels, STATE_W),
                   grp(n_levels, STATE_W)] + [c.out_spec for c in casts],
        out_shape=[
            jax.ShapeDtypeStruct((dg, SUB_W, TB_W), BF16),
            jax.ShapeDtypeStruct((dg, STATE_W, SUB_W), BF16),
            jax.ShapeDtypeStruct((dg, n_levels, STATE_W), F32),
            jax.ShapeDtypeStruct((dg, n_levels, STATE_W), F32),
        ] + [c.out_shape for c in casts],
        compiler_params=pltpu.CompilerParams(
            dimension_semantics=("arbitrary",), vmem_limit_bytes=VMEM_LIMIT),
        name="s5_prep",
    )(lam_rows, f(log_dt).reshape(dg, 1, 1), btr, bti, ctr, cti, d_rows, *[c.operand for c in casts])
    return outs[:4], outs[4:]


def _slab_permutations():
    r = jnp.arange(SUB * LANES)
    j, lane = r // LANES, r % LANES
    grp, ch = lane // SSM_GROUP_CH, lane % SSM_GROUP_CH
    col = grp * SUB_W + j * SSM_GROUP_CH + ch
    full = (col[:, None] == jnp.arange(SLAB_W)[None, :]).astype(BF16)
    split = lambda m, n: m.reshape(m.shape[0], n, m.shape[1] // n).transpose(1, 0, 2)
    return split(full, SLAB_GROUPS), split(full.T, SUB // 2)


S5_STAGES = 3


def _ssm_kernel(u_ref, pin_ref, pout_ref, tb_ref, c_ref, ar_ref, ai_ref, y_ref, xs_ref, ys_ref):
    t = pl.program_id(0)
    rows = xs_ref.shape[1]
    n_levels = rows.bit_length() - 1
    cur, prv = t % 2, (t + 1) % 2

    @pl.when(t == 0)
    def _():
        xs_ref[...] = jnp.zeros(xs_ref.shape, BF16)
        ys_ref[...] = jnp.zeros(ys_ref.shape, BF16)

    def shifted(a, sh):
        return jnp.concatenate([jnp.zeros((sh, a.shape[1]), a.dtype), a[:rows - sh]], axis=0)

    ys_done = ys_ref[cur]
    u_all = jnp.concatenate([u_ref[pl.ds(j, rows, stride=SUB), :] for j in range(SUB)],
                            axis=1).astype(BF16)
    def scan_inputs(g):
        return jnp.dot(xs_ref[prv, :, g * SUB_W:(g + 1) * SUB_W], tb_ref[g], preferred_element_type=F32)

    r_next = scan_inputs(0)
    for g in range(SLAB_GROUPS):
        r = r_next
        if g + 1 < SLAB_GROUPS:
            r_next = scan_inputs(g + 1)
        xs_ref[cur, :, g * SUB_W:(g + 1) * SUB_W] = jnp.dot(
            u_all, pin_ref[g], preferred_element_type=F32).astype(BF16)
        o = jnp.dot(ys_done, pout_ref[g], preferred_element_type=F32)
        y_ref[pl.ds(2 * g, rows, stride=SUB), :] = o[:, :LANES]
        y_ref[pl.ds(2 * g + 1, rows, stride=SUB), :] = o[:, LANES:]
        yt = r[:, :SUB_W]
        s = r[:, SUB_W:SUB_W + STATE_W]
        w = r[:, SUB_W + STATE_W:]
        for k in range(n_levels):
            sh = 1 << k
            ar = ar_ref[g, k:k + 1, :]
            ai = ai_ref[g, k:k + 1, :]
            ps, pw = shifted(s, sh), shifted(w, sh)
            s, w = s + ar * ps + ai * pw, (w + ar * pw - ai * ps if k + 1 < n_levels else None)
        y = yt + jnp.dot(shifted(s, 1).astype(BF16), c_ref[g], preferred_element_type=F32)
        ys_ref[prv, :, g * SUB_W:(g + 1) * SUB_W] = jax.nn.gelu(y).astype(BF16)


def _ssm(uf, perm_in, perm_out, tb, cmat, lev_r, lev_i, layer, batch, seq):
    t = uf.shape[0]
    n_sub = seq // SUB
    n_levels = lev_r.shape[1]
    n_items = batch * N_SLABS
    item = lambda t, stage: jnp.clip(t - stage, 0, n_items - 1)
    const3 = lambda t: (0, 0, 0)
    slab3 = lambda t: (layer * N_SLABS + item(t, 1) % N_SLABS, 0, 0)
    return pl.pallas_call(
        _ssm_kernel,
        grid=(n_items + S5_STAGES - 1,),
        in_specs=[
            pl.BlockSpec((seq, LANES), lambda t: (item(t, 0) // N_SLABS, item(t, 0) % N_SLABS)),
            pl.BlockSpec((SLAB_GROUPS, SUB * LANES, SUB_W), const3, pipeline_mode=pl.Buffered(1)),
            pl.BlockSpec((SUB // 2, SLAB_W, 2 * LANES), const3, pipeline_mode=pl.Buffered(1)),
            pl.BlockSpec((SLAB_GROUPS, SUB_W, TB_W), slab3),
            pl.BlockSpec((SLAB_GROUPS, STATE_W, SUB_W), slab3),
            pl.BlockSpec((SLAB_GROUPS, n_levels, STATE_W), slab3),
            pl.BlockSpec((SLAB_GROUPS, n_levels, STATE_W), slab3),
        ],
        out_specs=pl.BlockSpec((seq, LANES), lambda t: (item(t, 2) // N_SLABS, item(t, 2) % N_SLABS)),
        out_shape=jax.ShapeDtypeStruct((t, SSM_WIDTH), F32),
        scratch_shapes=[pltpu.VMEM((2, n_sub, SLAB_W), BF16), pltpu.VMEM((2, n_sub, SLAB_W), BF16)],
        compiler_params=pltpu.CompilerParams(
            dimension_semantics=("arbitrary",), vmem_limit_bytes=VMEM_LIMIT),
        name="s5_scan",
    )(uf, perm_in, perm_out, tb, cmat, lev_r, lev_i)


def _merge_kernel(ya_ref, yg_ref, gt_ref, x_ref, wglu_ref, bglu_ref, wa_ref, ws_ref, wo_ref,
                  gffn_ref, *refs):
    n_cast = (len(refs) - 2) // 2
    x1_ref, h2_ref = refs[n_cast:n_cast + 2]
    yg = yg_ref[...]
    t = jnp.dot(yg.astype(BF16), wglu_ref[...], preferred_element_type=F32) + bglu_ref[...]
    ys = (yg * _sigmoid(t)).astype(BF16)
    ma = jnp.dot(ya_ref[...], wa_ref[...], preferred_element_type=F32)
    ms = jnp.dot(ys, ws_ref[...], preferred_element_type=F32)
    merged = gt_ref[:, :D_MODEL].astype(F32) * ma + gt_ref[:, D_MODEL:].astype(F32) * ms
    x1 = x_ref[...] + jnp.dot(merged.astype(BF16), wo_ref[...], preferred_element_type=F32)
    x1_ref[...] = x1
    h2_ref[...] = _rms(x1, gffn_ref[...]).astype(BF16)
    _cast_blocks(refs[:n_cast], refs[n_cast + 2:])


def _merge(ya, yg, gates, x2, wglu, bglu, wa, ws, wo, gffn, cast_weights):
    t = x2.shape[0]
    tm = MERGE_TM
    casts = [_RowCast(w, wl, t // tm, lambda i: i) for w, wl in cast_weights]

    def wspec(rows, cols):
        return pl.BlockSpec((rows, cols), lambda i: (0, 0), pipeline_mode=pl.Buffered(1))

    outs = pl.pallas_call(
        _merge_kernel,
        grid=(t // tm,),
        in_specs=[
            pl.BlockSpec((tm, ATTN_WIDTH), lambda i: (i, 0)),
            pl.BlockSpec((tm, SSM_WIDTH), lambda i: (i, 0)),
            pl.BlockSpec((tm, 2 * D_MODEL), lambda i: (i, 0)),
            pl.BlockSpec((tm, D_MODEL), lambda i: (i, 0)),
            wspec(SSM_WIDTH, SSM_WIDTH),
            pl.BlockSpec((1, SSM_WIDTH), lambda i: (0, 0)),
            wspec(ATTN_WIDTH, D_MODEL),
            wspec(SSM_WIDTH, D_MODEL),
            wspec(D_MODEL, D_MODEL),
            pl.BlockSpec((1, D_MODEL), lambda i: (0, 0)),
        ] + [c.in_spec for c in casts],
        out_specs=[
            pl.BlockSpec((tm, D_MODEL), lambda i: (i, 0)),
            pl.BlockSpec((tm, D_MODEL), lambda i: (i, 0)),
        ] + [c.out_spec for c in casts],
        out_shape=[
            jax.ShapeDtypeStruct((t, D_MODEL), F32),
            jax.ShapeDtypeStruct((t, D_MODEL), BF16),
        ] + [c.out_shape for c in casts],
        compiler_params=pltpu.CompilerParams(
            dimension_semantics=("arbitrary",), vmem_limit_bytes=VMEM_LIMIT),
        name="merge_out",
    )(ya, yg, gates, x2, wglu, bglu, wa, ws, wo, gffn, *[c.operand for c in casts])
    return outs[:2], outs[2:]


def _ffn_kernel(h_ref, x_ref, wg_ref, wu_ref, wo_ref, o_ref):
    k = pl.program_id(1)

    def accumulate(base_ref):
        for r in range(FFN_TM // FFN_SUB):
            rs = pl.ds(r * FFN_SUB, FFN_SUB)
            h = h_ref[rs, :]
            g = jnp.dot(h, wg_ref[...], preferred_element_type=F32)
            u = jnp.dot(h, wu_ref[...], preferred_element_type=F32)
            act = (g * _sigmoid(g) * u).astype(BF16)
            o_ref[rs, :] = base_ref[rs, :] + jnp.dot(act, wo_ref[...], preferred_element_type=F32)

    @pl.when(k == 0)
    def _():
        accumulate(x_ref)

    @pl.when(k > 0)
    def _():
        accumulate(o_ref)


def _ffn(h2, x1, w_in, w_out):
    t = x1.shape[0]
    nk = D_FF // FFN_TF
    return pl.pallas_call(
        _ffn_kernel,
        grid=(t // FFN_TM, nk),
        in_specs=[
            pl.BlockSpec((FFN_TM, D_MODEL), lambda i, k: (i, 0)),
            pl.BlockSpec((FFN_TM, D_MODEL), lambda i, k: (i, 0)),
            pl.BlockSpec((D_MODEL, FFN_TF), lambda i, k: (0, k)),
            pl.BlockSpec((D_MODEL, FFN_TF), lambda i, k: (0, nk + k)),
            pl.BlockSpec((FFN_TF, D_MODEL), lambda i, k: (k, 0)),
        ],
        out_specs=pl.BlockSpec((FFN_TM, D_MODEL), lambda i, k: (i, 0)),
        out_shape=jax.ShapeDtypeStruct((t, D_MODEL), F32),
        compiler_params=pltpu.CompilerParams(
            dimension_semantics=("arbitrary", "arbitrary"), vmem_limit_bytes=FFN_VMEM_LIMIT),
        name="swiglu_ffn",
    )(h2, x1, w_in, w_in, w_out)


def kernel(x, norm_mix_g, w_in, gate_bias, q_norm_g, k_norm_g, attn_sinks, ssm_lambda_re, ssm_lambda_im, ssm_log_dt, ssm_b_re, ssm_b_im, ssm_c_re, ssm_c_im, ssm_d, ssm_glu_w, ssm_glu_b, w_attn_branch, w_ssm_branch, w_out, norm_ffn_g, w_ffn_in, w_ffn_out):
    batch, seq, _ = x.shape
    t = batch * seq
    n_levels = (seq // SUB).bit_length() - 1
    x2 = x.reshape(t, D_MODEL).astype(F32)
    f32 = lambda w: w.astype(F32)
    seg = _segment_ones()
    bias_tab = _attn_bias_tables()
    perm_in, perm_out = _slab_permutations()
    w_in_f = f32(w_in)
    (tb, cmat, lev_r, lev_i), (w_in_bf,) = _ssm_prep(
        ssm_lambda_re, ssm_lambda_im, ssm_log_dt, ssm_b_re, ssm_b_im, ssm_c_re, ssm_c_im, ssm_d,
        n_levels, [(w_in_f, 0)])
    merge_weights = [f32(ssm_glu_w), f32(w_attn_branch), f32(w_ssm_branch), f32(w_out)]
    ffn_weights = [f32(w_ffn_in), f32(w_ffn_out)]
    for l in range(DEPTH):
        next_w_in = [(w_in_f, l + 1)] if l + 1 < DEPTH else []
        (uf, gates, ya), cast_out = _inproj_attn(
            x2, norm_mix_g[l].reshape(1, D_MODEL).astype(F32), w_in_bf,
            gate_bias[l].reshape(1, 2 * D_MODEL).astype(F32), seg,
            _qk_norm_tables(q_norm_g[l], k_norm_g[l]), attn_sinks[l].astype(F32), bias_tab, seq,
            [(w, l) for w in merge_weights] + next_w_in)
        wglu_bf, wa_bf, ws_bf, wo_bf = cast_out[:4]
        w_in_bf = cast_out[4] if next_w_in else None
        yg = _ssm(uf, perm_in, perm_out, tb, cmat, lev_r, lev_i, l, batch, seq)
        (x1, h2), (wfi_bf, wfo_bf) = _merge(
            ya, yg, gates, x2, wglu_bf, ssm_glu_b[l].reshape(1, SSM_WIDTH).astype(F32),
            wa_bf, ws_bf, wo_bf, norm_ffn_g[l].reshape(1, D_MODEL).astype(F32),
            [(w, l) for w in ffn_weights])
        x2 = _ffn(h2, x1, wfi_bf, wfo_bf)
    return x2.reshape(batch, seq, D_MODEL).astype(x.dtype)
```

```python
import jax
import jax.numpy as jnp
from jax import lax
from jax.experimental import pallas as pl
from jax.experimental.pallas import tpu as pltpu

D_MODEL = 2048
DEPTH = 2
HEAD_DIM = 64
N_Q_HEADS = 16
N_KV_HEADS = 4
GQA_GROUP = N_Q_HEADS // N_KV_HEADS
ATTN_WIDTH = N_Q_HEADS * HEAD_DIM
KV_WIDTH = N_KV_HEADS * HEAD_DIM
WINDOW = 128
BLOCK = 128
SSM_WIDTH = D_MODEL // 2
SSM_GROUP_CH = 16
SSM_GROUPS = SSM_WIDTH // SSM_GROUP_CH
SSM_STATE = 64
D_FF = -(-8 * D_MODEL // (3 * 256)) * 256
OFF_K = ATTN_WIDTH
OFF_V = OFF_K + KV_WIDTH
OFF_U = OFF_V + KV_WIDTH
OFF_G = OFF_U + SSM_WIDTH
IN_WIDTH = OFF_G + 2 * D_MODEL
RMS_EPS = 1e-6

F32 = jnp.float32
BF16 = jnp.bfloat16
HIGHEST = lax.Precision.HIGHEST

LANES = 128
BF16_SUBLANES = 16
SUB = 16
SUB_W = SUB * SSM_GROUP_CH
STATE_W = 2 * SSM_STATE
TB_W = SUB_W + 2 * STATE_W
SLAB_GROUPS = LANES // SSM_GROUP_CH
N_SLABS = SSM_GROUPS // SLAB_GROUPS
SLAB_W = SLAB_GROUPS * SUB_W

VMEM_LIMIT = 56 * 1024 * 1024
FFN_VMEM_LIMIT = 60 * 1024 * 1024

INPROJ_TM = 256
INPROJ_TN = 512
MERGE_TM = 256
FFN_TM = 1024
FFN_TF = 512
FFN_SUB = 512


def _rms(x, g):
    return x * lax.rsqrt(jnp.mean(x * x, axis=-1, keepdims=True) + RMS_EPS) * g


def _sigmoid(x):
    return 0.5 * jnp.tanh(0.5 * x) + 0.5


class _RowCast:
    def __init__(self, stacked, layer, n_steps, step_of):
        _, rows, cols = stacked.shape
        blk = rows // n_steps
        assert blk * n_steps == rows and blk % BF16_SUBLANES == 0, (rows, n_steps)
        self.operand = stacked
        self.in_spec = pl.BlockSpec((None, blk, cols), lambda *ids: (layer, step_of(*ids), 0))
        self.out_spec = pl.BlockSpec((blk, cols), lambda *ids: (step_of(*ids), 0))
        self.out_shape = jax.ShapeDtypeStruct((rows, cols), BF16)


def _cast_blocks(src_refs, dst_refs):
    for src, dst in zip(src_refs, dst_refs):
        dst[...] = src[...].astype(BF16)


N_QKV_BLOCKS = OFF_U // INPROJ_TN
N_MAIN_BLOCKS = OFF_G // INPROJ_TN
N_IN_BLOCKS = IN_WIDTH // INPROJ_TN


QBLOCKS_PER_TILE = INPROJ_TM // BLOCK
ATTN_PROBLEMS_PER_GAP = 2


def _inproj_attn_kernel(sink_ref, x_ref, g_ref, w_ref, b_ref, seg_ref, nt_ref, bias0_ref, bias_ref, *refs):
    n_cast = (len(refs) - 5) // 2
    u_ref, gt_ref, ya_ref = refs[n_cast:n_cast + 3]
    q_ref, kv_ref = refs[2 * n_cast + 3:]

    @pl.when(pl.program_id(0) == 0)
    def _():
        kv_ref[...] = jnp.zeros(kv_ref.shape, BF16)

    kv_ref[0:BLOCK, :] = kv_ref[INPROJ_TM:INPROJ_TM + BLOCK, :]
    h = _rms(x_ref[...], g_ref[...]).astype(BF16)

    def project(j):
        return jnp.dot(h, w_ref[:, j * INPROJ_TN:(j + 1) * INPROJ_TN], preferred_element_type=F32)

    for j in range(N_QKV_BLOCKS):
        z = project(j)
        ssq = jnp.dot((z * z).astype(BF16), seg_ref[...], preferred_element_type=F32)
        inv = lax.rsqrt(ssq * (1.0 / HEAD_DIM) + RMS_EPS)
        fac = jnp.where(nt_ref[j, 1:2, :] > 0.0, inv, 1.0) * nt_ref[j, 0:1, :]
        if (j + 1) * INPROJ_TN <= ATTN_WIDTH:
            q_ref[:, j * INPROJ_TN:(j + 1) * INPROJ_TN] = (z * fac).astype(BF16)
        else:
            kv_ref[BLOCK:, :] = (z * fac).astype(BF16)

    def other_block(j):
        z = project(j)
        if j < N_MAIN_BLOCKS:
            c = j - N_QKV_BLOCKS
            u_ref[:, c * INPROJ_TN:(c + 1) * INPROJ_TN] = z
        else:
            cs = slice((j - N_MAIN_BLOCKS) * INPROJ_TN, (j - N_MAIN_BLOCKS + 1) * INPROJ_TN)
            gt_ref[:, cs] = _sigmoid(z + b_ref[:, cs]).astype(BF16)

    kv_rows = 2 * BLOCK
    all_rows = kv_ref.shape[0]
    left_kv = lax.broadcasted_iota(jnp.int32, (all_rows, LANES), 1) < HEAD_DIM
    left_q = lax.broadcasted_iota(jnp.int32, (BLOCK, LANES), 1) < HEAD_DIM
    zeros = jnp.zeros((all_rows, LANES), BF16)
    left_ones = lax.broadcasted_iota(jnp.int32, (kv_rows, LANES), 1) < HEAD_DIM
    ones_l = jnp.where(left_ones, 1.0, 0.0).astype(BF16)
    ones_r = jnp.where(left_ones, 0.0, 1.0).astype(BF16)
    contract_lanes = (((1,), (1,)), ((), ()))

    def slab(off):
        a = kv_ref[:, off:off + LANES]
        return a, pltpu.roll(a.astype(F32), HEAD_DIM, axis=1).astype(BF16)

    problems = []
    for c in range(N_KV_HEADS // 2):
        k_slabs = slab(c * LANES)
        v_slabs = slab(KV_WIDTH + c * LANES)
        for side in range(2):
            kh = 2 * c + side
            k_l = jnp.where(left_kv, k_slabs[side], zeros)
            k_r = jnp.where(left_kv, zeros, k_slabs[1 - side])
            v_l = jnp.where(left_kv, v_slabs[side], zeros)
            v_r = jnp.where(left_kv, zeros, v_slabs[1 - side])
            for qb in range(QBLOCKS_PER_TILE):
                rows = slice(qb * BLOCK, qb * BLOCK + kv_rows)
                kk = jnp.concatenate([k_l[rows], k_r[rows]], axis=0)
                vv = jnp.concatenate([jnp.concatenate([v_l[rows], ones_l], axis=1),
                                      jnp.concatenate([v_r[rows], ones_r], axis=1)], axis=0)
                for pair in range(GQA_GROUP // 2):
                    problems.append((qb, kh * GQA_GROUP + 2 * pair, kk, vv))

    def scores(qb, e, kk):
        q2 = q_ref[qb * BLOCK:(qb + 1) * BLOCK, e * HEAD_DIM:(e + 2) * HEAD_DIM]
        return lax.dot_general(q2, kk, contract_lanes, preferred_element_type=F32)

    def finish(qb, e, s2, vv):
        bias = bias0_ref if qb == 0 else bias_ref
        ps, ds = [], []
        for t in range(2):
            s = s2[:, t * kv_rows:(t + 1) * kv_rows] + bias[e + t]
            sink = sink_ref[e + t]
            m = jnp.maximum(jnp.max(s, axis=-1, keepdims=True), sink)
            ps.append(jnp.exp(s - m).astype(BF16))
            ds.append(jnp.exp(sink - m))
        r = jnp.dot(jnp.concatenate(ps, axis=1), vv, preferred_element_type=F32)
        denom = r[:, LANES:] + jnp.where(left_q, ds[0], ds[1])
        ya_ref[qb * BLOCK:(qb + 1) * BLOCK, e * HEAD_DIM:(e + 2) * HEAD_DIM] = (r[:, :LANES] / denom).astype(BF16)

    others = list(range(N_QKV_BLOCKS, N_IN_BLOCKS))
    for first in range(0, len(problems), ATTN_PROBLEMS_PER_GAP):
        group = problems[first:first + ATTN_PROBLEMS_PER_GAP]
        s2s = [scores(qb, e, kk) for qb, e, kk, _ in group]
        if others:
            other_block(others.pop(0))
        for (qb, e, _, vv), s2 in zip(group, s2s):
            finish(qb, e, s2, vv)
    for j in others:
        other_block(j)
    _cast_blocks(refs[:n_cast], refs[n_cast + 3:2 * n_cast + 3])


def _inproj_attn(x2, gain, w_bf, bias, seg, ntab, sinks, bias_tab, seq, cast_weights):
    t = x2.shape[0]
    tm = INPROJ_TM
    tiles_per_seq = seq // tm
    const2 = lambda i: (0, 0)
    bias_shape = (None, N_Q_HEADS, BLOCK, 2 * BLOCK)
    casts = [_RowCast(w, wl, t // tm, lambda i: i) for w, wl in cast_weights]
    outs = pl.pallas_call(
        _inproj_attn_kernel,
        grid=(t // tm,),
        in_specs=[
            pl.BlockSpec(memory_space=pltpu.SMEM),
            pl.BlockSpec((tm, D_MODEL), lambda i: (i, 0)),
            pl.BlockSpec((1, D_MODEL), const2),
            pl.BlockSpec((D_MODEL, IN_WIDTH), const2, pipeline_mode=pl.Buffered(1)),
            pl.BlockSpec((1, 2 * D_MODEL), const2),
            pl.BlockSpec((INPROJ_TN, INPROJ_TN), const2),
            pl.BlockSpec((N_QKV_BLOCKS, 2, INPROJ_TN), lambda i: (0, 0, 0)),
            pl.BlockSpec(bias_shape, lambda i: (jnp.minimum(i % tiles_per_seq, 1), 0, 0, 0)),
            pl.BlockSpec(bias_shape, lambda i: (1, 0, 0, 0), pipeline_mode=pl.Buffered(1)),
        ] + [c.in_spec for c in casts],
        out_specs=[
            pl.BlockSpec((tm, SSM_WIDTH), lambda i: (i, 0)),
            pl.BlockSpec((tm, 2 * D_MODEL), lambda i: (i, 0)),
            pl.BlockSpec((tm, ATTN_WIDTH), lambda i: (i, 0)),
        ] + [c.out_spec for c in casts],
        out_shape=[
            jax.ShapeDtypeStruct((t, SSM_WIDTH), F32),
            jax.ShapeDtypeStruct((t, 2 * D_MODEL), BF16),
            jax.ShapeDtypeStruct((t, ATTN_WIDTH), BF16),
        ] + [c.out_shape for c in casts],
        scratch_shapes=[pltpu.VMEM((tm, ATTN_WIDTH), BF16), pltpu.VMEM((BLOCK + tm, 2 * KV_WIDTH), BF16)],
        compiler_params=pltpu.CompilerParams(
            dimension_semantics=("arbitrary",), vmem_limit_bytes=VMEM_LIMIT),
        name="inproj_attn",
    )(sinks, x2, gain, w_bf, bias, seg, ntab, bias_tab, bias_tab, *[c.operand for c in casts])
    return outs[:3], outs[3:]


def _qk_norm_tables(q_gain, k_gain):
    qrow = jnp.tile(q_gain.astype(F32), INPROJ_TN // HEAD_DIM) * (HEAD_DIM ** -0.5)
    ones = jnp.ones((INPROJ_TN,), F32)
    kvrow = jnp.concatenate([jnp.tile(k_gain.astype(F32), N_KV_HEADS), jnp.ones((KV_WIDTH,), F32)])
    kvmask = jnp.concatenate([jnp.ones((KV_WIDTH,), F32), jnp.zeros((KV_WIDTH,), F32)])
    return jnp.stack([jnp.stack([qrow, ones]), jnp.stack([qrow, ones]), jnp.stack([kvrow, kvmask])])


def _segment_ones():
    r = jnp.arange(INPROJ_TN) // HEAD_DIM
    return (r[:, None] == r[None, :]).astype(BF16)


def _attn_bias_tables():
    t_loc = jnp.arange(BLOCK)[:, None]
    s_loc = jnp.arange(2 * BLOCK)[None, :] - BLOCK
    dist = (t_loc - s_loc).astype(F32)
    valid = (dist >= 0) & (dist < WINDOW)
    slopes = jnp.exp2(-8.0 * jnp.arange(1, N_Q_HEADS + 1, dtype=F32) / N_Q_HEADS)
    bias = -slopes[:, None, None] * dist[None]
    full = jnp.where(valid[None], bias, -jnp.inf)
    first = jnp.where((valid & (s_loc >= 0))[None], bias, -jnp.inf)
    return jnp.stack([first, full])


def _cmul(ar, ai, br, bi):
    return ar * br - ai * bi, ar * bi + ai * br


def _pow_by_bits(exps, squares):
    pr = pi = None
    for b, (sr, si) in enumerate(squares):
        on = ((exps >> b) & 1) == 1
        fr, fi = jnp.where(on, sr, 1.0), jnp.where(on, si, 0.0)
        pr, pi = (fr, fi) if pr is None else _cmul(pr, pi, fr, fi)
    return pr, pi


def _ssm_prep_kernel(lam_ref, ldt_ref, btr_ref, bti_ref, ctr_ref, cti_ref, d_ref, *refs):
    n_cast = (len(refs) - 4) // 2
    tb_ref, c_ref, levr_ref, levi_ref = refs[n_cast:n_cast + 4]
    _cast_blocks(refs[:n_cast], refs[n_cast + 4:])
    n_levels = levr_ref.shape[1]
    tau_lane = lax.broadcasted_iota(jnp.int32, (SSM_STATE, SUB_W), 1) // SSM_GROUP_CH
    row_h = lax.broadcasted_iota(jnp.int32, (SSM_GROUP_CH, SUB_W), 0)
    lane_h = lax.broadcasted_iota(jnp.int32, (SSM_GROUP_CH, SUB_W), 1)
    tile_ch = (lane_h % SSM_GROUP_CH == row_h).astype(F32)
    row_m = lax.broadcasted_iota(jnp.int32, (SUB, SSM_STATE), 0)
    eye_p = (lax.broadcasted_iota(jnp.int32, (SSM_STATE, SSM_STATE), 0)
             == lax.broadcasted_iota(jnp.int32, (SSM_STATE, SSM_STATE), 1))

    def to_col(v):
        return jnp.sum(jnp.where(eye_p, v, 0.0), axis=1, keepdims=True)

    def squares(a, n):
        out = [a]
        for _ in range(n - 1):
            out.append(_cmul(*out[-1], *out[-1]))
        return out

    for g in range(SLAB_GROUPS):
        dt = jnp.exp(ldt_ref[g])

        def discretise(lr, li):
            mag = jnp.exp(lr * dt)
            return mag * jnp.cos(li * dt), mag * jnp.sin(li * dt)

        lr_row, li_row = lam_ref[g, 0:1, :], lam_ref[g, 1:2, :]
        a_row = discretise(lr_row, li_row)
        sq_row = squares(a_row, 5)
        sq_col = squares((to_col(a_row[0]), to_col(a_row[1])), 4)
        ar, ai = sq_row[0]
        den = lr_row * lr_row + li_row * li_row
        fr = ((ar - 1.0) * lr_row + ai * li_row) / den
        fi = (ai * lr_row - (ar - 1.0) * li_row) / den
        bbr, bbi = _cmul(fr, fi, btr_ref[g], bti_ref[g])

        e0 = _pow_by_bits(tau_lane, sq_col)
        e1 = _cmul(*e0, *sq_col[0])
        ctr = jnp.dot(ctr_ref[g], tile_ch, precision=HIGHEST, preferred_element_type=F32)
        cti = jnp.dot(cti_ref[g], tile_ch, precision=HIGHEST, preferred_element_type=F32)
        mr, mi = _cmul(*e0, ctr, cti)
        kt = (jnp.dot(bbr, mr, precision=HIGHEST, preferred_element_type=F32)
              - jnp.dot(bbi, mi, precision=HIGHEST, preferred_element_type=F32))
        kt = kt + jnp.where(lane_h == row_h, d_ref[g], 0.0)
        pw = _pow_by_bits(row_m, sq_row[:4])
        for j in range(SUB):
            rows = pl.ds(j * SSM_GROUP_CH, SSM_GROUP_CH)
            tj = kt if j == 0 else jnp.where(lane_h >= j * SSM_GROUP_CH,
                                             pltpu.roll(kt, j * SSM_GROUP_CH, axis=1), 0.0)
            tb_ref[g, rows, 0:SUB_W] = tj.astype(BF16)
            m = SUB - 1 - j
            br_, bi_ = _cmul(pw[0][m:m + 1, :], pw[1][m:m + 1, :], bbr, bbi)
            tb_ref[g, rows, SUB_W:SUB_W + STATE_W] = jnp.concatenate([br_, bi_], axis=1).astype(BF16)
            tb_ref[g, rows, SUB_W + STATE_W:TB_W] = jnp.concatenate([bi_, br_], axis=1).astype(BF16)
        c_ref[g, 0:SSM_STATE, :] = (ctr * e1[0] - cti * e1[1]).astype(BF16)
        c_ref[g, SSM_STATE:STATE_W, :] = (-ctr * e1[1] - cti * e1[0]).astype(BF16)
        lv = sq_row[4]
        lev_r, lev_i = [], []
        for _ in range(n_levels):
            lev_r.append(jnp.concatenate([lv[0], lv[0]], axis=1))
            lev_i.append(jnp.concatenate([-lv[1], lv[1]], axis=1))
            lv = _cmul(*lv, *lv)
        levr_ref[g] = jnp.concatenate(lev_r, axis=0)
        levi_ref[g] = jnp.concatenate(lev_i, axis=0)


def _ssm_prep(lam_re, lam_im, log_dt, b_re, b_im, c_re, c_im, d_skip, n_levels, cast_weights):
    dg = lam_re.shape[0] * SSM_GROUPS
    casts = [_RowCast(w, wl, dg // SLAB_GROUPS, lambda s: s) for w, wl in cast_weights]
    h_, p_ = SSM_GROUP_CH, SSM_STATE
    f = lambda a: a.astype(F32)
    lam_rows = jnp.stack([f(lam_re), f(lam_im)], axis=2).reshape(dg, 2, p_)
    btr = f(b_re).transpose(0, 1, 3, 2).reshape(dg, h_, p_)
    bti = f(b_im).transpose(0, 1, 3, 2).reshape(dg, h_, p_)
    ctr = f(c_re).transpose(0, 1, 3, 2).reshape(dg, p_, h_)
    cti = f(c_im).transpose(0, 1, 3, 2).reshape(dg, p_, h_)
    d_rows = jnp.pad(f(d_skip).reshape(dg, 1, h_), ((0, 0), (0, 0), (0, SUB_W - h_)))
    grp = lambda *shape: pl.BlockSpec((SLAB_GROUPS,) + shape, lambda s: (s,) + (0,) * len(shape))
    outs = pl.pallas_call(
        _ssm_prep_kernel,
        grid=(dg // SLAB_GROUPS,),
        in_specs=[grp(2, p_), grp(1, 1), grp(h_, p_), grp(h_, p_),
                  grp(p_, h_), grp(p_, h_), grp(1, SUB_W)] + [c.in_spec for c in casts],
        out_specs=[grp(SUB_W, TB_W), grp(STATE_W, SUB_W), grp(n_levels, STATE_W),
                   grp(n_levels, STATE_W)] + [c.out_spec for c in casts],
        out_shape=[
            jax.ShapeDtypeStruct((dg, SUB_W, TB_W), BF16),
            jax.ShapeDtypeStruct((dg, STATE_W, SUB_W), BF16),
            jax.ShapeDtypeStruct((dg, n_levels, STATE_W), F32),
            jax.ShapeDtypeStruct((dg, n_levels, STATE_W), F32),
        ] + [c.out_shape for c in casts],
        compiler_params=pltpu.CompilerParams(
            dimension_semantics=("arbitrary",), vmem_limit_bytes=VMEM_LIMIT),
        name="s5_prep",
    )(lam_rows, f(log_dt).reshape(dg, 1, 1), btr, bti, ctr, cti, d_rows, *[c.operand for c in casts])
    return outs[:4], outs[4:]


def _slab_permutations():
    r = jnp.arange(SUB * LANES)
    j, lane = r // LANES, r % LANES
    grp, ch = lane // SSM_GROUP_CH, lane % SSM_GROUP_CH
    col = grp * SUB_W + j * SSM_GROUP_CH + ch
    full = (col[:, None] == jnp.arange(SLAB_W)[None, :]).astype(BF16)
    split = lambda m, n: m.reshape(m.shape[0], n, m.shape[1] // n).transpose(1, 0, 2)
    return split(full, SLAB_GROUPS), split(full.T, SUB // 2)


S5_STAGES = 3


def _ssm_kernel(u_ref, pin_ref, pout_ref, tb_ref, c_ref, ar_ref, ai_ref, y_ref, xs_ref, ys_ref):
    t = pl.program_id(0)
    rows = xs_ref.shape[1]
    n_levels = rows.bit_length() - 1
    cur, prv = t % 2, (t + 1) % 2

    @pl.when(t == 0)
    def _():
        xs_ref[...] = jnp.zeros(xs_ref.shape, BF16)
        ys_ref[...] = jnp.zeros(ys_ref.shape, BF16)

    def shifted(a, sh):
        return jnp.concatenate([jnp.zeros((sh, a.shape[1]), a.dtype), a[:rows - sh]], axis=0)

    ys_done = ys_ref[cur]
    u_all = jnp.concatenate([u_ref[pl.ds(j, rows, stride=SUB), :] for j in range(SUB)],
                            axis=1).astype(BF16)
    def scan_inputs(g):
        return jnp.dot(xs_ref[prv, :, g * SUB_W:(g + 1) * SUB_W], tb_ref[g], preferred_element_type=F32)

    r_next = scan_inputs(0)
    for g in range(SLAB_GROUPS):
        r = r_next
        if g + 1 < SLAB_GROUPS:
            r_next = scan_inputs(g + 1)
        o = jnp.dot(ys_done, pout_ref[g], preferred_element_type=F32)
        y_ref[pl.ds(2 * g, rows, stride=SUB), :] = o[:, :LANES]
        y_ref[pl.ds(2 * g + 1, rows, stride=SUB), :] = o[:, LANES:]
        xs_ref[cur, :, g * SUB_W:(g + 1) * SUB_W] = jnp.dot(
            u_all, pin_ref[g], preferred_element_type=F32).astype(BF16)
        yt = r[:, :SUB_W]
        s = r[:, SUB_W:SUB_W + STATE_W]
        w = r[:, SUB_W + STATE_W:]
        for k in range(n_levels):
            sh = 1 << k
            ar = ar_ref[g, k:k + 1, :]
            ai = ai_ref[g, k:k + 1, :]
            ps, pw = shifted(s, sh), shifted(w, sh)
            s, w = s + ar * ps + ai * pw, (w + ar * pw - ai * ps if k + 1 < n_levels else None)
        y = yt + jnp.dot(shifted(s, 1).astype(BF16), c_ref[g], preferred_element_type=F32)
        ys_ref[prv, :, g * SUB_W:(g + 1) * SUB_W] = jax.nn.gelu(y).astype(BF16)


def _ssm(uf, perm_in, perm_out, tb, cmat, lev_r, lev_i, layer, batch, seq):
    t = uf.shape[0]
    n_sub = seq // SUB
    n_levels = lev_r.shape[1]
    n_items = batch * N_SLABS
    item = lambda t, stage: jnp.clip(t - stage, 0, n_items - 1)
    const3 = lambda t: (0, 0, 0)
    slab3 = lambda t: (layer * N_SLABS + item(t, 1) % N_SLABS, 0, 0)
    return pl.pallas_call(
        _ssm_kernel,
        grid=(n_items + S5_STAGES - 1,),
        in_specs=[
            pl.BlockSpec((seq, LANES), lambda t: (item(t, 0) // N_SLABS, item(t, 0) % N_SLABS)),
            pl.BlockSpec((SLAB_GROUPS, SUB * LANES, SUB_W), const3, pipeline_mode=pl.Buffered(1)),
            pl.BlockSpec((SUB // 2, SLAB_W, 2 * LANES), const3, pipeline_mode=pl.Buffered(1)),
            pl.BlockSpec((SLAB_GROUPS, SUB_W, TB_W), slab3),
            pl.BlockSpec((SLAB_GROUPS, STATE_W, SUB_W), slab3),
            pl.BlockSpec((SLAB_GROUPS, n_levels, STATE_W), slab3),
            pl.BlockSpec((SLAB_GROUPS, n_levels, STATE_W), slab3),
        ],
        out_specs=pl.BlockSpec((seq, LANES), lambda t: (item(t, 2) // N_SLABS, item(t, 2) % N_SLABS)),
        out_shape=jax.ShapeDtypeStruct((t, SSM_WIDTH), F32),
        scratch_shapes=[pltpu.VMEM((2, n_sub, SLAB_W), BF16), pltpu.VMEM((2, n_sub, SLAB_W), BF16)],
        compiler_params=pltpu.CompilerParams(
            dimension_semantics=("arbitrary",), vmem_limit_bytes=VMEM_LIMIT),
        name="s5_scan",
    )(uf, perm_in, perm_out, tb, cmat, lev_r, lev_i)


def _merge_kernel(ya_ref, yg_ref, gt_ref, x_ref, wglu_ref, bglu_ref, wa_ref, ws_ref, wo_ref,
                  gffn_ref, *refs):
    n_cast = (len(refs) - 2) // 2
    x1_ref, h2_ref = refs[n_cast:n_cast + 2]
    yg = yg_ref[...]
    t = jnp.dot(yg.astype(BF16), wglu_ref[...], preferred_element_type=F32) + bglu_ref[...]
    ys = (yg * _sigmoid(t)).astype(BF16)
    ma = jnp.dot(ya_ref[...], wa_ref[...], preferred_element_type=F32)
    ms = jnp.dot(ys, ws_ref[...], preferred_element_type=F32)
    merged = gt_ref[:, :D_MODEL].astype(F32) * ma + gt_ref[:, D_MODEL:].astype(F32) * ms
    x1 = x_ref[...] + jnp.dot(merged.astype(BF16), wo_ref[...], preferred_element_type=F32)
    x1_ref[...] = x1
    h2_ref[...] = _rms(x1, gffn_ref[...]).astype(BF16)
    _cast_blocks(refs[:n_cast], refs[n_cast + 2:])


def _merge(ya, yg, gates, x2, wglu, bglu, wa, ws, wo, gffn, cast_weights):
    t = x2.shape[0]
    tm = MERGE_TM
    casts = [_RowCast(w, wl, t // tm, lambda i: i) for w, wl in cast_weights]

    def wspec(rows, cols):
        return pl.BlockSpec((rows, cols), lambda i: (0, 0), pipeline_mode=pl.Buffered(1))

    outs = pl.pallas_call(
        _merge_kernel,
        grid=(t // tm,),
        in_specs=[
            pl.BlockSpec((tm, ATTN_WIDTH), lambda i: (i, 0)),
            pl.BlockSpec((tm, SSM_WIDTH), lambda i: (i, 0)),
            pl.BlockSpec((tm, 2 * D_MODEL), lambda i: (i, 0)),
            pl.BlockSpec((tm, D_MODEL), lambda i: (i, 0)),
            wspec(SSM_WIDTH, SSM_WIDTH),
            pl.BlockSpec((1, SSM_WIDTH), lambda i: (0, 0)),
            wspec(ATTN_WIDTH, D_MODEL),
            wspec(SSM_WIDTH, D_MODEL),
            wspec(D_MODEL, D_MODEL),
            pl.BlockSpec((1, D_MODEL), lambda i: (0, 0)),
        ] + [c.in_spec for c in casts],
        out_specs=[
            pl.BlockSpec((tm, D_MODEL), lambda i: (i, 0)),
            pl.BlockSpec((tm, D_MODEL), lambda i: (i, 0)),
        ] + [c.out_spec for c in casts],
        out_shape=[
            jax.ShapeDtypeStruct((t, D_MODEL), F32),
            jax.ShapeDtypeStruct((t, D_MODEL), BF16),
        ] + [c.out_shape for c in casts],
        compiler_params=pltpu.CompilerParams(
            dimension_semantics=("arbitrary",), vmem_limit_bytes=VMEM_LIMIT),
        name="merge_out",
    )(ya, yg, gates, x2, wglu, bglu, wa, ws, wo, gffn, *[c.operand for c in casts])
    return outs[:2], outs[2:]


def _ffn_kernel(h_ref, x_ref, wg_ref, wu_ref, wo_ref, o_ref):
    k = pl.program_id(1)

    def accumulate(base_ref):
        for r in range(FFN_TM // FFN_SUB):
            rs = pl.ds(r * FFN_SUB, FFN_SUB)
            h = h_ref[rs, :]
            g = jnp.dot(h, wg_ref[...], preferred_element_type=F32)
            u = jnp.dot(h, wu_ref[...], preferred_element_type=F32)
            act = (g * _sigmoid(g) * u).astype(BF16)
            o_ref[rs, :] = base_ref[rs, :] + jnp.dot(act, wo_ref[...], preferred_element_type=F32)

    @pl.when(k == 0)
    def _():
        accumulate(x_ref)

    @pl.when(k > 0)
    def _():
        accumulate(o_ref)


def _ffn(h2, x1, w_in, w_out):
    t = x1.shape[0]
    nk = D_FF // FFN_TF
    return pl.pallas_call(
        _ffn_kernel,
        grid=(t // FFN_TM, nk),
        in_specs=[
            pl.BlockSpec((FFN_TM, D_MODEL), lambda i, k: (i, 0)),
            pl.BlockSpec((FFN_TM, D_MODEL), lambda i, k: (i, 0)),
            pl.BlockSpec((D_MODEL, FFN_TF), lambda i, k: (0, k)),
            pl.BlockSpec((D_MODEL, FFN_TF), lambda i, k: (0, nk + k)),
            pl.BlockSpec((FFN_TF, D_MODEL), lambda i, k: (k, 0)),
        ],
        out_specs=pl.BlockSpec((FFN_TM, D_MODEL), lambda i, k: (i, 0)),
        out_shape=jax.ShapeDtypeStruct((t, D_MODEL), F32),
        compiler_params=pltpu.CompilerParams(
            dimension_semantics=("arbitrary", "arbitrary"), vmem_limit_bytes=FFN_VMEM_LIMIT),
        name="swiglu_ffn",
    )(h2, x1, w_in, w_in, w_out)


def kernel(x, norm_mix_g, w_in, gate_bias, q_norm_g, k_norm_g, attn_sinks, ssm_lambda_re, ssm_lambda_im, ssm_log_dt, ssm_b_re, ssm_b_im, ssm_c_re, ssm_c_im, ssm_d, ssm_glu_w, ssm_glu_b, w_attn_branch, w_ssm_branch, w_out, norm_ffn_g, w_ffn_in, w_ffn_out):
    batch, seq, _ = x.shape
    t = batch * seq
    n_levels = (seq // SUB).bit_length() - 1
    x2 = x.reshape(t, D_MODEL).astype(F32)
    f32 = lambda w: w.astype(F32)
    seg = _segment_ones()
    bias_tab = _attn_bias_tables()
    perm_in, perm_out = _slab_permutations()
    w_in_f = f32(w_in)
    (tb, cmat, lev_r, lev_i), (w_in_bf,) = _ssm_prep(
        ssm_lambda_re, ssm_lambda_im, ssm_log_dt, ssm_b_re, ssm_b_im, ssm_c_re, ssm_c_im, ssm_d,
        n_levels, [(w_in_f, 0)])
    merge_weights = [f32(ssm_glu_w), f32(w_attn_branch), f32(w_ssm_branch), f32(w_out)]
    ffn_weights = [f32(w_ffn_in), f32(w_ffn_out)]
    for l in range(DEPTH):
        next_w_in = [(w_in_f, l + 1)] if l + 1 < DEPTH else []
        (uf, gates, ya), cast_out = _inproj_attn(
            x2, norm_mix_g[l].reshape(1, D_MODEL).astype(F32), w_in_bf,
            gate_bias[l].reshape(1, 2 * D_MODEL).astype(F32), seg,
            _qk_norm_tables(q_norm_g[l], k_norm_g[l]), attn_sinks[l].astype(F32), bias_tab, seq,
            [(w, l) for w in merge_weights] + next_w_in)
        wglu_bf, wa_bf, ws_bf, wo_bf = cast_out[:4]
        w_in_bf = cast_out[4] if next_w_in else None
        yg = _ssm(uf, perm_in, perm_out, tb, cmat, lev_r, lev_i, l, batch, seq)
        (x1, h2), (wfi_bf, wfo_bf) = _merge(
            ya, yg, gates, x2, wglu_bf, ssm_glu_b[l].reshape(1, SSM_WIDTH).astype(F32),
            wa_bf, ws_bf, wo_bf, norm_ffn_g[l].reshape(1, D_MODEL).astype(F32),
            [(w, l) for w in ffn_weights])
        x2 = _ffn(h2, x1, wfi_bf, wfo_bf)
    return x2.reshape(batch, seq, D_MODEL).astype(x.dtype)
```

```python
import jax
import jax.numpy as jnp
from jax import lax
from jax.experimental import pallas as pl
from jax.experimental.pallas import tpu as pltpu

D_MODEL = 2048
DEPTH = 2
HEAD_DIM = 64
N_Q_HEADS = 16
N_KV_HEADS = 4
GQA_GROUP = N_Q_HEADS // N_KV_HEADS
ATTN_WIDTH = N_Q_HEADS * HEAD_DIM
KV_WIDTH = N_KV_HEADS * HEAD_DIM
WINDOW = 128
BLOCK = 128
SSM_WIDTH = D_MODEL // 2
SSM_GROUP_CH = 16
SSM_GROUPS = SSM_WIDTH // SSM_GROUP_CH
SSM_STATE = 64
D_FF = -(-8 * D_MODEL // (3 * 256)) * 256
OFF_K = ATTN_WIDTH
OFF_V = OFF_K + KV_WIDTH
OFF_U = OFF_V + KV_WIDTH
OFF_G = OFF_U + SSM_WIDTH
IN_WIDTH = OFF_G + 2 * D_MODEL
RMS_EPS = 1e-6

F32 = jnp.float32
BF16 = jnp.bfloat16
HIGHEST = lax.Precision.HIGHEST

LANES = 128
BF16_SUBLANES = 16
SUB = 16
SUB_W = SUB * SSM_GROUP_CH
STATE_W = 2 * SSM_STATE
TB_W = SUB_W + 2 * STATE_W
SLAB_GROUPS = LANES // SSM_GROUP_CH
N_SLABS = SSM_GROUPS // SLAB_GROUPS
SLAB_W = SLAB_GROUPS * SUB_W

VMEM_LIMIT = 56 * 1024 * 1024
FFN_VMEM_LIMIT = 60 * 1024 * 1024

INPROJ_TM = 256
INPROJ_TN = 512
MERGE_TM = 256
FFN_TM = 1024
FFN_TF = 512
FFN_SUB = 512


def _rms(x, g):
    return x * lax.rsqrt(jnp.mean(x * x, axis=-1, keepdims=True) + RMS_EPS) * g


def _sigmoid(x):
    return 0.5 * jnp.tanh(0.5 * x) + 0.5


class _RowCast:
    def __init__(self, stacked, layer, n_steps, step_of):
        _, rows, cols = stacked.shape
        blk = rows // n_steps
        assert blk * n_steps == rows and blk % BF16_SUBLANES == 0, (rows, n_steps)
        self.operand = stacked
        self.in_spec = pl.BlockSpec((None, blk, cols), lambda *ids: (layer, step_of(*ids), 0))
        self.out_spec = pl.BlockSpec((blk, cols), lambda *ids: (step_of(*ids), 0))
        self.out_shape = jax.ShapeDtypeStruct((rows, cols), BF16)


def _cast_blocks(src_refs, dst_refs):
    for src, dst in zip(src_refs, dst_refs):
        dst[...] = src[...].astype(BF16)


N_QKV_BLOCKS = OFF_U // INPROJ_TN
N_MAIN_BLOCKS = OFF_G // INPROJ_TN
N_IN_BLOCKS = IN_WIDTH // INPROJ_TN


QBLOCKS_PER_TILE = INPROJ_TM // BLOCK
ATTN_PROBLEMS_PER_GAP = 2


def _inproj_attn_kernel(sink_ref, x_ref, g_ref, w_ref, b_ref, seg_ref, nt_ref, bias0_ref, bias_ref, *refs):
    n_cast = (len(refs) - 5) // 2
    u_ref, gt_ref, ya_ref = refs[n_cast:n_cast + 3]
    q_ref, kv_ref = refs[2 * n_cast + 3:]

    @pl.when(pl.program_id(0) == 0)
    def _():
        kv_ref[...] = jnp.zeros(kv_ref.shape, BF16)

    kv_ref[0:BLOCK, :] = kv_ref[INPROJ_TM:INPROJ_TM + BLOCK, :]
    h = _rms(x_ref[...], g_ref[...]).astype(BF16)

    def project(j):
        return jnp.dot(h, w_ref[:, j * INPROJ_TN:(j + 1) * INPROJ_TN], preferred_element_type=F32)

    for j in range(N_QKV_BLOCKS):
        z = project(j)
        ssq = jnp.dot((z * z).astype(BF16), seg_ref[...], preferred_element_type=F32)
        inv = lax.rsqrt(ssq * (1.0 / HEAD_DIM) + RMS_EPS)
        fac = jnp.where(nt_ref[j, 1:2, :] > 0.0, inv, 1.0) * nt_ref[j, 0:1, :]
        if (j + 1) * INPROJ_TN <= ATTN_WIDTH:
            q_ref[:, j * INPROJ_TN:(j + 1) * INPROJ_TN] = (z * fac).astype(BF16)
        else:
            kv_ref[BLOCK:, :] = (z * fac).astype(BF16)

    def other_block(j):
        z = project(j)
        if j < N_MAIN_BLOCKS:
            c = j - N_QKV_BLOCKS
            u_ref[:, c * INPROJ_TN:(c + 1) * INPROJ_TN] = z
        else:
            cs = slice((j - N_MAIN_BLOCKS) * INPROJ_TN, (j - N_MAIN_BLOCKS + 1) * INPROJ_TN)
            gt_ref[:, cs] = _sigmoid(z + b_ref[:, cs]).astype(BF16)

    kv_rows = 2 * BLOCK
    all_rows = kv_ref.shape[0]
    left_kv = lax.broadcasted_iota(jnp.int32, (all_rows, LANES), 1) < HEAD_DIM
    left_q = lax.broadcasted_iota(jnp.int32, (BLOCK, LANES), 1) < HEAD_DIM
    zeros = jnp.zeros((all_rows, LANES), BF16)
    left_ones = lax.broadcasted_iota(jnp.int32, (kv_rows, LANES), 1) < HEAD_DIM
    ones_l = jnp.where(left_ones, 1.0, 0.0).astype(BF16)
    ones_r = jnp.where(left_ones, 0.0, 1.0).astype(BF16)
    contract_lanes = (((1,), (1,)), ((), ()))

    def slab(off):
        a = kv_ref[:, off:off + LANES]
        return a, pltpu.roll(a.astype(F32), HEAD_DIM, axis=1).astype(BF16)

    problems = []
    for c in range(N_KV_HEADS // 2):
        k_slabs = slab(c * LANES)
        v_slabs = slab(KV_WIDTH + c * LANES)
        for side in range(2):
            kh = 2 * c + side
            k_l = jnp.where(left_kv, k_slabs[side], zeros)
            k_r = jnp.where(left_kv, zeros, k_slabs[1 - side])
            v_l = jnp.where(left_kv, v_slabs[side], zeros)
            v_r = jnp.where(left_kv, zeros, v_slabs[1 - side])
            for qb in range(QBLOCKS_PER_TILE):
                rows = slice(qb * BLOCK, qb * BLOCK + kv_rows)
                kk = jnp.concatenate([k_l[rows], k_r[rows]], axis=0)
                vv = jnp.concatenate([jnp.concatenate([v_l[rows], ones_l], axis=1),
                                      jnp.concatenate([v_r[rows], ones_r], axis=1)], axis=0)
                for pair in range(GQA_GROUP // 2):
                    problems.append((qb, kh * GQA_GROUP + 2 * pair, kk, vv))

    def scores(qb, e, kk):
        q2 = q_ref[qb * BLOCK:(qb + 1) * BLOCK, e * HEAD_DIM:(e + 2) * HEAD_DIM]
        return lax.dot_general(q2, kk, contract_lanes, preferred_element_type=F32)

    def finish(qb, e, s2, vv):
        bias = bias0_ref if qb == 0 else bias_ref
        ps, ds = [], []
        for t in range(2):
            s = s2[:, t * kv_rows:(t + 1) * kv_rows] + bias[e + t]
            sink = sink_ref[e + t]
            m = jnp.maximum(jnp.max(s, axis=-1, keepdims=True), sink)
            ps.append(jnp.exp(s - m).astype(BF16))
            ds.append(jnp.exp(sink - m))
        r = jnp.dot(jnp.concatenate(ps, axis=1), vv, preferred_element_type=F32)
        denom = r[:, LANES:] + jnp.where(left_q, ds[0], ds[1])
        ya_ref[qb * BLOCK:(qb + 1) * BLOCK, e * HEAD_DIM:(e + 2) * HEAD_DIM] = (r[:, :LANES] / denom).astype(BF16)

    others = list(range(N_QKV_BLOCKS, N_IN_BLOCKS))
    for first in range(0, len(problems), ATTN_PROBLEMS_PER_GAP):
        group = problems[first:first + ATTN_PROBLEMS_PER_GAP]
        s2s = [scores(qb, e, kk) for qb, e, kk, _ in group]
        if others:
            other_block(others.pop(0))
        for (qb, e, _, vv), s2 in zip(group, s2s):
            finish(qb, e, s2, vv)
    for j in others:
        other_block(j)
    _cast_blocks(refs[:n_cast], refs[n_cast + 3:2 * n_cast + 3])


def _inproj_attn(x2, gain, w_bf, bias, seg, ntab, sinks, bias_tab, seq, cast_weights):
    t = x2.shape[0]
    tm = INPROJ_TM
    tiles_per_seq = seq // tm
    const2 = lambda i: (0, 0)
    bias_shape = (None, N_Q_HEADS, BLOCK, 2 * BLOCK)
    casts = [_RowCast(w, wl, t // tm, lambda i: i) for w, wl in cast_weights]
    outs = pl.pallas_call(
        _inproj_attn_kernel,
        grid=(t // tm,),
        in_specs=[
            pl.BlockSpec(memory_space=pltpu.SMEM),
            pl.BlockSpec((tm, D_MODEL), lambda i: (i, 0)),
            pl.BlockSpec((1, D_MODEL), const2),
            pl.BlockSpec((D_MODEL, IN_WIDTH), const2, pipeline_mode=pl.Buffered(1)),
            pl.BlockSpec((1, 2 * D_MODEL), const2),
            pl.BlockSpec((INPROJ_TN, INPROJ_TN), const2),
            pl.BlockSpec((N_QKV_BLOCKS, 2, INPROJ_TN), lambda i: (0, 0, 0)),
            pl.BlockSpec(bias_shape, lambda i: (jnp.minimum(i % tiles_per_seq, 1), 0, 0, 0)),
            pl.BlockSpec(bias_shape, lambda i: (1, 0, 0, 0), pipeline_mode=pl.Buffered(1)),
        ] + [c.in_spec for c in casts],
        out_specs=[
            pl.BlockSpec((tm, SSM_WIDTH), lambda i: (i, 0)),
            pl.BlockSpec((tm, 2 * D_MODEL), lambda i: (i, 0)),
            pl.BlockSpec((tm, ATTN_WIDTH), lambda i: (i, 0)),
        ] + [c.out_spec for c in casts],
        out_shape=[
            jax.ShapeDtypeStruct((t, SSM_WIDTH), F32),
            jax.ShapeDtypeStruct((t, 2 * D_MODEL), BF16),
            jax.ShapeDtypeStruct((t, ATTN_WIDTH), BF16),
        ] + [c.out_shape for c in casts],
        scratch_shapes=[pltpu.VMEM((tm, ATTN_WIDTH), BF16), pltpu.VMEM((BLOCK + tm, 2 * KV_WIDTH), BF16)],
        compiler_params=pltpu.CompilerParams(
            dimension_semantics=("arbitrary",), vmem_limit_bytes=VMEM_LIMIT),
        name="inproj_attn",
    )(sinks, x2, gain, w_bf, bias, seg, ntab, bias_tab, bias_tab, *[c.operand for c in casts])
    return outs[:3], outs[3:]


def _qk_norm_tables(q_gain, k_gain):
    qrow = jnp.tile(q_gain.astype(F32), INPROJ_TN // HEAD_DIM) * (HEAD_DIM ** -0.5)
    ones = jnp.ones((INPROJ_TN,), F32)
    kvrow = jnp.concatenate([jnp.tile(k_gain.astype(F32), N_KV_HEADS), jnp.ones((KV_WIDTH,), F32)])
    kvmask = jnp.concatenate([jnp.ones((KV_WIDTH,), F32), jnp.zeros((KV_WIDTH,), F32)])
    return jnp.stack([jnp.stack([qrow, ones]), jnp.stack([qrow, ones]), jnp.stack([kvrow, kvmask])])


def _segment_ones():
    r = jnp.arange(INPROJ_TN) // HEAD_DIM
    return (r[:, None] == r[None, :]).astype(BF16)


def _attn_bias_tables():
    t_loc = jnp.arange(BLOCK)[:, None]
    s_loc = jnp.arange(2 * BLOCK)[None, :] - BLOCK
    dist = (t_loc - s_loc).astype(F32)
    valid = (dist >= 0) & (dist < WINDOW)
    slopes = jnp.exp2(-8.0 * jnp.arange(1, N_Q_HEADS + 1, dtype=F32) / N_Q_HEADS)
    bias = -slopes[:, None, None] * dist[None]
    full = jnp.where(valid[None], bias, -jnp.inf)
    first = jnp.where((valid & (s_loc >= 0))[None], bias, -jnp.inf)
    return jnp.stack([first, full])


def _cmul(ar, ai, br, bi):
    return ar * br - ai * bi, ar * bi + ai * br


def _pow_by_bits(exps, squares):
    pr = pi = None
    for b, (sr, si) in enumerate(squares):
        on = ((exps >> b) & 1) == 1
        fr, fi = jnp.where(on, sr, 1.0), jnp.where(on, si, 0.0)
        pr, pi = (fr, fi) if pr is None else _cmul(pr, pi, fr, fi)
    return pr, pi


def _ssm_prep_kernel(lam_ref, ldt_ref, btr_ref, bti_ref, ctr_ref, cti_ref, d_ref, *refs):
    n_cast = (len(refs) - 4) // 2
    tb_ref, c_ref, levr_ref, levi_ref = refs[n_cast:n_cast + 4]
    _cast_blocks(refs[:n_cast], refs[n_cast + 4:])
    n_levels = levr_ref.shape[1]
    tau_lane = lax.broadcasted_iota(jnp.int32, (SSM_STATE, SUB_W), 1) // SSM_GROUP_CH
    row_h = lax.broadcasted_iota(jnp.int32, (SSM_GROUP_CH, SUB_W), 0)
    lane_h = lax.broadcasted_iota(jnp.int32, (SSM_GROUP_CH, SUB_W), 1)
    tile_ch = (lane_h % SSM_GROUP_CH == row_h).astype(F32)
    row_m = lax.broadcasted_iota(jnp.int32, (SUB, SSM_STATE), 0)
    eye_p = (lax.broadcasted_iota(jnp.int32, (SSM_STATE, SSM_STATE), 0)
             == lax.broadcasted_iota(jnp.int32, (SSM_STATE, SSM_STATE), 1))

    def to_col(v):
        return jnp.sum(jnp.where(eye_p, v, 0.0), axis=1, keepdims=True)

    def squares(a, n):
        out = [a]
        for _ in range(n - 1):
            out.append(_cmul(*out[-1], *out[-1]))
        return out

    for g in range(SLAB_GROUPS):
        dt = jnp.exp(ldt_ref[g])

        def discretise(lr, li):
            mag = jnp.exp(lr * dt)
            return mag * jnp.cos(li * dt), mag * jnp.sin(li * dt)

        lr_row, li_row = lam_ref[g, 0:1, :], lam_ref[g, 1:2, :]
        a_row = discretise(lr_row, li_row)
        sq_row = squares(a_row, 5)
        sq_col = squares((to_col(a_row[0]), to_col(a_row[1])), 4)
        ar, ai = sq_row[0]
        den = lr_row * lr_row + li_row * li_row
        fr = ((ar - 1.0) * lr_row + ai * li_row) / den
        fi = (ai * lr_row - (ar - 1.0) * li_row) / den
        bbr, bbi = _cmul(fr, fi, btr_ref[g], bti_ref[g])

        e0 = _pow_by_bits(tau_lane, sq_col)
        e1 = _cmul(*e0, *sq_col[0])
        ctr = jnp.dot(ctr_ref[g], tile_ch, precision=HIGHEST, preferred_element_type=F32)
        cti = jnp.dot(cti_ref[g], tile_ch, precision=HIGHEST, preferred_element_type=F32)
        mr, mi = _cmul(*e0, ctr, cti)
        kt = (jnp.dot(bbr, mr, precision=HIGHEST, preferred_element_type=F32)
              - jnp.dot(bbi, mi, precision=HIGHEST, preferred_element_type=F32))
        kt = kt + jnp.where(lane_h == row_h, d_ref[g], 0.0)
        pw = _pow_by_bits(row_m, sq_row[:4])
        for j in range(SUB):
            rows = pl.ds(j * SSM_GROUP_CH, SSM_GROUP_CH)
            tj = kt if j == 0 else jnp.where(lane_h >= j * SSM_GROUP_CH,
                                             pltpu.roll(kt, j * SSM_GROUP_CH, axis=1), 0.0)
            tb_ref[g, rows, 0:SUB_W] = tj.astype(BF16)
            m = SUB - 1 - j
            br_, bi_ = _cmul(pw[0][m:m + 1, :], pw[1][m:m + 1, :], bbr, bbi)
            tb_ref[g, rows, SUB_W:SUB_W + STATE_W] = jnp.concatenate([br_, bi_], axis=1).astype(BF16)
            tb_ref[g, rows, SUB_W + STATE_W:TB_W] = jnp.concatenate([bi_, br_], axis=1).astype(BF16)
        c_ref[g, 0:SSM_STATE, :] = (ctr * e1[0] - cti * e1[1]).astype(BF16)
        c_ref[g, SSM_STATE:STATE_W, :] = (-ctr * e1[1] - cti * e1[0]).astype(BF16)
        lv = sq_row[4]
        lev_r, lev_i = [], []
        for _ in range(n_levels):
            lev_r.append(jnp.concatenate([lv[0], lv[0]], axis=1))
            lev_i.append(jnp.concatenate([-lv[1], lv[1]], axis=1))
            lv = _cmul(*lv, *lv)
        levr_ref[g] = jnp.concatenate(lev_r, axis=0)
        levi_ref[g] = jnp.concatenate(lev_i, axis=0)


def _ssm_prep(lam_re, lam_im, log_dt, b_re, b_im, c_re, c_im, d_skip, n_levels, cast_weights):
    dg = lam_re.shape[0] * SSM_GROUPS
    casts = [_RowCast(w, wl, dg // SLAB_GROUPS, lambda s: s) for w, wl in cast_weights]
    h_, p_ = SSM_GROUP_CH, SSM_STATE
    f = lambda a: a.astype(F32)
    lam_rows = jnp.stack([f(lam_re), f(lam_im)], axis=2).reshape(dg, 2, p_)
    btr = f(b_re).transpose(0, 1, 3, 2).reshape(dg, h_, p_)
    bti = f(b_im).transpose(0, 1, 3, 2).reshape(dg, h_, p_)
    ctr = f(c_re).transpose(0, 1, 3, 2).reshape(dg, p_, h_)
    cti = f(c_im).transpose(0, 1, 3, 2).reshape(dg, p_, h_)
    d_rows = jnp.pad(f(d_skip).reshape(dg, 1, h_), ((0, 0), (0, 0), (0, SUB_W - h_)))
    grp = lambda *shape: pl.BlockSpec((SLAB_GROUPS,) + shape, lambda s: (s,) + (0,) * len(shape))
    outs = pl.pallas_call(
        _ssm_prep_kernel,
        grid=(dg // SLAB_GROUPS,),
        in_specs=[grp(2, p_), grp(1, 1), grp(h_, p_), grp(h_, p_),
                  grp(p_, h_), grp(p_, h_), grp(1, SUB_W)] + [c.in_spec for c in casts],
        out_specs=[grp(SUB_W, TB_W), grp(STATE_W, SUB_W), grp(n_levels, STATE_W),
                   grp(n_levels, STATE_W)] + [c.out_spec for c in casts],
        out_shape=[
            jax.ShapeDtypeStruct((dg, SUB_W, TB_W), BF16),
            jax.ShapeDtypeStruct((dg, STATE_W, SUB_W), BF16),
            jax.ShapeDtypeStruct((dg, n_levels, STATE_W), F32),
            jax.ShapeDtypeStruct((dg, n_levels, STATE_W), F32),
        ] + [c.out_shape for c in casts],
        compiler_params=pltpu.CompilerParams(
            dimension_semantics=("arbitrary",), vmem_limit_bytes=VMEM_LIMIT),
        name="s5_prep",
    )(lam_rows, f(log_dt).reshape(dg, 1, 1), btr, bti, ctr, cti, d_rows, *[c.operand for c in casts])
    return outs[:4], outs[4:]


CHUNK = 2 * SSM_GROUP_CH
CHUNKS = LANES // CHUNK
PAIRS = SUB // 2
PAIR_W = 2 * LANES


def _pair_permutation():
    r = jnp.arange(PAIR_W)
    jj, lane = r // LANES, r % LANES
    col = (lane // SSM_GROUP_CH) * CHUNK + jj * SSM_GROUP_CH + lane % SSM_GROUP_CH
    return (col[:, None] == jnp.arange(PAIR_W)[None, :]).astype(BF16)


S5_STAGES = 3


def _ssm_kernel(u_ref, perm_ref, perm_t_ref, tb_ref, c_ref, ar_ref, ai_ref, y_ref, xs_ref, ys_ref, cs_ref):
    t = pl.program_id(0)
    rows = xs_ref.shape[1]
    n_levels = rows.bit_length() - 1
    cur, prv = t % 2, (t + 1) % 2

    @pl.when(t == 0)
    def _():
        xs_ref[...] = jnp.zeros(xs_ref.shape, BF16)
        ys_ref[...] = jnp.zeros(ys_ref.shape, BF16)

    def shifted(a, sh):
        return jnp.concatenate([jnp.zeros((sh, a.shape[1]), a.dtype), a[:rows - sh]], axis=0)

    def rotations(a):
        return [a] + [pltpu.roll(a, r * CHUNK, axis=1) for r in range(1, CHUNKS)]

    def chunk(a, c):
        return a[:, c * CHUNK:(c + 1) * CHUNK].astype(BF16)

    def scan_inputs(g):
        return jnp.dot(xs_ref[prv, :, g * SUB_W:(g + 1) * SUB_W], tb_ref[g], preferred_element_type=F32)

    r_next = scan_inputs(0)
    for k in range(SLAB_GROUPS):
        g = k
        r = r_next
        if g + 1 < SLAB_GROUPS:
            r_next = scan_inputs(g + 1)
        col, dst = k // CHUNKS, k % CHUNKS
        yk = ys_ref[cur, :, k * SUB_W:(k + 1) * SUB_W].astype(F32)
        for v in range(SUB_W // LANES):
            rots = rotations(yk[:, v * LANES:(v + 1) * LANES])
            for src in range(CHUNKS):
                pair = v * CHUNKS + src
                lane0 = pair * PAIR_W + col * LANES + dst * CHUNK
                cs_ref[:, lane0:lane0 + CHUNK] = chunk(rots[(dst - src) % CHUNKS], dst)
        pair_tile = jnp.concatenate([u_ref[pl.ds(2 * k, rows, stride=SUB), :],
                                     u_ref[pl.ds(2 * k + 1, rows, stride=SUB), :]], axis=1).astype(BF16)
        grouped = jnp.dot(pair_tile, perm_ref[...], preferred_element_type=F32)
        for w in range(PAIR_W // LANES):
            rots = rotations(grouped[:, w * LANES:(w + 1) * LANES])
            for src in range(CHUNKS):
                grp = w * CHUNKS + src
                lane0 = grp * SUB_W + col * LANES + dst * CHUNK
                xs_ref[cur, :, lane0:lane0 + CHUNK] = chunk(rots[(dst - src) % CHUNKS], dst)
        yt = r[:, :SUB_W]
        s = r[:, SUB_W:SUB_W + STATE_W]
        w = r[:, SUB_W + STATE_W:]
        for lvl in range(n_levels):
            sh = 1 << lvl
            ar = ar_ref[g, lvl:lvl + 1, :]
            ai = ai_ref[g, lvl:lvl + 1, :]
            ps, pw = shifted(s, sh), shifted(w, sh)
            s, w = s + ar * ps + ai * pw, (w + ar * pw - ai * ps if lvl + 1 < n_levels else None)
        y = yt + jnp.dot(shifted(s, 1).astype(BF16), c_ref[g], preferred_element_type=F32)
        ys_ref[prv, :, g * SUB_W:(g + 1) * SUB_W] = jax.nn.gelu(y).astype(BF16)

    for pair in range(PAIRS):
        o = jnp.dot(cs_ref[:, pair * PAIR_W:(pair + 1) * PAIR_W], perm_t_ref[...], preferred_element_type=F32)
        y_ref[pl.ds(2 * pair, rows, stride=SUB), :] = o[:, :LANES]
        y_ref[pl.ds(2 * pair + 1, rows, stride=SUB), :] = o[:, LANES:]


def _ssm(uf, perm, tb, cmat, lev_r, lev_i, layer, batch, seq):
    t = uf.shape[0]
    n_sub = seq // SUB
    n_levels = lev_r.shape[1]
    n_items = batch * N_SLABS
    item = lambda t, stage: jnp.clip(t - stage, 0, n_items - 1)
    slab3 = lambda t: (layer * N_SLABS + item(t, 1) % N_SLABS, 0, 0)
    return pl.pallas_call(
        _ssm_kernel,
        grid=(n_items + S5_STAGES - 1,),
        in_specs=[
            pl.BlockSpec((seq, LANES), lambda t: (item(t, 0) // N_SLABS, item(t, 0) % N_SLABS)),
            pl.BlockSpec((PAIR_W, PAIR_W), lambda t: (0, 0)),
            pl.BlockSpec((PAIR_W, PAIR_W), lambda t: (0, 0)),
            pl.BlockSpec((SLAB_GROUPS, SUB_W, TB_W), slab3),
            pl.BlockSpec((SLAB_GROUPS, STATE_W, SUB_W), slab3),
            pl.BlockSpec((SLAB_GROUPS, n_levels, STATE_W), slab3),
            pl.BlockSpec((SLAB_GROUPS, n_levels, STATE_W), slab3),
        ],
        out_specs=pl.BlockSpec((seq, LANES), lambda t: (item(t, 2) // N_SLABS, item(t, 2) % N_SLABS)),
        out_shape=jax.ShapeDtypeStruct((t, SSM_WIDTH), F32),
        scratch_shapes=[pltpu.VMEM((2, n_sub, SLAB_W), BF16), pltpu.VMEM((2, n_sub, SLAB_W), BF16),
                        pltpu.VMEM((n_sub, SLAB_W), BF16)],
        compiler_params=pltpu.CompilerParams(
            dimension_semantics=("arbitrary",), vmem_limit_bytes=VMEM_LIMIT),
        name="s5_scan",
    )(uf, perm, perm.T, tb, cmat, lev_r, lev_i)


def _merge_kernel(ya_ref, yg_ref, gt_ref, x_ref, wglu_ref, bglu_ref, wa_ref, ws_ref, wo_ref,
                  gffn_ref, *refs):
    n_cast = (len(refs) - 2) // 2
    x1_ref, h2_ref = refs[n_cast:n_cast + 2]
    yg = yg_ref[...]
    t = jnp.dot(yg.astype(BF16), wglu_ref[...], preferred_element_type=F32) + bglu_ref[...]
    ys = (yg * _sigmoid(t)).astype(BF16)
    ma = jnp.dot(ya_ref[...], wa_ref[...], preferred_element_type=F32)
    ms = jnp.dot(ys, ws_ref[...], preferred_element_type=F32)
    merged = gt_ref[:, :D_MODEL].astype(F32) * ma + gt_ref[:, D_MODEL:].astype(F32) * ms
    x1 = x_ref[...] + jnp.dot(merged.astype(BF16), wo_ref[...], preferred_element_type=F32)
    x1_ref[...] = x1
    h2_ref[...] = _rms(x1, gffn_ref[...]).astype(BF16)
    _cast_blocks(refs[:n_cast], refs[n_cast + 2:])


def _merge(ya, yg, gates, x2, wglu, bglu, wa, ws, wo, gffn, cast_weights):
    t = x2.shape[0]
    tm = MERGE_TM
    casts = [_RowCast(w, wl, t // tm, lambda i: i) for w, wl in cast_weights]

    def wspec(rows, cols):
        return pl.BlockSpec((rows, cols), lambda i: (0, 0), pipeline_mode=pl.Buffered(1))

    outs = pl.pallas_call(
        _merge_kernel,
        grid=(t // tm,),
        in_specs=[
            pl.BlockSpec((tm, ATTN_WIDTH), lambda i: (i, 0)),
            pl.BlockSpec((tm, SSM_WIDTH), lambda i: (i, 0)),
            pl.BlockSpec((tm, 2 * D_MODEL), lambda i: (i, 0)),
            pl.BlockSpec((tm, D_MODEL), lambda i: (i, 0)),
            wspec(SSM_WIDTH, SSM_WIDTH),
            pl.BlockSpec((1, SSM_WIDTH), lambda i: (0, 0)),
            wspec(ATTN_WIDTH, D_MODEL),
            wspec(SSM_WIDTH, D_MODEL),
            wspec(D_MODEL, D_MODEL),
            pl.BlockSpec((1, D_MODEL), lambda i: (0, 0)),
        ] + [c.in_spec for c in casts],
        out_specs=[
            pl.BlockSpec((tm, D_MODEL), lambda i: (i, 0)),
            pl.BlockSpec((tm, D_MODEL), lambda i: (i, 0)),
        ] + [c.out_spec for c in casts],
        out_shape=[
            jax.ShapeDtypeStruct((t, D_MODEL), F32),
            jax.ShapeDtypeStruct((t, D_MODEL), BF16),
        ] + [c.out_shape for c in casts],
        compiler_params=pltpu.CompilerParams(
            dimension_semantics=("arbitrary",), vmem_limit_bytes=VMEM_LIMIT),
        name="merge_out",
    )(ya, yg, gates, x2, wglu, bglu, wa, ws, wo, gffn, *[c.operand for c in casts])
    return outs[:2], outs[2:]


def _ffn_kernel(h_ref, x_ref, wg_ref, wu_ref, wo_ref, o_ref):
    k = pl.program_id(1)

    def accumulate(base_ref):
        for r in range(FFN_TM // FFN_SUB):
            rs = pl.ds(r * FFN_SUB, FFN_SUB)
            h = h_ref[rs, :]
            g = jnp.dot(h, wg_ref[...], preferred_element_type=F32)
            u = jnp.dot(h, wu_ref[...], preferred_element_type=F32)
            act = (g * _sigmoid(g) * u).astype(BF16)
            o_ref[rs, :] = base_ref[rs, :] + jnp.dot(act, wo_ref[...], preferred_element_type=F32)

    @pl.when(k == 0)
    def _():
        accumulate(x_ref)

    @pl.when(k > 0)
    def _():
        accumulate(o_ref)


def _ffn(h2, x1, w_in, w_out):
    t = x1.shape[0]
    nk = D_FF // FFN_TF
    return pl.pallas_call(
        _ffn_kernel,
        grid=(t // FFN_TM, nk),
        in_specs=[
            pl.BlockSpec((FFN_TM, D_MODEL), lambda i, k: (i, 0)),
            pl.BlockSpec((FFN_TM, D_MODEL), lambda i, k: (i, 0)),
            pl.BlockSpec((D_MODEL, FFN_TF), lambda i, k: (0, k)),
            pl.BlockSpec((D_MODEL, FFN_TF), lambda i, k: (0, nk + k)),
            pl.BlockSpec((FFN_TF, D_MODEL), lambda i, k: (k, 0)),
        ],
        out_specs=pl.BlockSpec((FFN_TM, D_MODEL), lambda i, k: (i, 0)),
        out_shape=jax.ShapeDtypeStruct((t, D_MODEL), F32),
        compiler_params=pltpu.CompilerParams(
            dimension_semantics=("arbitrary", "arbitrary"), vmem_limit_bytes=FFN_VMEM_LIMIT),
        name="swiglu_ffn",
    )(h2, x1, w_in, w_in, w_out)


def kernel(x, norm_mix_g, w_in, gate_bias, q_norm_g, k_norm_g, attn_sinks, ssm_lambda_re, ssm_lambda_im, ssm_log_dt, ssm_b_re, ssm_b_im, ssm_c_re, ssm_c_im, ssm_d, ssm_glu_w, ssm_glu_b, w_attn_branch, w_ssm_branch, w_out, norm_ffn_g, w_ffn_in, w_ffn_out):
    batch, seq, _ = x.shape
    t = batch * seq
    n_levels = (seq // SUB).bit_length() - 1
    x2 = x.reshape(t, D_MODEL).astype(F32)
    f32 = lambda w: w.astype(F32)
    seg = _segment_ones()
    bias_tab = _attn_bias_tables()
    perm = _pair_permutation()
    w_in_f = f32(w_in)
    (tb, cmat, lev_r, lev_i), (w_in_bf,) = _ssm_prep(
        ssm_lambda_re, ssm_lambda_im, ssm_log_dt, ssm_b_re, ssm_b_im, ssm_c_re, ssm_c_im, ssm_d,
        n_levels, [(w_in_f, 0)])
    merge_weights = [f32(ssm_glu_w), f32(w_attn_branch), f32(w_ssm_branch), f32(w_out)]
    ffn_weights = [f32(w_ffn_in), f32(w_ffn_out)]
    for l in range(DEPTH):
        next_w_in = [(w_in_f, l + 1)] if l + 1 < DEPTH else []
        (uf, gates, ya), cast_out = _inproj_attn(
            x2, norm_mix_g[l].reshape(1, D_MODEL).astype(F32), w_in_bf,
            gate_bias[l].reshape(1, 2 * D_MODEL).astype(F32), seg,
            _qk_norm_tables(q_norm_g[l], k_norm_g[l]), attn_sinks[l].astype(F32), bias_tab, seq,
            [(w, l) for w in merge_weights] + next_w_in)
        wglu_bf, wa_bf, ws_bf, wo_bf = cast_out[:4]
        w_in_bf = cast_out[4] if next_w_in else None
        yg = _ssm(uf, perm, tb, cmat, lev_r, lev_i, l, batch, seq)
        (x1, h2), (wfi_bf, wfo_bf) = _merge(
            ya, yg, gates, x2, wglu_bf, ssm_glu_b[l].reshape(1, SSM_WIDTH).astype(F32),
            wa_bf, ws_bf, wo_bf, norm_ffn_g[l].reshape(1, D_MODEL).astype(F32),
            [(w, l) for w in ffn_weights])
        x2 = _ffn(h2, x1, wfi_bf, wfo_bf)
    return x2.reshape(batch, seq, D_MODEL).astype(x.dtype)
```

```python
import jax
import jax.numpy as jnp
from jax import lax
from jax.experimental import pallas as pl
from jax.experimental.pallas import tpu as pltpu

D_MODEL = 2048
DEPTH = 2
HEAD_DIM = 64
N_Q_HEADS = 16
N_KV_HEADS = 4
GQA_GROUP = N_Q_HEADS // N_KV_HEADS
ATTN_WIDTH = N_Q_HEADS * HEAD_DIM
KV_WIDTH = N_KV_HEADS * HEAD_DIM
WINDOW = 128
BLOCK = 128
SSM_WIDTH = D_MODEL // 2
SSM_GROUP_CH = 16
SSM_GROUPS = SSM_WIDTH // SSM_GROUP_CH
SSM_STATE = 64
D_FF = -(-8 * D_MODEL // (3 * 256)) * 256
OFF_K = ATTN_WIDTH
OFF_V = OFF_K + KV_WIDTH
OFF_U = OFF_V + KV_WIDTH
OFF_G = OFF_U + SSM_WIDTH
IN_WIDTH = OFF_G + 2 * D_MODEL
RMS_EPS = 1e-6

F32 = jnp.float32
BF16 = jnp.bfloat16
HIGHEST = lax.Precision.HIGHEST

LANES = 128
SUBLANES = 8
BF16_SUBLANES = 16
SUB = 16
SUB_W = SUB * SSM_GROUP_CH
STATE_W = 2 * SSM_STATE
TB_W = 2 * SUB_W + 2 * STATE_W
SLAB_GROUPS = LANES // SSM_GROUP_CH
SLAB_PAIRS = SLAB_GROUPS // 2
N_SLABS = SSM_GROUPS // SLAB_GROUPS
SLAB_W = SLAB_GROUPS * SUB_W

VMEM_LIMIT = 56 * 1024 * 1024
FFN_VMEM_LIMIT = 60 * 1024 * 1024

INPROJ_TM = 256
INPROJ_TN = 512
MERGE_TM = 256
FFN_TM = 1024
FFN_TF = 512
FFN_SUB = 512


def _rms(x, g):
    return x * lax.rsqrt(jnp.mean(x * x, axis=-1, keepdims=True) + RMS_EPS) * g


def _sigmoid(x):
    return 0.5 * jnp.tanh(0.5 * x) + 0.5


class _RowCast:
    def __init__(self, stacked, layer, n_steps, step_of):
        _, rows, cols = stacked.shape
        blk = rows // n_steps
        assert blk * n_steps == rows and blk % BF16_SUBLANES == 0, (rows, n_steps)
        self.operand = stacked
        self.in_spec = pl.BlockSpec((None, blk, cols), lambda *ids: (layer, step_of(*ids), 0))
        self.out_spec = pl.BlockSpec((blk, cols), lambda *ids: (step_of(*ids), 0))
        self.out_shape = jax.ShapeDtypeStruct((rows, cols), BF16)


def _cast_blocks(src_refs, dst_refs):
    for src, dst in zip(src_refs, dst_refs):
        dst[...] = src[...].astype(BF16)


N_QKV_BLOCKS = OFF_U // INPROJ_TN
N_MAIN_BLOCKS = OFF_G // INPROJ_TN
N_IN_BLOCKS = IN_WIDTH // INPROJ_TN


QBLOCKS_PER_TILE = INPROJ_TM // BLOCK
ATTN_PROBLEMS_PER_GAP = 2


def _inproj_attn_kernel(sink_ref, x_ref, g_ref, w_ref, b_ref, seg_ref, nt_ref, bias0_ref, bias_ref, *refs):
    n_cast = (len(refs) - 5) // 2
    u_ref, gt_ref, ya_ref = refs[n_cast:n_cast + 3]
    q_ref, kv_ref = refs[2 * n_cast + 3:]

    @pl.when(pl.program_id(0) == 0)
    def _():
        kv_ref[...] = jnp.zeros(kv_ref.shape, BF16)

    kv_ref[0:BLOCK, :] = kv_ref[INPROJ_TM:INPROJ_TM + BLOCK, :]
    h = _rms(x_ref[...], g_ref[...]).astype(BF16)

    def project(j):
        return jnp.dot(h, w_ref[:, j * INPROJ_TN:(j + 1) * INPROJ_TN], preferred_element_type=F32)

    for j in range(N_QKV_BLOCKS):
        z = project(j)
        ssq = jnp.dot((z * z).astype(BF16), seg_ref[...], preferred_element_type=F32)
        inv = lax.rsqrt(ssq * (1.0 / HEAD_DIM) + RMS_EPS)
        fac = jnp.where(nt_ref[j, 1:2, :] > 0.0, inv, 1.0) * nt_ref[j, 0:1, :]
        if (j + 1) * INPROJ_TN <= ATTN_WIDTH:
            q_ref[:, j * INPROJ_TN:(j + 1) * INPROJ_TN] = (z * fac).astype(BF16)
        else:
            kv_ref[BLOCK:, :] = (z * fac).astype(BF16)

    def other_block(j):
        z = project(j)
        if j < N_MAIN_BLOCKS:
            c = j - N_QKV_BLOCKS
            u_ref[:, c * INPROJ_TN:(c + 1) * INPROJ_TN] = z
        else:
            cs = slice((j - N_MAIN_BLOCKS) * INPROJ_TN, (j - N_MAIN_BLOCKS + 1) * INPROJ_TN)
            gt_ref[:, cs] = _sigmoid(z + b_ref[:, cs]).astype(BF16)

    kv_rows = 2 * BLOCK
    all_rows = kv_ref.shape[0]
    left_kv = lax.broadcasted_iota(jnp.int32, (all_rows, LANES), 1) < HEAD_DIM
    left_q = lax.broadcasted_iota(jnp.int32, (BLOCK, LANES), 1) < HEAD_DIM
    zeros = jnp.zeros((all_rows, LANES), BF16)
    left_ones = lax.broadcasted_iota(jnp.int32, (kv_rows, LANES), 1) < HEAD_DIM
    ones_l = jnp.where(left_ones, 1.0, 0.0).astype(BF16)
    ones_r = jnp.where(left_ones, 0.0, 1.0).astype(BF16)
    contract_lanes = (((1,), (1,)), ((), ()))

    def slab(off):
        a = kv_ref[:, off:off + LANES]
        return a, pltpu.roll(a.astype(F32), HEAD_DIM, axis=1).astype(BF16)

    problems = []
    for c in range(N_KV_HEADS // 2):
        k_slabs = slab(c * LANES)
        v_slabs = slab(KV_WIDTH + c * LANES)
        for side in range(2):
            kh = 2 * c + side
            k_l = jnp.where(left_kv, k_slabs[side], zeros)
            k_r = jnp.where(left_kv, zeros, k_slabs[1 - side])
            v_l = jnp.where(left_kv, v_slabs[side], zeros)
            v_r = jnp.where(left_kv, zeros, v_slabs[1 - side])
            for qb in range(QBLOCKS_PER_TILE):
                rows = slice(qb * BLOCK, qb * BLOCK + kv_rows)
                kk = jnp.concatenate([k_l[rows], k_r[rows]], axis=0)
                vv = jnp.concatenate([jnp.concatenate([v_l[rows], ones_l], axis=1),
                                      jnp.concatenate([v_r[rows], ones_r], axis=1)], axis=0)
                for pair in range(GQA_GROUP // 2):
                    problems.append((qb, kh * GQA_GROUP + 2 * pair, kk, vv))

    def scores(qb, e, kk):
        q2 = q_ref[qb * BLOCK:(qb + 1) * BLOCK, e * HEAD_DIM:(e + 2) * HEAD_DIM]
        return lax.dot_general(q2, kk, contract_lanes, preferred_element_type=F32)

    def finish(qb, e, s2, vv):
        bias = bias0_ref if qb == 0 else bias_ref
        ps, ds = [], []
        for t in range(2):
            s = s2[:, t * kv_rows:(t + 1) * kv_rows] + bias[e + t]
            sink = sink_ref[e + t]
            m = jnp.maximum(jnp.max(s, axis=-1, keepdims=True), sink)
            ps.append(jnp.exp(s - m).astype(BF16))
            ds.append(jnp.exp(sink - m))
        r = jnp.dot(jnp.concatenate(ps, axis=1), vv, preferred_element_type=F32)
        denom = r[:, LANES:] + jnp.where(left_q, ds[0], ds[1])
        ya_ref[qb * BLOCK:(qb + 1) * BLOCK, e * HEAD_DIM:(e + 2) * HEAD_DIM] = (r[:, :LANES] / denom).astype(BF16)

    others = list(range(N_QKV_BLOCKS, N_IN_BLOCKS))
    for first in range(0, len(problems), ATTN_PROBLEMS_PER_GAP):
        group = problems[first:first + ATTN_PROBLEMS_PER_GAP]
        s2s = [scores(qb, e, kk) for qb, e, kk, _ in group]
        if others:
            other_block(others.pop(0))
        for (qb, e, _, vv), s2 in zip(group, s2s):
            finish(qb, e, s2, vv)
    for j in others:
        other_block(j)
    _cast_blocks(refs[:n_cast], refs[n_cast + 3:2 * n_cast + 3])


def _inproj_attn(x2, gain, w_bf, bias, seg, ntab, sinks, bias_tab, seq, cast_weights):
    t = x2.shape[0]
    tm = INPROJ_TM
    tiles_per_seq = seq // tm
    const2 = lambda i: (0, 0)
    bias_shape = (None, N_Q_HEADS, BLOCK, 2 * BLOCK)
    casts = [_RowCast(w, wl, t // tm, lambda i: i) for w, wl in cast_weights]
    outs = pl.pallas_call(
        _inproj_attn_kernel,
        grid=(t // tm,),
        in_specs=[
            pl.BlockSpec(memory_space=pltpu.SMEM),
            pl.BlockSpec((tm, D_MODEL), lambda i: (i, 0)),
            pl.BlockSpec((1, D_MODEL), const2),
            pl.BlockSpec((D_MODEL, IN_WIDTH), const2, pipeline_mode=pl.Buffered(1)),
            pl.BlockSpec((1, 2 * D_MODEL), const2),
            pl.BlockSpec((INPROJ_TN, INPROJ_TN), const2),
            pl.BlockSpec((N_QKV_BLOCKS, 2, INPROJ_TN), lambda i: (0, 0, 0)),
            pl.BlockSpec(bias_shape, lambda i: (jnp.minimum(i % tiles_per_seq, 1), 0, 0, 0)),
            pl.BlockSpec(bias_shape, lambda i: (1, 0, 0, 0), pipeline_mode=pl.Buffered(1)),
        ] + [c.in_spec for c in casts],
        out_specs=[
            pl.BlockSpec((tm, SSM_WIDTH), lambda i: (i, 0)),
            pl.BlockSpec((tm, 2 * D_MODEL), lambda i: (i, 0)),
            pl.BlockSpec((tm, ATTN_WIDTH), lambda i: (i, 0)),
        ] + [c.out_spec for c in casts],
        out_shape=[
            jax.ShapeDtypeStruct((t, SSM_WIDTH), F32),
            jax.ShapeDtypeStruct((t, 2 * D_MODEL), BF16),
            jax.ShapeDtypeStruct((t, ATTN_WIDTH), BF16),
        ] + [c.out_shape for c in casts],
        scratch_shapes=[pltpu.VMEM((tm, ATTN_WIDTH), BF16), pltpu.VMEM((BLOCK + tm, 2 * KV_WIDTH), BF16)],
        compiler_params=pltpu.CompilerParams(
            dimension_semantics=("arbitrary",), vmem_limit_bytes=VMEM_LIMIT),
        name="inproj_attn",
    )(sinks, x2, gain, w_bf, bias, seg, ntab, bias_tab, bias_tab, *[c.operand for c in casts])
    return outs[:3], outs[3:]


def _qk_norm_tables(q_gain, k_gain):
    qrow = jnp.tile(q_gain.astype(F32), INPROJ_TN // HEAD_DIM) * (HEAD_DIM ** -0.5)
    ones = jnp.ones((INPROJ_TN,), F32)
    kvrow = jnp.concatenate([jnp.tile(k_gain.astype(F32), N_KV_HEADS), jnp.ones((KV_WIDTH,), F32)])
    kvmask = jnp.concatenate([jnp.ones((KV_WIDTH,), F32), jnp.zeros((KV_WIDTH,), F32)])
    return jnp.stack([jnp.stack([qrow, ones]), jnp.stack([qrow, ones]), jnp.stack([kvrow, kvmask])])


def _segment_ones():
    r = jnp.arange(INPROJ_TN) // HEAD_DIM
    return (r[:, None] == r[None, :]).astype(BF16)


def _attn_bias_tables():
    t_loc = jnp.arange(BLOCK)[:, None]
    s_loc = jnp.arange(2 * BLOCK)[None, :] - BLOCK
    dist = (t_loc - s_loc).astype(F32)
    valid = (dist >= 0) & (dist < WINDOW)
    slopes = jnp.exp2(-8.0 * jnp.arange(1, N_Q_HEADS + 1, dtype=F32) / N_Q_HEADS)
    bias = -slopes[:, None, None] * dist[None]
    full = jnp.where(valid[None], bias, -jnp.inf)
    first = jnp.where((valid & (s_loc >= 0))[None], bias, -jnp.inf)
    return jnp.stack([first, full])


def _cmul(ar, ai, br, bi):
    return ar * br - ai * bi, ar * bi + ai * br


def _pow_by_bits(exps, squares):
    pr = pi = None
    for b, (sr, si) in enumerate(squares):
        on = ((exps >> b) & 1) == 1
        fr, fi = jnp.where(on, sr, 1.0), jnp.where(on, si, 0.0)
        pr, pi = (fr, fi) if pr is None else _cmul(pr, pi, fr, fi)
    return pr, pi


def _ssm_prep_kernel(lam_ref, ldt_ref, btr_ref, bti_ref, ctr_ref, cti_ref, d_ref, *refs):
    n_cast = (len(refs) - 4) // 2
    tb_ref, c_ref, levr_ref, levi_ref = refs[n_cast:n_cast + 4]
    _cast_blocks(refs[:n_cast], refs[n_cast + 4:])
    n_levels = levr_ref.shape[1]
    tau_lane = lax.broadcasted_iota(jnp.int32, (SSM_STATE, SUB_W), 1) // SSM_GROUP_CH
    row_h = lax.broadcasted_iota(jnp.int32, (SSM_GROUP_CH, SUB_W), 0)
    lane_h = lax.broadcasted_iota(jnp.int32, (SSM_GROUP_CH, SUB_W), 1)
    tile_ch = (lane_h % SSM_GROUP_CH == row_h).astype(F32)
    row_m = lax.broadcasted_iota(jnp.int32, (SUB, SSM_STATE), 0)
    eye_p = (lax.broadcasted_iota(jnp.int32, (SSM_STATE, SSM_STATE), 0)
             == lax.broadcasted_iota(jnp.int32, (SSM_STATE, SSM_STATE), 1))

    def to_col(v):
        return jnp.sum(jnp.where(eye_p, v, 0.0), axis=1, keepdims=True)

    def squares(a, n):
        out = [a]
        for _ in range(n - 1):
            out.append(_cmul(*out[-1], *out[-1]))
        return out

    for g in range(SLAB_GROUPS):
        dt = jnp.exp(ldt_ref[g])

        def discretise(lr, li):
            mag = jnp.exp(lr * dt)
            return mag * jnp.cos(li * dt), mag * jnp.sin(li * dt)

        lr_row, li_row = lam_ref[g, 0:1, :], lam_ref[g, 1:2, :]
        a_row = discretise(lr_row, li_row)
        sq_row = squares(a_row, 5)
        sq_col = squares((to_col(a_row[0]), to_col(a_row[1])), 4)
        ar, ai = sq_row[0]
        den = lr_row * lr_row + li_row * li_row
        fr = ((ar - 1.0) * lr_row + ai * li_row) / den
        fi = (ai * lr_row - (ar - 1.0) * li_row) / den
        bbr, bbi = _cmul(fr, fi, btr_ref[g], bti_ref[g])

        e0 = _pow_by_bits(tau_lane, sq_col)
        e1 = _cmul(*e0, *sq_col[0])
        ctr = jnp.dot(ctr_ref[g], tile_ch, precision=HIGHEST, preferred_element_type=F32)
        cti = jnp.dot(cti_ref[g], tile_ch, precision=HIGHEST, preferred_element_type=F32)
        mr, mi = _cmul(*e0, ctr, cti)
        kt = (jnp.dot(bbr, mr, precision=HIGHEST, preferred_element_type=F32)
              - jnp.dot(bbi, mi, precision=HIGHEST, preferred_element_type=F32))
        kt = kt + jnp.where(lane_h == row_h, d_ref[g], 0.0)
        pw = _pow_by_bits(row_m, sq_row[:4])
        q, side = g // 2, g % 2
        own = lambda width: slice(side * width, (side + 1) * width)
        other = lambda width: slice((1 - side) * width, (2 - side) * width)
        zeros_p = jnp.zeros((SSM_GROUP_CH, SSM_STATE), F32)

        def paired(v):
            return jnp.concatenate([v, zeros_p] if side == 0 else [zeros_p, v], axis=1).astype(BF16)

        for j in range(SUB):
            rows = pl.ds(side * SUB_W + j * SSM_GROUP_CH, SSM_GROUP_CH)
            tj = kt if j == 0 else jnp.where(lane_h >= j * SSM_GROUP_CH,
                                             pltpu.roll(kt, j * SSM_GROUP_CH, axis=1), 0.0)
            tb_ref[q, rows, own(SUB_W)] = tj.astype(BF16)
            tb_ref[q, rows, other(SUB_W)] = jnp.zeros((SSM_GROUP_CH, SUB_W), BF16)
            m = SUB - 1 - j
            br_, bi_ = _cmul(pw[0][m:m + 1, :], pw[1][m:m + 1, :], bbr, bbi)
            tb_ref[q, rows, 2 * SUB_W:2 * SUB_W + STATE_W] = paired(br_)
            tb_ref[q, rows, 2 * SUB_W + STATE_W:TB_W] = paired(bi_)
        for part, val in enumerate([ctr * e1[0] - cti * e1[1], -ctr * e1[1] - cti * e1[0]]):
            rows = pl.ds(part * STATE_W + side * SSM_STATE, SSM_STATE)
            c_ref[q, rows, own(SUB_W)] = val.astype(BF16)
            c_ref[q, rows, other(SUB_W)] = jnp.zeros((SSM_STATE, SUB_W), BF16)
        lv = sq_row[4]
        lev_r, lev_i = [], []
        for _ in range(n_levels):
            lev_r.append(lv[0])
            lev_i.append(lv[1])
            lv = _cmul(*lv, *lv)
        levels = (jnp.concatenate(lev_r, axis=0), jnp.concatenate(lev_i, axis=0))
        if side == 0:
            first_levels = levels
        else:
            levr_ref[q] = jnp.concatenate([first_levels[0], levels[0]], axis=1)
            levi_ref[q] = jnp.concatenate([first_levels[1], levels[1]], axis=1)


def _ssm_prep(lam_re, lam_im, log_dt, b_re, b_im, c_re, c_im, d_skip, n_levels, cast_weights):
    dg = lam_re.shape[0] * SSM_GROUPS
    casts = [_RowCast(w, wl, dg // SLAB_GROUPS, lambda s: s) for w, wl in cast_weights]
    h_, p_ = SSM_GROUP_CH, SSM_STATE
    f = lambda a: a.astype(F32)
    lam_rows = jnp.stack([f(lam_re), f(lam_im)], axis=2).reshape(dg, 2, p_)
    btr = f(b_re).transpose(0, 1, 3, 2).reshape(dg, h_, p_)
    bti = f(b_im).transpose(0, 1, 3, 2).reshape(dg, h_, p_)
    ctr = f(c_re).transpose(0, 1, 3, 2).reshape(dg, p_, h_)
    cti = f(c_im).transpose(0, 1, 3, 2).reshape(dg, p_, h_)
    d_rows = jnp.pad(f(d_skip).reshape(dg, 1, h_), ((0, 0), (0, 0), (0, SUB_W - h_)))
    grp = lambda *shape: pl.BlockSpec((SLAB_GROUPS,) + shape, lambda s: (s,) + (0,) * len(shape))
    pair = lambda *shape: pl.BlockSpec((SLAB_PAIRS,) + shape, lambda s: (s,) + (0,) * len(shape))
    outs = pl.pallas_call(
        _ssm_prep_kernel,
        grid=(dg // SLAB_GROUPS,),
        in_specs=[grp(2, p_), grp(1, 1), grp(h_, p_), grp(h_, p_),
                  grp(p_, h_), grp(p_, h_), grp(1, SUB_W)] + [c.in_spec for c in casts],
        out_specs=[pair(2 * SUB_W, TB_W), pair(2 * STATE_W, 2 * SUB_W), pair(n_levels, STATE_W),
                   pair(n_levels, STATE_W)] + [c.out_spec for c in casts],
        out_shape=[
            jax.ShapeDtypeStruct((dg // 2, 2 * SUB_W, TB_W), BF16),
            jax.ShapeDtypeStruct((dg // 2, 2 * STATE_W, 2 * SUB_W), BF16),
            jax.ShapeDtypeStruct((dg // 2, n_levels, STATE_W), F32),
            jax.ShapeDtypeStruct((dg // 2, n_levels, STATE_W), F32),
        ] + [c.out_shape for c in casts],
        compiler_params=pltpu.CompilerParams(
            dimension_semantics=("arbitrary",), vmem_limit_bytes=VMEM_LIMIT),
        name="s5_prep",
    )(lam_rows, f(log_dt).reshape(dg, 1, 1), btr, bti, ctr, cti, d_rows, *[c.operand for c in casts])
    return outs[:4], outs[4:]


CHUNK = 2 * SSM_GROUP_CH
CHUNKS = LANES // CHUNK
PAIRS = SUB // 2
PAIR_W = 2 * LANES


def _pair_permutation():
    r = jnp.arange(PAIR_W)
    jj, lane = r // LANES, r % LANES
    col = (lane // SSM_GROUP_CH) * CHUNK + jj * SSM_GROUP_CH + lane % SSM_GROUP_CH
    return (col[:, None] == jnp.arange(PAIR_W)[None, :]).astype(BF16)


S5_STAGES = 3


def _ssm_kernel(u_ref, perm_ref, perm_t_ref, tb_ref, c_ref, ar_ref, ai_ref, y_ref, xs_ref, ys_ref, cs_ref):
    t = pl.program_id(0)
    rows = xs_ref.shape[1]
    n_levels = rows.bit_length() - 1
    cur, prv = t % 2, (t + 1) % 2

    @pl.when(t == 0)
    def _():
        xs_ref[...] = jnp.zeros(xs_ref.shape, BF16)
        ys_ref[...] = jnp.zeros(ys_ref.shape, BF16)

    def shifted(a, sh):
        return jnp.concatenate([jnp.zeros((sh, a.shape[1]), a.dtype), a[:rows - sh]], axis=0)

    def rotations(a):
        return [a] + [pltpu.roll(a, r * CHUNK, axis=1) for r in range(1, CHUNKS)]

    def chunk(a, c):
        return a[:, c * CHUNK:(c + 1) * CHUNK].astype(BF16)

    def scan_inputs(q):
        return jnp.dot(xs_ref[prv, :, q * 2 * SUB_W:(q + 1) * 2 * SUB_W], tb_ref[q],
                       preferred_element_type=F32)

    def layout_out_chunks(k):
        col, dst = k // CHUNKS, k % CHUNKS
        yk = ys_ref[cur, :, k * SUB_W:(k + 1) * SUB_W].astype(F32)
        for v in range(SUB_W // LANES):
            rots = rotations(yk[:, v * LANES:(v + 1) * LANES])
            for src in range(CHUNKS):
                lane0 = (v * CHUNKS + src) * PAIR_W + col * LANES + dst * CHUNK
                cs_ref[:, lane0:lane0 + CHUNK] = chunk(rots[(dst - src) % CHUNKS], dst)

    def layout_in_chunks(k):
        col, dst = k // CHUNKS, k % CHUNKS
        pair_tile = jnp.concatenate([u_ref[pl.ds(2 * k, rows, stride=SUB), :],
                                     u_ref[pl.ds(2 * k + 1, rows, stride=SUB), :]], axis=1).astype(BF16)
        grouped = jnp.dot(pair_tile, perm_ref[...], preferred_element_type=F32)
        for w in range(PAIR_W // LANES):
            rots = rotations(grouped[:, w * LANES:(w + 1) * LANES])
            for src in range(CHUNKS):
                lane0 = (w * CHUNKS + src) * SUB_W + col * LANES + dst * CHUNK
                xs_ref[cur, :, lane0:lane0 + CHUNK] = chunk(rots[(dst - src) % CHUNKS], dst)

    r_next = scan_inputs(0)
    for q in range(SLAB_PAIRS):
        r = r_next
        if q + 1 < SLAB_PAIRS:
            r_next = scan_inputs(q + 1)
        for k in (2 * q, 2 * q + 1):
            layout_out_chunks(k)
            layout_in_chunks(k)
        yt = r[:, :2 * SUB_W]
        re = r[:, 2 * SUB_W:2 * SUB_W + STATE_W]
        im = r[:, 2 * SUB_W + STATE_W:]
        for lvl in range(n_levels):
            sh = 1 << lvl
            ar = ar_ref[q, lvl:lvl + 1, :]
            ai = ai_ref[q, lvl:lvl + 1, :]
            if sh % SUBLANES:
                pr, pi = shifted(re, sh), shifted(im, sh)
                re, im = re + ar * pr - ai * pi, im + ar * pi + ai * pr
            else:
                pr, pi = re[:rows - sh], im[:rows - sh]
                re, im = (jnp.concatenate([re[:sh], re[sh:] + ar * pr - ai * pi], axis=0),
                          jnp.concatenate([im[:sh], im[sh:] + ar * pi + ai * pr], axis=0))
        prev_state = jnp.concatenate([shifted(re, 1), shifted(im, 1)], axis=1).astype(BF16)
        y = yt + jnp.dot(prev_state, c_ref[q], preferred_element_type=F32)
        ys_ref[prv, :, q * 2 * SUB_W:(q + 1) * 2 * SUB_W] = jax.nn.gelu(y).astype(BF16)

    for pair in range(PAIRS):
        o = jnp.dot(cs_ref[:, pair * PAIR_W:(pair + 1) * PAIR_W], perm_t_ref[...], preferred_element_type=F32)
        y_ref[pl.ds(2 * pair, rows, stride=SUB), :] = o[:, :LANES]
        y_ref[pl.ds(2 * pair + 1, rows, stride=SUB), :] = o[:, LANES:]


def _ssm(uf, perm, tb, cmat, lev_r, lev_i, layer, batch, seq):
    t = uf.shape[0]
    n_sub = seq // SUB
    n_levels = lev_r.shape[1]
    n_items = batch * N_SLABS
    item = lambda t, stage: jnp.clip(t - stage, 0, n_items - 1)
    slab3 = lambda t: (layer * N_SLABS + item(t, 1) % N_SLABS, 0, 0)
    return pl.pallas_call(
        _ssm_kernel,
        grid=(n_items + S5_STAGES - 1,),
        in_specs=[
            pl.BlockSpec((seq, LANES), lambda t: (item(t, 0) // N_SLABS, item(t, 0) % N_SLABS)),
            pl.BlockSpec((PAIR_W, PAIR_W), lambda t: (0, 0)),
            pl.BlockSpec((PAIR_W, PAIR_W), lambda t: (0, 0)),
            pl.BlockSpec((SLAB_PAIRS, 2 * SUB_W, TB_W), slab3),
            pl.BlockSpec((SLAB_PAIRS, 2 * STATE_W, 2 * SUB_W), slab3),
            pl.BlockSpec((SLAB_PAIRS, n_levels, STATE_W), slab3),
            pl.BlockSpec((SLAB_PAIRS, n_levels, STATE_W), slab3),
        ],
        out_specs=pl.BlockSpec((seq, LANES), lambda t: (item(t, 2) // N_SLABS, item(t, 2) % N_SLABS)),
        out_shape=jax.ShapeDtypeStruct((t, SSM_WIDTH), F32),
        scratch_shapes=[pltpu.VMEM((2, n_sub, SLAB_W), BF16), pltpu.VMEM((2, n_sub, SLAB_W), BF16),
                        pltpu.VMEM((n_sub, SLAB_W), BF16)],
        compiler_params=pltpu.CompilerParams(
            dimension_semantics=("arbitrary",), vmem_limit_bytes=VMEM_LIMIT),
        name="s5_scan",
    )(uf, perm, perm.T, tb, cmat, lev_r, lev_i)


def _merge_kernel(ya_ref, yg_ref, gt_ref, x_ref, wglu_ref, bglu_ref, wa_ref, ws_ref, wo_ref,
                  gffn_ref, *refs):
    n_cast = (len(refs) - 2) // 2
    x1_ref, h2_ref = refs[n_cast:n_cast + 2]
    yg = yg_ref[...]
    t = jnp.dot(yg.astype(BF16), wglu_ref[...], preferred_element_type=F32) + bglu_ref[...]
    ys = (yg * _sigmoid(t)).astype(BF16)
    ma = jnp.dot(ya_ref[...], wa_ref[...], preferred_element_type=F32)
    ms = jnp.dot(ys, ws_ref[...], preferred_element_type=F32)
    merged = gt_ref[:, :D_MODEL].astype(F32) * ma + gt_ref[:, D_MODEL:].astype(F32) * ms
    x1 = x_ref[...] + jnp.dot(merged.astype(BF16), wo_ref[...], preferred_element_type=F32)
    x1_ref[...] = x1
    h2_ref[...] = _rms(x1, gffn_ref[...]).astype(BF16)
    _cast_blocks(refs[:n_cast], refs[n_cast + 2:])


def _merge(ya, yg, gates, x2, wglu, bglu, wa, ws, wo, gffn, cast_weights):
    t = x2.shape[0]
    tm = MERGE_TM
    casts = [_RowCast(w, wl, t // tm, lambda i: i) for w, wl in cast_weights]

    def wspec(rows, cols):
        return pl.BlockSpec((rows, cols), lambda i: (0, 0), pipeline_mode=pl.Buffered(1))

    outs = pl.pallas_call(
        _merge_kernel,
        grid=(t // tm,),
        in_specs=[
            pl.BlockSpec((tm, ATTN_WIDTH), lambda i: (i, 0)),
            pl.BlockSpec((tm, SSM_WIDTH), lambda i: (i, 0)),
            pl.BlockSpec((tm, 2 * D_MODEL), lambda i: (i, 0)),
            pl.BlockSpec((tm, D_MODEL), lambda i: (i, 0)),
            wspec(SSM_WIDTH, SSM_WIDTH),
            pl.BlockSpec((1, SSM_WIDTH), lambda i: (0, 0)),
            wspec(ATTN_WIDTH, D_MODEL),
            wspec(SSM_WIDTH, D_MODEL),
            wspec(D_MODEL, D_MODEL),
            pl.BlockSpec((1, D_MODEL), lambda i: (0, 0)),
        ] + [c.in_spec for c in casts],
        out_specs=[
            pl.BlockSpec((tm, D_MODEL), lambda i: (i, 0)),
            pl.BlockSpec((tm, D_MODEL), lambda i: (i, 0)),
        ] + [c.out_spec for c in casts],
        out_shape=[
            jax.ShapeDtypeStruct((t, D_MODEL), F32),
            jax.ShapeDtypeStruct((t, D_MODEL), BF16),
        ] + [c.out_shape for c in casts],
        compiler_params=pltpu.CompilerParams(
            dimension_semantics=("arbitrary",), vmem_limit_bytes=VMEM_LIMIT),
        name="merge_out",
    )(ya, yg, gates, x2, wglu, bglu, wa, ws, wo, gffn, *[c.operand for c in casts])
    return outs[:2], outs[2:]


def _ffn_kernel(h_ref, x_ref, wg_ref, wu_ref, wo_ref, o_ref):
    k = pl.program_id(1)

    def accumulate(base_ref):
        for r in range(FFN_TM // FFN_SUB):
            rs = pl.ds(r * FFN_SUB, FFN_SUB)
            h = h_ref[rs, :]
            g = jnp.dot(h, wg_ref[...], preferred_element_type=F32)
            u = jnp.dot(h, wu_ref[...], preferred_element_type=F32)
            act = (g * _sigmoid(g) * u).astype(BF16)
            o_ref[rs, :] = base_ref[rs, :] + jnp.dot(act, wo_ref[...], preferred_element_type=F32)

    @pl.when(k == 0)
    def _():
        accumulate(x_ref)

    @pl.when(k > 0)
    def _():
        accumulate(o_ref)


def _ffn(h2, x1, w_in, w_out):
    t = x1.shape[0]
    nk = D_FF // FFN_TF
    return pl.pallas_call(
        _ffn_kernel,
        grid=(t // FFN_TM, nk),
        in_specs=[
            pl.BlockSpec((FFN_TM, D_MODEL), lambda i, k: (i, 0)),
            pl.BlockSpec((FFN_TM, D_MODEL), lambda i, k: (i, 0)),
            pl.BlockSpec((D_MODEL, FFN_TF), lambda i, k: (0, k)),
            pl.BlockSpec((D_MODEL, FFN_TF), lambda i, k: (0, nk + k)),
            pl.BlockSpec((FFN_TF, D_MODEL), lambda i, k: (k, 0)),
        ],
        out_specs=pl.BlockSpec((FFN_TM, D_MODEL), lambda i, k: (i, 0)),
        out_shape=jax.ShapeDtypeStruct((t, D_MODEL), F32),
        compiler_params=pltpu.CompilerParams(
            dimension_semantics=("arbitrary", "arbitrary"), vmem_limit_bytes=FFN_VMEM_LIMIT),
        name="swiglu_ffn",
    )(h2, x1, w_in, w_in, w_out)


def kernel(x, norm_mix_g, w_in, gate_bias, q_norm_g, k_norm_g, attn_sinks, ssm_lambda_re, ssm_lambda_im, ssm_log_dt, ssm_b_re, ssm_b_im, ssm_c_re, ssm_c_im, ssm_d, ssm_glu_w, ssm_glu_b, w_attn_branch, w_ssm_branch, w_out, norm_ffn_g, w_ffn_in, w_ffn_out):
    batch, seq, _ = x.shape
    t = batch * seq
    n_levels = (seq // SUB).bit_length() - 1
    x2 = x.reshape(t, D_MODEL).astype(F32)
    f32 = lambda w: w.astype(F32)
    seg = _segment_ones()
    bias_tab = _attn_bias_tables()
    perm = _pair_permutation()
    w_in_f = f32(w_in)
    (tb, cmat, lev_r, lev_i), (w_in_bf,) = _ssm_prep(
        ssm_lambda_re, ssm_lambda_im, ssm_log_dt, ssm_b_re, ssm_b_im, ssm_c_re, ssm_c_im, ssm_d,
        n_levels, [(w_in_f, 0)])
    merge_weights = [f32(ssm_glu_w), f32(w_attn_branch), f32(w_ssm_branch), f32(w_out)]
    ffn_weights = [f32(w_ffn_in), f32(w_ffn_out)]
    for l in range(DEPTH):
        next_w_in = [(w_in_f, l + 1)] if l + 1 < DEPTH else []
        (uf, gates, ya), cast_out = _inproj_attn(
            x2, norm_mix_g[l].reshape(1, D_MODEL).astype(F32), w_in_bf,
            gate_bias[l].reshape(1, 2 * D_MODEL).astype(F32), seg,
            _qk_norm_tables(q_norm_g[l], k_norm_g[l]), attn_sinks[l].astype(F32), bias_tab, seq,
            [(w, l) for w in merge_weights] + next_w_in)
        wglu_bf, wa_bf, ws_bf, wo_bf = cast_out[:4]
        w_in_bf = cast_out[4] if next_w_in else None
        yg = _ssm(uf, perm, tb, cmat, lev_r, lev_i, l, batch, seq)
        (x1, h2), (wfi_bf, wfo_bf) = _merge(
            ya, yg, gates, x2, wglu_bf, ssm_glu_b[l].reshape(1, SSM_WIDTH).astype(F32),
            wa_bf, ws_bf, wo_bf, norm_ffn_g[l].reshape(1, D_MODEL).astype(F32),
            [(w, l) for w in ffn_weights])
        x2 = _ffn(h2, x1, wfi_bf, wfo_bf)
    return x2.reshape(batch, seq, D_MODEL).astype(x.dtype)
```

```python
import jax
import jax.numpy as jnp
from jax import lax
from jax.experimental import pallas as pl
from jax.experimental.pallas import tpu as pltpu

D_MODEL = 2048
DEPTH = 2
HEAD_DIM = 64
N_Q_HEADS = 16
N_KV_HEADS = 4
GQA_GROUP = N_Q_HEADS // N_KV_HEADS
ATTN_WIDTH = N_Q_HEADS * HEAD_DIM
KV_WIDTH = N_KV_HEADS * HEAD_DIM
WINDOW = 128
BLOCK = 128
SSM_WIDTH = D_MODEL // 2
SSM_GROUP_CH = 16
SSM_GROUPS = SSM_WIDTH // SSM_GROUP_CH
SSM_STATE = 64
D_FF = -(-8 * D_MODEL // (3 * 256)) * 256
OFF_K = ATTN_WIDTH
OFF_V = OFF_K + KV_WIDTH
OFF_U = OFF_V + KV_WIDTH
OFF_G = OFF_U + SSM_WIDTH
IN_WIDTH = OFF_G + 2 * D_MODEL
RMS_EPS = 1e-6

F32 = jnp.float32
BF16 = jnp.bfloat16
WORD = jnp.uint32
HIGHEST = lax.Precision.HIGHEST

LANES = 128
SUBLANES = 8
BF16_SUBLANES = 16
SUB = 16
SUB_W = SUB * SSM_GROUP_CH
STATE_W = 2 * SSM_STATE
TB_W = 2 * SUB_W + 2 * STATE_W
SLAB_GROUPS = LANES // SSM_GROUP_CH
SLAB_PAIRS = SLAB_GROUPS // 2
N_SLABS = SSM_GROUPS // SLAB_GROUPS
SLAB_W = SLAB_GROUPS * SUB_W

V7X_VMEM_BYTES = 64 * 1024 * 1024
VMEM_LIMIT = V7X_VMEM_BYTES * 7 // 8
FFN_VMEM_LIMIT = V7X_VMEM_BYTES * 15 // 16

INPROJ_TM = 256
INPROJ_TN = 512
MERGE_TM = 256
FFN_TM = 1024
FFN_TF = 512
FFN_SUB = 512


def _rms(x, g):
    return x * lax.rsqrt(jnp.mean(x * x, axis=-1, keepdims=True) + RMS_EPS) * g


def _sigmoid(x):
    return 0.5 * jnp.tanh(0.5 * x) + 0.5


class _RowCast:
    def __init__(self, stacked, layer, n_steps, step_of):
        _, rows, cols = stacked.shape
        blk = rows // n_steps
        assert blk * n_steps == rows and blk % BF16_SUBLANES == 0, (rows, n_steps)
        self.operand = stacked
        self.in_spec = pl.BlockSpec((None, blk, cols), lambda *ids: (layer, step_of(*ids), 0))
        self.out_spec = pl.BlockSpec((blk, cols), lambda *ids: (step_of(*ids), 0))
        self.out_shape = jax.ShapeDtypeStruct((rows, cols), BF16)


def _cast_blocks(src_refs, dst_refs):
    for src, dst in zip(src_refs, dst_refs):
        dst[...] = src[...].astype(BF16)


N_QKV_BLOCKS = OFF_U // INPROJ_TN
N_MAIN_BLOCKS = OFF_G // INPROJ_TN
N_IN_BLOCKS = IN_WIDTH // INPROJ_TN


QBLOCKS_PER_TILE = INPROJ_TM // BLOCK
ATTN_PROBLEMS_PER_GAP = 2


def _inproj_attn_kernel(sink_ref, x_ref, g_ref, w_ref, b_ref, seg_ref, nt_ref, bias0_ref, bias_ref, *refs):
    n_cast = (len(refs) - 5) // 2
    u_ref, gt_ref, ya_ref = refs[n_cast:n_cast + 3]
    q_ref, kv_ref = refs[2 * n_cast + 3:]

    @pl.when(pl.program_id(0) == 0)
    def _():
        kv_ref[...] = jnp.zeros(kv_ref.shape, BF16)

    kv_ref[0:BLOCK, :] = kv_ref[INPROJ_TM:INPROJ_TM + BLOCK, :]
    h = _rms(x_ref[...], g_ref[...]).astype(BF16)

    def project(j):
        return jnp.dot(h, w_ref[:, j * INPROJ_TN:(j + 1) * INPROJ_TN], preferred_element_type=F32)

    for j in range(N_QKV_BLOCKS):
        z = project(j)
        ssq = jnp.dot((z * z).astype(BF16), seg_ref[...], preferred_element_type=F32)
        inv = lax.rsqrt(ssq * (1.0 / HEAD_DIM) + RMS_EPS)
        fac = jnp.where(nt_ref[j, 1:2, :] > 0.0, inv, 1.0) * nt_ref[j, 0:1, :]
        if (j + 1) * INPROJ_TN <= ATTN_WIDTH:
            q_ref[:, j * INPROJ_TN:(j + 1) * INPROJ_TN] = (z * fac).astype(BF16)
        else:
            kv_ref[BLOCK:, :] = (z * fac).astype(BF16)

    def other_block(j):
        z = project(j)
        if j < N_MAIN_BLOCKS:
            c = j - N_QKV_BLOCKS
            u_ref[:, c * INPROJ_TN:(c + 1) * INPROJ_TN] = z
        else:
            cs = slice((j - N_MAIN_BLOCKS) * INPROJ_TN, (j - N_MAIN_BLOCKS + 1) * INPROJ_TN)
            gt_ref[:, cs] = _sigmoid(z + b_ref[:, cs]).astype(BF16)

    kv_rows = 2 * BLOCK
    all_rows = kv_ref.shape[0]
    left_kv = lax.broadcasted_iota(jnp.int32, (all_rows, LANES), 1) < HEAD_DIM
    left_q = lax.broadcasted_iota(jnp.int32, (BLOCK, LANES), 1) < HEAD_DIM
    zeros = jnp.zeros((all_rows, LANES), BF16)
    left_ones = lax.broadcasted_iota(jnp.int32, (kv_rows, LANES), 1) < HEAD_DIM
    ones_l = jnp.where(left_ones, 1.0, 0.0).astype(BF16)
    ones_r = jnp.where(left_ones, 0.0, 1.0).astype(BF16)
    contract_lanes = (((1,), (1,)), ((), ()))

    def slab(off):
        a = kv_ref[:, off:off + LANES]
        return a, pltpu.roll(a.astype(F32), HEAD_DIM, axis=1).astype(BF16)

    problems = []
    for c in range(N_KV_HEADS // 2):
        k_slabs = slab(c * LANES)
        v_slabs = slab(KV_WIDTH + c * LANES)
        for side in range(2):
            kh = 2 * c + side
            k_l = jnp.where(left_kv, k_slabs[side], zeros)
            k_r = jnp.where(left_kv, zeros, k_slabs[1 - side])
            v_l = jnp.where(left_kv, v_slabs[side], zeros)
            v_r = jnp.where(left_kv, zeros, v_slabs[1 - side])
            for qb in range(QBLOCKS_PER_TILE):
                rows = slice(qb * BLOCK, qb * BLOCK + kv_rows)
                kk = jnp.concatenate([k_l[rows], k_r[rows]], axis=0)
                vv = jnp.concatenate([jnp.concatenate([v_l[rows], ones_l], axis=1),
                                      jnp.concatenate([v_r[rows], ones_r], axis=1)], axis=0)
                for pair in range(GQA_GROUP // 2):
                    problems.append((qb, kh * GQA_GROUP + 2 * pair, kk, vv))

    def scores(qb, e, kk):
        q2 = q_ref[qb * BLOCK:(qb + 1) * BLOCK, e * HEAD_DIM:(e + 2) * HEAD_DIM]
        return lax.dot_general(q2, kk, contract_lanes, preferred_element_type=F32)

    def finish(qb, e, s2, vv):
        bias = bias0_ref if qb == 0 else bias_ref
        ps, ds = [], []
        for t in range(2):
            s = s2[:, t * kv_rows:(t + 1) * kv_rows] + bias[e + t]
            sink = sink_ref[e + t]
            m = jnp.maximum(jnp.max(s, axis=-1, keepdims=True), sink)
            ps.append(jnp.exp(s - m).astype(BF16))
            ds.append(jnp.exp(sink - m))
        r = jnp.dot(jnp.concatenate(ps, axis=1), vv, preferred_element_type=F32)
        denom = r[:, LANES:] + jnp.where(left_q, ds[0], ds[1])
        ya_ref[qb * BLOCK:(qb + 1) * BLOCK, e * HEAD_DIM:(e + 2) * HEAD_DIM] = (r[:, :LANES] / denom).astype(BF16)

    others = list(range(N_QKV_BLOCKS, N_IN_BLOCKS))
    for first in range(0, len(problems), ATTN_PROBLEMS_PER_GAP):
        group = problems[first:first + ATTN_PROBLEMS_PER_GAP]
        s2s = [scores(qb, e, kk) for qb, e, kk, _ in group]
        if others:
            other_block(others.pop(0))
        for (qb, e, _, vv), s2 in zip(group, s2s):
            finish(qb, e, s2, vv)
    for j in others:
        other_block(j)
    _cast_blocks(refs[:n_cast], refs[n_cast + 3:2 * n_cast + 3])


def _inproj_attn(x2, gain, w_bf, bias, seg, ntab, sinks, bias_tab, seq, cast_weights):
    t = x2.shape[0]
    tm = INPROJ_TM
    tiles_per_seq = seq // tm
    const2 = lambda i: (0, 0)
    bias_shape = (None, N_Q_HEADS, BLOCK, 2 * BLOCK)
    casts = [_RowCast(w, wl, t // tm, lambda i: i) for w, wl in cast_weights]
    outs = pl.pallas_call(
        _inproj_attn_kernel,
        grid=(t // tm,),
        in_specs=[
            pl.BlockSpec(memory_space=pltpu.SMEM),
            pl.BlockSpec((tm, D_MODEL), lambda i: (i, 0)),
            pl.BlockSpec((1, D_MODEL), const2),
            pl.BlockSpec((D_MODEL, IN_WIDTH), const2, pipeline_mode=pl.Buffered(1)),
            pl.BlockSpec((1, 2 * D_MODEL), const2),
            pl.BlockSpec((INPROJ_TN, INPROJ_TN), const2),
            pl.BlockSpec((N_QKV_BLOCKS, 2, INPROJ_TN), lambda i: (0, 0, 0)),
            pl.BlockSpec(bias_shape, lambda i: (jnp.minimum(i % tiles_per_seq, 1), 0, 0, 0)),
            pl.BlockSpec(bias_shape, lambda i: (1, 0, 0, 0), pipeline_mode=pl.Buffered(1)),
        ] + [c.in_spec for c in casts],
        out_specs=[
            pl.BlockSpec((tm, SSM_WIDTH), lambda i: (i, 0)),
            pl.BlockSpec((tm, 2 * D_MODEL), lambda i: (i, 0)),
            pl.BlockSpec((tm, ATTN_WIDTH), lambda i: (i, 0)),
        ] + [c.out_spec for c in casts],
        out_shape=[
            jax.ShapeDtypeStruct((t, SSM_WIDTH), F32),
            jax.ShapeDtypeStruct((t, 2 * D_MODEL), BF16),
            jax.ShapeDtypeStruct((t, ATTN_WIDTH), BF16),
        ] + [c.out_shape for c in casts],
        scratch_shapes=[pltpu.VMEM((tm, ATTN_WIDTH), BF16), pltpu.VMEM((BLOCK + tm, 2 * KV_WIDTH), BF16)],
        compiler_params=pltpu.CompilerParams(
            dimension_semantics=("arbitrary",), vmem_limit_bytes=VMEM_LIMIT),
        name="inproj_attn",
    )(sinks, x2, gain, w_bf, bias, seg, ntab, bias_tab, bias_tab, *[c.operand for c in casts])
    return outs[:3], outs[3:]


def _qk_norm_tables(q_gain, k_gain):
    qrow = jnp.tile(q_gain.astype(F32), INPROJ_TN // HEAD_DIM) * (HEAD_DIM ** -0.5)
    ones = jnp.ones((INPROJ_TN,), F32)
    kvrow = jnp.concatenate([jnp.tile(k_gain.astype(F32), N_KV_HEADS), jnp.ones((KV_WIDTH,), F32)])
    kvmask = jnp.concatenate([jnp.ones((KV_WIDTH,), F32), jnp.zeros((KV_WIDTH,), F32)])
    return jnp.stack([jnp.stack([qrow, ones]), jnp.stack([qrow, ones]), jnp.stack([kvrow, kvmask])])


def _segment_ones():
    r = jnp.arange(INPROJ_TN) // HEAD_DIM
    return (r[:, None] == r[None, :]).astype(BF16)


def _attn_bias_tables():
    t_loc = jnp.arange(BLOCK)[:, None]
    s_loc = jnp.arange(2 * BLOCK)[None, :] - BLOCK
    dist = (t_loc - s_loc).astype(F32)
    valid = (dist >= 0) & (dist < WINDOW)
    slopes = jnp.exp2(-8.0 * jnp.arange(1, N_Q_HEADS + 1, dtype=F32) / N_Q_HEADS)
    bias = -slopes[:, None, None] * dist[None]
    full = jnp.where(valid[None], bias, -jnp.inf)
    first = jnp.where((valid & (s_loc >= 0))[None], bias, -jnp.inf)
    return jnp.stack([first, full])


def _cmul(ar, ai, br, bi):
    return ar * br - ai * bi, ar * bi + ai * br


def _pow_by_bits(exps, squares):
    pr = pi = None
    for b, (sr, si) in enumerate(squares):
        on = ((exps >> b) & 1) == 1
        fr, fi = jnp.where(on, sr, 1.0), jnp.where(on, si, 0.0)
        pr, pi = (fr, fi) if pr is None else _cmul(pr, pi, fr, fi)
    return pr, pi


def _ssm_prep_kernel(lam_ref, ldt_ref, btr_ref, bti_ref, ctr_ref, cti_ref, d_ref, *refs):
    n_cast = (len(refs) - 4) // 2
    tb_ref, c_ref, levr_ref, levi_ref = refs[n_cast:n_cast + 4]
    _cast_blocks(refs[:n_cast], refs[n_cast + 4:])
    n_levels = levr_ref.shape[1]
    tau_lane = lax.broadcasted_iota(jnp.int32, (SSM_STATE, SUB_W), 1) // SSM_GROUP_CH
    row_h = lax.broadcasted_iota(jnp.int32, (SSM_GROUP_CH, SUB_W), 0)
    lane_h = lax.broadcasted_iota(jnp.int32, (SSM_GROUP_CH, SUB_W), 1)
    tile_ch = (lane_h % SSM_GROUP_CH == row_h).astype(F32)
    row_m = lax.broadcasted_iota(jnp.int32, (SUB, SSM_STATE), 0)
    eye_p = (lax.broadcasted_iota(jnp.int32, (SSM_STATE, SSM_STATE), 0)
             == lax.broadcasted_iota(jnp.int32, (SSM_STATE, SSM_STATE), 1))

    def to_col(v):
        return jnp.sum(jnp.where(eye_p, v, 0.0), axis=1, keepdims=True)

    def squares(a, n):
        out = [a]
        for _ in range(n - 1):
            out.append(_cmul(*out[-1], *out[-1]))
        return out

    for g in range(SLAB_GROUPS):
        dt = jnp.exp(ldt_ref[g])

        def discretise(lr, li):
            mag = jnp.exp(lr * dt)
            return mag * jnp.cos(li * dt), mag * jnp.sin(li * dt)

        lr_row, li_row = lam_ref[g, 0:1, :], lam_ref[g, 1:2, :]
        a_row = discretise(lr_row, li_row)
        sq_row = squares(a_row, 5)
        sq_col = squares((to_col(a_row[0]), to_col(a_row[1])), 4)
        ar, ai = sq_row[0]
        den = lr_row * lr_row + li_row * li_row
        fr = ((ar - 1.0) * lr_row + ai * li_row) / den
        fi = (ai * lr_row - (ar - 1.0) * li_row) / den
        bbr, bbi = _cmul(fr, fi, btr_ref[g], bti_ref[g])

        e0 = _pow_by_bits(tau_lane, sq_col)
        e1 = _cmul(*e0, *sq_col[0])
        ctr = jnp.dot(ctr_ref[g], tile_ch, precision=HIGHEST, preferred_element_type=F32)
        cti = jnp.dot(cti_ref[g], tile_ch, precision=HIGHEST, preferred_element_type=F32)
        mr, mi = _cmul(*e0, ctr, cti)
        kt = (jnp.dot(bbr, mr, precision=HIGHEST, preferred_element_type=F32)
              - jnp.dot(bbi, mi, precision=HIGHEST, preferred_element_type=F32))
        kt = kt + jnp.where(lane_h == row_h, d_ref[g], 0.0)
        pw = _pow_by_bits(row_m, sq_row[:4])
        q, side = g // 2, g % 2
        own = lambda width: slice(side * width, (side + 1) * width)
        other = lambda width: slice((1 - side) * width, (2 - side) * width)
        zeros_p = jnp.zeros((SSM_GROUP_CH, SSM_STATE), F32)

        def paired(v):
            return jnp.concatenate([v, zeros_p] if side == 0 else [zeros_p, v], axis=1).astype(BF16)

        for j in range(SUB):
            rows = pl.ds(side * SUB_W + j * SSM_GROUP_CH, SSM_GROUP_CH)
            tj = kt if j == 0 else jnp.where(lane_h >= j * SSM_GROUP_CH,
                                             pltpu.roll(kt, j * SSM_GROUP_CH, axis=1), 0.0)
            tb_ref[q, rows, own(SUB_W)] = tj.astype(BF16)
            tb_ref[q, rows, other(SUB_W)] = jnp.zeros((SSM_GROUP_CH, SUB_W), BF16)
            m = SUB - 1 - j
            br_, bi_ = _cmul(pw[0][m:m + 1, :], pw[1][m:m + 1, :], bbr, bbi)
            tb_ref[q, rows, 2 * SUB_W:2 * SUB_W + STATE_W] = paired(br_)
            tb_ref[q, rows, 2 * SUB_W + STATE_W:TB_W] = paired(bi_)
        for part, val in enumerate([ctr * e1[0] - cti * e1[1], -ctr * e1[1] - cti * e1[0]]):
            rows = pl.ds(part * STATE_W + side * SSM_STATE, SSM_STATE)
            c_ref[q, rows, own(SUB_W)] = val.astype(BF16)
            c_ref[q, rows, other(SUB_W)] = jnp.zeros((SSM_STATE, SUB_W), BF16)
        lv = sq_row[4]
        lev_r, lev_i = [], []
        for _ in range(n_levels):
            lev_r.append(lv[0])
            lev_i.append(lv[1])
            lv = _cmul(*lv, *lv)
        levels = (jnp.concatenate(lev_r, axis=0), jnp.concatenate(lev_i, axis=0))
        if side == 0:
            first_levels = levels
        else:
            levr_ref[q] = jnp.concatenate([first_levels[0], levels[0]], axis=1)
            levi_ref[q] = jnp.concatenate([first_levels[1], levels[1]], axis=1)


def _ssm_prep(lam_re, lam_im, log_dt, b_re, b_im, c_re, c_im, d_skip, n_levels, cast_weights):
    dg = lam_re.shape[0] * SSM_GROUPS
    casts = [_RowCast(w, wl, dg // SLAB_GROUPS, lambda s: s) for w, wl in cast_weights]
    h_, p_ = SSM_GROUP_CH, SSM_STATE
    f = lambda a: a.astype(F32)
    lam_rows = jnp.stack([f(lam_re), f(lam_im)], axis=2).reshape(dg, 2, p_)
    btr = f(b_re).transpose(0, 1, 3, 2).reshape(dg, h_, p_)
    bti = f(b_im).transpose(0, 1, 3, 2).reshape(dg, h_, p_)
    ctr = f(c_re).transpose(0, 1, 3, 2).reshape(dg, p_, h_)
    cti = f(c_im).transpose(0, 1, 3, 2).reshape(dg, p_, h_)
    d_rows = jnp.pad(f(d_skip).reshape(dg, 1, h_), ((0, 0), (0, 0), (0, SUB_W - h_)))
    grp = lambda *shape: pl.BlockSpec((SLAB_GROUPS,) + shape, lambda s: (s,) + (0,) * len(shape))
    pair = lambda *shape: pl.BlockSpec((SLAB_PAIRS,) + shape, lambda s: (s,) + (0,) * len(shape))
    outs = pl.pallas_call(
        _ssm_prep_kernel,
        grid=(dg // SLAB_GROUPS,),
        in_specs=[grp(2, p_), grp(1, 1), grp(h_, p_), grp(h_, p_),
                  grp(p_, h_), grp(p_, h_), grp(1, SUB_W)] + [c.in_spec for c in casts],
        out_specs=[pair(2 * SUB_W, TB_W), pair(2 * STATE_W, 2 * SUB_W), pair(n_levels, STATE_W),
                   pair(n_levels, STATE_W)] + [c.out_spec for c in casts],
        out_shape=[
            jax.ShapeDtypeStruct((dg // 2, 2 * SUB_W, TB_W), BF16),
            jax.ShapeDtypeStruct((dg // 2, 2 * STATE_W, 2 * SUB_W), BF16),
            jax.ShapeDtypeStruct((dg // 2, n_levels, STATE_W), F32),
            jax.ShapeDtypeStruct((dg // 2, n_levels, STATE_W), F32),
        ] + [c.out_shape for c in casts],
        compiler_params=pltpu.CompilerParams(
            dimension_semantics=("arbitrary",), vmem_limit_bytes=VMEM_LIMIT),
        name="s5_prep",
    )(lam_rows, f(log_dt).reshape(dg, 1, 1), btr, bti, ctr, cti, d_rows, *[c.operand for c in casts])
    return outs[:4], outs[4:]


CHUNK = 2 * SSM_GROUP_CH
CHUNKS = LANES // CHUNK
PAIRS = SUB // 2
PAIR_W = 2 * LANES


def _pair_permutation():
    r = jnp.arange(PAIR_W)
    jj, lane = r // LANES, r % LANES
    col = (lane // SSM_GROUP_CH) * CHUNK + jj * SSM_GROUP_CH + lane % SSM_GROUP_CH
    return (col[:, None] == jnp.arange(PAIR_W)[None, :]).astype(BF16)


S5_STAGES = 3


def _ssm_kernel(u_ref, perm_ref, perm_t_ref, tb_ref, c_ref, ar_ref, ai_ref, y_ref, xs_ref, ys_ref, cs_ref):
    t = pl.program_id(0)
    rows = 2 * xs_ref.shape[1]
    n_levels = rows.bit_length() - 1
    cur, prv = t % 2, (t + 1) % 2

    @pl.when(t == 0)
    def _():
        xs_ref[...] = jnp.zeros(xs_ref.shape, WORD)
        ys_ref[...] = jnp.zeros(ys_ref.shape, WORD)

    def shifted(a, sh):
        return jnp.concatenate([jnp.zeros((sh, a.shape[1]), a.dtype), a[:rows - sh]], axis=0)

    def rotations(a):
        return [a] + [pltpu.roll(a, r * CHUNK, axis=1) for r in range(1, CHUNKS)]

    def chunk(a, c):
        return a[:, c * CHUNK:(c + 1) * CHUNK]

    def to_words(a):
        return pltpu.bitcast(a.astype(BF16), WORD)

    def from_words(a):
        return pltpu.bitcast(a, BF16)

    def scan_inputs(q):
        return jnp.dot(from_words(xs_ref[prv, :, q * 2 * SUB_W:(q + 1) * 2 * SUB_W]), tb_ref[q],
                       preferred_element_type=F32)

    def layout_out_chunks(k):
        col, dst = k // CHUNKS, k % CHUNKS
        yk = ys_ref[cur, :, k * SUB_W:(k + 1) * SUB_W]
        for v in range(SUB_W // LANES):
            rots = rotations(yk[:, v * LANES:(v + 1) * LANES])
            for src in range(CHUNKS):
                lane0 = (v * CHUNKS + src) * PAIR_W + col * LANES + dst * CHUNK
                cs_ref[:, lane0:lane0 + CHUNK] = chunk(rots[(dst - src) % CHUNKS], dst)

    def layout_in_chunks(k):
        col, dst = k // CHUNKS, k % CHUNKS
        pair_tile = jnp.concatenate([u_ref[pl.ds(2 * k, rows, stride=SUB), :],
                                     u_ref[pl.ds(2 * k + 1, rows, stride=SUB), :]], axis=1).astype(BF16)
        grouped = to_words(jnp.dot(pair_tile, perm_ref[...], preferred_element_type=F32))
        for w in range(PAIR_W // LANES):
            rots = rotations(grouped[:, w * LANES:(w + 1) * LANES])
            for src in range(CHUNKS):
                lane0 = (w * CHUNKS + src) * SUB_W + col * LANES + dst * CHUNK
                xs_ref[cur, :, lane0:lane0 + CHUNK] = chunk(rots[(dst - src) % CHUNKS], dst)

    r_next = scan_inputs(0)
    for q in range(SLAB_PAIRS):
        r = r_next
        if q + 1 < SLAB_PAIRS:
            r_next = scan_inputs(q + 1)
        for k in (2 * q, 2 * q + 1):
            layout_out_chunks(k)
            layout_in_chunks(k)
        yt = r[:, :2 * SUB_W]
        re = r[:, 2 * SUB_W:2 * SUB_W + STATE_W]
        im = r[:, 2 * SUB_W + STATE_W:]
        for lvl in range(n_levels):
            sh = 1 << lvl
            ar = ar_ref[q, lvl:lvl + 1, :]
            ai = ai_ref[q, lvl:lvl + 1, :]
            if sh % SUBLANES:
                pr, pi = shifted(re, sh), shifted(im, sh)
                re, im = re + ar * pr - ai * pi, im + ar * pi + ai * pr
            else:
                pr, pi = re[:rows - sh], im[:rows - sh]
                re, im = (jnp.concatenate([re[:sh], re[sh:] + ar * pr - ai * pi], axis=0),
                          jnp.concatenate([im[:sh], im[sh:] + ar * pi + ai * pr], axis=0))
        prev_state = jnp.concatenate([shifted(re, 1), shifted(im, 1)], axis=1).astype(BF16)
        y = yt + jnp.dot(prev_state, c_ref[q], preferred_element_type=F32)
        ys_ref[prv, :, q * 2 * SUB_W:(q + 1) * 2 * SUB_W] = to_words(jax.nn.gelu(y))

    for pair in range(PAIRS):
        o = jnp.dot(from_words(cs_ref[:, pair * PAIR_W:(pair + 1) * PAIR_W]), perm_t_ref[...],
                    preferred_element_type=F32)
        y_ref[pl.ds(2 * pair, rows, stride=SUB), :] = o[:, :LANES]
        y_ref[pl.ds(2 * pair + 1, rows, stride=SUB), :] = o[:, LANES:]


def _ssm(uf, perm, tb, cmat, lev_r, lev_i, layer, batch, seq):
    t = uf.shape[0]
    n_sub = seq // SUB
    n_levels = lev_r.shape[1]
    n_items = batch * N_SLABS
    item = lambda t, stage: jnp.clip(t - stage, 0, n_items - 1)
    slab3 = lambda t: (layer * N_SLABS + item(t, 1) % N_SLABS, 0, 0)
    return pl.pallas_call(
        _ssm_kernel,
        grid=(n_items + S5_STAGES - 1,),
        in_specs=[
            pl.BlockSpec((seq, LANES), lambda t: (item(t, 0) // N_SLABS, item(t, 0) % N_SLABS)),
            pl.BlockSpec((PAIR_W, PAIR_W), lambda t: (0, 0)),
            pl.BlockSpec((PAIR_W, PAIR_W), lambda t: (0, 0)),
            pl.BlockSpec((SLAB_PAIRS, 2 * SUB_W, TB_W), slab3),
            pl.BlockSpec((SLAB_PAIRS, 2 * STATE_W, 2 * SUB_W), slab3),
            pl.BlockSpec((SLAB_PAIRS, n_levels, STATE_W), slab3),
            pl.BlockSpec((SLAB_PAIRS, n_levels, STATE_W), slab3),
        ],
        out_specs=pl.BlockSpec((seq, LANES), lambda t: (item(t, 2) // N_SLABS, item(t, 2) % N_SLABS)),
        out_shape=jax.ShapeDtypeStruct((t, SSM_WIDTH), F32),
        scratch_shapes=[pltpu.VMEM((2, n_sub // 2, SLAB_W), WORD), pltpu.VMEM((2, n_sub // 2, SLAB_W), WORD),
                        pltpu.VMEM((n_sub // 2, SLAB_W), WORD)],
        compiler_params=pltpu.CompilerParams(
            dimension_semantics=("arbitrary",), vmem_limit_bytes=VMEM_LIMIT),
        name="s5_scan",
    )(uf, perm, perm.T, tb, cmat, lev_r, lev_i)


def _merge_kernel(ya_ref, yg_ref, gt_ref, x_ref, wglu_ref, bglu_ref, wa_ref, ws_ref, wo_ref,
                  gffn_ref, *refs):
    n_cast = (len(refs) - 2) // 2
    x1_ref, h2_ref = refs[n_cast:n_cast + 2]
    yg = yg_ref[...]
    t = jnp.dot(yg.astype(BF16), wglu_ref[...], preferred_element_type=F32) + bglu_ref[...]
    ys = (yg * _sigmoid(t)).astype(BF16)
    ma = jnp.dot(ya_ref[...], wa_ref[...], preferred_element_type=F32)
    ms = jnp.dot(ys, ws_ref[...], preferred_element_type=F32)
    merged = gt_ref[:, :D_MODEL].astype(F32) * ma + gt_ref[:, D_MODEL:].astype(F32) * ms
    x1 = x_ref[...] + jnp.dot(merged.astype(BF16), wo_ref[...], preferred_element_type=F32)
    x1_ref[...] = x1
    h2_ref[...] = _rms(x1, gffn_ref[...]).astype(BF16)
    _cast_blocks(refs[:n_cast], refs[n_cast + 2:])


def _merge(ya, yg, gates, x2, wglu, bglu, wa, ws, wo, gffn, cast_weights):
    t = x2.shape[0]
    tm = MERGE_TM
    casts = [_RowCast(w, wl, t // tm, lambda i: i) for w, wl in cast_weights]

    def wspec(rows, cols):
        return pl.BlockSpec((rows, cols), lambda i: (0, 0), pipeline_mode=pl.Buffered(1))

    outs = pl.pallas_call(
        _merge_kernel,
        grid=(t // tm,),
        in_specs=[
            pl.BlockSpec((tm, ATTN_WIDTH), lambda i: (i, 0)),
            pl.BlockSpec((tm, SSM_WIDTH), lambda i: (i, 0)),
            pl.BlockSpec((tm, 2 * D_MODEL), lambda i: (i, 0)),
            pl.BlockSpec((tm, D_MODEL), lambda i: (i, 0)),
            wspec(SSM_WIDTH, SSM_WIDTH),
            pl.BlockSpec((1, SSM_WIDTH), lambda i: (0, 0)),
            wspec(ATTN_WIDTH, D_MODEL),
            wspec(SSM_WIDTH, D_MODEL),
            wspec(D_MODEL, D_MODEL),
            pl.BlockSpec((1, D_MODEL), lambda i: (0, 0)),
        ] + [c.in_spec for c in casts],
        out_specs=[
            pl.BlockSpec((tm, D_MODEL), lambda i: (i, 0)),
            pl.BlockSpec((tm, D_MODEL), lambda i: (i, 0)),
        ] + [c.out_spec for c in casts],
        out_shape=[
            jax.ShapeDtypeStruct((t, D_MODEL), F32),
            jax.ShapeDtypeStruct((t, D_MODEL), BF16),
        ] + [c.out_shape for c in casts],
        compiler_params=pltpu.CompilerParams(
            dimension_semantics=("arbitrary",), vmem_limit_bytes=VMEM_LIMIT),
        name="merge_out",
    )(ya, yg, gates, x2, wglu, bglu, wa, ws, wo, gffn, *[c.operand for c in casts])
    return outs[:2], outs[2:]


def _ffn_kernel(h_ref, x_ref, wg_ref, wu_ref, wo_ref, o_ref):
    k = pl.program_id(1)

    def accumulate(base_ref):
        for r in range(FFN_TM // FFN_SUB):
            rs = pl.ds(r * FFN_SUB, FFN_SUB)
            h = h_ref[rs, :]
            g = jnp.dot(h, wg_ref[...], preferred_element_type=F32)
            u = jnp.dot(h, wu_ref[...], preferred_element_type=F32)
            act = (g * _sigmoid(g) * u).astype(BF16)
            o_ref[rs, :] = base_ref[rs, :] + jnp.dot(act, wo_ref[...], preferred_element_type=F32)

    @pl.when(k == 0)
    def _():
        accumulate(x_ref)

    @pl.when(k > 0)
    def _():
        accumulate(o_ref)


def _ffn(h2, x1, w_in, w_out):
    t = x1.shape[0]
    nk = D_FF // FFN_TF
    return pl.pallas_call(
        _ffn_kernel,
        grid=(t // FFN_TM, nk),
        in_specs=[
            pl.BlockSpec((FFN_TM, D_MODEL), lambda i, k: (i, 0)),
            pl.BlockSpec((FFN_TM, D_MODEL), lambda i, k: (i, 0)),
            pl.BlockSpec((D_MODEL, FFN_TF), lambda i, k: (0, k)),
            pl.BlockSpec((D_MODEL, FFN_TF), lambda i, k: (0, nk + k)),
            pl.BlockSpec((FFN_TF, D_MODEL), lambda i, k: (k, 0)),
        ],
        out_specs=pl.BlockSpec((FFN_TM, D_MODEL), lambda i, k: (i, 0)),
        out_shape=jax.ShapeDtypeStruct((t, D_MODEL), F32),
        compiler_params=pltpu.CompilerParams(
            dimension_semantics=("arbitrary", "arbitrary"), vmem_limit_bytes=FFN_VMEM_LIMIT),
        name="swiglu_ffn",
    )(h2, x1, w_in, w_in, w_out)


def kernel(x, norm_mix_g, w_in, gate_bias, q_norm_g, k_norm_g, attn_sinks, ssm_lambda_re, ssm_lambda_im, ssm_log_dt, ssm_b_re, ssm_b_im, ssm_c_re, ssm_c_im, ssm_d, ssm_glu_w, ssm_glu_b, w_attn_branch, w_ssm_branch, w_out, norm_ffn_g, w_ffn_in, w_ffn_out):
    batch, seq, _ = x.shape
    t = batch * seq
    n_levels = (seq // SUB).bit_length() - 1
    x2 = x.reshape(t, D_MODEL).astype(F32)
    f32 = lambda w: w.astype(F32)
    seg = _segment_ones()
    bias_tab = _attn_bias_tables()
    perm = _pair_permutation()
    w_in_f = f32(w_in)
    (tb, cmat, lev_r, lev_i), (w_in_bf,) = _ssm_prep(
        ssm_lambda_re, ssm_lambda_im, ssm_log_dt, ssm_b_re, ssm_b_im, ssm_c_re, ssm_c_im, ssm_d,
        n_levels, [(w_in_f, 0)])
    merge_weights = [f32(ssm_glu_w), f32(w_attn_branch), f32(w_ssm_branch), f32(w_out)]
    ffn_weights = [f32(w_ffn_in), f32(w_ffn_out)]
    for l in range(DEPTH):
        next_w_in = [(w_in_f, l + 1)] if l + 1 < DEPTH else []
        (uf, gates, ya), cast_out = _inproj_attn(
            x2, norm_mix_g[l].reshape(1, D_MODEL).astype(F32), w_in_bf,
            gate_bias[l].reshape(1, 2 * D_MODEL).astype(F32), seg,
            _qk_norm_tables(q_norm_g[l], k_norm_g[l]), attn_sinks[l].astype(F32), bias_tab, seq,
            [(w, l) for w in merge_weights] + next_w_in)
        wglu_bf, wa_bf, ws_bf, wo_bf = cast_out[:4]
        w_in_bf = cast_out[4] if next_w_in else None
        yg = _ssm(uf, perm, tb, cmat, lev_r, lev_i, l, batch, seq)
        (x1, h2), (wfi_bf, wfo_bf) = _merge(
            ya, yg, gates, x2, wglu_bf, ssm_glu_b[l].reshape(1, SSM_WIDTH).astype(F32),
            wa_bf, ws_bf, wo_bf, norm_ffn_g[l].reshape(1, D_MODEL).astype(F32),
            [(w, l) for w in ffn_weights])
        x2 = _ffn(h2, x1, wfi_bf, wfo_bf)
    return x2.reshape(batch, seq, D_MODEL).astype(x.dtype)
```

```python
import jax
import jax.numpy as jnp
from jax import lax
from jax.experimental import pallas as pl
from jax.experimental.pallas import tpu as pltpu

D_MODEL = 2048
DEPTH = 2
HEAD_DIM = 64
N_Q_HEADS = 16
N_KV_HEADS = 4
GQA_GROUP = N_Q_HEADS // N_KV_HEADS
ATTN_WIDTH = N_Q_HEADS * HEAD_DIM
KV_WIDTH = N_KV_HEADS * HEAD_DIM
WINDOW = 128
BLOCK = 128
SSM_WIDTH = D_MODEL // 2
SSM_GROUP_CH = 16
SSM_GROUPS = SSM_WIDTH // SSM_GROUP_CH
SSM_STATE = 64
D_FF = -(-8 * D_MODEL // (3 * 256)) * 256
OFF_K = ATTN_WIDTH
OFF_V = OFF_K + KV_WIDTH
OFF_U = OFF_V + KV_WIDTH
OFF_G = OFF_U + SSM_WIDTH
IN_WIDTH = OFF_G + 2 * D_MODEL
RMS_EPS = 1e-6

F32 = jnp.float32
BF16 = jnp.bfloat16
WORD = jnp.uint32
HIGHEST = lax.Precision.HIGHEST

LANES = 128
SUBLANES = 8
BF16_SUBLANES = 16
SUB = 16
SUB_W = SUB * SSM_GROUP_CH
STATE_W = 2 * SSM_STATE
TB_W = 2 * SUB_W + 2 * STATE_W
SLAB_GROUPS = LANES // SSM_GROUP_CH
SLAB_PAIRS = SLAB_GROUPS // 2
N_SLABS = SSM_GROUPS // SLAB_GROUPS
SLAB_W = SLAB_GROUPS * SUB_W

V7X_VMEM_BYTES = 64 * 1024 * 1024
VMEM_LIMIT = V7X_VMEM_BYTES * 7 // 8
FFN_VMEM_LIMIT = V7X_VMEM_BYTES * 15 // 16

INPROJ_TM = 256
INPROJ_TN = 512
MERGE_TM = 256
FFN_TM = 1024
FFN_TF = 512
FFN_SUB = 512


def _rms(x, g):
    return x * lax.rsqrt(jnp.mean(x * x, axis=-1, keepdims=True) + RMS_EPS) * g


def _sigmoid(x):
    return 0.5 * jnp.tanh(0.5 * x) + 0.5


class _RowCast:
    def __init__(self, stacked, layer, n_steps, step_of):
        _, rows, cols = stacked.shape
        blk = rows // n_steps
        assert blk * n_steps == rows and blk % BF16_SUBLANES == 0, (rows, n_steps)
        self.operand = stacked
        self.in_spec = pl.BlockSpec((None, blk, cols), lambda *ids: (layer, step_of(*ids), 0))
        self.out_spec = pl.BlockSpec((blk, cols), lambda *ids: (step_of(*ids), 0))
        self.out_shape = jax.ShapeDtypeStruct((rows, cols), BF16)


def _cast_blocks(src_refs, dst_refs):
    for src, dst in zip(src_refs, dst_refs):
        dst[...] = src[...].astype(BF16)


N_QKV_BLOCKS = OFF_U // INPROJ_TN
N_MAIN_BLOCKS = OFF_G // INPROJ_TN
N_IN_BLOCKS = IN_WIDTH // INPROJ_TN


QBLOCKS_PER_TILE = INPROJ_TM // BLOCK
ATTN_PROBLEMS_PER_GAP = 2


def _inproj_attn_kernel(sink_ref, x_ref, g_ref, w_ref, b_ref, seg_ref, nt_ref, bias0_ref, bias_ref, *refs):
    n_cast = (len(refs) - 5) // 2
    u_ref, gt_ref, ya_ref = refs[n_cast:n_cast + 3]
    q_ref, kv_ref = refs[2 * n_cast + 3:]

    @pl.when(pl.program_id(0) == 0)
    def _():
        kv_ref[...] = jnp.zeros(kv_ref.shape, BF16)

    kv_ref[0:BLOCK, :] = kv_ref[INPROJ_TM:INPROJ_TM + BLOCK, :]
    h = _rms(x_ref[...], g_ref[...]).astype(BF16)

    def project(j):
        return jnp.dot(h, w_ref[:, j * INPROJ_TN:(j + 1) * INPROJ_TN], preferred_element_type=F32)

    for j in range(N_QKV_BLOCKS):
        z = project(j)
        ssq = jnp.dot((z * z).astype(BF16), seg_ref[...], preferred_element_type=F32)
        inv = lax.rsqrt(ssq * (1.0 / HEAD_DIM) + RMS_EPS)
        fac = jnp.where(nt_ref[j, 1:2, :] > 0.0, inv, 1.0) * nt_ref[j, 0:1, :]
        if (j + 1) * INPROJ_TN <= ATTN_WIDTH:
            q_ref[:, j * INPROJ_TN:(j + 1) * INPROJ_TN] = (z * fac).astype(BF16)
        else:
            kv_ref[BLOCK:, :] = (z * fac).astype(BF16)

    def other_block(j):
        z = project(j)
        if j < N_MAIN_BLOCKS:
            c = j - N_QKV_BLOCKS
            u_ref[:, c * INPROJ_TN:(c + 1) * INPROJ_TN] = z
        else:
            cs = slice((j - N_MAIN_BLOCKS) * INPROJ_TN, (j - N_MAIN_BLOCKS + 1) * INPROJ_TN)
            gt_ref[:, cs] = _sigmoid(z + b_ref[:, cs]).astype(BF16)

    kv_rows = 2 * BLOCK
    all_rows = kv_ref.shape[0]
    left_kv = lax.broadcasted_iota(jnp.int32, (all_rows, LANES), 1) < HEAD_DIM
    left_q = lax.broadcasted_iota(jnp.int32, (BLOCK, LANES), 1) < HEAD_DIM
    zeros = jnp.zeros((all_rows, LANES), BF16)
    left_ones = lax.broadcasted_iota(jnp.int32, (kv_rows, LANES), 1) < HEAD_DIM
    ones_l = jnp.where(left_ones, 1.0, 0.0).astype(BF16)
    ones_r = jnp.where(left_ones, 0.0, 1.0).astype(BF16)
    contract_lanes = (((1,), (1,)), ((), ()))

    def slab(off):
        a = kv_ref[:, off:off + LANES]
        return a, pltpu.roll(a.astype(F32), HEAD_DIM, axis=1).astype(BF16)

    problems = []
    for c in range(N_KV_HEADS // 2):
        k_slabs = slab(c * LANES)
        v_slabs = slab(KV_WIDTH + c * LANES)
        for side in range(2):
            kh = 2 * c + side
            k_l = jnp.where(left_kv, k_slabs[side], zeros)
            k_r = jnp.where(left_kv, zeros, k_slabs[1 - side])
            v_l = jnp.where(left_kv, v_slabs[side], zeros)
            v_r = jnp.where(left_kv, zeros, v_slabs[1 - side])
            for qb in range(QBLOCKS_PER_TILE):
                rows = slice(qb * BLOCK, qb * BLOCK + kv_rows)
                kk = jnp.concatenate([k_l[rows], k_r[rows]], axis=0)
                vv = jnp.concatenate([jnp.concatenate([v_l[rows], ones_l], axis=1),
                                      jnp.concatenate([v_r[rows], ones_r], axis=1)], axis=0)
                for pair in range(GQA_GROUP // 2):
                    problems.append((qb, kh * GQA_GROUP + 2 * pair, kk, vv))

    def scores(qb, e, kk):
        q2 = q_ref[qb * BLOCK:(qb + 1) * BLOCK, e * HEAD_DIM:(e + 2) * HEAD_DIM]
        return lax.dot_general(q2, kk, contract_lanes, preferred_element_type=F32)

    def finish(qb, e, s2, vv):
        bias = bias0_ref if qb == 0 else bias_ref
        ps, ds = [], []
        for t in range(2):
            s = s2[:, t * kv_rows:(t + 1) * kv_rows] + bias[e + t]
            sink = sink_ref[e + t]
            m = jnp.maximum(jnp.max(s, axis=-1, keepdims=True), sink)
            ps.append(jnp.exp(s - m).astype(BF16))
            ds.append(jnp.exp(sink - m))
        r = jnp.dot(jnp.concatenate(ps, axis=1), vv, preferred_element_type=F32)
        denom = r[:, LANES:] + jnp.where(left_q, ds[0], ds[1])
        ya_ref[qb * BLOCK:(qb + 1) * BLOCK, e * HEAD_DIM:(e + 2) * HEAD_DIM] = (r[:, :LANES] / denom).astype(BF16)

    others = list(range(N_QKV_BLOCKS, N_IN_BLOCKS))
    for first in range(0, len(problems), ATTN_PROBLEMS_PER_GAP):
        group = problems[first:first + ATTN_PROBLEMS_PER_GAP]
        s2s = [scores(qb, e, kk) for qb, e, kk, _ in group]
        if others:
            other_block(others.pop(0))
        for (qb, e, _, vv), s2 in zip(group, s2s):
            finish(qb, e, s2, vv)
    for j in others:
        other_block(j)
    _cast_blocks(refs[:n_cast], refs[n_cast + 3:2 * n_cast + 3])


def _inproj_attn(x2, gain, w_bf, bias, seg, ntab, sinks, bias_tab, seq, cast_weights):
    t = x2.shape[0]
    tm = INPROJ_TM
    tiles_per_seq = seq // tm
    const2 = lambda i: (0, 0)
    bias_shape = (None, N_Q_HEADS, BLOCK, 2 * BLOCK)
    casts = [_RowCast(w, wl, t // tm, lambda i: i) for w, wl in cast_weights]
    outs = pl.pallas_call(
        _inproj_attn_kernel,
        grid=(t // tm,),
        in_specs=[
            pl.BlockSpec(memory_space=pltpu.SMEM),
            pl.BlockSpec((tm, D_MODEL), lambda i: (i, 0)),
            pl.BlockSpec((1, D_MODEL), const2),
            pl.BlockSpec((D_MODEL, IN_WIDTH), const2, pipeline_mode=pl.Buffered(1)),
            pl.BlockSpec((1, 2 * D_MODEL), const2),
            pl.BlockSpec((INPROJ_TN, INPROJ_TN), const2),
            pl.BlockSpec((N_QKV_BLOCKS, 2, INPROJ_TN), lambda i: (0, 0, 0)),
            pl.BlockSpec(bias_shape, lambda i: (jnp.minimum(i % tiles_per_seq, 1), 0, 0, 0)),
            pl.BlockSpec(bias_shape, lambda i: (1, 0, 0, 0), pipeline_mode=pl.Buffered(1)),
        ] + [c.in_spec for c in casts],
        out_specs=[
            pl.BlockSpec((tm, SSM_WIDTH), lambda i: (i, 0)),
            pl.BlockSpec((tm, 2 * D_MODEL), lambda i: (i, 0)),
            pl.BlockSpec((tm, ATTN_WIDTH), lambda i: (i, 0)),
        ] + [c.out_spec for c in casts],
        out_shape=[
            jax.ShapeDtypeStruct((t, SSM_WIDTH), F32),
            jax.ShapeDtypeStruct((t, 2 * D_MODEL), BF16),
            jax.ShapeDtypeStruct((t, ATTN_WIDTH), BF16),
        ] + [c.out_shape for c in casts],
        scratch_shapes=[pltpu.VMEM((tm, ATTN_WIDTH), BF16), pltpu.VMEM((BLOCK + tm, 2 * KV_WIDTH), BF16)],
        compiler_params=pltpu.CompilerParams(
            dimension_semantics=("arbitrary",), vmem_limit_bytes=VMEM_LIMIT),
        name="inproj_attn",
    )(sinks, x2, gain, w_bf, bias, seg, ntab, bias_tab, bias_tab, *[c.operand for c in casts])
    return outs[:3], outs[3:]


def _qk_norm_tables(q_gain, k_gain):
    qrow = jnp.tile(q_gain.astype(F32), INPROJ_TN // HEAD_DIM) * (HEAD_DIM ** -0.5)
    ones = jnp.ones((INPROJ_TN,), F32)
    kvrow = jnp.concatenate([jnp.tile(k_gain.astype(F32), N_KV_HEADS), jnp.ones((KV_WIDTH,), F32)])
    kvmask = jnp.concatenate([jnp.ones((KV_WIDTH,), F32), jnp.zeros((KV_WIDTH,), F32)])
    return jnp.stack([jnp.stack([qrow, ones]), jnp.stack([qrow, ones]), jnp.stack([kvrow, kvmask])])


def _segment_ones():
    r = jnp.arange(INPROJ_TN) // HEAD_DIM
    return (r[:, None] == r[None, :]).astype(BF16)


def _attn_bias_tables():
    t_loc = jnp.arange(BLOCK)[:, None]
    s_loc = jnp.arange(2 * BLOCK)[None, :] - BLOCK
    dist = (t_loc - s_loc).astype(F32)
    valid = (dist >= 0) & (dist < WINDOW)
    slopes = jnp.exp2(-8.0 * jnp.arange(1, N_Q_HEADS + 1, dtype=F32) / N_Q_HEADS)
    bias = -slopes[:, None, None] * dist[None]
    full = jnp.where(valid[None], bias, -jnp.inf)
    first = jnp.where((valid & (s_loc >= 0))[None], bias, -jnp.inf)
    return jnp.stack([first, full])


def _cmul(ar, ai, br, bi):
    return ar * br - ai * bi, ar * bi + ai * br


def _pow_by_bits(exps, squares):
    pr = pi = None
    for b, (sr, si) in enumerate(squares):
        on = ((exps >> b) & 1) == 1
        fr, fi = jnp.where(on, sr, 1.0), jnp.where(on, si, 0.0)
        pr, pi = (fr, fi) if pr is None else _cmul(pr, pi, fr, fi)
    return pr, pi


def _ssm_prep_kernel(lam_ref, ldt_ref, btr_ref, bti_ref, ctr_ref, cti_ref, d_ref, *refs):
    n_cast = (len(refs) - 4) // 2
    tb_ref, c_ref, levr_ref, levi_ref = refs[n_cast:n_cast + 4]
    _cast_blocks(refs[:n_cast], refs[n_cast + 4:])
    n_levels = levr_ref.shape[1]
    tau_lane = lax.broadcasted_iota(jnp.int32, (SSM_STATE, SUB_W), 1) // SSM_GROUP_CH
    row_h = lax.broadcasted_iota(jnp.int32, (SSM_GROUP_CH, SUB_W), 0)
    lane_h = lax.broadcasted_iota(jnp.int32, (SSM_GROUP_CH, SUB_W), 1)
    tile_ch = (lane_h % SSM_GROUP_CH == row_h).astype(F32)
    row_m = lax.broadcasted_iota(jnp.int32, (SUB, SSM_STATE), 0)
    eye_p = (lax.broadcasted_iota(jnp.int32, (SSM_STATE, SSM_STATE), 0)
             == lax.broadcasted_iota(jnp.int32, (SSM_STATE, SSM_STATE), 1))

    def to_col(v):
        return jnp.sum(jnp.where(eye_p, v, 0.0), axis=1, keepdims=True)

    def squares(a, n):
        out = [a]
        for _ in range(n - 1):
            out.append(_cmul(*out[-1], *out[-1]))
        return out

    for g in range(SLAB_GROUPS):
        dt = jnp.exp(ldt_ref[g])

        def discretise(lr, li):
            mag = jnp.exp(lr * dt)
            return mag * jnp.cos(li * dt), mag * jnp.sin(li * dt)

        lr_row, li_row = lam_ref[g, 0:1, :], lam_ref[g, 1:2, :]
        a_row = discretise(lr_row, li_row)
        sq_row = squares(a_row, 5)
        sq_col = squares((to_col(a_row[0]), to_col(a_row[1])), 4)
        ar, ai = sq_row[0]
        den = lr_row * lr_row + li_row * li_row
        fr = ((ar - 1.0) * lr_row + ai * li_row) / den
        fi = (ai * lr_row - (ar - 1.0) * li_row) / den
        bbr, bbi = _cmul(fr, fi, btr_ref[g], bti_ref[g])

        e0 = _pow_by_bits(tau_lane, sq_col)
        e1 = _cmul(*e0, *sq_col[0])
        ctr = jnp.dot(ctr_ref[g], tile_ch, precision=HIGHEST, preferred_element_type=F32)
        cti = jnp.dot(cti_ref[g], tile_ch, precision=HIGHEST, preferred_element_type=F32)
        mr, mi = _cmul(*e0, ctr, cti)
        kt = (jnp.dot(bbr, mr, precision=HIGHEST, preferred_element_type=F32)
              - jnp.dot(bbi, mi, precision=HIGHEST, preferred_element_type=F32))
        kt = kt + jnp.where(lane_h == row_h, d_ref[g], 0.0)
        pw = _pow_by_bits(row_m, sq_row[:4])
        q, side = g // 2, g % 2
        own = lambda width: slice(side * width, (side + 1) * width)
        other = lambda width: slice((1 - side) * width, (2 - side) * width)
        zeros_p = jnp.zeros((SSM_GROUP_CH, SSM_STATE), F32)

        def paired(v):
            return jnp.concatenate([v, zeros_p] if side == 0 else [zeros_p, v], axis=1).astype(BF16)

        for j in range(SUB):
            rows = pl.ds(side * SUB_W + j * SSM_GROUP_CH, SSM_GROUP_CH)
            tj = kt if j == 0 else jnp.where(lane_h >= j * SSM_GROUP_CH,
                                             pltpu.roll(kt, j * SSM_GROUP_CH, axis=1), 0.0)
            tb_ref[q, rows, own(SUB_W)] = tj.astype(BF16)
            tb_ref[q, rows, other(SUB_W)] = jnp.zeros((SSM_GROUP_CH, SUB_W), BF16)
            m = SUB - 1 - j
            br_, bi_ = _cmul(pw[0][m:m + 1, :], pw[1][m:m + 1, :], bbr, bbi)
            tb_ref[q, rows, 2 * SUB_W:2 * SUB_W + STATE_W] = paired(br_)
            tb_ref[q, rows, 2 * SUB_W + STATE_W:TB_W] = paired(bi_)
        for part, val in enumerate([ctr * e1[0] - cti * e1[1], -ctr * e1[1] - cti * e1[0]]):
            rows = pl.ds(part * STATE_W + side * SSM_STATE, SSM_STATE)
            c_ref[q, rows, own(SUB_W)] = val.astype(BF16)
            c_ref[q, rows, other(SUB_W)] = jnp.zeros((SSM_STATE, SUB_W), BF16)
        lv = sq_row[4]
        lev_r, lev_i = [], []
        for _ in range(n_levels):
            lev_r.append(lv[0])
            lev_i.append(lv[1])
            lv = _cmul(*lv, *lv)
        levels = (jnp.concatenate(lev_r, axis=0), jnp.concatenate(lev_i, axis=0))
        if side == 0:
            first_levels = levels
        else:
            levr_ref[q] = jnp.concatenate([first_levels[0], levels[0]], axis=1)
            levi_ref[q] = jnp.concatenate([first_levels[1], levels[1]], axis=1)


def _ssm_prep(lam_re, lam_im, log_dt, b_re, b_im, c_re, c_im, d_skip, n_levels, cast_weights):
    dg = lam_re.shape[0] * SSM_GROUPS
    casts = [_RowCast(w, wl, dg // SLAB_GROUPS, lambda s: s) for w, wl in cast_weights]
    h_, p_ = SSM_GROUP_CH, SSM_STATE
    f = lambda a: a.astype(F32)
    lam_rows = jnp.stack([f(lam_re), f(lam_im)], axis=2).reshape(dg, 2, p_)
    btr = f(b_re).transpose(0, 1, 3, 2).reshape(dg, h_, p_)
    bti = f(b_im).transpose(0, 1, 3, 2).reshape(dg, h_, p_)
    ctr = f(c_re).transpose(0, 1, 3, 2).reshape(dg, p_, h_)
    cti = f(c_im).transpose(0, 1, 3, 2).reshape(dg, p_, h_)
    d_rows = jnp.pad(f(d_skip).reshape(dg, 1, h_), ((0, 0), (0, 0), (0, SUB_W - h_)))
    grp = lambda *shape: pl.BlockSpec((SLAB_GROUPS,) + shape, lambda s: (s,) + (0,) * len(shape))
    pair = lambda *shape: pl.BlockSpec((SLAB_PAIRS,) + shape, lambda s: (s,) + (0,) * len(shape))
    outs = pl.pallas_call(
        _ssm_prep_kernel,
        grid=(dg // SLAB_GROUPS,),
        in_specs=[grp(2, p_), grp(1, 1), grp(h_, p_), grp(h_, p_),
                  grp(p_, h_), grp(p_, h_), grp(1, SUB_W)] + [c.in_spec for c in casts],
        out_specs=[pair(2 * SUB_W, TB_W), pair(2 * STATE_W, 2 * SUB_W), pair(n_levels, STATE_W),
                   pair(n_levels, STATE_W)] + [c.out_spec for c in casts],
        out_shape=[
            jax.ShapeDtypeStruct((dg // 2, 2 * SUB_W, TB_W), BF16),
            jax.ShapeDtypeStruct((dg // 2, 2 * STATE_W, 2 * SUB_W), BF16),
            jax.ShapeDtypeStruct((dg // 2, n_levels, STATE_W), F32),
            jax.ShapeDtypeStruct((dg // 2, n_levels, STATE_W), F32),
        ] + [c.out_shape for c in casts],
        compiler_params=pltpu.CompilerParams(
            dimension_semantics=("arbitrary",), vmem_limit_bytes=VMEM_LIMIT),
        name="s5_prep",
    )(lam_rows, f(log_dt).reshape(dg, 1, 1), btr, bti, ctr, cti, d_rows, *[c.operand for c in casts])
    return outs[:4], outs[4:]


CHUNK = 2 * SSM_GROUP_CH
CHUNKS = LANES // CHUNK
PAIRS = SUB // 2
PAIR_W = 2 * LANES


def _pair_permutation():
    r = jnp.arange(PAIR_W)
    jj, lane = r // LANES, r % LANES
    col = (lane // SSM_GROUP_CH) * CHUNK + jj * SSM_GROUP_CH + lane % SSM_GROUP_CH
    return (col[:, None] == jnp.arange(PAIR_W)[None, :]).astype(BF16)


S5_STAGES = 3


def _ssm_kernel(u_ref, perm_ref, perm_t_ref, tb_ref, c_ref, ar_ref, ai_ref, y_ref, xs_ref, ys_ref, cs_ref):
    t = pl.program_id(0)
    rows = 2 * xs_ref.shape[1]
    n_levels = rows.bit_length() - 1
    cur, prv = t % 2, (t + 1) % 2

    @pl.when(t == 0)
    def _():
        xs_ref[...] = jnp.zeros(xs_ref.shape, WORD)
        ys_ref[...] = jnp.zeros(ys_ref.shape, WORD)

    def shifted(a, sh):
        return jnp.concatenate([jnp.zeros((sh, a.shape[1]), a.dtype), a[:rows - sh]], axis=0)

    def rotations(a):
        return [a] + [pltpu.roll(a, r * CHUNK, axis=1) for r in range(1, CHUNKS)]

    def chunk(a, c):
        return a[:, c * CHUNK:(c + 1) * CHUNK]

    def to_words(a):
        return pltpu.bitcast(a.astype(BF16), WORD)

    def from_words(a):
        return pltpu.bitcast(a, BF16)

    def scan_inputs(q):
        return jnp.dot(from_words(xs_ref[prv, :, q * 2 * SUB_W:(q + 1) * 2 * SUB_W]), tb_ref[q],
                       preferred_element_type=F32)

    def layout_out_chunks(k):
        col, dst = k // CHUNKS, k % CHUNKS
        yk = ys_ref[cur, :, k * SUB_W:(k + 1) * SUB_W]
        for v in range(SUB_W // LANES):
            rots = rotations(yk[:, v * LANES:(v + 1) * LANES])
            for src in range(CHUNKS):
                lane0 = (v * CHUNKS + src) * PAIR_W + col * LANES + dst * CHUNK
                cs_ref[:, lane0:lane0 + CHUNK] = chunk(rots[(dst - src) % CHUNKS], dst)

    def layout_in_chunks(k):
        col, dst = k // CHUNKS, k % CHUNKS
        pair_tile = jnp.concatenate([u_ref[pl.ds(2 * k, rows, stride=SUB), :],
                                     u_ref[pl.ds(2 * k + 1, rows, stride=SUB), :]], axis=1).astype(BF16)
        grouped = to_words(jnp.dot(pair_tile, perm_ref[...], preferred_element_type=F32))
        for w in range(PAIR_W // LANES):
            rots = rotations(grouped[:, w * LANES:(w + 1) * LANES])
            for src in range(CHUNKS):
                lane0 = (w * CHUNKS + src) * SUB_W + col * LANES + dst * CHUNK
                xs_ref[cur, :, lane0:lane0 + CHUNK] = chunk(rots[(dst - src) % CHUNKS], dst)

    r_next = scan_inputs(0)
    for q in range(SLAB_PAIRS):
        r = r_next
        if q + 1 < SLAB_PAIRS:
            r_next = scan_inputs(q + 1)
        for k in (2 * q, 2 * q + 1):
            layout_out_chunks(k)
            layout_in_chunks(k)
        yt = r[:, :2 * SUB_W]
        re = r[:, 2 * SUB_W:2 * SUB_W + STATE_W]
        im = r[:, 2 * SUB_W + STATE_W:]
        for lvl in range(n_levels):
            sh = 1 << lvl
            ar = ar_ref[q, lvl:lvl + 1, :]
            ai = ai_ref[q, lvl:lvl + 1, :]
            if sh % SUBLANES:
                pr, pi = shifted(re, sh), shifted(im, sh)
                re, im = re + ar * pr - ai * pi, im + ar * pi + ai * pr
            else:
                pr, pi = re[:rows - sh], im[:rows - sh]
                re, im = (jnp.concatenate([re[:sh], re[sh:] + ar * pr - ai * pi], axis=0),
                          jnp.concatenate([im[:sh], im[sh:] + ar * pi + ai * pr], axis=0))
        prev_state = jnp.concatenate([shifted(re, 1), shifted(im, 1)], axis=1).astype(BF16)
        y = yt + jnp.dot(prev_state, c_ref[q], preferred_element_type=F32)
        ys_ref[prv, :, q * 2 * SUB_W:(q + 1) * 2 * SUB_W] = to_words(jax.nn.gelu(y))

    for pair in range(PAIRS):
        o = jnp.dot(from_words(cs_ref[:, pair * PAIR_W:(pair + 1) * PAIR_W]), perm_t_ref[...],
                    preferred_element_type=F32)
        y_ref[pl.ds(2 * pair, rows, stride=SUB), :] = o[:, :LANES]
        y_ref[pl.ds(2 * pair + 1, rows, stride=SUB), :] = o[:, LANES:]


def _ssm(uf, perm, tb, cmat, lev_r, lev_i, layer, batch, seq):
    t = uf.shape[0]
    n_sub = seq // SUB
    n_levels = lev_r.shape[1]
    n_items = batch * N_SLABS
    item = lambda t, stage: jnp.clip(t - stage, 0, n_items - 1)
    slab3 = lambda t: (layer * N_SLABS + item(t, 1) % N_SLABS, 0, 0)
    return pl.pallas_call(
        _ssm_kernel,
        grid=(n_items + S5_STAGES - 1,),
        in_specs=[
            pl.BlockSpec((seq, LANES), lambda t: (item(t, 0) // N_SLABS, item(t, 0) % N_SLABS)),
            pl.BlockSpec((PAIR_W, PAIR_W), lambda t: (0, 0)),
            pl.BlockSpec((PAIR_W, PAIR_W), lambda t: (0, 0)),
            pl.BlockSpec((SLAB_PAIRS, 2 * SUB_W, TB_W), slab3),
            pl.BlockSpec((SLAB_PAIRS, 2 * STATE_W, 2 * SUB_W), slab3),
            pl.BlockSpec((SLAB_PAIRS, n_levels, STATE_W), slab3),
            pl.BlockSpec((SLAB_PAIRS, n_levels, STATE_W), slab3),
        ],
        out_specs=pl.BlockSpec((seq, LANES), lambda t: (item(t, 2) // N_SLABS, item(t, 2) % N_SLABS)),
        out_shape=jax.ShapeDtypeStruct((t, SSM_WIDTH), F32),
        scratch_shapes=[pltpu.VMEM((2, n_sub // 2, SLAB_W), WORD), pltpu.VMEM((2, n_sub // 2, SLAB_W), WORD),
                        pltpu.VMEM((n_sub // 2, SLAB_W), WORD)],
        compiler_params=pltpu.CompilerParams(
            dimension_semantics=("arbitrary",), vmem_limit_bytes=VMEM_LIMIT),
        name="s5_scan",
    )(uf, perm, perm.T, tb, cmat, lev_r, lev_i)


def _merge_kernel(ya_ref, yg_ref, gt_ref, x_ref, wglu_ref, bglu_ref, wa_ref, ws_ref, wo_ref,
                  gffn_ref, *refs):
    n_cast = (len(refs) - 2) // 2
    x1_ref, h2_ref = refs[n_cast:n_cast + 2]
    yg = yg_ref[...]
    t = jnp.dot(yg.astype(BF16), wglu_ref[...], preferred_element_type=F32) + bglu_ref[...]
    ys = (yg * _sigmoid(t)).astype(BF16)
    ma = jnp.dot(ya_ref[...], wa_ref[...], preferred_element_type=F32)
    ms = jnp.dot(ys, ws_ref[...], preferred_element_type=F32)
    merged = gt_ref[:, :D_MODEL].astype(F32) * ma + gt_ref[:, D_MODEL:].astype(F32) * ms
    x1 = x_ref[...] + jnp.dot(merged.astype(BF16), wo_ref[...], preferred_element_type=F32)
    x1_ref[...] = x1
    h2_ref[...] = _rms(x1, gffn_ref[...]).astype(BF16)
    _cast_blocks(refs[:n_cast], refs[n_cast + 2:])


def _merge(ya, yg, gates, x2, wglu, bglu, wa, ws, wo, gffn, cast_weights):
    t = x2.shape[0]
    tm = MERGE_TM
    casts = [_RowCast(w, wl, t // tm, lambda i: i) for w, wl in cast_weights]

    def wspec(rows, cols):
        return pl.BlockSpec((rows, cols), lambda i: (0, 0), pipeline_mode=pl.Buffered(1))

    outs = pl.pallas_call(
        _merge_kernel,
        grid=(t // tm,),
        in_specs=[
            pl.BlockSpec((tm, ATTN_WIDTH), lambda i: (i, 0)),
            pl.BlockSpec((tm, SSM_WIDTH), lambda i: (i, 0)),
            pl.BlockSpec((tm, 2 * D_MODEL), lambda i: (i, 0)),
            pl.BlockSpec((tm, D_MODEL), lambda i: (i, 0)),
            wspec(SSM_WIDTH, SSM_WIDTH),
            pl.BlockSpec((1, SSM_WIDTH), lambda i: (0, 0)),
            wspec(ATTN_WIDTH, D_MODEL),
            wspec(SSM_WIDTH, D_MODEL),
            wspec(D_MODEL, D_MODEL),
            pl.BlockSpec((1, D_MODEL), lambda i: (0, 0)),
        ] + [c.in_spec for c in casts],
        out_specs=[
            pl.BlockSpec((tm, D_MODEL), lambda i: (i, 0)),
            pl.BlockSpec((tm, D_MODEL), lambda i: (i, 0)),
        ] + [c.out_spec for c in casts],
        out_shape=[
            jax.ShapeDtypeStruct((t, D_MODEL), F32),
            jax.ShapeDtypeStruct((t, D_MODEL), BF16),
        ] + [c.out_shape for c in casts],
        compiler_params=pltpu.CompilerParams(
            dimension_semantics=("arbitrary",), vmem_limit_bytes=VMEM_LIMIT),
        name="merge_out",
    )(ya, yg, gates, x2, wglu, bglu, wa, ws, wo, gffn, *[c.operand for c in casts])
    return outs[:2], outs[2:]


FFN_CHUNKS = D_FF // FFN_TF
FFN_STEPS = -(-FFN_CHUNKS // 2)
FFN_X_PARTS = 4
assert FFN_X_PARTS <= FFN_STEPS


def _ffn_kernel(h_ref, x_ref, wga_ref, wua_ref, woa_ref, wgb_ref, wub_ref, wob_ref, o_ref):
    k = pl.program_id(1)

    def chunk_out(rs, wg_ref, wu_ref, wo_ref):
        h = h_ref[rs, :]
        g = jnp.dot(h, wg_ref[...], preferred_element_type=F32)
        u = jnp.dot(h, wu_ref[...], preferred_element_type=F32)
        act = (g * _sigmoid(g) * u).astype(BF16)
        return jnp.dot(act, wo_ref[...], preferred_element_type=F32)

    def accumulate(first, both):
        for r in range(FFN_TM // FFN_SUB):
            rs = pl.ds(r * FFN_SUB, FFN_SUB)
            delta = chunk_out(rs, wga_ref, wua_ref, woa_ref)
            if both:
                delta = delta + chunk_out(rs, wgb_ref, wub_ref, wob_ref)
            o_ref[rs, :] = delta if first else o_ref[rs, :] + delta

    last_single = FFN_CHUNKS % 2 == 1
    pl.when(k == 0)(lambda: accumulate(True, True))
    if last_single:
        pl.when((k > 0) & (k < FFN_STEPS - 1))(lambda: accumulate(False, True))
        pl.when(k == FFN_STEPS - 1)(lambda: accumulate(False, False))
    else:
        pl.when(k > 0)(lambda: accumulate(False, True))

    @pl.when(k < FFN_X_PARTS)
    def _():
        part = FFN_TM // FFN_X_PARTS
        rows = pl.ds(pl.multiple_of(k * part, part), part)
        o_ref[rows, :] += x_ref[...]


def _ffn(h2, x1, w_in, w_out):
    t = x1.shape[0]
    nk = FFN_CHUNKS
    first = lambda k: 2 * k
    second = lambda k: jnp.minimum(2 * k + 1, nk - 1)
    return pl.pallas_call(
        _ffn_kernel,
        grid=(t // FFN_TM, FFN_STEPS),
        in_specs=[
            pl.BlockSpec((FFN_TM, D_MODEL), lambda i, k: (i, 0)),
            pl.BlockSpec((FFN_TM // FFN_X_PARTS, D_MODEL),
                         lambda i, k: (i * FFN_X_PARTS + jnp.minimum(k, FFN_X_PARTS - 1), 0)),
            pl.BlockSpec((D_MODEL, FFN_TF), lambda i, k: (0, first(k))),
            pl.BlockSpec((D_MODEL, FFN_TF), lambda i, k: (0, nk + first(k))),
            pl.BlockSpec((FFN_TF, D_MODEL), lambda i, k: (first(k), 0)),
            pl.BlockSpec((D_MODEL, FFN_TF), lambda i, k: (0, second(k))),
            pl.BlockSpec((D_MODEL, FFN_TF), lambda i, k: (0, nk + second(k))),
            pl.BlockSpec((FFN_TF, D_MODEL), lambda i, k: (second(k), 0)),
        ],
        out_specs=pl.BlockSpec((FFN_TM, D_MODEL), lambda i, k: (i, 0)),
        out_shape=jax.ShapeDtypeStruct((t, D_MODEL), F32),
        compiler_params=pltpu.CompilerParams(
            dimension_semantics=("arbitrary", "arbitrary"), vmem_limit_bytes=FFN_VMEM_LIMIT),
        name="swiglu_ffn",
    )(h2, x1, w_in, w_in, w_out, w_in, w_in, w_out)


def kernel(x, norm_mix_g, w_in, gate_bias, q_norm_g, k_norm_g, attn_sinks, ssm_lambda_re, ssm_lambda_im, ssm_log_dt, ssm_b_re, ssm_b_im, ssm_c_re, ssm_c_im, ssm_d, ssm_glu_w, ssm_glu_b, w_attn_branch, w_ssm_branch, w_out, norm_ffn_g, w_ffn_in, w_ffn_out):
    batch, seq, _ = x.shape
    t = batch * seq
    n_levels = (seq // SUB).bit_length() - 1
    x2 = x.reshape(t, D_MODEL).astype(F32)
    f32 = lambda w: w.astype(F32)
    seg = _segment_ones()
    bias_tab = _attn_bias_tables()
    perm = _pair_permutation()
    w_in_f = f32(w_in)
    (tb, cmat, lev_r, lev_i), (w_in_bf,) = _ssm_prep(
        ssm_lambda_re, ssm_lambda_im, ssm_log_dt, ssm_b_re, ssm_b_im, ssm_c_re, ssm_c_im, ssm_d,
        n_levels, [(w_in_f, 0)])
    merge_weights = [f32(ssm_glu_w), f32(w_attn_branch), f32(w_ssm_branch), f32(w_out)]
    ffn_weights = [f32(w_ffn_in), f32(w_ffn_out)]
    for l in range(DEPTH):
        next_w_in = [(w_in_f, l + 1)] if l + 1 < DEPTH else []
        (uf, gates, ya), cast_out = _inproj_attn(
            x2, norm_mix_g[l].reshape(1, D_MODEL).astype(F32), w_in_bf,
            gate_bias[l].reshape(1, 2 * D_MODEL).astype(F32), seg,
            _qk_norm_tables(q_norm_g[l], k_norm_g[l]), attn_sinks[l].astype(F32), bias_tab, seq,
            [(w, l) for w in merge_weights] + next_w_in)
        wglu_bf, wa_bf, ws_bf, wo_bf = cast_out[:4]
        w_in_bf = cast_out[4] if next_w_in else None
        yg = _ssm(uf, perm, tb, cmat, lev_r, lev_i, l, batch, seq)
        (x1, h2), (wfi_bf, wfo_bf) = _merge(
            ya, yg, gates, x2, wglu_bf, ssm_glu_b[l].reshape(1, SSM_WIDTH).astype(F32),
            wa_bf, ws_bf, wo_bf, norm_ffn_g[l].reshape(1, D_MODEL).astype(F32),
            [(w, l) for w in ffn_weights])
        x2 = _ffn(h2, x1, wfi_bf, wfo_bf)
    return x2.reshape(batch, seq, D_MODEL).astype(x.dtype)
```

```python
from functools import partial

import jax
import jax.numpy as jnp
from jax import lax
from jax.experimental import pallas as pl
from jax.experimental.pallas import tpu as pltpu

D_MODEL = 2048
DEPTH = 2
HEAD_DIM = 64
N_Q_HEADS = 16
N_KV_HEADS = 4
GQA_GROUP = N_Q_HEADS // N_KV_HEADS
ATTN_WIDTH = N_Q_HEADS * HEAD_DIM
KV_WIDTH = N_KV_HEADS * HEAD_DIM
WINDOW = 128
BLOCK = 128
SSM_WIDTH = D_MODEL // 2
SSM_GROUP_CH = 16
SSM_GROUPS = SSM_WIDTH // SSM_GROUP_CH
SSM_STATE = 64
D_FF = -(-8 * D_MODEL // (3 * 256)) * 256
OFF_K = ATTN_WIDTH
OFF_V = OFF_K + KV_WIDTH
OFF_U = OFF_V + KV_WIDTH
OFF_G = OFF_U + SSM_WIDTH
IN_WIDTH = OFF_G + 2 * D_MODEL
RMS_EPS = 1e-6

F32 = jnp.float32
BF16 = jnp.bfloat16
WORD = jnp.uint32
HIGHEST = lax.Precision.HIGHEST

LANES = 128
SUBLANES = 8
BF16_SUBLANES = 16
SUB = 16
SUB_W = SUB * SSM_GROUP_CH
STATE_W = 2 * SSM_STATE
TB_W = 2 * SUB_W + 2 * STATE_W
SLAB_GROUPS = LANES // SSM_GROUP_CH
SLAB_PAIRS = SLAB_GROUPS // 2
N_SLABS = SSM_GROUPS // SLAB_GROUPS
SLAB_W = SLAB_GROUPS * SUB_W

V7X_VMEM_BYTES = 64 * 1024 * 1024
VMEM_LIMIT = V7X_VMEM_BYTES * 7 // 8
FFN_VMEM_LIMIT = V7X_VMEM_BYTES * 15 // 16

INPROJ_TM = 256
INPROJ_TN = 512
MERGE_TM = 256
FFN_TM = 1024
FFN_TF = 512
FFN_SUB = 512


def _rms(x, g):
    return x * lax.rsqrt(jnp.mean(x * x, axis=-1, keepdims=True) + RMS_EPS) * g


def _sigmoid(x):
    return 0.5 * jnp.tanh(0.5 * x) + 0.5


class _RowCast:
    def __init__(self, stacked, layer, n_steps, step_of):
        _, rows, cols = stacked.shape
        blk = rows // n_steps
        assert blk * n_steps == rows and blk % BF16_SUBLANES == 0, (rows, n_steps)
        self.operand = stacked
        self.in_spec = pl.BlockSpec((None, blk, cols), lambda *ids: (layer, step_of(*ids), 0))
        self.out_spec = pl.BlockSpec((blk, cols), lambda *ids: (step_of(*ids), 0))
        self.out_shape = jax.ShapeDtypeStruct((rows, cols), BF16)


def _cast_blocks(src_refs, dst_refs):
    for src, dst in zip(src_refs, dst_refs):
        dst[...] = src[...].astype(BF16)


N_QKV_BLOCKS = OFF_U // INPROJ_TN
N_MAIN_BLOCKS = OFF_G // INPROJ_TN
N_IN_BLOCKS = IN_WIDTH // INPROJ_TN


QBLOCKS_PER_TILE = INPROJ_TM // BLOCK
ATTN_PROBLEMS_PER_GAP = 2


def _inproj_attn_kernel(sink_ref, x_ref, g_ref, w_ref, b_ref, seg_ref, nt_ref, bias_ref, *refs, tiles_per_seq):
    n_cast = (len(refs) - 5) // 2
    u_ref, gt_ref, ya_ref = refs[n_cast:n_cast + 3]
    q_ref, kv_ref = refs[2 * n_cast + 3:]

    @pl.when(pl.program_id(0) == 0)
    def _():
        kv_ref[...] = jnp.zeros(kv_ref.shape, BF16)

    kv_ref[0:BLOCK, :] = kv_ref[INPROJ_TM:INPROJ_TM + BLOCK, :]
    h = _rms(x_ref[...], g_ref[...]).astype(BF16)

    def project(j):
        return jnp.dot(h, w_ref[:, j * INPROJ_TN:(j + 1) * INPROJ_TN], preferred_element_type=F32)

    for j in range(N_QKV_BLOCKS):
        z = project(j)
        ssq = jnp.dot((z * z).astype(BF16), seg_ref[...], preferred_element_type=F32)
        inv = lax.rsqrt(ssq * (1.0 / HEAD_DIM) + RMS_EPS)
        fac = jnp.where(nt_ref[j, 1:2, :] > 0.0, inv, 1.0) * nt_ref[j, 0:1, :]
        if (j + 1) * INPROJ_TN <= ATTN_WIDTH:
            q_ref[:, j * INPROJ_TN:(j + 1) * INPROJ_TN] = (z * fac).astype(BF16)
        else:
            kv_ref[BLOCK:, :] = (z * fac).astype(BF16)

    def other_block(j):
        z = project(j)
        if j < N_MAIN_BLOCKS:
            c = j - N_QKV_BLOCKS
            u_ref[:, c * INPROJ_TN:(c + 1) * INPROJ_TN] = z
        else:
            cs = slice((j - N_MAIN_BLOCKS) * INPROJ_TN, (j - N_MAIN_BLOCKS + 1) * INPROJ_TN)
            gt_ref[:, cs] = _sigmoid(z + b_ref[:, cs]).astype(BF16)

    kv_rows = 2 * BLOCK
    all_rows = kv_ref.shape[0]
    left_kv = lax.broadcasted_iota(jnp.int32, (all_rows, LANES), 1) < HEAD_DIM
    left_q = lax.broadcasted_iota(jnp.int32, (BLOCK, LANES), 1) < HEAD_DIM
    zeros = jnp.zeros((all_rows, LANES), BF16)
    left_ones = lax.broadcasted_iota(jnp.int32, (kv_rows, LANES), 1) < HEAD_DIM
    ones_l = jnp.where(left_ones, 1.0, 0.0).astype(BF16)
    ones_r = jnp.where(left_ones, 0.0, 1.0).astype(BF16)
    contract_lanes = (((1,), (1,)), ((), ()))

    def slab(off):
        a = kv_ref[:, off:off + LANES]
        return a, pltpu.roll(a.astype(F32), HEAD_DIM, axis=1).astype(BF16)

    problems = []
    for c in range(N_KV_HEADS // 2):
        k_slabs = slab(c * LANES)
        v_slabs = slab(KV_WIDTH + c * LANES)
        for side in range(2):
            kh = 2 * c + side
            k_l = jnp.where(left_kv, k_slabs[side], zeros)
            k_r = jnp.where(left_kv, zeros, k_slabs[1 - side])
            v_l = jnp.where(left_kv, v_slabs[side], zeros)
            v_r = jnp.where(left_kv, zeros, v_slabs[1 - side])
            for qb in range(QBLOCKS_PER_TILE):
                rows = slice(qb * BLOCK, qb * BLOCK + kv_rows)
                kk = jnp.concatenate([k_l[rows], k_r[rows]], axis=0)
                vv = jnp.concatenate([jnp.concatenate([v_l[rows], ones_l], axis=1),
                                      jnp.concatenate([v_r[rows], ones_r], axis=1)], axis=0)
                for pair in range(GQA_GROUP // 2):
                    problems.append((qb, kh * GQA_GROUP + 2 * pair, kk, vv))

    def scores(qb, e, kk):
        q2 = q_ref[qb * BLOCK:(qb + 1) * BLOCK, e * HEAD_DIM:(e + 2) * HEAD_DIM]
        return lax.dot_general(q2, kk, contract_lanes, preferred_element_type=F32)

    no_prev = jnp.logical_and(lax.broadcasted_iota(jnp.int32, (BLOCK, kv_rows), 1) < BLOCK,
                              pl.program_id(0) % tiles_per_seq == 0)

    def finish(qb, e, s2, vv):
        ps, ds = [], []
        for t in range(2):
            s = s2[:, t * kv_rows:(t + 1) * kv_rows] + bias_ref[e + t]
            if qb == 0:
                s = jnp.where(no_prev, -jnp.inf, s)
            sink = sink_ref[e + t]
            m = jnp.maximum(jnp.max(s, axis=-1, keepdims=True), sink)
            ps.append(jnp.exp(s - m).astype(BF16))
            ds.append(jnp.exp(sink - m))
        r = jnp.dot(jnp.concatenate(ps, axis=1), vv, preferred_element_type=F32)
        denom = r[:, LANES:] + jnp.where(left_q, ds[0], ds[1])
        ya_ref[qb * BLOCK:(qb + 1) * BLOCK, e * HEAD_DIM:(e + 2) * HEAD_DIM] = (r[:, :LANES] / denom).astype(BF16)

    others = list(range(N_QKV_BLOCKS, N_IN_BLOCKS))
    for first in range(0, len(problems), ATTN_PROBLEMS_PER_GAP):
        group = problems[first:first + ATTN_PROBLEMS_PER_GAP]
        s2s = [scores(qb, e, kk) for qb, e, kk, _ in group]
        if others:
            other_block(others.pop(0))
        for (qb, e, _, vv), s2 in zip(group, s2s):
            finish(qb, e, s2, vv)
    for j in others:
        other_block(j)
    _cast_blocks(refs[:n_cast], refs[n_cast + 3:2 * n_cast + 3])


def _inproj_attn(x2, gain, w_bf, bias, seg, ntab, sinks, bias_tab, seq, cast_weights):
    t = x2.shape[0]
    tm = INPROJ_TM
    tiles_per_seq = seq // tm
    const2 = lambda i: (0, 0)
    casts = [_RowCast(w, wl, t // tm, lambda i: i) for w, wl in cast_weights]
    outs = pl.pallas_call(
        partial(_inproj_attn_kernel, tiles_per_seq=tiles_per_seq),
        grid=(t // tm,),
        in_specs=[
            pl.BlockSpec(memory_space=pltpu.SMEM),
            pl.BlockSpec((tm, D_MODEL), lambda i: (i, 0)),
            pl.BlockSpec((1, D_MODEL), const2),
            pl.BlockSpec((D_MODEL, IN_WIDTH), const2, pipeline_mode=pl.Buffered(1)),
            pl.BlockSpec((1, 2 * D_MODEL), const2),
            pl.BlockSpec((INPROJ_TN, INPROJ_TN), const2),
            pl.BlockSpec((N_QKV_BLOCKS, 2, INPROJ_TN), lambda i: (0, 0, 0)),
            pl.BlockSpec((N_Q_HEADS, BLOCK, 2 * BLOCK), lambda i: (0, 0, 0), pipeline_mode=pl.Buffered(1)),
        ] + [c.in_spec for c in casts],
        out_specs=[
            pl.BlockSpec((tm, SSM_WIDTH), lambda i: (i, 0)),
            pl.BlockSpec((tm, 2 * D_MODEL), lambda i: (i, 0)),
            pl.BlockSpec((tm, ATTN_WIDTH), lambda i: (i, 0)),
        ] + [c.out_spec for c in casts],
        out_shape=[
            jax.ShapeDtypeStruct((t, SSM_WIDTH), F32),
            jax.ShapeDtypeStruct((t, 2 * D_MODEL), BF16),
            jax.ShapeDtypeStruct((t, ATTN_WIDTH), BF16),
        ] + [c.out_shape for c in casts],
        scratch_shapes=[pltpu.VMEM((tm, ATTN_WIDTH), BF16), pltpu.VMEM((BLOCK + tm, 2 * KV_WIDTH), BF16)],
        compiler_params=pltpu.CompilerParams(
            dimension_semantics=("arbitrary",), vmem_limit_bytes=VMEM_LIMIT),
        name="inproj_attn",
    )(sinks, x2, gain, w_bf, bias, seg, ntab, bias_tab, *[c.operand for c in casts])
    return outs[:3], outs[3:]


def _qk_norm_tables(q_gain, k_gain):
    qrow = jnp.tile(q_gain.astype(F32), INPROJ_TN // HEAD_DIM) * (HEAD_DIM ** -0.5)
    ones = jnp.ones((INPROJ_TN,), F32)
    kvrow = jnp.concatenate([jnp.tile(k_gain.astype(F32), N_KV_HEADS), jnp.ones((KV_WIDTH,), F32)])
    kvmask = jnp.concatenate([jnp.ones((KV_WIDTH,), F32), jnp.zeros((KV_WIDTH,), F32)])
    return jnp.stack([jnp.stack([qrow, ones]), jnp.stack([qrow, ones]), jnp.stack([kvrow, kvmask])])


def _segment_ones():
    r = jnp.arange(INPROJ_TN) // HEAD_DIM
    return (r[:, None] == r[None, :]).astype(BF16)


def _attn_bias_table():
    t_loc = jnp.arange(BLOCK)[:, None]
    s_loc = jnp.arange(2 * BLOCK)[None, :] - BLOCK
    dist = (t_loc - s_loc).astype(F32)
    valid = (dist >= 0) & (dist < WINDOW)
    slopes = jnp.exp2(-8.0 * jnp.arange(1, N_Q_HEADS + 1, dtype=F32) / N_Q_HEADS)
    return jnp.where(valid[None], -slopes[:, None, None] * dist[None], -jnp.inf)


def _cmul(ar, ai, br, bi):
    return ar * br - ai * bi, ar * bi + ai * br


def _pow_by_bits(exps, squares):
    pr = pi = None
    for b, (sr, si) in enumerate(squares):
        on = ((exps >> b) & 1) == 1
        fr, fi = jnp.where(on, sr, 1.0), jnp.where(on, si, 0.0)
        pr, pi = (fr, fi) if pr is None else _cmul(pr, pi, fr, fi)
    return pr, pi


def _ssm_prep_kernel(lam_ref, ldt_ref, btr_ref, bti_ref, ctr_ref, cti_ref, d_ref, *refs):
    n_cast = (len(refs) - 4) // 2
    tb_ref, c_ref, levr_ref, levi_ref = refs[n_cast:n_cast + 4]
    _cast_blocks(refs[:n_cast], refs[n_cast + 4:])
    n_levels = levr_ref.shape[1]
    tau_lane = lax.broadcasted_iota(jnp.int32, (SSM_STATE, SUB_W), 1) // SSM_GROUP_CH
    row_h = lax.broadcasted_iota(jnp.int32, (SSM_GROUP_CH, SUB_W), 0)
    lane_h = lax.broadcasted_iota(jnp.int32, (SSM_GROUP_CH, SUB_W), 1)
    tile_ch = (lane_h % SSM_GROUP_CH == row_h).astype(F32)
    row_m = lax.broadcasted_iota(jnp.int32, (SUB, SSM_STATE), 0)
    eye_p = (lax.broadcasted_iota(jnp.int32, (SSM_STATE, SSM_STATE), 0)
             == lax.broadcasted_iota(jnp.int32, (SSM_STATE, SSM_STATE), 1))

    def to_col(v):
        return jnp.sum(jnp.where(eye_p, v, 0.0), axis=1, keepdims=True)

    def squares(a, n):
        out = [a]
        for _ in range(n - 1):
            out.append(_cmul(*out[-1], *out[-1]))
        return out

    for g in range(SLAB_GROUPS):
        dt = jnp.exp(ldt_ref[g])

        def discretise(lr, li):
            mag = jnp.exp(lr * dt)
            return mag * jnp.cos(li * dt), mag * jnp.sin(li * dt)

        lr_row, li_row = lam_ref[g, 0:1, :], lam_ref[g, 1:2, :]
        a_row = discretise(lr_row, li_row)
        sq_row = squares(a_row, 5)
        sq_col = squares((to_col(a_row[0]), to_col(a_row[1])), 4)
        ar, ai = sq_row[0]
        den = lr_row * lr_row + li_row * li_row
        fr = ((ar - 1.0) * lr_row + ai * li_row) / den
        fi = (ai * lr_row - (ar - 1.0) * li_row) / den
        bbr, bbi = _cmul(fr, fi, btr_ref[g], bti_ref[g])

        e0 = _pow_by_bits(tau_lane, sq_col)
        e1 = _cmul(*e0, *sq_col[0])
        ctr = jnp.dot(ctr_ref[g], tile_ch, precision=HIGHEST, preferred_element_type=F32)
        cti = jnp.dot(cti_ref[g], tile_ch, precision=HIGHEST, preferred_element_type=F32)
        mr, mi = _cmul(*e0, ctr, cti)
        kt = (jnp.dot(bbr, mr, precision=HIGHEST, preferred_element_type=F32)
              - jnp.dot(bbi, mi, precision=HIGHEST, preferred_element_type=F32))
        kt = kt + jnp.where(lane_h == row_h, d_ref[g], 0.0)
        pw = _pow_by_bits(row_m, sq_row[:4])
        q, side = g // 2, g % 2
        own = lambda width: slice(side * width, (side + 1) * width)
        other = lambda width: slice((1 - side) * width, (2 - side) * width)
        zeros_p = jnp.zeros((SSM_GROUP_CH, SSM_STATE), F32)

        def paired(v):
            return jnp.concatenate([v, zeros_p] if side == 0 else [zeros_p, v], axis=1).astype(BF16)

        for j in range(SUB):
            rows = pl.ds(side * SUB_W + j * SSM_GROUP_CH, SSM_GROUP_CH)
            tj = kt if j == 0 else jnp.where(lane_h >= j * SSM_GROUP_CH,
                                             pltpu.roll(kt, j * SSM_GROUP_CH, axis=1), 0.0)
            tb_ref[q, rows, own(SUB_W)] = tj.astype(BF16)
            tb_ref[q, rows, other(SUB_W)] = jnp.zeros((SSM_GROUP_CH, SUB_W), BF16)
            m = SUB - 1 - j
            br_, bi_ = _cmul(pw[0][m:m + 1, :], pw[1][m:m + 1, :], bbr, bbi)
            tb_ref[q, rows, 2 * SUB_W:2 * SUB_W + STATE_W] = paired(br_)
            tb_ref[q, rows, 2 * SUB_W + STATE_W:TB_W] = paired(bi_)
        for part, val in enumerate([ctr * e1[0] - cti * e1[1], -ctr * e1[1] - cti * e1[0]]):
            rows = pl.ds(part * STATE_W + side * SSM_STATE, SSM_STATE)
            c_ref[q, rows, own(SUB_W)] = val.astype(BF16)
            c_ref[q, rows, other(SUB_W)] = jnp.zeros((SSM_STATE, SUB_W), BF16)
        lv = sq_row[4]
        lev_r, lev_i = [], []
        for _ in range(n_levels):
            lev_r.append(lv[0])
            lev_i.append(lv[1])
            lv = _cmul(*lv, *lv)
        levels = (jnp.concatenate(lev_r, axis=0), jnp.concatenate(lev_i, axis=0))
        if side == 0:
            first_levels = levels
        else:
            levr_ref[q] = jnp.concatenate([first_levels[0], levels[0]], axis=1)
            levi_ref[q] = jnp.concatenate([first_levels[1], levels[1]], axis=1)


def _ssm_prep(lam_re, lam_im, log_dt, b_re, b_im, c_re, c_im, d_skip, n_levels, cast_weights):
    dg = lam_re.shape[0] * SSM_GROUPS
    casts = [_RowCast(w, wl, dg // SLAB_GROUPS, lambda s: s) for w, wl in cast_weights]
    h_, p_ = SSM_GROUP_CH, SSM_STATE
    f = lambda a: a.astype(F32)
    lam_rows = jnp.stack([f(lam_re), f(lam_im)], axis=2).reshape(dg, 2, p_)
    btr = f(b_re).transpose(0, 1, 3, 2).reshape(dg, h_, p_)
    bti = f(b_im).transpose(0, 1, 3, 2).reshape(dg, h_, p_)
    ctr = f(c_re).transpose(0, 1, 3, 2).reshape(dg, p_, h_)
    cti = f(c_im).transpose(0, 1, 3, 2).reshape(dg, p_, h_)
    d_rows = jnp.pad(f(d_skip).reshape(dg, 1, h_), ((0, 0), (0, 0), (0, SUB_W - h_)))
    grp = lambda *shape: pl.BlockSpec((SLAB_GROUPS,) + shape, lambda s: (s,) + (0,) * len(shape))
    pair = lambda *shape: pl.BlockSpec((SLAB_PAIRS,) + shape, lambda s: (s,) + (0,) * len(shape))
    outs = pl.pallas_call(
        _ssm_prep_kernel,
        grid=(dg // SLAB_GROUPS,),
        in_specs=[grp(2, p_), grp(1, 1), grp(h_, p_), grp(h_, p_),
                  grp(p_, h_), grp(p_, h_), grp(1, SUB_W)] + [c.in_spec for c in casts],
        out_specs=[pair(2 * SUB_W, TB_W), pair(2 * STATE_W, 2 * SUB_W), pair(n_levels, STATE_W),
                   pair(n_levels, STATE_W)] + [c.out_spec for c in casts],
        out_shape=[
            jax.ShapeDtypeStruct((dg // 2, 2 * SUB_W, TB_W), BF16),
            jax.ShapeDtypeStruct((dg // 2, 2 * STATE_W, 2 * SUB_W), BF16),
            jax.ShapeDtypeStruct((dg // 2, n_levels, STATE_W), F32),
            jax.ShapeDtypeStruct((dg // 2, n_levels, STATE_W), F32),
        ] + [c.out_shape for c in casts],
        compiler_params=pltpu.CompilerParams(
            dimension_semantics=("arbitrary",), vmem_limit_bytes=VMEM_LIMIT),
        name="s5_prep",
    )(lam_rows, f(log_dt).reshape(dg, 1, 1), btr, bti, ctr, cti, d_rows, *[c.operand for c in casts])
    return outs[:4], outs[4:]


CHUNK = 2 * SSM_GROUP_CH
CHUNKS = LANES // CHUNK
PAIRS = SUB // 2
PAIR_W = 2 * LANES


def _pair_permutation():
    r = jnp.arange(PAIR_W)
    jj, lane = r // LANES, r % LANES
    col = (lane // SSM_GROUP_CH) * CHUNK + jj * SSM_GROUP_CH + lane % SSM_GROUP_CH
    return (col[:, None] == jnp.arange(PAIR_W)[None, :]).astype(BF16)


S5_STAGES = 3


def _ssm_kernel(u_ref, perm_ref, perm_t_ref, tb_ref, c_ref, ar_ref, ai_ref, y_ref, xs_ref, ys_ref, cs_ref):
    t = pl.program_id(0)
    rows = 2 * xs_ref.shape[1]
    n_levels = rows.bit_length() - 1
    cur, prv = t % 2, (t + 1) % 2

    @pl.when(t == 0)
    def _():
        xs_ref[...] = jnp.zeros(xs_ref.shape, WORD)
        ys_ref[...] = jnp.zeros(ys_ref.shape, WORD)

    def shifted(a, sh):
        return jnp.concatenate([jnp.zeros((sh, a.shape[1]), a.dtype), a[:rows - sh]], axis=0)

    def rotations(a):
        return [a] + [pltpu.roll(a, r * CHUNK, axis=1) for r in range(1, CHUNKS)]

    def chunk(a, c):
        return a[:, c * CHUNK:(c + 1) * CHUNK]

    def to_words(a):
        return pltpu.bitcast(a.astype(BF16), WORD)

    def from_words(a):
        return pltpu.bitcast(a, BF16)

    def scan_inputs(q):
        return jnp.dot(from_words(xs_ref[prv, :, q * 2 * SUB_W:(q + 1) * 2 * SUB_W]), tb_ref[q],
                       preferred_element_type=F32)

    def layout_out_chunks(k):
        col, dst = k // CHUNKS, k % CHUNKS
        yk = ys_ref[cur, :, k * SUB_W:(k + 1) * SUB_W]
        for v in range(SUB_W // LANES):
            rots = rotations(yk[:, v * LANES:(v + 1) * LANES])
            for src in range(CHUNKS):
                lane0 = (v * CHUNKS + src) * PAIR_W + col * LANES + dst * CHUNK
                cs_ref[:, lane0:lane0 + CHUNK] = chunk(rots[(dst - src) % CHUNKS], dst)

    def layout_in_chunks(k):
        col, dst = k // CHUNKS, k % CHUNKS
        pair_tile = jnp.concatenate([u_ref[pl.ds(2 * k, rows, stride=SUB), :],
                                     u_ref[pl.ds(2 * k + 1, rows, stride=SUB), :]], axis=1).astype(BF16)
        grouped = to_words(jnp.dot(pair_tile, perm_ref[...], preferred_element_type=F32))
        for w in range(PAIR_W // LANES):
            rots = rotations(grouped[:, w * LANES:(w + 1) * LANES])
            for src in range(CHUNKS):
                lane0 = (w * CHUNKS + src) * SUB_W + col * LANES + dst * CHUNK
                xs_ref[cur, :, lane0:lane0 + CHUNK] = chunk(rots[(dst - src) % CHUNKS], dst)

    r_next = scan_inputs(0)
    for q in range(SLAB_PAIRS):
        r = r_next
        if q + 1 < SLAB_PAIRS:
            r_next = scan_inputs(q + 1)
        for k in (2 * q, 2 * q + 1):
            layout_out_chunks(k)
            layout_in_chunks(k)
        yt = r[:, :2 * SUB_W]
        re = r[:, 2 * SUB_W:2 * SUB_W + STATE_W]
        im = r[:, 2 * SUB_W + STATE_W:]
        for lvl in range(n_levels):
            sh = 1 << lvl
            ar = ar_ref[q, lvl:lvl + 1, :]
            ai = ai_ref[q, lvl:lvl + 1, :]
            if sh % SUBLANES:
                pr, pi = shifted(re, sh), shifted(im, sh)
                re, im = re + ar * pr - ai * pi, im + ar * pi + ai * pr
            else:
                pr, pi = re[:rows - sh], im[:rows - sh]
                re, im = (jnp.concatenate([re[:sh], re[sh:] + ar * pr - ai * pi], axis=0),
                          jnp.concatenate([im[:sh], im[sh:] + ar * pi + ai * pr], axis=0))
        prev_state = jnp.concatenate([shifted(re, 1), shifted(im, 1)], axis=1).astype(BF16)
        y = yt + jnp.dot(prev_state, c_ref[q], preferred_element_type=F32)
        ys_ref[prv, :, q * 2 * SUB_W:(q + 1) * 2 * SUB_W] = to_words(jax.nn.gelu(y))

    for pair in range(PAIRS):
        o = jnp.dot(from_words(cs_ref[:, pair * PAIR_W:(pair + 1) * PAIR_W]), perm_t_ref[...],
                    preferred_element_type=F32)
        y_ref[pl.ds(2 * pair, rows, stride=SUB), :] = o[:, :LANES]
        y_ref[pl.ds(2 * pair + 1, rows, stride=SUB), :] = o[:, LANES:]


def _ssm(uf, perm, tb, cmat, lev_r, lev_i, layer, batch, seq):
    t = uf.shape[0]
    n_sub = seq // SUB
    n_levels = lev_r.shape[1]
    n_items = batch * N_SLABS
    item = lambda t, stage: jnp.clip(t - stage, 0, n_items - 1)
    slab3 = lambda t: (layer * N_SLABS + item(t, 1) % N_SLABS, 0, 0)
    return pl.pallas_call(
        _ssm_kernel,
        grid=(n_items + S5_STAGES - 1,),
        in_specs=[
            pl.BlockSpec((seq, LANES), lambda t: (item(t, 0) // N_SLABS, item(t, 0) % N_SLABS)),
            pl.BlockSpec((PAIR_W, PAIR_W), lambda t: (0, 0)),
            pl.BlockSpec((PAIR_W, PAIR_W), lambda t: (0, 0)),
            pl.BlockSpec((SLAB_PAIRS, 2 * SUB_W, TB_W), slab3),
            pl.BlockSpec((SLAB_PAIRS, 2 * STATE_W, 2 * SUB_W), slab3),
            pl.BlockSpec((SLAB_PAIRS, n_levels, STATE_W), slab3),
            pl.BlockSpec((SLAB_PAIRS, n_levels, STATE_W), slab3),
        ],
        out_specs=pl.BlockSpec((seq, LANES), lambda t: (item(t, 2) // N_SLABS, item(t, 2) % N_SLABS)),
        out_shape=jax.ShapeDtypeStruct((t, SSM_WIDTH), F32),
        scratch_shapes=[pltpu.VMEM((2, n_sub // 2, SLAB_W), WORD), pltpu.VMEM((2, n_sub // 2, SLAB_W), WORD),
                        pltpu.VMEM((n_sub // 2, SLAB_W), WORD)],
        compiler_params=pltpu.CompilerParams(
            dimension_semantics=("arbitrary",), vmem_limit_bytes=VMEM_LIMIT),
        name="s5_scan",
    )(uf, perm, perm.T, tb, cmat, lev_r, lev_i)


def _merge_kernel(ya_ref, yg_ref, gt_ref, x_ref, wglu_ref, bglu_ref, wa_ref, ws_ref, wo_ref,
                  gffn_ref, *refs):
    n_cast = (len(refs) - 2) // 2
    x1_ref, h2_ref = refs[n_cast:n_cast + 2]
    yg = yg_ref[...]
    t = jnp.dot(yg.astype(BF16), wglu_ref[...], preferred_element_type=F32) + bglu_ref[...]
    ys = (yg * _sigmoid(t)).astype(BF16)
    ma = jnp.dot(ya_ref[...], wa_ref[...], preferred_element_type=F32)
    ms = jnp.dot(ys, ws_ref[...], preferred_element_type=F32)
    merged = gt_ref[:, :D_MODEL].astype(F32) * ma + gt_ref[:, D_MODEL:].astype(F32) * ms
    x1 = x_ref[...] + jnp.dot(merged.astype(BF16), wo_ref[...], preferred_element_type=F32)
    x1_ref[...] = x1
    h2_ref[...] = _rms(x1, gffn_ref[...]).astype(BF16)
    _cast_blocks(refs[:n_cast], refs[n_cast + 2:])


def _merge(ya, yg, gates, x2, wglu, bglu, wa, ws, wo, gffn, cast_weights):
    t = x2.shape[0]
    tm = MERGE_TM
    casts = [_RowCast(w, wl, t // tm, lambda i: i) for w, wl in cast_weights]

    def wspec(rows, cols):
        return pl.BlockSpec((rows, cols), lambda i: (0, 0), pipeline_mode=pl.Buffered(1))

    outs = pl.pallas_call(
        _merge_kernel,
        grid=(t // tm,),
        in_specs=[
            pl.BlockSpec((tm, ATTN_WIDTH), lambda i: (i, 0)),
            pl.BlockSpec((tm, SSM_WIDTH), lambda i: (i, 0)),
            pl.BlockSpec((tm, 2 * D_MODEL), lambda i: (i, 0)),
            pl.BlockSpec((tm, D_MODEL), lambda i: (i, 0)),
            wspec(SSM_WIDTH, SSM_WIDTH),
            pl.BlockSpec((1, SSM_WIDTH), lambda i: (0, 0)),
            wspec(ATTN_WIDTH, D_MODEL),
            wspec(SSM_WIDTH, D_MODEL),
            wspec(D_MODEL, D_MODEL),
            pl.BlockSpec((1, D_MODEL), lambda i: (0, 0)),
        ] + [c.in_spec for c in casts],
        out_specs=[
            pl.BlockSpec((tm, D_MODEL), lambda i: (i, 0)),
            pl.BlockSpec((tm, D_MODEL), lambda i: (i, 0)),
        ] + [c.out_spec for c in casts],
        out_shape=[
            jax.ShapeDtypeStruct((t, D_MODEL), F32),
            jax.ShapeDtypeStruct((t, D_MODEL), BF16),
        ] + [c.out_shape for c in casts],
        compiler_params=pltpu.CompilerParams(
            dimension_semantics=("arbitrary",), vmem_limit_bytes=VMEM_LIMIT),
        name="merge_out",
    )(ya, yg, gates, x2, wglu, bglu, wa, ws, wo, gffn, *[c.operand for c in casts])
    return outs[:2], outs[2:]


def _ffn_kernel(h_ref, x_ref, wg_ref, wu_ref, wo_ref, o_ref):
    k = pl.program_id(1)

    def accumulate(base_ref):
        for r in range(FFN_TM // FFN_SUB):
            rs = pl.ds(r * FFN_SUB, FFN_SUB)
            h = h_ref[rs, :]
            g = jnp.dot(h, wg_ref[...], preferred_element_type=F32)
            u = jnp.dot(h, wu_ref[...], preferred_element_type=F32)
            act = (g * _sigmoid(g) * u).astype(BF16)
            o_ref[rs, :] = base_ref[rs, :] + jnp.dot(act, wo_ref[...], preferred_element_type=F32)

    @pl.when(k == 0)
    def _():
        accumulate(x_ref)

    @pl.when(k > 0)
    def _():
        accumulate(o_ref)


def _ffn(h2, x1, w_in, w_out):
    t = x1.shape[0]
    nk = D_FF // FFN_TF
    return pl.pallas_call(
        _ffn_kernel,
        grid=(t // FFN_TM, nk),
        in_specs=[
            pl.BlockSpec((FFN_TM, D_MODEL), lambda i, k: (i, 0)),
            pl.BlockSpec((FFN_TM, D_MODEL), lambda i, k: (i, 0)),
            pl.BlockSpec((D_MODEL, FFN_TF), lambda i, k: (0, k)),
            pl.BlockSpec((D_MODEL, FFN_TF), lambda i, k: (0, nk + k)),
            pl.BlockSpec((FFN_TF, D_MODEL), lambda i, k: (k, 0)),
        ],
        out_specs=pl.BlockSpec((FFN_TM, D_MODEL), lambda i, k: (i, 0)),
        out_shape=jax.ShapeDtypeStruct((t, D_MODEL), F32),
        compiler_params=pltpu.CompilerParams(
            dimension_semantics=("arbitrary", "arbitrary"), vmem_limit_bytes=FFN_VMEM_LIMIT),
        name="swiglu_ffn",
    )(h2, x1, w_in, w_in, w_out)


def kernel(x, norm_mix_g, w_in, gate_bias, q_norm_g, k_norm_g, attn_sinks, ssm_lambda_re, ssm_lambda_im, ssm_log_dt, ssm_b_re, ssm_b_im, ssm_c_re, ssm_c_im, ssm_d, ssm_glu_w, ssm_glu_b, w_attn_branch, w_ssm_branch, w_out, norm_ffn_g, w_ffn_in, w_ffn_out):
    batch, seq, _ = x.shape
    t = batch * seq
    n_levels = (seq // SUB).bit_length() - 1
    x2 = x.reshape(t, D_MODEL).astype(F32)
    f32 = lambda w: w.astype(F32)
    seg = _segment_ones()
    bias_tab = _attn_bias_table()
    perm = _pair_permutation()
    w_in_f = f32(w_in)
    (tb, cmat, lev_r, lev_i), (w_in_bf,) = _ssm_prep(
        ssm_lambda_re, ssm_lambda_im, ssm_log_dt, ssm_b_re, ssm_b_im, ssm_c_re, ssm_c_im, ssm_d,
        n_levels, [(w_in_f, 0)])
    merge_weights = [f32(ssm_glu_w), f32(w_attn_branch), f32(w_ssm_branch), f32(w_out)]
    ffn_weights = [f32(w_ffn_in), f32(w_ffn_out)]
    for l in range(DEPTH):
        next_w_in = [(w_in_f, l + 1)] if l + 1 < DEPTH else []
        (uf, gates, ya), cast_out = _inproj_attn(
            x2, norm_mix_g[l].reshape(1, D_MODEL).astype(F32), w_in_bf,
            gate_bias[l].reshape(1, 2 * D_MODEL).astype(F32), seg,
            _qk_norm_tables(q_norm_g[l], k_norm_g[l]), attn_sinks[l].astype(F32), bias_tab, seq,
            [(w, l) for w in merge_weights] + next_w_in)
        wglu_bf, wa_bf, ws_bf, wo_bf = cast_out[:4]
        w_in_bf = cast_out[4] if next_w_in else None
        yg = _ssm(uf, perm, tb, cmat, lev_r, lev_i, l, batch, seq)
        (x1, h2), (wfi_bf, wfo_bf) = _merge(
            ya, yg, gates, x2, wglu_bf, ssm_glu_b[l].reshape(1, SSM_WIDTH).astype(F32),
            wa_bf, ws_bf, wo_bf, norm_ffn_g[l].reshape(1, D_MODEL).astype(F32),
            [(w, l) for w in ffn_weights])
        x2 = _ffn(h2, x1, wfi_bf, wfo_bf)
    return x2.reshape(batch, seq, D_MODEL).astype(x.dtype)
```

```python
from functools import partial

import jax
import jax.numpy as jnp
from jax import lax
from jax.experimental import pallas as pl
from jax.experimental.pallas import tpu as pltpu

D_MODEL = 2048
DEPTH = 2
HEAD_DIM = 64
N_Q_HEADS = 16
N_KV_HEADS = 4
GQA_GROUP = N_Q_HEADS // N_KV_HEADS
ATTN_WIDTH = N_Q_HEADS * HEAD_DIM
KV_WIDTH = N_KV_HEADS * HEAD_DIM
WINDOW = 128
BLOCK = 128
SSM_WIDTH = D_MODEL // 2
SSM_GROUP_CH = 16
SSM_GROUPS = SSM_WIDTH // SSM_GROUP_CH
SSM_STATE = 64
D_FF = -(-8 * D_MODEL // (3 * 256)) * 256
OFF_K = ATTN_WIDTH
OFF_V = OFF_K + KV_WIDTH
OFF_U = OFF_V + KV_WIDTH
OFF_G = OFF_U + SSM_WIDTH
IN_WIDTH = OFF_G + 2 * D_MODEL
RMS_EPS = 1e-6

F32 = jnp.float32
BF16 = jnp.bfloat16
WORD = jnp.uint32
HIGHEST = lax.Precision.HIGHEST

LANES = 128
SUBLANES = 8
BF16_SUBLANES = 16
SUB = 16
SUB_W = SUB * SSM_GROUP_CH
STATE_W = 2 * SSM_STATE
TB_W = 2 * SUB_W + 2 * STATE_W
SLAB_GROUPS = LANES // SSM_GROUP_CH
SLAB_PAIRS = SLAB_GROUPS // 2
N_SLABS = SSM_GROUPS // SLAB_GROUPS
SLAB_W = SLAB_GROUPS * SUB_W

V7X_VMEM_BYTES = 64 * 1024 * 1024
VMEM_LIMIT = V7X_VMEM_BYTES * 7 // 8
FFN_VMEM_LIMIT = V7X_VMEM_BYTES * 15 // 16

INPROJ_TM = 256
INPROJ_TN = 512
MERGE_TM = 256
FFN_TM = 1024
FFN_TF = 512
FFN_SUB = 512


def _rms(x, g):
    return x * lax.rsqrt(jnp.mean(x * x, axis=-1, keepdims=True) + RMS_EPS) * g


def _sigmoid(x):
    return 0.5 * jnp.tanh(0.5 * x) + 0.5


class _RowCast:
    def __init__(self, stacked, layer, n_steps, step_of):
        _, rows, cols = stacked.shape
        blk = rows // n_steps
        assert blk * n_steps == rows and blk % BF16_SUBLANES == 0, (rows, n_steps)
        self.operand = stacked
        self.in_spec = pl.BlockSpec((None, blk, cols), lambda *ids: (layer, step_of(*ids), 0))
        self.out_spec = pl.BlockSpec((blk, cols), lambda *ids: (step_of(*ids), 0))
        self.out_shape = jax.ShapeDtypeStruct((rows, cols), BF16)


def _cast_blocks(src_refs, dst_refs):
    for src, dst in zip(src_refs, dst_refs):
        dst[...] = src[...].astype(BF16)


N_QKV_BLOCKS = OFF_U // INPROJ_TN
N_MAIN_BLOCKS = OFF_G // INPROJ_TN
N_IN_BLOCKS = IN_WIDTH // INPROJ_TN


QBLOCKS_PER_TILE = INPROJ_TM // BLOCK
ATTN_PROBLEMS_PER_GAP = 2


def _inproj_attn_kernel(sink_ref, x_ref, g_ref, w_ref, w_hbm_ref, b_ref, seg_ref, nt_ref, bias_ref, *refs,
                        tiles_per_seq):
    n_cast = (len(refs) - 7) // 2
    u_ref, gt_ref, ya_ref = refs[n_cast:n_cast + 3]
    q_ref, kv_ref, wgate_ref, wgate_sem = refs[2 * n_cast + 3:]
    first_step = pl.program_id(0) == 0

    def gate_weights_copy():
        return pltpu.make_async_copy(w_hbm_ref.at[:, pl.ds(OFF_G, 2 * D_MODEL)], wgate_ref, wgate_sem)

    @pl.when(first_step)
    def _():
        gate_weights_copy().start()
        kv_ref[...] = jnp.zeros(kv_ref.shape, BF16)

    kv_ref[0:BLOCK, :] = kv_ref[INPROJ_TM:INPROJ_TM + BLOCK, :]
    h = _rms(x_ref[...], g_ref[...]).astype(BF16)

    def project(j):
        if j < N_MAIN_BLOCKS:
            w = w_ref[:, j * INPROJ_TN:(j + 1) * INPROJ_TN]
        else:
            w = wgate_ref[:, (j - N_MAIN_BLOCKS) * INPROJ_TN:(j - N_MAIN_BLOCKS + 1) * INPROJ_TN]
        return jnp.dot(h, w, preferred_element_type=F32)

    for j in range(N_QKV_BLOCKS):
        z = project(j)
        ssq = jnp.dot((z * z).astype(BF16), seg_ref[...], preferred_element_type=F32)
        inv = lax.rsqrt(ssq * (1.0 / HEAD_DIM) + RMS_EPS)
        fac = jnp.where(nt_ref[j, 1:2, :] > 0.0, inv, 1.0) * nt_ref[j, 0:1, :]
        if (j + 1) * INPROJ_TN <= ATTN_WIDTH:
            q_ref[:, j * INPROJ_TN:(j + 1) * INPROJ_TN] = (z * fac).astype(BF16)
        else:
            kv_ref[BLOCK:, :] = (z * fac).astype(BF16)

    def other_block(j):
        if j == N_MAIN_BLOCKS:
            pl.when(first_step)(lambda: gate_weights_copy().wait())
        z = project(j)
        if j < N_MAIN_BLOCKS:
            c = j - N_QKV_BLOCKS
            u_ref[:, c * INPROJ_TN:(c + 1) * INPROJ_TN] = z
        else:
            cs = slice((j - N_MAIN_BLOCKS) * INPROJ_TN, (j - N_MAIN_BLOCKS + 1) * INPROJ_TN)
            gt_ref[:, cs] = _sigmoid(z + b_ref[:, cs]).astype(BF16)

    kv_rows = 2 * BLOCK
    all_rows = kv_ref.shape[0]
    left_kv = lax.broadcasted_iota(jnp.int32, (all_rows, LANES), 1) < HEAD_DIM
    left_q = lax.broadcasted_iota(jnp.int32, (BLOCK, LANES), 1) < HEAD_DIM
    zeros = jnp.zeros((all_rows, LANES), BF16)
    left_ones = lax.broadcasted_iota(jnp.int32, (kv_rows, LANES), 1) < HEAD_DIM
    ones_l = jnp.where(left_ones, 1.0, 0.0).astype(BF16)
    ones_r = jnp.where(left_ones, 0.0, 1.0).astype(BF16)
    contract_lanes = (((1,), (1,)), ((), ()))

    def slab(off):
        a = kv_ref[:, off:off + LANES]
        return a, pltpu.roll(a.astype(F32), HEAD_DIM, axis=1).astype(BF16)

    problems = []
    for c in range(N_KV_HEADS // 2):
        k_slabs = slab(c * LANES)
        v_slabs = slab(KV_WIDTH + c * LANES)
        for side in range(2):
            kh = 2 * c + side
            k_l = jnp.where(left_kv, k_slabs[side], zeros)
            k_r = jnp.where(left_kv, zeros, k_slabs[1 - side])
            v_l = jnp.where(left_kv, v_slabs[side], zeros)
            v_r = jnp.where(left_kv, zeros, v_slabs[1 - side])
            for qb in range(QBLOCKS_PER_TILE):
                rows = slice(qb * BLOCK, qb * BLOCK + kv_rows)
                kk = jnp.concatenate([k_l[rows], k_r[rows]], axis=0)
                vv = jnp.concatenate([jnp.concatenate([v_l[rows], ones_l], axis=1),
                                      jnp.concatenate([v_r[rows], ones_r], axis=1)], axis=0)
                for pair in range(GQA_GROUP // 2):
                    problems.append((qb, kh * GQA_GROUP + 2 * pair, kk, vv))

    def scores(qb, e, kk):
        q2 = q_ref[qb * BLOCK:(qb + 1) * BLOCK, e * HEAD_DIM:(e + 2) * HEAD_DIM]
        return lax.dot_general(q2, kk, contract_lanes, preferred_element_type=F32)

    no_prev = jnp.logical_and(lax.broadcasted_iota(jnp.int32, (BLOCK, kv_rows), 1) < BLOCK,
                              pl.program_id(0) % tiles_per_seq == 0)

    def finish(qb, e, s2, vv):
        ps, ds = [], []
        for t in range(2):
            s = s2[:, t * kv_rows:(t + 1) * kv_rows] + bias_ref[e + t]
            if qb == 0:
                s = jnp.where(no_prev, -jnp.inf, s)
            sink = sink_ref[e + t]
            m = jnp.maximum(jnp.max(s, axis=-1, keepdims=True), sink)
            ps.append(jnp.exp(s - m).astype(BF16))
            ds.append(jnp.exp(sink - m))
        r = jnp.dot(jnp.concatenate(ps, axis=1), vv, preferred_element_type=F32)
        denom = r[:, LANES:] + jnp.where(left_q, ds[0], ds[1])
        ya_ref[qb * BLOCK:(qb + 1) * BLOCK, e * HEAD_DIM:(e + 2) * HEAD_DIM] = (r[:, :LANES] / denom).astype(BF16)

    others = list(range(N_QKV_BLOCKS, N_IN_BLOCKS))
    for first in range(0, len(problems), ATTN_PROBLEMS_PER_GAP):
        group = problems[first:first + ATTN_PROBLEMS_PER_GAP]
        s2s = [scores(qb, e, kk) for qb, e, kk, _ in group]
        if others:
            other_block(others.pop(0))
        for (qb, e, _, vv), s2 in zip(group, s2s):
            finish(qb, e, s2, vv)
    for j in others:
        other_block(j)
    _cast_blocks(refs[:n_cast], refs[n_cast + 3:2 * n_cast + 3])


def _inproj_attn(x2, gain, w_bf, bias, seg, ntab, sinks, bias_tab, seq, cast_weights):
    t = x2.shape[0]
    tm = INPROJ_TM
    tiles_per_seq = seq // tm
    const2 = lambda i: (0, 0)
    casts = [_RowCast(w, wl, t // tm, lambda i: i) for w, wl in cast_weights]
    outs = pl.pallas_call(
        partial(_inproj_attn_kernel, tiles_per_seq=tiles_per_seq),
        grid=(t // tm,),
        in_specs=[
            pl.BlockSpec(memory_space=pltpu.SMEM),
            pl.BlockSpec((tm, D_MODEL), lambda i: (i, 0)),
            pl.BlockSpec((1, D_MODEL), const2),
            pl.BlockSpec((D_MODEL, OFF_G), const2, pipeline_mode=pl.Buffered(1)),
            pl.BlockSpec(memory_space=pl.ANY),
            pl.BlockSpec((1, 2 * D_MODEL), const2),
            pl.BlockSpec((INPROJ_TN, INPROJ_TN), const2),
            pl.BlockSpec((N_QKV_BLOCKS, 2, INPROJ_TN), lambda i: (0, 0, 0)),
            pl.BlockSpec((N_Q_HEADS, BLOCK, 2 * BLOCK), lambda i: (0, 0, 0), pipeline_mode=pl.Buffered(1)),
        ] + [c.in_spec for c in casts],
        out_specs=[
            pl.BlockSpec((tm, SSM_WIDTH), lambda i: (i, 0)),
            pl.BlockSpec((tm, 2 * D_MODEL), lambda i: (i, 0)),
            pl.BlockSpec((tm, ATTN_WIDTH), lambda i: (i, 0)),
        ] + [c.out_spec for c in casts],
        out_shape=[
            jax.ShapeDtypeStruct((t, SSM_WIDTH), F32),
            jax.ShapeDtypeStruct((t, 2 * D_MODEL), BF16),
            jax.ShapeDtypeStruct((t, ATTN_WIDTH), BF16),
        ] + [c.out_shape for c in casts],
        scratch_shapes=[pltpu.VMEM((tm, ATTN_WIDTH), BF16), pltpu.VMEM((BLOCK + tm, 2 * KV_WIDTH), BF16),
                        pltpu.VMEM((D_MODEL, 2 * D_MODEL), BF16), pltpu.SemaphoreType.DMA(())],
        compiler_params=pltpu.CompilerParams(
            dimension_semantics=("arbitrary",), vmem_limit_bytes=VMEM_LIMIT),
        name="inproj_attn",
    )(sinks, x2, gain, w_bf, w_bf, bias, seg, ntab, bias_tab, *[c.operand for c in casts])
    return outs[:3], outs[3:]


def _qk_norm_tables(q_gain, k_gain):
    qrow = jnp.tile(q_gain.astype(F32), INPROJ_TN // HEAD_DIM) * (HEAD_DIM ** -0.5)
    ones = jnp.ones((INPROJ_TN,), F32)
    kvrow = jnp.concatenate([jnp.tile(k_gain.astype(F32), N_KV_HEADS), jnp.ones((KV_WIDTH,), F32)])
    kvmask = jnp.concatenate([jnp.ones((KV_WIDTH,), F32), jnp.zeros((KV_WIDTH,), F32)])
    return jnp.stack([jnp.stack([qrow, ones]), jnp.stack([qrow, ones]), jnp.stack([kvrow, kvmask])])


def _segment_ones():
    r = jnp.arange(INPROJ_TN) // HEAD_DIM
    return (r[:, None] == r[None, :]).astype(BF16)


def _attn_bias_table():
    t_loc = jnp.arange(BLOCK)[:, None]
    s_loc = jnp.arange(2 * BLOCK)[None, :] - BLOCK
    dist = (t_loc - s_loc).astype(F32)
    valid = (dist >= 0) & (dist < WINDOW)
    slopes = jnp.exp2(-8.0 * jnp.arange(1, N_Q_HEADS + 1, dtype=F32) / N_Q_HEADS)
    return jnp.where(valid[None], -slopes[:, None, None] * dist[None], -jnp.inf)


def _cmul(ar, ai, br, bi):
    return ar * br - ai * bi, ar * bi + ai * br


def _pow_by_bits(exps, squares):
    pr = pi = None
    for b, (sr, si) in enumerate(squares):
        on = ((exps >> b) & 1) == 1
        fr, fi = jnp.where(on, sr, 1.0), jnp.where(on, si, 0.0)
        pr, pi = (fr, fi) if pr is None else _cmul(pr, pi, fr, fi)
    return pr, pi


def _ssm_prep_kernel(lam_ref, ldt_ref, btr_ref, bti_ref, ctr_ref, cti_ref, d_ref, *refs):
    n_cast = (len(refs) - 4) // 2
    tb_ref, c_ref, levr_ref, levi_ref = refs[n_cast:n_cast + 4]
    _cast_blocks(refs[:n_cast], refs[n_cast + 4:])
    n_levels = levr_ref.shape[1]
    tau_lane = lax.broadcasted_iota(jnp.int32, (SSM_STATE, SUB_W), 1) // SSM_GROUP_CH
    row_h = lax.broadcasted_iota(jnp.int32, (SSM_GROUP_CH, SUB_W), 0)
    lane_h = lax.broadcasted_iota(jnp.int32, (SSM_GROUP_CH, SUB_W), 1)
    tile_ch = (lane_h % SSM_GROUP_CH == row_h).astype(F32)
    row_m = lax.broadcasted_iota(jnp.int32, (SUB, SSM_STATE), 0)
    eye_p = (lax.broadcasted_iota(jnp.int32, (SSM_STATE, SSM_STATE), 0)
             == lax.broadcasted_iota(jnp.int32, (SSM_STATE, SSM_STATE), 1))

    def to_col(v):
        return jnp.sum(jnp.where(eye_p, v, 0.0), axis=1, keepdims=True)

    def squares(a, n):
        out = [a]
        for _ in range(n - 1):
            out.append(_cmul(*out[-1], *out[-1]))
        return out

    for g in range(SLAB_GROUPS):
        dt = jnp.exp(ldt_ref[g])

        def discretise(lr, li):
            mag = jnp.exp(lr * dt)
            return mag * jnp.cos(li * dt), mag * jnp.sin(li * dt)

        lr_row, li_row = lam_ref[g, 0:1, :], lam_ref[g, 1:2, :]
        a_row = discretise(lr_row, li_row)
        sq_row = squares(a_row, 5)
        sq_col = squares((to_col(a_row[0]), to_col(a_row[1])), 4)
        ar, ai = sq_row[0]
        den = lr_row * lr_row + li_row * li_row
        fr = ((ar - 1.0) * lr_row + ai * li_row) / den
        fi = (ai * lr_row - (ar - 1.0) * li_row) / den
        bbr, bbi = _cmul(fr, fi, btr_ref[g], bti_ref[g])

        e0 = _pow_by_bits(tau_lane, sq_col)
        e1 = _cmul(*e0, *sq_col[0])
        ctr = jnp.dot(ctr_ref[g], tile_ch, precision=HIGHEST, preferred_element_type=F32)
        cti = jnp.dot(cti_ref[g], tile_ch, precision=HIGHEST, preferred_element_type=F32)
        mr, mi = _cmul(*e0, ctr, cti)
        kt = (jnp.dot(bbr, mr, precision=HIGHEST, preferred_element_type=F32)
              - jnp.dot(bbi, mi, precision=HIGHEST, preferred_element_type=F32))
        kt = kt + jnp.where(lane_h == row_h, d_ref[g], 0.0)
        pw = _pow_by_bits(row_m, sq_row[:4])
        q, side = g // 2, g % 2
        own = lambda width: slice(side * width, (side + 1) * width)
        other = lambda width: slice((1 - side) * width, (2 - side) * width)
        zeros_p = jnp.zeros((SSM_GROUP_CH, SSM_STATE), F32)

        def paired(v):
            return jnp.concatenate([v, zeros_p] if side == 0 else [zeros_p, v], axis=1).astype(BF16)

        for j in range(SUB):
            rows = pl.ds(side * SUB_W + j * SSM_GROUP_CH, SSM_GROUP_CH)
            tj = kt if j == 0 else jnp.where(lane_h >= j * SSM_GROUP_CH,
                                             pltpu.roll(kt, j * SSM_GROUP_CH, axis=1), 0.0)
            tb_ref[q, rows, own(SUB_W)] = tj.astype(BF16)
            tb_ref[q, rows, other(SUB_W)] = jnp.zeros((SSM_GROUP_CH, SUB_W), BF16)
            m = SUB - 1 - j
            br_, bi_ = _cmul(pw[0][m:m + 1, :], pw[1][m:m + 1, :], bbr, bbi)
            tb_ref[q, rows, 2 * SUB_W:2 * SUB_W + STATE_W] = paired(br_)
            tb_ref[q, rows, 2 * SUB_W + STATE_W:TB_W] = paired(bi_)
        for part, val in enumerate([ctr * e1[0] - cti * e1[1], -ctr * e1[1] - cti * e1[0]]):
            rows = pl.ds(part * STATE_W + side * SSM_STATE, SSM_STATE)
            c_ref[q, rows, own(SUB_W)] = val.astype(BF16)
            c_ref[q, rows, other(SUB_W)] = jnp.zeros((SSM_STATE, SUB_W), BF16)
        lv = sq_row[4]
        lev_r, lev_i = [], []
        for _ in range(n_levels):
            lev_r.append(lv[0])
            lev_i.append(lv[1])
            lv = _cmul(*lv, *lv)
        levels = (jnp.concatenate(lev_r, axis=0), jnp.concatenate(lev_i, axis=0))
        if side == 0:
            first_levels = levels
        else:
            levr_ref[q] = jnp.concatenate([first_levels[0], levels[0]], axis=1)
            levi_ref[q] = jnp.concatenate([first_levels[1], levels[1]], axis=1)


def _ssm_prep(lam_re, lam_im, log_dt, b_re, b_im, c_re, c_im, d_skip, n_levels, cast_weights):
    dg = lam_re.shape[0] * SSM_GROUPS
    casts = [_RowCast(w, wl, dg // SLAB_GROUPS, lambda s: s) for w, wl in cast_weights]
    h_, p_ = SSM_GROUP_CH, SSM_STATE
    f = lambda a: a.astype(F32)
    lam_rows = jnp.stack([f(lam_re), f(lam_im)], axis=2).reshape(dg, 2, p_)
    btr = f(b_re).transpose(0, 1, 3, 2).reshape(dg, h_, p_)
    bti = f(b_im).transpose(0, 1, 3, 2).reshape(dg, h_, p_)
    ctr = f(c_re).transpose(0, 1, 3, 2).reshape(dg, p_, h_)
    cti = f(c_im).transpose(0, 1, 3, 2).reshape(dg, p_, h_)
    d_rows = jnp.pad(f(d_skip).reshape(dg, 1, h_), ((0, 0), (0, 0), (0, SUB_W - h_)))
    grp = lambda *shape: pl.BlockSpec((SLAB_GROUPS,) + shape, lambda s: (s,) + (0,) * len(shape))
    pair = lambda *shape: pl.BlockSpec((SLAB_PAIRS,) + shape, lambda s: (s,) + (0,) * len(shape))
    outs = pl.pallas_call(
        _ssm_prep_kernel,
        grid=(dg // SLAB_GROUPS,),
        in_specs=[grp(2, p_), grp(1, 1), grp(h_, p_), grp(h_, p_),
                  grp(p_, h_), grp(p_, h_), grp(1, SUB_W)] + [c.in_spec for c in casts],
        out_specs=[pair(2 * SUB_W, TB_W), pair(2 * STATE_W, 2 * SUB_W), pair(n_levels, STATE_W),
                   pair(n_levels, STATE_W)] + [c.out_spec for c in casts],
        out_shape=[
            jax.ShapeDtypeStruct((dg // 2, 2 * SUB_W, TB_W), BF16),
            jax.ShapeDtypeStruct((dg // 2, 2 * STATE_W, 2 * SUB_W), BF16),
            jax.ShapeDtypeStruct((dg // 2, n_levels, STATE_W), F32),
            jax.ShapeDtypeStruct((dg // 2, n_levels, STATE_W), F32),
        ] + [c.out_shape for c in casts],
        compiler_params=pltpu.CompilerParams(
            dimension_semantics=("arbitrary",), vmem_limit_bytes=VMEM_LIMIT),
        name="s5_prep",
    )(lam_rows, f(log_dt).reshape(dg, 1, 1), btr, bti, ctr, cti, d_rows, *[c.operand for c in casts])
    return outs[:4], outs[4:]


CHUNK = 2 * SSM_GROUP_CH
CHUNKS = LANES // CHUNK
PAIRS = SUB // 2
PAIR_W = 2 * LANES


def _pair_permutation():
    r = jnp.arange(PAIR_W)
    jj, lane = r // LANES, r % LANES
    col = (lane // SSM_GROUP_CH) * CHUNK + jj * SSM_GROUP_CH + lane % SSM_GROUP_CH
    return (col[:, None] == jnp.arange(PAIR_W)[None, :]).astype(BF16)


S5_STAGES = 3


def _ssm_kernel(u_ref, perm_ref, perm_t_ref, tb_ref, c_ref, ar_ref, ai_ref, y_ref, xs_ref, ys_ref, cs_ref):
    t = pl.program_id(0)
    rows = 2 * xs_ref.shape[1]
    n_levels = rows.bit_length() - 1
    cur, prv = t % 2, (t + 1) % 2

    @pl.when(t == 0)
    def _():
        xs_ref[...] = jnp.zeros(xs_ref.shape, WORD)
        ys_ref[...] = jnp.zeros(ys_ref.shape, WORD)

    def shifted(a, sh):
        return jnp.concatenate([jnp.zeros((sh, a.shape[1]), a.dtype), a[:rows - sh]], axis=0)

    def rotations(a):
        return [a] + [pltpu.roll(a, r * CHUNK, axis=1) for r in range(1, CHUNKS)]

    def chunk(a, c):
        return a[:, c * CHUNK:(c + 1) * CHUNK]

    def to_words(a):
        return pltpu.bitcast(a.astype(BF16), WORD)

    def from_words(a):
        return pltpu.bitcast(a, BF16)

    def scan_inputs(q):
        return jnp.dot(from_words(xs_ref[prv, :, q * 2 * SUB_W:(q + 1) * 2 * SUB_W]), tb_ref[q],
                       preferred_element_type=F32)

    def layout_out_chunks(k):
        col, dst = k // CHUNKS, k % CHUNKS
        yk = ys_ref[cur, :, k * SUB_W:(k + 1) * SUB_W]
        for v in range(SUB_W // LANES):
            rots = rotations(yk[:, v * LANES:(v + 1) * LANES])
            for src in range(CHUNKS):
                lane0 = (v * CHUNKS + src) * PAIR_W + col * LANES + dst * CHUNK
                cs_ref[:, lane0:lane0 + CHUNK] = chunk(rots[(dst - src) % CHUNKS], dst)

    def layout_in_chunks(k):
        col, dst = k // CHUNKS, k % CHUNKS
        pair_tile = jnp.concatenate([u_ref[pl.ds(2 * k, rows, stride=SUB), :],
                                     u_ref[pl.ds(2 * k + 1, rows, stride=SUB), :]], axis=1).astype(BF16)
        grouped = to_words(jnp.dot(pair_tile, perm_ref[...], preferred_element_type=F32))
        for w in range(PAIR_W // LANES):
            rots = rotations(grouped[:, w * LANES:(w + 1) * LANES])
            for src in range(CHUNKS):
                lane0 = (w * CHUNKS + src) * SUB_W + col * LANES + dst * CHUNK
                xs_ref[cur, :, lane0:lane0 + CHUNK] = chunk(rots[(dst - src) % CHUNKS], dst)

    r_next = scan_inputs(0)
    for q in range(SLAB_PAIRS):
        r = r_next
        if q + 1 < SLAB_PAIRS:
            r_next = scan_inputs(q + 1)
        for k in (2 * q, 2 * q + 1):
            layout_out_chunks(k)
            layout_in_chunks(k)
        yt = r[:, :2 * SUB_W]
        re = r[:, 2 * SUB_W:2 * SUB_W + STATE_W]
        im = r[:, 2 * SUB_W + STATE_W:]
        for lvl in range(n_levels):
            sh = 1 << lvl
            ar = ar_ref[q, lvl:lvl + 1, :]
            ai = ai_ref[q, lvl:lvl + 1, :]
            if sh % SUBLANES:
                pr, pi = shifted(re, sh), shifted(im, sh)
                re, im = re + ar * pr - ai * pi, im + ar * pi + ai * pr
            else:
                pr, pi = re[:rows - sh], im[:rows - sh]
                re, im = (jnp.concatenate([re[:sh], re[sh:] + ar * pr - ai * pi], axis=0),
                          jnp.concatenate([im[:sh], im[sh:] + ar * pi + ai * pr], axis=0))
        prev_state = jnp.concatenate([shifted(re, 1), shifted(im, 1)], axis=1).astype(BF16)
        y = yt + jnp.dot(prev_state, c_ref[q], preferred_element_type=F32)
        ys_ref[prv, :, q * 2 * SUB_W:(q + 1) * 2 * SUB_W] = to_words(jax.nn.gelu(y))

    for pair in range(PAIRS):
        o = jnp.dot(from_words(cs_ref[:, pair * PAIR_W:(pair + 1) * PAIR_W]), perm_t_ref[...],
                    preferred_element_type=F32)
        y_ref[pl.ds(2 * pair, rows, stride=SUB), :] = o[:, :LANES]
        y_ref[pl.ds(2 * pair + 1, rows, stride=SUB), :] = o[:, LANES:]


def _ssm(uf, perm, tb, cmat, lev_r, lev_i, layer, batch, seq):
    t = uf.shape[0]
    n_sub = seq // SUB
    n_levels = lev_r.shape[1]
    n_items = batch * N_SLABS
    item = lambda t, stage: jnp.clip(t - stage, 0, n_items - 1)
    slab3 = lambda t: (layer * N_SLABS + item(t, 1) % N_SLABS, 0, 0)
    return pl.pallas_call(
        _ssm_kernel,
        grid=(n_items + S5_STAGES - 1,),
        in_specs=[
            pl.BlockSpec((seq, LANES), lambda t: (item(t, 0) // N_SLABS, item(t, 0) % N_SLABS)),
            pl.BlockSpec((PAIR_W, PAIR_W), lambda t: (0, 0)),
            pl.BlockSpec((PAIR_W, PAIR_W), lambda t: (0, 0)),
            pl.BlockSpec((SLAB_PAIRS, 2 * SUB_W, TB_W), slab3),
            pl.BlockSpec((SLAB_PAIRS, 2 * STATE_W, 2 * SUB_W), slab3),
            pl.BlockSpec((SLAB_PAIRS, n_levels, STATE_W), slab3),
            pl.BlockSpec((SLAB_PAIRS, n_levels, STATE_W), slab3),
        ],
        out_specs=pl.BlockSpec((seq, LANES), lambda t: (item(t, 2) // N_SLABS, item(t, 2) % N_SLABS)),
        out_shape=jax.ShapeDtypeStruct((t, SSM_WIDTH), F32),
        scratch_shapes=[pltpu.VMEM((2, n_sub // 2, SLAB_W), WORD), pltpu.VMEM((2, n_sub // 2, SLAB_W), WORD),
                        pltpu.VMEM((n_sub // 2, SLAB_W), WORD)],
        compiler_params=pltpu.CompilerParams(
            dimension_semantics=("arbitrary",), vmem_limit_bytes=VMEM_LIMIT),
        name="s5_scan",
    )(uf, perm, perm.T, tb, cmat, lev_r, lev_i)


def _merge_kernel(ya_ref, yg_ref, gt_ref, x_ref, wglu_ref, bglu_ref, wa_ref, ws_ref, wo_ref,
                  gffn_ref, *refs):
    n_cast = (len(refs) - 2) // 2
    x1_ref, h2_ref = refs[n_cast:n_cast + 2]
    yg = yg_ref[...]
    t = jnp.dot(yg.astype(BF16), wglu_ref[...], preferred_element_type=F32) + bglu_ref[...]
    ys = (yg * _sigmoid(t)).astype(BF16)
    ma = jnp.dot(ya_ref[...], wa_ref[...], preferred_element_type=F32)
    ms = jnp.dot(ys, ws_ref[...], preferred_element_type=F32)
    merged = gt_ref[:, :D_MODEL].astype(F32) * ma + gt_ref[:, D_MODEL:].astype(F32) * ms
    x1 = x_ref[...] + jnp.dot(merged.astype(BF16), wo_ref[...], preferred_element_type=F32)
    x1_ref[...] = x1
    h2_ref[...] = _rms(x1, gffn_ref[...]).astype(BF16)
    _cast_blocks(refs[:n_cast], refs[n_cast + 2:])


def _merge(ya, yg, gates, x2, wglu, bglu, wa, ws, wo, gffn, cast_weights):
    t = x2.shape[0]
    tm = MERGE_TM
    casts = [_RowCast(w, wl, t // tm, lambda i: i) for w, wl in cast_weights]

    def wspec(rows, cols):
        return pl.BlockSpec((rows, cols), lambda i: (0, 0), pipeline_mode=pl.Buffered(1))

    outs = pl.pallas_call(
        _merge_kernel,
        grid=(t // tm,),
        in_specs=[
            pl.BlockSpec((tm, ATTN_WIDTH), lambda i: (i, 0)),
            pl.BlockSpec((tm, SSM_WIDTH), lambda i: (i, 0)),
            pl.BlockSpec((tm, 2 * D_MODEL), lambda i: (i, 0)),
            pl.BlockSpec((tm, D_MODEL), lambda i: (i, 0)),
            wspec(SSM_WIDTH, SSM_WIDTH),
            pl.BlockSpec((1, SSM_WIDTH), lambda i: (0, 0)),
            wspec(ATTN_WIDTH, D_MODEL),
            wspec(SSM_WIDTH, D_MODEL),
            wspec(D_MODEL, D_MODEL),
            pl.BlockSpec((1, D_MODEL), lambda i: (0, 0)),
        ] + [c.in_spec for c in casts],
        out_specs=[
            pl.BlockSpec((tm, D_MODEL), lambda i: (i, 0)),
            pl.BlockSpec((tm, D_MODEL), lambda i: (i, 0)),
        ] + [c.out_spec for c in casts],
        out_shape=[
            jax.ShapeDtypeStruct((t, D_MODEL), F32),
            jax.ShapeDtypeStruct((t, D_MODEL), BF16),
        ] + [c.out_shape for c in casts],
        compiler_params=pltpu.CompilerParams(
            dimension_semantics=("arbitrary",), vmem_limit_bytes=VMEM_LIMIT),
        name="merge_out",
    )(ya, yg, gates, x2, wglu, bglu, wa, ws, wo, gffn, *[c.operand for c in casts])
    return outs[:2], outs[2:]


def _ffn_kernel(h_ref, x_ref, wg_ref, wu_ref, wo_ref, o_ref):
    k = pl.program_id(1)

    def accumulate(base_ref):
        for r in range(FFN_TM // FFN_SUB):
            rs = pl.ds(r * FFN_SUB, FFN_SUB)
            h = h_ref[rs, :]
            g = jnp.dot(h, wg_ref[...], preferred_element_type=F32)
            u = jnp.dot(h, wu_ref[...], preferred_element_type=F32)
            act = (g * _sigmoid(g) * u).astype(BF16)
            o_ref[rs, :] = base_ref[rs, :] + jnp.dot(act, wo_ref[...], preferred_element_type=F32)

    @pl.when(k == 0)
    def _():
        accumulate(x_ref)

    @pl.when(k > 0)
    def _():
        accumulate(o_ref)


def _ffn(h2, x1, w_in, w_out):
    t = x1.shape[0]
    nk = D_FF // FFN_TF
    return pl.pallas_call(
        _ffn_kernel,
        grid=(t // FFN_TM, nk),
        in_specs=[
            pl.BlockSpec((FFN_TM, D_MODEL), lambda i, k: (i, 0)),
            pl.BlockSpec((FFN_TM, D_MODEL), lambda i, k: (i, 0)),
            pl.BlockSpec((D_MODEL, FFN_TF), lambda i, k: (0, k)),
            pl.BlockSpec((D_MODEL, FFN_TF), lambda i, k: (0, nk + k)),
            pl.BlockSpec((FFN_TF, D_MODEL), lambda i, k: (k, 0)),
        ],
        out_specs=pl.BlockSpec((FFN_TM, D_MODEL), lambda i, k: (i, 0)),
        out_shape=jax.ShapeDtypeStruct((t, D_MODEL), F32),
        compiler_params=pltpu.CompilerParams(
            dimension_semantics=("arbitrary", "arbitrary"), vmem_limit_bytes=FFN_VMEM_LIMIT),
        name="swiglu_ffn",
    )(h2, x1, w_in, w_in, w_out)


def kernel(x, norm_mix_g, w_in, gate_bias, q_norm_g, k_norm_g, attn_sinks, ssm_lambda_re, ssm_lambda_im, ssm_log_dt, ssm_b_re, ssm_b_im, ssm_c_re, ssm_c_im, ssm_d, ssm_glu_w, ssm_glu_b, w_attn_branch, w_ssm_branch, w_out, norm_ffn_g, w_ffn_in, w_ffn_out):
    batch, seq, _ = x.shape
    t = batch * seq
    n_levels = (seq // SUB).bit_length() - 1
    x2 = x.reshape(t, D_MODEL).astype(F32)
    f32 = lambda w: w.astype(F32)
    seg = _segment_ones()
    bias_tab = _attn_bias_table()
    perm = _pair_permutation()
    w_in_f = f32(w_in)
    (tb, cmat, lev_r, lev_i), (w_in_bf,) = _ssm_prep(
        ssm_lambda_re, ssm_lambda_im, ssm_log_dt, ssm_b_re, ssm_b_im, ssm_c_re, ssm_c_im, ssm_d,
        n_levels, [(w_in_f, 0)])
    merge_weights = [f32(ssm_glu_w), f32(w_attn_branch), f32(w_ssm_branch), f32(w_out)]
    ffn_weights = [f32(w_ffn_in), f32(w_ffn_out)]
    for l in range(DEPTH):
        next_w_in = [(w_in_f, l + 1)] if l + 1 < DEPTH else []
        (uf, gates, ya), cast_out = _inproj_attn(
            x2, norm_mix_g[l].reshape(1, D_MODEL).astype(F32), w_in_bf,
            gate_bias[l].reshape(1, 2 * D_MODEL).astype(F32), seg,
            _qk_norm_tables(q_norm_g[l], k_norm_g[l]), attn_sinks[l].astype(F32), bias_tab, seq,
            [(w, l) for w in merge_weights] + next_w_in)
        wglu_bf, wa_bf, ws_bf, wo_bf = cast_out[:4]
        w_in_bf = cast_out[4] if next_w_in else None
        yg = _ssm(uf, perm, tb, cmat, lev_r, lev_i, l, batch, seq)
        (x1, h2), (wfi_bf, wfo_bf) = _merge(
            ya, yg, gates, x2, wglu_bf, ssm_glu_b[l].reshape(1, SSM_WIDTH).astype(F32),
            wa_bf, ws_bf, wo_bf, norm_ffn_g[l].reshape(1, D_MODEL).astype(F32),
            [(w, l) for w in ffn_weights])
        x2 = _ffn(h2, x1, wfi_bf, wfo_bf)
    return x2.reshape(batch, seq, D_MODEL).astype(x.dtype)
```

```python
from functools import partial

import jax
import jax.numpy as jnp
from jax import lax
from jax.experimental import pallas as pl
from jax.experimental.pallas import tpu as pltpu

D_MODEL = 2048
DEPTH = 2
HEAD_DIM = 64
N_Q_HEADS = 16
N_KV_HEADS = 4
GQA_GROUP = N_Q_HEADS // N_KV_HEADS
ATTN_WIDTH = N_Q_HEADS * HEAD_DIM
KV_WIDTH = N_KV_HEADS * HEAD_DIM
WINDOW = 128
BLOCK = 128
SSM_WIDTH = D_MODEL // 2
SSM_GROUP_CH = 16
SSM_GROUPS = SSM_WIDTH // SSM_GROUP_CH
SSM_STATE = 64
D_FF = -(-8 * D_MODEL // (3 * 256)) * 256
OFF_K = ATTN_WIDTH
OFF_V = OFF_K + KV_WIDTH
OFF_U = OFF_V + KV_WIDTH
OFF_G = OFF_U + SSM_WIDTH
IN_WIDTH = OFF_G + 2 * D_MODEL
RMS_EPS = 1e-6

F32 = jnp.float32
BF16 = jnp.bfloat16
WORD = jnp.uint32
HIGHEST = lax.Precision.HIGHEST

LANES = 128
SUBLANES = 8
BF16_SUBLANES = 16
SUB = 16
SUB_W = SUB * SSM_GROUP_CH
STATE_W = 2 * SSM_STATE
TB_W = 2 * SUB_W + 2 * STATE_W
SLAB_GROUPS = LANES // SSM_GROUP_CH
SLAB_PAIRS = SLAB_GROUPS // 2
N_SLABS = SSM_GROUPS // SLAB_GROUPS
SLAB_W = SLAB_GROUPS * SUB_W

V7X_VMEM_BYTES = 64 * 1024 * 1024
VMEM_LIMIT = V7X_VMEM_BYTES * 7 // 8
FFN_VMEM_LIMIT = V7X_VMEM_BYTES * 15 // 16

INPROJ_TM = 256
INPROJ_TN = 512
MERGE_TM = 256
FFN_TM = 1024
FFN_TF = 512
FFN_SUB = 512


def _rms(x, g):
    return x * lax.rsqrt(jnp.mean(x * x, axis=-1, keepdims=True) + RMS_EPS) * g


def _sigmoid(x):
    return 0.5 * jnp.tanh(0.5 * x) + 0.5


class _RowCast:
    def __init__(self, stacked, layer, n_steps, step_of):
        _, rows, cols = stacked.shape
        blk = rows // n_steps
        assert blk * n_steps == rows and blk % BF16_SUBLANES == 0, (rows, n_steps)
        self.operand = stacked
        self.in_spec = pl.BlockSpec((None, blk, cols), lambda *ids: (layer, step_of(*ids), 0))
        self.out_spec = pl.BlockSpec((blk, cols), lambda *ids: (step_of(*ids), 0))
        self.out_shape = jax.ShapeDtypeStruct((rows, cols), BF16)


def _cast_blocks(src_refs, dst_refs):
    for src, dst in zip(src_refs, dst_refs):
        dst[...] = src[...].astype(BF16)


N_QKV_BLOCKS = OFF_U // INPROJ_TN
N_MAIN_BLOCKS = OFF_G // INPROJ_TN
N_IN_BLOCKS = IN_WIDTH // INPROJ_TN


QBLOCKS_PER_TILE = INPROJ_TM // BLOCK
ATTN_PROBLEMS_PER_GAP = 2


def _inproj_attn_kernel(sink_ref, x_ref, g_ref, w_ref, b_ref, seg_ref, nt_ref, bias_ref, *refs, tiles_per_seq):
    n_cast = (len(refs) - 5) // 2
    u_ref, gt_ref, ya_ref = refs[n_cast:n_cast + 3]
    q_ref, kv_ref = refs[2 * n_cast + 3:]

    @pl.when(pl.program_id(0) == 0)
    def _():
        kv_ref[...] = jnp.zeros(kv_ref.shape, BF16)

    kv_ref[0:BLOCK, :] = kv_ref[INPROJ_TM:INPROJ_TM + BLOCK, :]
    h = _rms(x_ref[...], g_ref[...]).astype(BF16)

    def project(j):
        return jnp.dot(h, w_ref[:, j * INPROJ_TN:(j + 1) * INPROJ_TN], preferred_element_type=F32)

    for j in range(N_QKV_BLOCKS):
        z = project(j)
        ssq = jnp.dot((z * z).astype(BF16), seg_ref[...], preferred_element_type=F32)
        inv = lax.rsqrt(ssq * (1.0 / HEAD_DIM) + RMS_EPS)
        fac = jnp.where(nt_ref[j, 1:2, :] > 0.0, inv, 1.0) * nt_ref[j, 0:1, :]
        if (j + 1) * INPROJ_TN <= ATTN_WIDTH:
            q_ref[:, j * INPROJ_TN:(j + 1) * INPROJ_TN] = (z * fac).astype(BF16)
        else:
            kv_ref[BLOCK:, :] = (z * fac).astype(BF16)

    def other_block(j):
        z = project(j)
        if j < N_MAIN_BLOCKS:
            c = j - N_QKV_BLOCKS
            u_ref[:, c * INPROJ_TN:(c + 1) * INPROJ_TN] = z
        else:
            cs = slice((j - N_MAIN_BLOCKS) * INPROJ_TN, (j - N_MAIN_BLOCKS + 1) * INPROJ_TN)
            gt_ref[:, cs] = _sigmoid(z + b_ref[:, cs]).astype(BF16)

    kv_rows = 2 * BLOCK
    all_rows = kv_ref.shape[0]
    left_kv = lax.broadcasted_iota(jnp.int32, (all_rows, LANES), 1) < HEAD_DIM
    left_q = lax.broadcasted_iota(jnp.int32, (BLOCK, LANES), 1) < HEAD_DIM
    zeros = jnp.zeros((all_rows, LANES), BF16)
    left_ones = lax.broadcasted_iota(jnp.int32, (kv_rows, LANES), 1) < HEAD_DIM
    ones_l = jnp.where(left_ones, 1.0, 0.0).astype(BF16)
    ones_r = jnp.where(left_ones, 0.0, 1.0).astype(BF16)
    contract_lanes = (((1,), (1,)), ((), ()))

    def slab(off):
        a = kv_ref[:, off:off + LANES]
        return a, pltpu.roll(a.astype(F32), HEAD_DIM, axis=1).astype(BF16)

    problems = []
    for c in range(N_KV_HEADS // 2):
        k_slabs = slab(c * LANES)
        v_slabs = slab(KV_WIDTH + c * LANES)
        for side in range(2):
            kh = 2 * c + side
            k_l = jnp.where(left_kv, k_slabs[side], zeros)
            k_r = jnp.where(left_kv, zeros, k_slabs[1 - side])
            v_l = jnp.where(left_kv, v_slabs[side], zeros)
            v_r = jnp.where(left_kv, zeros, v_slabs[1 - side])
            for qb in range(QBLOCKS_PER_TILE):
                rows = slice(qb * BLOCK, qb * BLOCK + kv_rows)
                kk = jnp.concatenate([k_l[rows], k_r[rows]], axis=0)
                vv = jnp.concatenate([jnp.concatenate([v_l[rows], ones_l], axis=1),
                                      jnp.concatenate([v_r[rows], ones_r], axis=1)], axis=0)
                for pair in range(GQA_GROUP // 2):
                    problems.append((qb, kh * GQA_GROUP + 2 * pair, kk, vv))

    def scores(qb, e, kk):
        q2 = q_ref[qb * BLOCK:(qb + 1) * BLOCK, e * HEAD_DIM:(e + 2) * HEAD_DIM]
        return lax.dot_general(q2, kk, contract_lanes, preferred_element_type=F32)

    no_prev = jnp.logical_and(lax.broadcasted_iota(jnp.int32, (BLOCK, kv_rows), 1) < BLOCK,
                              pl.program_id(0) % tiles_per_seq == 0)

    def finish(qb, e, s2, vv):
        ps, ds = [], []
        for t in range(2):
            s = s2[:, t * kv_rows:(t + 1) * kv_rows] + bias_ref[e + t]
            if qb == 0:
                s = jnp.where(no_prev, -jnp.inf, s)
            sink = sink_ref[e + t]
            m = jnp.maximum(jnp.max(s, axis=-1, keepdims=True), sink)
            ps.append(jnp.exp(s - m).astype(BF16))
            ds.append(jnp.exp(sink - m))
        r = jnp.dot(jnp.concatenate(ps, axis=1), vv, preferred_element_type=F32)
        denom = r[:, LANES:] + jnp.where(left_q, ds[0], ds[1])
        ya_ref[qb * BLOCK:(qb + 1) * BLOCK, e * HEAD_DIM:(e + 2) * HEAD_DIM] = (r[:, :LANES] / denom).astype(BF16)

    others = list(range(N_QKV_BLOCKS, N_IN_BLOCKS))
    for first in range(0, len(problems), ATTN_PROBLEMS_PER_GAP):
        group = problems[first:first + ATTN_PROBLEMS_PER_GAP]
        s2s = [scores(qb, e, kk) for qb, e, kk, _ in group]
        if others:
            other_block(others.pop(0))
        for (qb, e, _, vv), s2 in zip(group, s2s):
            finish(qb, e, s2, vv)
    for j in others:
        other_block(j)
    _cast_blocks(refs[:n_cast], refs[n_cast + 3:2 * n_cast + 3])


def _inproj_attn(x2, gain, w_bf, bias, seg, ntab, sinks, bias_tab, seq, cast_weights):
    t = x2.shape[0]
    tm = INPROJ_TM
    tiles_per_seq = seq // tm
    const2 = lambda i: (0, 0)
    casts = [_RowCast(w, wl, t // tm, lambda i: i) for w, wl in cast_weights]
    outs = pl.pallas_call(
        partial(_inproj_attn_kernel, tiles_per_seq=tiles_per_seq),
        grid=(t // tm,),
        in_specs=[
            pl.BlockSpec(memory_space=pltpu.SMEM),
            pl.BlockSpec((tm, D_MODEL), lambda i: (i, 0)),
            pl.BlockSpec((1, D_MODEL), const2),
            pl.BlockSpec((D_MODEL, IN_WIDTH), const2, pipeline_mode=pl.Buffered(1)),
            pl.BlockSpec((1, 2 * D_MODEL), const2),
            pl.BlockSpec((INPROJ_TN, INPROJ_TN), const2),
            pl.BlockSpec((N_QKV_BLOCKS, 2, INPROJ_TN), lambda i: (0, 0, 0)),
            pl.BlockSpec((N_Q_HEADS, BLOCK, 2 * BLOCK), lambda i: (0, 0, 0), pipeline_mode=pl.Buffered(1)),
        ] + [c.in_spec for c in casts],
        out_specs=[
            pl.BlockSpec((tm, SSM_WIDTH), lambda i: (i, 0)),
            pl.BlockSpec((tm, 2 * D_MODEL), lambda i: (i, 0)),
            pl.BlockSpec((tm, ATTN_WIDTH), lambda i: (i, 0)),
        ] + [c.out_spec for c in casts],
        out_shape=[
            jax.ShapeDtypeStruct((t, SSM_WIDTH), F32),
            jax.ShapeDtypeStruct((t, 2 * D_MODEL), BF16),
            jax.ShapeDtypeStruct((t, ATTN_WIDTH), BF16),
        ] + [c.out_shape for c in casts],
        scratch_shapes=[pltpu.VMEM((tm, ATTN_WIDTH), BF16), pltpu.VMEM((BLOCK + tm, 2 * KV_WIDTH), BF16)],
        compiler_params=pltpu.CompilerParams(
            dimension_semantics=("arbitrary",), vmem_limit_bytes=VMEM_LIMIT),
        name="inproj_attn",
    )(sinks, x2, gain, w_bf, bias, seg, ntab, bias_tab, *[c.operand for c in casts])
    return outs[:3], outs[3:]


def _qk_norm_tables(q_gain, k_gain):
    qrow = jnp.tile(q_gain.astype(F32), INPROJ_TN // HEAD_DIM) * (HEAD_DIM ** -0.5)
    ones = jnp.ones((INPROJ_TN,), F32)
    kvrow = jnp.concatenate([jnp.tile(k_gain.astype(F32), N_KV_HEADS), jnp.ones((KV_WIDTH,), F32)])
    kvmask = jnp.concatenate([jnp.ones((KV_WIDTH,), F32), jnp.zeros((KV_WIDTH,), F32)])
    return jnp.stack([jnp.stack([qrow, ones]), jnp.stack([qrow, ones]), jnp.stack([kvrow, kvmask])])


def _segment_ones():
    r = jnp.arange(INPROJ_TN) // HEAD_DIM
    return (r[:, None] == r[None, :]).astype(BF16)


def _attn_bias_table():
    t_loc = jnp.arange(BLOCK)[:, None]
    s_loc = jnp.arange(2 * BLOCK)[None, :] - BLOCK
    dist = (t_loc - s_loc).astype(F32)
    valid = (dist >= 0) & (dist < WINDOW)
    slopes = jnp.exp2(-8.0 * jnp.arange(1, N_Q_HEADS + 1, dtype=F32) / N_Q_HEADS)
    return jnp.where(valid[None], -slopes[:, None, None] * dist[None], -jnp.inf)


def _cmul(ar, ai, br, bi):
    return ar * br - ai * bi, ar * bi + ai * br


def _pow_by_bits(exps, squares):
    pr = pi = None
    for b, (sr, si) in enumerate(squares):
        on = ((exps >> b) & 1) == 1
        fr, fi = jnp.where(on, sr, 1.0), jnp.where(on, si, 0.0)
        pr, pi = (fr, fi) if pr is None else _cmul(pr, pi, fr, fi)
    return pr, pi


def _ssm_prep_kernel(lam_ref, ldt_ref, btr_ref, bti_ref, ctr_ref, cti_ref, d_ref, *refs):
    n_cast = (len(refs) - 4) // 2
    tb_ref, c_ref, levr_ref, levi_ref = refs[n_cast:n_cast + 4]
    _cast_blocks(refs[:n_cast], refs[n_cast + 4:])
    n_levels = levr_ref.shape[1]
    tau_lane = lax.broadcasted_iota(jnp.int32, (SSM_STATE, SUB_W), 1) // SSM_GROUP_CH
    row_h = lax.broadcasted_iota(jnp.int32, (SSM_GROUP_CH, SUB_W), 0)
    lane_h = lax.broadcasted_iota(jnp.int32, (SSM_GROUP_CH, SUB_W), 1)
    tile_ch = (lane_h % SSM_GROUP_CH == row_h).astype(F32)
    row_m = lax.broadcasted_iota(jnp.int32, (SUB, SSM_STATE), 0)
    eye_p = (lax.broadcasted_iota(jnp.int32, (SSM_STATE, SSM_STATE), 0)
             == lax.broadcasted_iota(jnp.int32, (SSM_STATE, SSM_STATE), 1))

    def to_col(v):
        return jnp.sum(jnp.where(eye_p, v, 0.0), axis=1, keepdims=True)

    def squares(a, n):
        out = [a]
        for _ in range(n - 1):
            out.append(_cmul(*out[-1], *out[-1]))
        return out

    for g in range(SLAB_GROUPS):
        dt = jnp.exp(ldt_ref[g])

        def discretise(lr, li):
            mag = jnp.exp(lr * dt)
            return mag * jnp.cos(li * dt), mag * jnp.sin(li * dt)

        lr_row, li_row = lam_ref[g, 0:1, :], lam_ref[g, 1:2, :]
        a_row = discretise(lr_row, li_row)
        sq_row = squares(a_row, 5)
        sq_col = squares((to_col(a_row[0]), to_col(a_row[1])), 4)
        ar, ai = sq_row[0]
        den = lr_row * lr_row + li_row * li_row
        fr = ((ar - 1.0) * lr_row + ai * li_row) / den
        fi = (ai * lr_row - (ar - 1.0) * li_row) / den
        bbr, bbi = _cmul(fr, fi, btr_ref[g], bti_ref[g])

        e0 = _pow_by_bits(tau_lane, sq_col)
        e1 = _cmul(*e0, *sq_col[0])
        ctr = jnp.dot(ctr_ref[g], tile_ch, precision=HIGHEST, preferred_element_type=F32)
        cti = jnp.dot(cti_ref[g], tile_ch, precision=HIGHEST, preferred_element_type=F32)
        mr, mi = _cmul(*e0, ctr, cti)
        kt = (jnp.dot(bbr, mr, precision=HIGHEST, preferred_element_type=F32)
              - jnp.dot(bbi, mi, precision=HIGHEST, preferred_element_type=F32))
        kt = kt + jnp.where(lane_h == row_h, d_ref[g], 0.0)
        pw = _pow_by_bits(row_m, sq_row[:4])
        q, side = g // 2, g % 2
        own = lambda width: slice(side * width, (side + 1) * width)
        other = lambda width: slice((1 - side) * width, (2 - side) * width)
        zeros_p = jnp.zeros((SSM_GROUP_CH, SSM_STATE), F32)

        def paired(v):
            return jnp.concatenate([v, zeros_p] if side == 0 else [zeros_p, v], axis=1).astype(BF16)

        for j in range(SUB):
            rows = pl.ds(side * SUB_W + j * SSM_GROUP_CH, SSM_GROUP_CH)
            tj = kt if j == 0 else jnp.where(lane_h >= j * SSM_GROUP_CH,
                                             pltpu.roll(kt, j * SSM_GROUP_CH, axis=1), 0.0)
            tb_ref[q, rows, own(SUB_W)] = tj.astype(BF16)
            tb_ref[q, rows, other(SUB_W)] = jnp.zeros((SSM_GROUP_CH, SUB_W), BF16)
            m = SUB - 1 - j
            br_, bi_ = _cmul(pw[0][m:m + 1, :], pw[1][m:m + 1, :], bbr, bbi)
            tb_ref[q, rows, 2 * SUB_W:2 * SUB_W + STATE_W] = paired(br_)
            tb_ref[q, rows, 2 * SUB_W + STATE_W:TB_W] = paired(bi_)
        for part, val in enumerate([ctr * e1[0] - cti * e1[1], -ctr * e1[1] - cti * e1[0]]):
            rows = pl.ds(part * STATE_W + side * SSM_STATE, SSM_STATE)
            c_ref[q, rows, own(SUB_W)] = val.astype(BF16)
            c_ref[q, rows, other(SUB_W)] = jnp.zeros((SSM_STATE, SUB_W), BF16)
        lv = sq_row[4]
        lev_r, lev_i = [], []
        for _ in range(n_levels):
            lev_r.append(lv[0])
            lev_i.append(lv[1])
            lv = _cmul(*lv, *lv)
        levels = (jnp.concatenate(lev_r, axis=0), jnp.concatenate(lev_i, axis=0))
        if side == 0:
            first_levels = levels
        else:
            levr_ref[q] = jnp.concatenate([first_levels[0], levels[0]], axis=1)
            levi_ref[q] = jnp.concatenate([first_levels[1], levels[1]], axis=1)


def _ssm_prep(lam_re, lam_im, log_dt, b_re, b_im, c_re, c_im, d_skip, n_levels, cast_weights):
    dg = lam_re.shape[0] * SSM_GROUPS
    casts = [_RowCast(w, wl, dg // SLAB_GROUPS, lambda s: s) for w, wl in cast_weights]
    h_, p_ = SSM_GROUP_CH, SSM_STATE
    f = lambda a: a.astype(F32)
    lam_rows = jnp.stack([f(lam_re), f(lam_im)], axis=2).reshape(dg, 2, p_)
    btr = f(b_re).transpose(0, 1, 3, 2).reshape(dg, h_, p_)
    bti = f(b_im).transpose(0, 1, 3, 2).reshape(dg, h_, p_)
    ctr = f(c_re).transpose(0, 1, 3, 2).reshape(dg, p_, h_)
    cti = f(c_im).transpose(0, 1, 3, 2).reshape(dg, p_, h_)
    d_rows = jnp.pad(f(d_skip).reshape(dg, 1, h_), ((0, 0), (0, 0), (0, SUB_W - h_)))
    grp = lambda *shape: pl.BlockSpec((SLAB_GROUPS,) + shape, lambda s: (s,) + (0,) * len(shape))
    pair = lambda *shape: pl.BlockSpec((SLAB_PAIRS,) + shape, lambda s: (s,) + (0,) * len(shape))
    outs = pl.pallas_call(
        _ssm_prep_kernel,
        grid=(dg // SLAB_GROUPS,),
        in_specs=[grp(2, p_), grp(1, 1), grp(h_, p_), grp(h_, p_),
                  grp(p_, h_), grp(p_, h_), grp(1, SUB_W)] + [c.in_spec for c in casts],
        out_specs=[pair(2 * SUB_W, TB_W), pair(2 * STATE_W, 2 * SUB_W), pair(n_levels, STATE_W),
                   pair(n_levels, STATE_W)] + [c.out_spec for c in casts],
        out_shape=[
            jax.ShapeDtypeStruct((dg // 2, 2 * SUB_W, TB_W), BF16),
            jax.ShapeDtypeStruct((dg // 2, 2 * STATE_W, 2 * SUB_W), BF16),
            jax.ShapeDtypeStruct((dg // 2, n_levels, STATE_W), F32),
            jax.ShapeDtypeStruct((dg // 2, n_levels, STATE_W), F32),
        ] + [c.out_shape for c in casts],
        compiler_params=pltpu.CompilerParams(
            dimension_semantics=("arbitrary",), vmem_limit_bytes=VMEM_LIMIT),
        name="s5_prep",
    )(lam_rows, f(log_dt).reshape(dg, 1, 1), btr, bti, ctr, cti, d_rows, *[c.operand for c in casts])
    return outs[:4], outs[4:]


CHUNK = 2 * SSM_GROUP_CH
CHUNKS = LANES // CHUNK
PAIRS = SUB // 2
PAIR_W = 2 * LANES


def _pair_permutation():
    r = jnp.arange(PAIR_W)
    jj, lane = r // LANES, r % LANES
    col = (lane // SSM_GROUP_CH) * CHUNK + jj * SSM_GROUP_CH + lane % SSM_GROUP_CH
    return (col[:, None] == jnp.arange(PAIR_W)[None, :]).astype(BF16)


S5_STAGES = 3


def _ssm_kernel(u_ref, perm_ref, perm_t_ref, tb_ref, c_ref, ar_ref, ai_ref, y_ref, xs_ref, ys_ref, cs_ref):
    t = pl.program_id(0)
    rows = 2 * xs_ref.shape[1]
    n_levels = rows.bit_length() - 1
    cur, prv = t % 2, (t + 1) % 2

    @pl.when(t == 0)
    def _():
        xs_ref[...] = jnp.zeros(xs_ref.shape, WORD)
        ys_ref[...] = jnp.zeros(ys_ref.shape, WORD)

    def shifted(a, sh):
        return jnp.concatenate([jnp.zeros((sh, a.shape[1]), a.dtype), a[:rows - sh]], axis=0)

    def rotations(a):
        return [a] + [pltpu.roll(a, r * CHUNK, axis=1) for r in range(1, CHUNKS)]

    def chunk(a, c):
        return a[:, c * CHUNK:(c + 1) * CHUNK]

    def to_words(a):
        return pltpu.bitcast(a.astype(BF16), WORD)

    def from_words(a):
        return pltpu.bitcast(a, BF16)

    def scan_inputs(q):
        return jnp.dot(from_words(xs_ref[prv, :, q * 2 * SUB_W:(q + 1) * 2 * SUB_W]), tb_ref[q],
                       preferred_element_type=F32)

    def layout_out_chunks(k):
        col, dst = k // CHUNKS, k % CHUNKS
        yk = ys_ref[cur, :, k * SUB_W:(k + 1) * SUB_W]
        for v in range(SUB_W // LANES):
            rots = rotations(yk[:, v * LANES:(v + 1) * LANES])
            for src in range(CHUNKS):
                lane0 = (v * CHUNKS + src) * PAIR_W + col * LANES + dst * CHUNK
                cs_ref[:, lane0:lane0 + CHUNK] = chunk(rots[(dst - src) % CHUNKS], dst)

    def layout_in_chunks(k):
        col, dst = k // CHUNKS, k % CHUNKS
        pair_tile = jnp.concatenate([u_ref[pl.ds(2 * k, rows, stride=SUB), :],
                                     u_ref[pl.ds(2 * k + 1, rows, stride=SUB), :]], axis=1).astype(BF16)
        grouped = to_words(jnp.dot(pair_tile, perm_ref[...], preferred_element_type=F32))
        for w in range(PAIR_W // LANES):
            rots = rotations(grouped[:, w * LANES:(w + 1) * LANES])
            for src in range(CHUNKS):
                lane0 = (w * CHUNKS + src) * SUB_W + col * LANES + dst * CHUNK
                xs_ref[cur, :, lane0:lane0 + CHUNK] = chunk(rots[(dst - src) % CHUNKS], dst)

    r_next = scan_inputs(0)
    for q in range(SLAB_PAIRS):
        r = r_next
        if q + 1 < SLAB_PAIRS:
            r_next = scan_inputs(q + 1)
        for k in (2 * q, 2 * q + 1):
            layout_out_chunks(k)
            layout_in_chunks(k)
        yt = r[:, :2 * SUB_W]
        re = r[:, 2 * SUB_W:2 * SUB_W + STATE_W]
        im = r[:, 2 * SUB_W + STATE_W:]
        for lvl in range(n_levels):
            sh = 1 << lvl
            ar = ar_ref[q, lvl:lvl + 1, :]
            ai = ai_ref[q, lvl:lvl + 1, :]
            if sh % SUBLANES:
                pr, pi = shifted(re, sh), shifted(im, sh)
                re, im = re + ar * pr - ai * pi, im + ar * pi + ai * pr
            else:
                pr, pi = re[:rows - sh], im[:rows - sh]
                re, im = (jnp.concatenate([re[:sh], re[sh:] + ar * pr - ai * pi], axis=0),
                          jnp.concatenate([im[:sh], im[sh:] + ar * pi + ai * pr], axis=0))
        prev_state = jnp.concatenate([shifted(re, 1), shifted(im, 1)], axis=1).astype(BF16)
        y = yt + jnp.dot(prev_state, c_ref[q], preferred_element_type=F32)
        ys_ref[prv, :, q * 2 * SUB_W:(q + 1) * 2 * SUB_W] = to_words(jax.nn.gelu(y))

    for pair in range(PAIRS):
        o = jnp.dot(from_words(cs_ref[:, pair * PAIR_W:(pair + 1) * PAIR_W]), perm_t_ref[...],
                    preferred_element_type=F32)
        y_ref[pl.ds(2 * pair, rows, stride=SUB), :] = o[:, :LANES]
        y_ref[pl.ds(2 * pair + 1, rows, stride=SUB), :] = o[:, LANES:]


def _ssm(uf, perm, tb, cmat, lev_r, lev_i, layer, batch, seq):
    t = uf.shape[0]
    n_sub = seq // SUB
    n_levels = lev_r.shape[1]
    n_items = batch * N_SLABS
    item = lambda t, stage: jnp.clip(t - stage, 0, n_items - 1)
    slab3 = lambda t: (layer * N_SLABS + item(t, 1) % N_SLABS, 0, 0)
    return pl.pallas_call(
        _ssm_kernel,
        grid=(n_items + S5_STAGES - 1,),
        in_specs=[
            pl.BlockSpec((seq, LANES), lambda t: (item(t, 0) // N_SLABS, item(t, 0) % N_SLABS)),
            pl.BlockSpec((PAIR_W, PAIR_W), lambda t: (0, 0)),
            pl.BlockSpec((PAIR_W, PAIR_W), lambda t: (0, 0)),
            pl.BlockSpec((SLAB_PAIRS, 2 * SUB_W, TB_W), slab3),
            pl.BlockSpec((SLAB_PAIRS, 2 * STATE_W, 2 * SUB_W), slab3),
            pl.BlockSpec((SLAB_PAIRS, n_levels, STATE_W), slab3),
            pl.BlockSpec((SLAB_PAIRS, n_levels, STATE_W), slab3),
        ],
        out_specs=pl.BlockSpec((seq, LANES), lambda t: (item(t, 2) // N_SLABS, item(t, 2) % N_SLABS)),
        out_shape=jax.ShapeDtypeStruct((t, SSM_WIDTH), F32),
        scratch_shapes=[pltpu.VMEM((2, n_sub // 2, SLAB_W), WORD), pltpu.VMEM((2, n_sub // 2, SLAB_W), WORD),
                        pltpu.VMEM((n_sub // 2, SLAB_W), WORD)],
        compiler_params=pltpu.CompilerParams(
            dimension_semantics=("arbitrary",), vmem_limit_bytes=VMEM_LIMIT),
        name="s5_scan",
    )(uf, perm, perm.T, tb, cmat, lev_r, lev_i)


def _merge_kernel(ya_ref, yg_ref, gt_ref, x_ref, wglu_ref, bglu_ref, wa_ref, ws_ref, wo_ref,
                  gffn_ref, *refs):
    n_cast = (len(refs) - 2) // 2
    x1_ref, h2_ref = refs[n_cast:n_cast + 2]
    yg = yg_ref[...]
    t = jnp.dot(yg.astype(BF16), wglu_ref[...], preferred_element_type=F32) + bglu_ref[...]
    ys = (yg * _sigmoid(t)).astype(BF16)
    ma = jnp.dot(ya_ref[...], wa_ref[...], preferred_element_type=F32)
    ms = jnp.dot(ys, ws_ref[...], preferred_element_type=F32)
    merged = gt_ref[:, :D_MODEL].astype(F32) * ma + gt_ref[:, D_MODEL:].astype(F32) * ms
    x1 = x_ref[...] + jnp.dot(merged.astype(BF16), wo_ref[...], preferred_element_type=F32)
    x1_ref[...] = x1
    h2_ref[...] = _rms(x1, gffn_ref[...]).astype(BF16)
    _cast_blocks(refs[:n_cast], refs[n_cast + 2:])


def _merge(ya, yg, gates, x2, wglu, bglu, wa, ws, wo, gffn, cast_weights):
    t = x2.shape[0]
    tm = MERGE_TM
    casts = [_RowCast(w, wl, t // tm, lambda i: i) for w, wl in cast_weights]

    def wspec(rows, cols):
        return pl.BlockSpec((rows, cols), lambda i: (0, 0), pipeline_mode=pl.Buffered(1))

    outs = pl.pallas_call(
        _merge_kernel,
        grid=(t // tm,),
        in_specs=[
            pl.BlockSpec((tm, ATTN_WIDTH), lambda i: (i, 0)),
            pl.BlockSpec((tm, SSM_WIDTH), lambda i: (i, 0)),
            pl.BlockSpec((tm, 2 * D_MODEL), lambda i: (i, 0)),
            pl.BlockSpec((tm, D_MODEL), lambda i: (i, 0)),
            wspec(SSM_WIDTH, SSM_WIDTH),
            pl.BlockSpec((1, SSM_WIDTH), lambda i: (0, 0)),
            wspec(ATTN_WIDTH, D_MODEL),
            wspec(SSM_WIDTH, D_MODEL),
            wspec(D_MODEL, D_MODEL),
            pl.BlockSpec((1, D_MODEL), lambda i: (0, 0)),
        ] + [c.in_spec for c in casts],
        out_specs=[
            pl.BlockSpec((tm, D_MODEL), lambda i: (i, 0)),
            pl.BlockSpec((tm, D_MODEL), lambda i: (i, 0)),
        ] + [c.out_spec for c in casts],
        out_shape=[
            jax.ShapeDtypeStruct((t, D_MODEL), F32),
            jax.ShapeDtypeStruct((t, D_MODEL), BF16),
        ] + [c.out_shape for c in casts],
        compiler_params=pltpu.CompilerParams(
            dimension_semantics=("arbitrary",), vmem_limit_bytes=VMEM_LIMIT),
        name="merge_out",
    )(ya, yg, gates, x2, wglu, bglu, wa, ws, wo, gffn, *[c.operand for c in casts])
    return outs[:2], outs[2:]


def _ffn_kernel(h_ref, x_ref, wg_ref, wu_ref, wo_ref, o_ref):
    k = pl.program_id(1)

    def accumulate(base_ref):
        for r in range(FFN_TM // FFN_SUB):
            rs = pl.ds(r * FFN_SUB, FFN_SUB)
            h = h_ref[rs, :]
            g = jnp.dot(h, wg_ref[...], preferred_element_type=F32)
            u = jnp.dot(h, wu_ref[...], preferred_element_type=F32)
            act = (g * _sigmoid(g) * u).astype(BF16)
            o_ref[rs, :] = base_ref[rs, :] + jnp.dot(act, wo_ref[...], preferred_element_type=F32)

    @pl.when(k == 0)
    def _():
        accumulate(x_ref)

    @pl.when(k > 0)
    def _():
        accumulate(o_ref)


def _ffn(h2, x1, w_in, w_out):
    t = x1.shape[0]
    nk = D_FF // FFN_TF
    return pl.pallas_call(
        _ffn_kernel,
        grid=(t // FFN_TM, nk),
        in_specs=[
            pl.BlockSpec((FFN_TM, D_MODEL), lambda i, k: (i, 0)),
            pl.BlockSpec((FFN_TM, D_MODEL), lambda i, k: (i, 0)),
            pl.BlockSpec((D_MODEL, FFN_TF), lambda i, k: (0, k)),
            pl.BlockSpec((D_MODEL, FFN_TF), lambda i, k: (0, nk + k)),
            pl.BlockSpec((FFN_TF, D_MODEL), lambda i, k: (k, 0)),
        ],
        out_specs=pl.BlockSpec((FFN_TM, D_MODEL), lambda i, k: (i, 0)),
        out_shape=jax.ShapeDtypeStruct((t, D_MODEL), F32),
        compiler_params=pltpu.CompilerParams(
            dimension_semantics=("arbitrary", "arbitrary"), vmem_limit_bytes=FFN_VMEM_LIMIT),
        name="swiglu_ffn",
    )(h2, x1, w_in, w_in, w_out)


def kernel(x, norm_mix_g, w_in, gate_bias, q_norm_g, k_norm_g, attn_sinks, ssm_lambda_re, ssm_lambda_im, ssm_log_dt, ssm_b_re, ssm_b_im, ssm_c_re, ssm_c_im, ssm_d, ssm_glu_w, ssm_glu_b, w_attn_branch, w_ssm_branch, w_out, norm_ffn_g, w_ffn_in, w_ffn_out):
    batch, seq, _ = x.shape
    t = batch * seq
    n_levels = (seq // SUB).bit_length() - 1
    x2 = x.reshape(t, D_MODEL).astype(F32)
    f32 = lambda w: w.astype(F32)
    seg = _segment_ones()
    bias_tab = _attn_bias_table()
    perm = _pair_permutation()
    w_in_f = f32(w_in)
    (tb, cmat, lev_r, lev_i), (w_in_bf,) = _ssm_prep(
        ssm_lambda_re, ssm_lambda_im, ssm_log_dt, ssm_b_re, ssm_b_im, ssm_c_re, ssm_c_im, ssm_d,
        n_levels, [(w_in_f, 0)])
    merge_weights = [f32(ssm_glu_w), f32(w_attn_branch), f32(w_ssm_branch), f32(w_out)]
    w_ffn_in_f, w_ffn_out_f = f32(w_ffn_in), f32(w_ffn_out)
    for l in range(DEPTH):
        next_w_in = [(w_in_f, l + 1)] if l + 1 < DEPTH else []
        (uf, gates, ya), cast_out = _inproj_attn(
            x2, norm_mix_g[l].reshape(1, D_MODEL).astype(F32), w_in_bf,
            gate_bias[l].reshape(1, 2 * D_MODEL).astype(F32), seg,
            _qk_norm_tables(q_norm_g[l], k_norm_g[l]), attn_sinks[l].astype(F32), bias_tab, seq,
            [(w, l) for w in merge_weights] + [(w_ffn_out_f, l)] + next_w_in)
        wglu_bf, wa_bf, ws_bf, wo_bf, wfo_bf = cast_out[:5]
        w_in_bf = cast_out[5] if next_w_in else None
        yg = _ssm(uf, perm, tb, cmat, lev_r, lev_i, l, batch, seq)
        (x1, h2), (wfi_bf,) = _merge(
            ya, yg, gates, x2, wglu_bf, ssm_glu_b[l].reshape(1, SSM_WIDTH).astype(F32),
            wa_bf, ws_bf, wo_bf, norm_ffn_g[l].reshape(1, D_MODEL).astype(F32),
            [(w_ffn_in_f, l)])
        x2 = _ffn(h2, x1, wfi_bf, wfo_bf)
    return x2.reshape(batch, seq, D_MODEL).astype(x.dtype)
```
